```python
import jax, jax.numpy as jnp
from jax import lax
import numpy as np

D_MODEL = 1024
BATCH = 8
SEQ = 8192
DEPTH = 1

N_META = 16
GLA_HEADS = 4
GLA_DK = D_MODEL // 2
GLA_DV = D_MODEL
GLA_DKH = GLA_DK // GLA_HEADS
GLA_DVH = GLA_DV // GLA_HEADS
GLA_RANK = 16
GATE_TAU = 16.0
CHUNK = 64
META_PAD = CHUNK - N_META
CONF_CH = D_MODEL
CONF_K = 31
D_FF = 2816
FFN_K = 3
IN_WIDTHS = (GLA_DK, GLA_DK, GLA_DV, GLA_DV, GLA_RANK, 2 * CONF_CH, D_MODEL, D_MODEL)
N_IN = sum(IN_WIDTHS)
RMS_EPS = 1e-6
LN_EPS = 1e-5

kernel_name = "gla_conformer_gated_hybrid_block"


def split_points(widths):
    pts, acc = [], 0
    for w in widths[:-1]:
        acc += w
        pts.append(acc)
    return pts


def rms_norm(x, g):
    xf = x.astype(jnp.float32)
    y = xf * lax.rsqrt(jnp.mean(xf * xf, axis=-1, keepdims=True) + RMS_EPS)
    return (y * g.astype(jnp.float32)).astype(x.dtype)


def layer_norm(x, g, b):
    xf = x.astype(jnp.float32)
    mu = jnp.mean(xf, axis=-1, keepdims=True)
    var = jnp.mean(jnp.square(xf - mu), axis=-1, keepdims=True)
    y = (xf - mu) * lax.rsqrt(var + LN_EPS)
    return (y * g.astype(jnp.float32) + b.astype(jnp.float32)).astype(x.dtype)


def causal_dwconv(x, w, b):
    K, C = w.shape
    out = lax.conv_general_dilated(
        x, w[:, None, :].astype(x.dtype), window_strides=(1,), padding=[(K - 1, 0)],
        dimension_numbers=("NWC", "WIO", "NWC"), feature_group_count=C)
    return out + b.astype(x.dtype)


def gla_chunked(q, k, v, log_a):
    out_dtype = v.dtype
    q, k, v, log_a = (t.astype(jnp.float32) for t in (q, k, v, log_a))
    B, H, T, dk = q.shape
    dv = v.shape[-1]
    n = T // CHUNK

    def to_chunks(t):
        return jnp.moveaxis(t.reshape(B, H, n, CHUNK, t.shape[-1]), 2, 0)

    qc, kc, vc, ac = (to_chunks(t) for t in (q, k, v, log_a))
    mask = jnp.tril(jnp.ones((CHUNK, CHUNK), dtype=bool))[:, :, None]

    def step(S, inp):
        qi, ki, vi, ai = inp
        b = jnp.cumsum(ai, axis=-2)
        b_last = b[..., -1:, :]
        o_inter = jnp.einsum('bhck,bhkv->bhcv', qi * jnp.exp(b), S)
        diff = b[..., :, None, :] - b[..., None, :, :]
        decay = jnp.where(mask, jnp.exp(jnp.where(mask, diff, 0.0)), 0.0)
        scores = jnp.einsum('bhik,bhjk,bhijk->bhij', qi, ki, decay)
        o = o_inter + jnp.einsum('bhij,bhjv->bhiv', scores, vi)
        S_new = S * jnp.exp(b_last)[..., 0, :, None] + jnp.einsum(
            'bhck,bhcv->bhkv', ki * jnp.exp(b_last - b), vi)
        return S_new, o

    S0 = jnp.zeros((B, H, dk, dv), jnp.float32)
    _, oc = lax.scan(step, S0, (qc, kc, vc, ac))
    return jnp.moveaxis(oc, 0, 2).reshape(B, H, T, dv).astype(out_dtype)


def _fwd_setup_inputs(seed: int = 0) -> dict:
    key = jax.random.key(seed)
    ks = jax.random.split(key, 24)
    nrm = lambda k, shape, s: jax.random.normal(k, shape, jnp.float32) * s
    return {
        "x": nrm(ks[0], (BATCH, SEQ, D_MODEL), 1.0),
        "meta_tokens": nrm(ks[1], (N_META, D_MODEL), 1.0),
        "norm_mix_g": 1.0 + nrm(ks[2], (DEPTH, D_MODEL), 0.02),
        "w_in": nrm(ks[3], (DEPTH, D_MODEL, N_IN), D_MODEL ** -0.5),
        "w_alpha_up": nrm(ks[4], (DEPTH, GLA_RANK, GLA_DK), GLA_RANK ** -0.5),
        "b_alpha": nrm(ks[5], (DEPTH, GLA_DK), 0.1),
        "gla_norm_g": 1.0 + nrm(ks[6], (DEPTH, GLA_DV), 0.02),
        "w_gla_o": nrm(ks[7], (DEPTH, GLA_DV, D_MODEL), GLA_DV ** -0.5),
        "conf_dw_w": nrm(ks[8], (DEPTH, CONF_K, CONF_CH), CONF_K ** -0.5),
        "conf_dw_b": nrm(ks[9], (DEPTH, CONF_CH), 0.02),
        "conf_ln_g": 1.0 + nrm(ks[10], (DEPTH, CONF_CH), 0.02),
        "conf_ln_b": nrm(ks[11], (DEPTH, CONF_CH), 0.02),
        "w_conf_o": nrm(ks[12], (DEPTH, CONF_CH, D_MODEL), CONF_CH ** -0.5),
        "w_out": nrm(ks[13], (DEPTH, D_MODEL, D_MODEL), D_MODEL ** -0.5),
        "norm_ffn_g": 1.0 + nrm(ks[14], (DEPTH, D_MODEL), 0.02),
        "w_up": nrm(ks[15], (DEPTH, D_MODEL, 2 * D_FF), D_MODEL ** -0.5),
        "ffn_dw_w": nrm(ks[16], (DEPTH, FFN_K, D_FF), FFN_K ** -0.5),
        "ffn_dw_b": nrm(ks[17], (DEPTH, D_FF), 0.02),
        "w_down": nrm(ks[18], (DEPTH, D_FF, D_MODEL), D_FF ** -0.5),
        "final_norm_g": 1.0 + nrm(ks[19], (D_MODEL,), 0.02),
    }


def _fwd_reference(x, meta_tokens, norm_mix_g, w_in, w_alpha_up, b_alpha, gla_norm_g, w_gla_o,
              conf_dw_w, conf_dw_b, conf_ln_g, conf_ln_b, w_conf_o, w_out, norm_ffn_g,
              w_up, ffn_dw_w, ffn_dw_b, w_down, final_norm_g):
    B, S, D = x.shape
    L = S + N_META
    meta = jnp.broadcast_to(meta_tokens[None].astype(x.dtype), (B, N_META, D))
    h = jnp.concatenate([meta, x], axis=1)
    pts = split_points(IN_WIDTHS)
    seq_pad = [(0, 0), (0, 0), (META_PAD, 0), (0, 0)]

    def heads(t):
        return t.reshape(B, L, GLA_HEADS, -1).transpose(0, 2, 1, 3)

    for l in range(DEPTH):
        u = rms_norm(h, norm_mix_g[l])
        proj = u @ w_in[l]
        q, k, v, r, a_lr, c_in, g_gla, g_conf = jnp.split(proj, pts, axis=-1)

        log_a = jax.nn.log_sigmoid(
            (a_lr @ w_alpha_up[l] + b_alpha[l]).astype(jnp.float32)) / GATE_TAU
        qh = jnp.pad(heads(q) * (GLA_DKH ** -0.5), seq_pad)
        kh = jnp.pad(heads(k), seq_pad)
        vh = jnp.pad(heads(v), seq_pad)
        ah = jnp.pad(heads(log_a), seq_pad)
        o = gla_chunked(qh, kh, vh, ah)[:, :, META_PAD:]
        o = rms_norm(o.transpose(0, 2, 1, 3), gla_norm_g[l].reshape(GLA_HEADS, GLA_DVH))
        o = o.reshape(B, L, GLA_DV) * jax.nn.silu(r)
        br_gla = o @ w_gla_o[l]

        c1, c2 = jnp.split(c_in, 2, axis=-1)
        c = c1 * jax.nn.sigmoid(c2)
        c = causal_dwconv(c, conf_dw_w[l], conf_dw_b[l])
        c = layer_norm(c, conf_ln_g[l], conf_ln_b[l])
        br_conf = jax.nn.silu(c) @ w_conf_o[l]

        merged = jax.nn.sigmoid(g_gla) * br_gla + jax.nn.sigmoid(g_conf) * br_conf
        h = h + merged @ w_out[l]

        u = rms_norm(h, norm_ffn_g[l])
        a, bv = jnp.split(u @ w_up[l], 2, axis=-1)
        a = causal_dwconv(a, ffn_dw_w[l], ffn_dw_b[l])
        h = h + (jax.nn.silu(a) * bv) @ w_down[l]

    h = rms_norm(h, final_norm_g)
    return h[:, N_META:]


import jax as _jax
import jax.numpy as _jnp

TWIN_FORMAT = 'train_step'
FWD_PARAMS = ['x', 'meta_tokens', 'norm_mix_g', 'w_in', 'w_alpha_up', 'b_alpha', 'gla_norm_g', 'w_gla_o', 'conf_dw_w', 'conf_dw_b', 'conf_ln_g', 'conf_ln_b', 'w_conf_o', 'w_out', 'norm_ffn_g', 'w_up', 'ffn_dw_w', 'ffn_dw_b', 'w_down', 'final_norm_g']
TWIN_WEIGHTS = ['meta_tokens', 'norm_mix_g', 'w_in', 'w_alpha_up', 'b_alpha', 'gla_norm_g', 'w_gla_o', 'conf_dw_w', 'conf_dw_b', 'conf_ln_g', 'conf_ln_b', 'w_conf_o', 'w_out', 'norm_ffn_g', 'w_up', 'ffn_dw_w', 'ffn_dw_b', 'w_down', 'final_norm_g']
TWIN_DIFF_INPUT = 'x'
TWIN_INPUTS = ['x', 'meta_tokens', 'norm_mix_g', 'w_in', 'w_alpha_up', 'b_alpha', 'gla_norm_g', 'w_gla_o', 'conf_dw_w', 'conf_dw_b', 'conf_ln_g', 'conf_ln_b', 'w_conf_o', 'w_out', 'norm_ffn_g', 'w_up', 'ffn_dw_w', 'ffn_dw_b', 'w_down', 'final_norm_g', 'loss_target', 'm_meta_tokens', 'm_norm_mix_g', 'm_w_in', 'm_w_alpha_up', 'm_b_alpha', 'm_gla_norm_g', 'm_w_gla_o', 'm_conf_dw_w', 'm_conf_dw_b', 'm_conf_ln_g', 'm_conf_ln_b', 'm_w_conf_o', 'm_w_out', 'm_norm_ffn_g', 'm_w_up', 'm_ffn_dw_w', 'm_ffn_dw_b', 'm_w_down', 'm_final_norm_g', 'v_meta_tokens', 'v_norm_mix_g', 'v_w_in', 'v_w_alpha_up', 'v_b_alpha', 'v_gla_norm_g', 'v_w_gla_o', 'v_conf_dw_w', 'v_conf_dw_b', 'v_conf_ln_g', 'v_conf_ln_b', 'v_w_conf_o', 'v_w_out', 'v_norm_ffn_g', 'v_w_up', 'v_ffn_dw_w', 'v_ffn_dw_b', 'v_w_down', 'v_final_norm_g']
TWIN_OUTPUTS = ['loss', 'grad_x', 'grad_meta_tokens', 'grad_norm_mix_g', 'grad_w_in', 'grad_w_alpha_up', 'grad_b_alpha', 'grad_gla_norm_g', 'grad_w_gla_o', 'grad_conf_dw_w', 'grad_conf_dw_b', 'grad_conf_ln_g', 'grad_conf_ln_b', 'grad_w_conf_o', 'grad_w_out', 'grad_norm_ffn_g', 'grad_w_up', 'grad_ffn_dw_w', 'grad_ffn_dw_b', 'grad_w_down', 'grad_final_norm_g', 'delta_meta_tokens', 'delta_norm_mix_g', 'delta_w_in', 'delta_w_alpha_up', 'delta_b_alpha', 'delta_gla_norm_g', 'delta_w_gla_o', 'delta_conf_dw_w', 'delta_conf_dw_b', 'delta_conf_ln_g', 'delta_conf_ln_b', 'delta_w_conf_o', 'delta_w_out', 'delta_norm_ffn_g', 'delta_w_up', 'delta_ffn_dw_w', 'delta_ffn_dw_b', 'delta_w_down', 'delta_final_norm_g', 'new_m_meta_tokens', 'new_m_norm_mix_g', 'new_m_w_in', 'new_m_w_alpha_up', 'new_m_b_alpha', 'new_m_gla_norm_g', 'new_m_w_gla_o', 'new_m_conf_dw_w', 'new_m_conf_dw_b', 'new_m_conf_ln_g', 'new_m_conf_ln_b', 'new_m_w_conf_o', 'new_m_w_out', 'new_m_norm_ffn_g', 'new_m_w_up', 'new_m_ffn_dw_w', 'new_m_ffn_dw_b', 'new_m_w_down', 'new_m_final_norm_g', 'new_v_meta_tokens', 'new_v_norm_mix_g', 'new_v_w_in', 'new_v_w_alpha_up', 'new_v_b_alpha', 'new_v_gla_norm_g', 'new_v_w_gla_o', 'new_v_conf_dw_w', 'new_v_conf_dw_b', 'new_v_conf_ln_g', 'new_v_conf_ln_b', 'new_v_w_conf_o', 'new_v_w_out', 'new_v_norm_ffn_g', 'new_v_w_up', 'new_v_ffn_dw_w', 'new_v_ffn_dw_b', 'new_v_w_down', 'new_v_final_norm_g']
TWIN_LEAF_KINDS = {'loss': 'loss', 'grad_x': 'grad_x', 'grad_meta_tokens': 'grad_w', 'grad_norm_mix_g': 'grad_w', 'grad_w_in': 'grad_w', 'grad_w_alpha_up': 'grad_w', 'grad_b_alpha': 'grad_w', 'grad_gla_norm_g': 'grad_w', 'grad_w_gla_o': 'grad_w', 'grad_conf_dw_w': 'grad_w', 'grad_conf_dw_b': 'grad_w', 'grad_conf_ln_g': 'grad_w', 'grad_conf_ln_b': 'grad_w', 'grad_w_conf_o': 'grad_w', 'grad_w_out': 'grad_w', 'grad_norm_ffn_g': 'grad_w', 'grad_w_up': 'grad_w', 'grad_ffn_dw_w': 'grad_w', 'grad_ffn_dw_b': 'grad_w', 'grad_w_down': 'grad_w', 'grad_final_norm_g': 'grad_w', 'delta_meta_tokens': 'delta_w', 'delta_norm_mix_g': 'delta_w', 'delta_w_in': 'delta_w', 'delta_w_alpha_up': 'delta_w', 'delta_b_alpha': 'delta_w', 'delta_gla_norm_g': 'delta_w', 'delta_w_gla_o': 'delta_w', 'delta_conf_dw_w': 'delta_w', 'delta_conf_dw_b': 'delta_w', 'delta_conf_ln_g': 'delta_w', 'delta_conf_ln_b': 'delta_w', 'delta_w_conf_o': 'delta_w', 'delta_w_out': 'delta_w', 'delta_norm_ffn_g': 'delta_w', 'delta_w_up': 'delta_w', 'delta_ffn_dw_w': 'delta_w', 'delta_ffn_dw_b': 'delta_w', 'delta_w_down': 'delta_w', 'delta_final_norm_g': 'delta_w', 'new_m_meta_tokens': 'new_m', 'new_m_norm_mix_g': 'new_m', 'new_m_w_in': 'new_m', 'new_m_w_alpha_up': 'new_m', 'new_m_b_alpha': 'new_m', 'new_m_gla_norm_g': 'new_m', 'new_m_w_gla_o': 'new_m', 'new_m_conf_dw_w': 'new_m', 'new_m_conf_dw_b': 'new_m', 'new_m_conf_ln_g': 'new_m', 'new_m_conf_ln_b': 'new_m', 'new_m_w_conf_o': 'new_m', 'new_m_w_out': 'new_m', 'new_m_norm_ffn_g': 'new_m', 'new_m_w_up': 'new_m', 'new_m_ffn_dw_w': 'new_m', 'new_m_ffn_dw_b': 'new_m', 'new_m_w_down': 'new_m', 'new_m_final_norm_g': 'new_m', 'new_v_meta_tokens': 'new_v', 'new_v_norm_mix_g': 'new_v', 'new_v_w_in': 'new_v', 'new_v_w_alpha_up': 'new_v', 'new_v_b_alpha': 'new_v', 'new_v_gla_norm_g': 'new_v', 'new_v_w_gla_o': 'new_v', 'new_v_conf_dw_w': 'new_v', 'new_v_conf_dw_b': 'new_v', 'new_v_conf_ln_g': 'new_v', 'new_v_conf_ln_b': 'new_v', 'new_v_w_conf_o': 'new_v', 'new_v_w_out': 'new_v', 'new_v_norm_ffn_g': 'new_v', 'new_v_w_up': 'new_v', 'new_v_ffn_dw_w': 'new_v', 'new_v_ffn_dw_b': 'new_v', 'new_v_w_down': 'new_v', 'new_v_final_norm_g': 'new_v'}


def _forward(args):
    return _fwd_reference(*[args[k] for k in FWD_PARAMS])


def _output_shape():
    def fwd():
        inp = _fwd_setup_inputs(0)
        return _fwd_reference(*[inp[k] for k in FWD_PARAMS])
    out = _jax.eval_shape(fwd)
    return out.shape, out.dtype

N_MICROBATCH = 1
ADAM_LR = 0.001
ADAM_B1 = 0.9
ADAM_B2 = 0.999
ADAM_EPS = 1e-08
ADAM_WD = 0.01
ADAM_STEP = 10
PER_EXAMPLE_BATCH_AXIS = {'x': 0, 'loss_target': 0}
SHARED_INPUTS = []
_WEIGHT_DTYPES = {'meta_tokens': _jnp.float32, 'norm_mix_g': _jnp.float32, 'w_in': _jnp.float32, 'w_alpha_up': _jnp.float32, 'b_alpha': _jnp.float32, 'gla_norm_g': _jnp.float32, 'w_gla_o': _jnp.float32, 'conf_dw_w': _jnp.float32, 'conf_dw_b': _jnp.float32, 'conf_ln_g': _jnp.float32, 'conf_ln_b': _jnp.float32, 'w_conf_o': _jnp.float32, 'w_out': _jnp.float32, 'norm_ffn_g': _jnp.float32, 'w_up': _jnp.float32, 'ffn_dw_w': _jnp.float32, 'ffn_dw_b': _jnp.float32, 'w_down': _jnp.float32, 'final_norm_g': _jnp.float32}
MOMENT_SCALE = {'meta_tokens': 8.110323e-03, 'norm_mix_g': 2.134546e-01, 'w_in': 7.457716e-02, 'w_alpha_up': 1.377920e-02, 'b_alpha': 5.741226e-02, 'gla_norm_g': 8.279911e-02, 'w_gla_o': 8.195008e-02, 'conf_dw_w': 8.615917e-02, 'conf_dw_b': 1.639185e-01, 'conf_ln_g': 1.016300e-01, 'conf_ln_b': 8.756707e-02, 'w_conf_o': 8.511819e-02, 'w_out': 1.182700e-01, 'norm_ffn_g': 1.769210e-01, 'w_up': 7.449544e-02, 'ffn_dw_w': 7.639496e-02, 'ffn_dw_b': 7.388076e-02, 'w_down': 1.223338e-01, 'final_norm_g': 6.400226e+01}


def _to_microbatches(a, axis):
    t = _jnp.moveaxis(a, axis, 0)
    t = t.reshape((N_MICROBATCH, t.shape[0] // N_MICROBATCH) + t.shape[1:])
    return _jnp.moveaxis(t, 1, axis + 1)


def setup_inputs(seed: int = 0) -> dict:
    inp = _fwd_setup_inputs(seed)
    key = _jax.random.fold_in(_jax.random.key(seed), 7919)
    shape, _ = _output_shape()
    out = dict(inp)
    out["loss_target"] = _jax.random.normal(_jax.random.fold_in(key, 0), shape, _jnp.float32)
    for i, name in enumerate(TWIN_WEIGHTS):
        w = inp[name].astype(_jnp.float32)
        if MOMENT_SCALE is None:
            s = _jnp.sqrt(_jnp.mean(_jnp.square(w)) + 1e-30)
        else:
            s = MOMENT_SCALE[name]
        km, kv = _jax.random.split(_jax.random.fold_in(key, i + 1))
        out[name] = w
        out["m_" + name] = s * _jax.random.normal(km, w.shape, _jnp.float32)
        out["v_" + name] = (s * s) * _jax.random.uniform(kv, w.shape, _jnp.float32, 0.5, 1.5)
    if N_MICROBATCH > 1:
        for name, axis in PER_EXAMPLE_BATCH_AXIS.items():
            out[name] = _to_microbatches(out[name], axis)
    return {'x': out['x'], 'meta_tokens': out['meta_tokens'], 'norm_mix_g': out['norm_mix_g'], 'w_in': out['w_in'], 'w_alpha_up': out['w_alpha_up'], 'b_alpha': out['b_alpha'], 'gla_norm_g': out['gla_norm_g'], 'w_gla_o': out['w_gla_o'], 'conf_dw_w': out['conf_dw_w'], 'conf_dw_b': out['conf_dw_b'], 'conf_ln_g': out['conf_ln_g'], 'conf_ln_b': out['conf_ln_b'], 'w_conf_o': out['w_conf_o'], 'w_out': out['w_out'], 'norm_ffn_g': out['norm_ffn_g'], 'w_up': out['w_up'], 'ffn_dw_w': out['ffn_dw_w'], 'ffn_dw_b': out['ffn_dw_b'], 'w_down': out['w_down'], 'final_norm_g': out['final_norm_g'], 'loss_target': out['loss_target'], 'm_meta_tokens': out['m_meta_tokens'], 'm_norm_mix_g': out['m_norm_mix_g'], 'm_w_in': out['m_w_in'], 'm_w_alpha_up': out['m_w_alpha_up'], 'm_b_alpha': out['m_b_alpha'], 'm_gla_norm_g': out['m_gla_norm_g'], 'm_w_gla_o': out['m_w_gla_o'], 'm_conf_dw_w': out['m_conf_dw_w'], 'm_conf_dw_b': out['m_conf_dw_b'], 'm_conf_ln_g': out['m_conf_ln_g'], 'm_conf_ln_b': out['m_conf_ln_b'], 'm_w_conf_o': out['m_w_conf_o'], 'm_w_out': out['m_w_out'], 'm_norm_ffn_g': out['m_norm_ffn_g'], 'm_w_up': out['m_w_up'], 'm_ffn_dw_w': out['m_ffn_dw_w'], 'm_ffn_dw_b': out['m_ffn_dw_b'], 'm_w_down': out['m_w_down'], 'm_final_norm_g': out['m_final_norm_g'], 'v_meta_tokens': out['v_meta_tokens'], 'v_norm_mix_g': out['v_norm_mix_g'], 'v_w_in': out['v_w_in'], 'v_w_alpha_up': out['v_w_alpha_up'], 'v_b_alpha': out['v_b_alpha'], 'v_gla_norm_g': out['v_gla_norm_g'], 'v_w_gla_o': out['v_w_gla_o'], 'v_conf_dw_w': out['v_conf_dw_w'], 'v_conf_dw_b': out['v_conf_dw_b'], 'v_conf_ln_g': out['v_conf_ln_g'], 'v_conf_ln_b': out['v_conf_ln_b'], 'v_w_conf_o': out['v_w_conf_o'], 'v_w_out': out['v_w_out'], 'v_norm_ffn_g': out['v_norm_ffn_g'], 'v_w_up': out['v_w_up'], 'v_ffn_dw_w': out['v_ffn_dw_w'], 'v_ffn_dw_b': out['v_ffn_dw_b'], 'v_w_down': out['v_w_down'], 'v_final_norm_g': out['v_final_norm_g']}


def _loss(weights, diff, rest, loss_target):
    with _jax.named_scope("forward"):
        args = {**rest, TWIN_DIFF_INPUT: diff, **{k: w.astype(_WEIGHT_DTYPES[k]) for k, w in weights.items()}}
        y = _forward(args)
    with _jax.named_scope("loss_head"):
        err = _jnp.square(y.astype(_jnp.float32) - loss_target)
        return 0.5 * _jnp.sum(_jnp.mean(err, axis=-1)) if err.ndim else 0.5 * err


def _adamw(w, g, m, v):
    m = ADAM_B1 * m + (1.0 - ADAM_B1) * g
    v = ADAM_B2 * v + (1.0 - ADAM_B2) * _jnp.square(g)
    m_hat = m / (1.0 - ADAM_B1 ** ADAM_STEP)
    v_hat = v / (1.0 - ADAM_B2 ** ADAM_STEP)
    delta = -ADAM_LR * (m_hat / (_jnp.sqrt(v_hat) + ADAM_EPS) + ADAM_WD * w)
    return delta, m, v


def reference(x, meta_tokens, norm_mix_g, w_in, w_alpha_up, b_alpha, gla_norm_g, w_gla_o, conf_dw_w, conf_dw_b, conf_ln_g, conf_ln_b, w_conf_o, w_out, norm_ffn_g, w_up, ffn_dw_w, ffn_dw_b, w_down, final_norm_g, loss_target, m_meta_tokens, m_norm_mix_g, m_w_in, m_w_alpha_up, m_b_alpha, m_gla_norm_g, m_w_gla_o, m_conf_dw_w, m_conf_dw_b, m_conf_ln_g, m_conf_ln_b, m_w_conf_o, m_w_out, m_norm_ffn_g, m_w_up, m_ffn_dw_w, m_ffn_dw_b, m_w_down, m_final_norm_g, v_meta_tokens, v_norm_mix_g, v_w_in, v_w_alpha_up, v_b_alpha, v_gla_norm_g, v_w_gla_o, v_conf_dw_w, v_conf_dw_b, v_conf_ln_g, v_conf_ln_b, v_w_conf_o, v_w_out, v_norm_ffn_g, v_w_up, v_ffn_dw_w, v_ffn_dw_b, v_w_down, v_final_norm_g):
    given = dict(x=x, meta_tokens=meta_tokens, norm_mix_g=norm_mix_g, w_in=w_in, w_alpha_up=w_alpha_up, b_alpha=b_alpha, gla_norm_g=gla_norm_g, w_gla_o=w_gla_o, conf_dw_w=conf_dw_w, conf_dw_b=conf_dw_b, conf_ln_g=conf_ln_g, conf_ln_b=conf_ln_b, w_conf_o=w_conf_o, w_out=w_out, norm_ffn_g=norm_ffn_g, w_up=w_up, ffn_dw_w=ffn_dw_w, ffn_dw_b=ffn_dw_b, w_down=w_down, final_norm_g=final_norm_g, loss_target=loss_target, m_meta_tokens=m_meta_tokens, m_norm_mix_g=m_norm_mix_g, m_w_in=m_w_in, m_w_alpha_up=m_w_alpha_up, m_b_alpha=m_b_alpha, m_gla_norm_g=m_gla_norm_g, m_w_gla_o=m_w_gla_o, m_conf_dw_w=m_conf_dw_w, m_conf_dw_b=m_conf_dw_b, m_conf_ln_g=m_conf_ln_g, m_conf_ln_b=m_conf_ln_b, m_w_conf_o=m_w_conf_o, m_w_out=m_w_out, m_norm_ffn_g=m_norm_ffn_g, m_w_up=m_w_up, m_ffn_dw_w=m_ffn_dw_w, m_ffn_dw_b=m_ffn_dw_b, m_w_down=m_w_down, m_final_norm_g=m_final_norm_g, v_meta_tokens=v_meta_tokens, v_norm_mix_g=v_norm_mix_g, v_w_in=v_w_in, v_w_alpha_up=v_w_alpha_up, v_b_alpha=v_b_alpha, v_gla_norm_g=v_gla_norm_g, v_w_gla_o=v_w_gla_o, v_conf_dw_w=v_conf_dw_w, v_conf_dw_b=v_conf_dw_b, v_conf_ln_g=v_conf_ln_g, v_conf_ln_b=v_conf_ln_b, v_w_conf_o=v_w_conf_o, v_w_out=v_w_out, v_norm_ffn_g=v_norm_ffn_g, v_w_up=v_w_up, v_ffn_dw_w=v_ffn_dw_w, v_ffn_dw_b=v_ffn_dw_b, v_w_down=v_w_down, v_final_norm_g=v_final_norm_g)
    weights = {n: given[n] for n in TWIN_WEIGHTS}
    shared = {n: given[n] for n in SHARED_INPUTS}
    per_example = {n: given[n] for n in ['x']}
    grad_fn = _jax.value_and_grad(_loss, argnums=(0, 1))

    def one_microbatch(ex, loss_target):
        ex = dict(ex)
        diff = ex.pop(TWIN_DIFF_INPUT)
        return grad_fn(weights, diff, {**shared, **ex}, loss_target)

    if N_MICROBATCH == 1:
        loss, (grad_w, grad_x) = one_microbatch(per_example, given["loss_target"])
    else:
        def body(carry, xs):
            loss_sum, grad_sum = carry
            l_k, (gw_k, gx_k) = one_microbatch(xs[0], xs[1])
            with _jax.named_scope("update"):
                return (loss_sum + l_k, _jax.tree.map(_jnp.add, grad_sum, gw_k)), gx_k

        init = (_jnp.zeros((), _jnp.float32), _jax.tree.map(_jnp.zeros_like, weights))
        (loss, grad_w), grad_x = _jax.lax.scan(body, init, (per_example, given["loss_target"]))
    with _jax.named_scope("update"):
        delta_w, new_m, new_v = {}, {}, {}
        for n in TWIN_WEIGHTS:
            delta_w[n], new_m[n], new_v[n] = _adamw(weights[n], grad_w[n], given["m_" + n], given["v_" + n])
    return (loss, grad_x, *[grad_w[n] for n in TWIN_WEIGHTS], *[delta_w[n] for n in TWIN_WEIGHTS],
            *[new_m[n] for n in TWIN_WEIGHTS], *[new_v[n] for n in TWIN_WEIGHTS])
```

```python
import functools
import math

import jax
import jax.numpy as jnp
from jax import lax
from jax.experimental import pallas as pl
from jax.experimental.pallas import tpu as pltpu

F32 = jnp.float32
BF16 = jnp.bfloat16

N_META = 16
PAD = 240
HEAD_ROWS = PAD + N_META
HEADS = 4
GLA_RANK = 16
RANK_PAD = 128
GATE_TAU = 16.0
CHUNK = 64
CONF_K = 31
FFN_K = 3
RMS_EPS = 1e-6
LN_EPS = 1e-5
ADAM_LR, ADAM_B1, ADAM_B2, ADAM_EPS, ADAM_WD, ADAM_STEP = 0.001, 0.9, 0.999, 1e-08, 0.01, 10

ROW_TILE = 256
HALO = 32
LANES = 128
V7X_VMEM_LIMIT = 56 * 1024 * 1024
MESH = pl.DeviceIdType.MESH


def _cparams(sem):
    return pltpu.CompilerParams(dimension_semantics=sem, vmem_limit_bytes=V7X_VMEM_LIMIT)


def _sigmoid(x):
    return 1.0 / (1.0 + jnp.exp(-x))


def _pick(n, prefs):
    for p in prefs:
        if n % p == 0:
            return p
    return n


def _matmul(a, b, *, dims, name, tm=None, tn=None, tk=None, out_dtype=F32, add=None):
    if dims == "nn":
        (m, k), (_, n) = a.shape, b.shape
    elif dims == "nt":
        (m, k), (n, _) = a.shape, b.shape
    else:
        (k, m), (_, n) = a.shape, b.shape
    tm = tm or _pick(m, (768, 1024, 1408, 512, 256, 128))
    tn = tn or _pick(n, (1024, 1408, 512, 256, 128))
    tk = tk or _pick(k, (1024, 768, 1408, 512, 256, 128))
    nk = k // tk
    assert m % tm == 0 and n % tn == 0 and k % tk == 0, (name, m, n, k, tm, tn, tk)
    a_spec = {"nn": pl.BlockSpec((tm, tk), lambda i, j, kk: (i, kk)),
              "nt": pl.BlockSpec((tm, tk), lambda i, j, kk: (i, kk)),
              "tn": pl.BlockSpec((tk, tm), lambda i, j, kk: (kk, i))}[dims]
    b_spec = {"nn": pl.BlockSpec((tk, tn), lambda i, j, kk: (kk, j)),
              "nt": pl.BlockSpec((tn, tk), lambda i, j, kk: (j, kk)),
              "tn": pl.BlockSpec((tk, tn), lambda i, j, kk: (kk, j))}[dims]
    contract = {"nn": (((1,), (0,)), ((), ())), "nt": (((1,), (1,)), ((), ())), "tn": (((0,), (0,)), ((), ()))}[dims]
    o_spec = pl.BlockSpec((tm, tn), lambda i, j, kk: (i, j))
    has_add = add is not None

    def body(*refs):
        if has_add:
            a_ref, b_ref, add_ref, o_ref, acc_ref = refs
        else:
            a_ref, b_ref, o_ref, acc_ref = refs
            add_ref = None
        kk = pl.program_id(2)
        prod = lax.dot_general(a_ref[...].astype(BF16), b_ref[...].astype(BF16), contract,
                               preferred_element_type=F32)

        @pl.when(kk == 0)
        def _():
            acc_ref[...] = prod + add_ref[...].astype(F32) if has_add else prod

        @pl.when(kk > 0)
        def _():
            acc_ref[...] += prod

        @pl.when(kk == nk - 1)
        def _():
            o_ref[...] = acc_ref[...].astype(out_dtype)

    in_specs = [a_spec, b_spec] + ([o_spec] if has_add else [])
    args = (a, b) + ((add,) if has_add else ())
    return pl.pallas_call(
        body, name=name, grid=(m // tm, n // tn, nk),
        in_specs=in_specs, out_specs=o_spec,
        out_shape=jax.ShapeDtypeStruct((m, n), out_dtype),
        scratch_shapes=[pltpu.VMEM((tm, tn), F32)],
        compiler_params=_cparams(("parallel", "parallel", "arbitrary")),
    )(*args)


def _row_mask(tile_index, rows):
    r = tile_index * rows + lax.broadcasted_iota(jnp.int32, (rows, 1), 0)
    return (r >= PAD).astype(F32)


def _prep(x, meta, g1):
    s, d = x.shape
    t = HEAD_ROWS + s
    tm = ROW_TILE

    def body(x_ref, meta_ref, g_ref, h_ref, u_ref):
        i = pl.program_id(0)

        @pl.when(i == 0)
        def _():
            h_ref[0:PAD, :] = jnp.zeros((PAD, d), F32)
            h_ref[PAD:HEAD_ROWS, :] = meta_ref[...]

        @pl.when(i > 0)
        def _():
            h_ref[...] = x_ref[...]

        h = h_ref[...]
        r = lax.rsqrt(jnp.mean(h * h, axis=-1, keepdims=True) + RMS_EPS)
        u_ref[...] = (h * r * g_ref[...]).astype(BF16)

    return pl.pallas_call(
        body, name="prep", grid=(t // tm,),
        in_specs=[pl.BlockSpec((tm, d), lambda i: (jnp.maximum(i - 1, 0), 0)),
                  pl.BlockSpec((N_META, d), lambda i: (0, 0)),
                  pl.BlockSpec((1, d), lambda i: (0, 0))],
        out_specs=[pl.BlockSpec((tm, d), lambda i: (i, 0)), pl.BlockSpec((tm, d), lambda i: (i, 0))],
        out_shape=[jax.ShapeDtypeStruct((t, d), F32), jax.ShapeDtypeStruct((t, d), BF16)],
        compiler_params=_cparams(("parallel",)),
    )(x, meta, g1)


def _log_gate(alr, wau, b_alpha):
    t = alr.shape[0]
    dk = wau.shape[1]
    tm = ROW_TILE

    def body(alr_ref, w_ref, b_ref, o_ref):
        z = jnp.dot(alr_ref[...].astype(BF16), w_ref[...].astype(BF16), preferred_element_type=F32) + b_ref[...]
        ls = jnp.minimum(z, 0.0) - jnp.log(1.0 + jnp.exp(-jnp.abs(z)))
        o_ref[...] = ls * (1.0 / GATE_TAU) * _row_mask(pl.program_id(0), tm)

    return pl.pallas_call(
        body, name="log_gate", grid=(t // tm,),
        in_specs=[pl.BlockSpec((tm, RANK_PAD), lambda i: (i, 0)),
                  pl.BlockSpec((RANK_PAD, dk), lambda i: (0, 0)),
                  pl.BlockSpec((1, dk), lambda i: (0, 0))],
        out_specs=pl.BlockSpec((tm, dk), lambda i: (i, 0)),
        out_shape=jax.ShapeDtypeStruct((t, dk), F32),
        compiler_params=_cparams(("parallel",)),
    )(alr, wau, b_alpha)


def _tri(n, upper=False):
    r = lax.broadcasted_iota(jnp.int32, (n, n), 0)
    c = lax.broadcasted_iota(jnp.int32, (n, n), 1)
    return (r <= c) if upper else (r >= c)


_NT = (((1,), (1,)), ((), ()))
_TN = (((0,), (0,)), ((), ()))


def _dot(a, b):
    return jnp.dot(a, b, preferred_element_type=F32)


def _dot_nt(a, b):
    return lax.dot_general(a, b, _NT, preferred_element_type=F32)


def _dot_tn(a, b):
    return lax.dot_general(a, b, _TN, preferred_element_type=F32)


def _chunk_decays(la, tri_f32):
    b = jnp.dot(tri_f32, la, preferred_element_type=F32, precision=lax.Precision.HIGHEST)
    bl = b[CHUNK - 1:CHUNK, :]
    return jnp.exp(b), jnp.exp(bl - b), jnp.exp(-bl), jnp.exp(bl)


def _gla_fwd(p7, log_a, gla_g, d):
    t = p7.shape[0]
    dk_all = d // 2
    dkh, dvh = dk_all // HEADS, d // HEADS
    cb = ROW_TILE
    ncb = cb // CHUNK
    scale = dkh ** -0.5

    def body(qk_ref, v_ref, r_ref, la_ref, g_ref, o_ref, y_ref, s_ref, st_scr):
        @pl.when(pl.program_id(0) == 0)
        def _():
            st_scr[...] = jnp.zeros_like(st_scr)

        tri = _tri(CHUNK)
        tri_f = tri.astype(F32)
        for c in range(ncb):
            rows = slice(c * CHUNK, (c + 1) * CHUNK)
            eb, ekl, ebl_inv, gam = _chunk_decays(la_ref[rows, :], tri_f)
            for h in range(HEADS):
                ks = slice(h * dkh, (h + 1) * dkh)
                vs = slice(h * dvh, (h + 1) * dvh)
                q = qk_ref[rows, ks] * scale
                k = qk_ref[rows, dk_all + h * dkh:dk_all + (h + 1) * dkh]
                v = v_ref[rows, vs].astype(BF16)
                qb = q * eb[:, ks]
                kh = (k * ekl[:, ks]).astype(BF16)
                qc = (qb * ebl_inv[:, ks]).astype(BF16)
                a = jnp.where(tri, _dot_nt(qc, kh), 0.0)
                st = st_scr[h]
                st_b = st.astype(BF16)
                s_ref[c, h] = st_b
                o = _dot_nt(qb.astype(BF16), st_b) + _dot(a.astype(BF16), v)
                st_scr[h] = st * gam[:, ks] + _dot_tn(v, kh)
                o_ref[rows, vs] = o
                rr = lax.rsqrt(jnp.mean(o * o, axis=-1, keepdims=True) + RMS_EPS)
                r = r_ref[rows, vs]
                y_ref[rows, vs] = (o * rr * g_ref[:, vs] * (r * _sigmoid(r))).astype(BF16)

    return pl.pallas_call(
        body, name="gla_fwd", grid=(t // cb,),
        in_specs=[pl.BlockSpec((cb, d), lambda i: (i, 0)),
                  pl.BlockSpec((cb, d), lambda i: (i, 1)),
                  pl.BlockSpec((cb, d), lambda i: (i, 2)),
                  pl.BlockSpec((cb, dk_all), lambda i: (i, 0)),
                  pl.BlockSpec((1, d), lambda i: (0, 0))],
        out_specs=[pl.BlockSpec((cb, d), lambda i: (i, 0)),
                   pl.BlockSpec((cb, d), lambda i: (i, 0)),
                   pl.BlockSpec((ncb, HEADS, dvh, dkh), lambda i: (i, 0, 0, 0))],
        out_shape=[jax.ShapeDtypeStruct((t, d), F32), jax.ShapeDtypeStruct((t, d), BF16),
                   jax.ShapeDtypeStruct((t // CHUNK, HEADS, dvh, dkh), BF16)],
        scratch_shapes=[pltpu.VMEM((HEADS, dvh, dkh), F32)],
        compiler_params=_cparams(("arbitrary",)),
    )(p7, p7, p7, log_a, gla_g)


def _conv_taps(ext_ref, w_ref, n_taps, first, rows, d, reverse=False):
    outs = []
    for s0 in range(0, d, LANES):
        ls = slice(s0, min(s0 + LANES, d))
        acc = None
        for j in range(n_taps):
            wj = w_ref[(n_taps - 1 - j) if reverse else j, ls]
            term = ext_ref[first + j:first + j + rows, ls] * wj
            acc = term if acc is None else acc + term
        outs.append(acc)
    return outs


def _conf_fwd(p7, conv_w, conv_b, ln_g, ln_b, d):
    t = p7.shape[0]
    tm = ROW_TILE

    def body(c1_ref, c2_ref, w_ref, b_ref, g_ref, bb_ref, cc_ref, sc_ref, ext):
        @pl.when(pl.program_id(0) == 0)
        def _():
            ext[0:HALO, :] = jnp.zeros((HALO, d), F32)

        ext[HALO:HALO + tm, :] = c1_ref[...] * _sigmoid(c2_ref[...])
        strips = _conv_taps(ext, w_ref, CONF_K, HALO - (CONF_K - 1), tm, d)
        for n, s0 in enumerate(range(0, d, LANES)):
            cc_ref[:, s0:s0 + LANES] = strips[n] + b_ref[:, s0:s0 + LANES]
        ext[0:HALO, :] = ext[tm:tm + HALO, :]
        cc = cc_ref[...]
        mu = jnp.mean(cc, axis=-1, keepdims=True)
        xc = cc - mu
        rstd = lax.rsqrt(jnp.mean(xc * xc, axis=-1, keepdims=True) + LN_EPS)
        cn = xc * rstd * g_ref[...] + bb_ref[...]
        sc_ref[...] = (cn * _sigmoid(cn)).astype(BF16)

    vec = pl.BlockSpec((1, d), lambda i: (0, 0))
    return pl.pallas_call(
        body, name="conf_fwd", grid=(t // tm,),
        in_specs=[pl.BlockSpec((tm, d), lambda i: (i, 3)), pl.BlockSpec((tm, d), lambda i: (i, 4)),
                  pl.BlockSpec((CONF_K, d), lambda i: (0, 0)), vec, vec, vec],
        out_specs=[pl.BlockSpec((tm, d), lambda i: (i, 0)), pl.BlockSpec((tm, d), lambda i: (i, 0))],
        out_shape=[jax.ShapeDtypeStruct((t, d), F32), jax.ShapeDtypeStruct((t, d), BF16)],
        scratch_shapes=[pltpu.VMEM((tm + HALO, d), F32)],
        compiler_params=_cparams(("arbitrary",)),
    )(p7, p7, conv_w, conv_b, ln_g, ln_b)


def _merge(p7, br_gla, br_conf, d):
    t = p7.shape[0]
    tm = ROW_TILE

    def body(gg_ref, gc_ref, a_ref, b_ref, o_ref):
        o_ref[...] = (_sigmoid(gg_ref[...]) * a_ref[...] + _sigmoid(gc_ref[...]) * b_ref[...]).astype(BF16)

    row = pl.BlockSpec((tm, d), lambda i: (i, 0))
    return pl.pallas_call(
        body, name="merge", grid=(t // tm,),
        in_specs=[pl.BlockSpec((tm, d), lambda i: (i, 5)), pl.BlockSpec((tm, d), lambda i: (i, 6)), row, row],
        out_specs=row, out_shape=jax.ShapeDtypeStruct((t, d), BF16),
        compiler_params=_cparams(("parallel",)),
    )(p7, p7, br_gla, br_conf)


def _rms_fwd(h, g, name):
    t, d = h.shape
    tm = ROW_TILE

    def body(h_ref, g_ref, u_ref):
        x = h_ref[...]
        r = lax.rsqrt(jnp.mean(x * x, axis=-1, keepdims=True) + RMS_EPS)
        u_ref[...] = (x * r * g_ref[...]).astype(BF16)

    return pl.pallas_call(
        body, name=name, grid=(t // tm,),
        in_specs=[pl.BlockSpec((tm, d), lambda i: (i, 0)), pl.BlockSpec((1, d), lambda i: (0, 0))],
        out_specs=pl.BlockSpec((tm, d), lambda i: (i, 0)), out_shape=jax.ShapeDtypeStruct((t, d), BF16),
        compiler_params=_cparams(("parallel",)),
    )(h, g)


def _ffn_mid(up, w, b, dff):
    t = up.shape[0]
    tm = ROW_TILE
    hal = 8

    def body(a_ref, bv_ref, w_ref, b_ref, y_ref, ext):
        i = pl.program_id(0)

        @pl.when(i == 0)
        def _():
            ext[0:hal, :] = jnp.zeros((hal, dff), F32)

        ext[hal:hal + tm, :] = a_ref[...] * _row_mask(i, tm)
        strips = _conv_taps(ext, w_ref, FFN_K, hal - (FFN_K - 1), tm, dff)
        for n, s0 in enumerate(range(0, dff, LANES)):
            ls = slice(s0, s0 + LANES)
            ac = strips[n] + b_ref[:, ls]
            y_ref[:, ls] = (ac * _sigmoid(ac) * bv_ref[:, ls]).astype(BF16)
        ext[0:hal, :] = ext[tm:tm + hal, :]

    return pl.pallas_call(
        body, name="ffn_mid", grid=(t // tm,),
        in_specs=[pl.BlockSpec((tm, dff), lambda i: (i, 0)), pl.BlockSpec((tm, dff), lambda i: (i, 1)),
                  pl.BlockSpec((FFN_K, dff), lambda i: (0, 0)), pl.BlockSpec((1, dff), lambda i: (0, 0))],
        out_specs=pl.BlockSpec((tm, dff), lambda i: (i, 0)), out_shape=jax.ShapeDtypeStruct((t, dff), BF16),
        scratch_shapes=[pltpu.VMEM((tm + hal, dff), F32)],
        compiler_params=_cparams(("arbitrary",)),
    )(up, up, w, b)


def _loss_head(h2, target, gf):
    t, d = h2.shape
    tm = ROW_TILE

    def body(h_ref, tg_ref, g_ref, loss_ref, dg_ref, dh_ref):
        i = pl.program_id(0)

        @pl.when(i == 0)
        def _():
            loss_ref[...] = jnp.zeros_like(loss_ref)
            dg_ref[...] = jnp.zeros_like(dg_ref)
            dh_ref[...] = jnp.zeros_like(dh_ref)

        @pl.when(i > 0)
        def _():
            h = h_ref[...]
            g = g_ref[...]
            r = lax.rsqrt(jnp.mean(h * h, axis=-1, keepdims=True) + RMS_EPS)
            n = h * r
            err = n * g - tg_ref[...]
            loss_ref[...] += 0.5 * jnp.sum(jnp.mean(err * err, axis=-1, keepdims=True), axis=0, keepdims=True)
            dout = err * (1.0 / d)
            dg_ref[...] += jnp.sum(dout * n, axis=0, keepdims=True)
            dn = dout * g
            dh_ref[...] = r * (dn - n * jnp.mean(dn * n, axis=-1, keepdims=True))

    return pl.pallas_call(
        body, name="loss_head", grid=(t // tm,),
        in_specs=[pl.BlockSpec((tm, d), lambda i: (i, 0)),
                  pl.BlockSpec((tm, d), lambda i: (jnp.maximum(i - 1, 0), 0)),
                  pl.BlockSpec((1, d), lambda i: (0, 0))],
        out_specs=[pl.BlockSpec((1, 1), lambda i: (0, 0)), pl.BlockSpec((1, d), lambda i: (0, 0)),
                   pl.BlockSpec((tm, d), lambda i: (i, 0))],
        out_shape=[jax.ShapeDtypeStruct((1, 1), F32), jax.ShapeDtypeStruct((1, d), F32),
                   jax.ShapeDtypeStruct((t, d), F32)],
        compiler_params=_cparams(("arbitrary",)),
    )(h2, target, gf)


def _dsilu(x, sig):
    return sig * (1.0 + x * (1.0 - sig))


def _ffn_mid_bwd(up, dy, w, b, dff):
    t = up.shape[0]
    tm = ROW_TILE
    hal = 8
    nt = t // tm

    def body(a_ref, ap_ref, bv_ref, dy_ref, w_ref, b_ref, dup_ref, dw_ref, db_ref, ext, dext):
        i = pl.program_id(0)
        tile = nt - 1 - i

        @pl.when(i == 0)
        def _():
            dext[tm:tm + hal, :] = jnp.zeros((hal, dff), F32)
            dw_ref[...] = jnp.zeros_like(dw_ref)
            db_ref[...] = jnp.zeros_like(db_ref)

        mask = _row_mask(tile, tm)
        ext[0:hal, :] = ap_ref[...] * jnp.where(tile > 0, 1.0, 0.0)
        ext[hal:hal + tm, :] = a_ref[...] * mask
        strips = _conv_taps(ext, w_ref, FFN_K, hal - (FFN_K - 1), tm, dff)
        for n, s0 in enumerate(range(0, dff, LANES)):
            ls = slice(s0, s0 + LANES)
            ac = strips[n] + b_ref[:, ls]
            sig = _sigmoid(ac)
            dyv = dy_ref[:, ls]
            dup_ref[:, dff + s0:dff + s0 + LANES] = (dyv * ac * sig).astype(BF16)
            dext[0:tm, ls] = dyv * bv_ref[:, ls] * _dsilu(ac, sig)
        dac = dext[0:tm, :]
        db_ref[...] += jnp.sum(dac, axis=0, keepdims=True)
        for j in range(FFN_K):
            sh = ext[hal - (FFN_K - 1) + j:hal - (FFN_K - 1) + j + tm, :]
            dw_ref[j:j + 1, :] += jnp.sum(dac * sh, axis=0, keepdims=True)
        dstrips = _conv_taps(dext, w_ref, FFN_K, 0, tm, dff, reverse=True)
        for n, s0 in enumerate(range(0, dff, LANES)):
            ls = slice(s0, s0 + LANES)
            dup_ref[:, ls] = (dstrips[n] * mask).astype(BF16)
        dext[tm:tm + hal, :] = dext[0:hal, :]

    rev = lambda i: (nt - 1 - i, 0)
    return pl.pallas_call(
        body, name="ffn_mid_bwd", grid=(nt,),
        in_specs=[pl.BlockSpec((tm, dff), rev),
                  pl.BlockSpec((hal, dff), lambda i: (jnp.maximum((nt - 1 - i) * (tm // hal) - 1, 0), 0)),
                  pl.BlockSpec((tm, dff), lambda i: (nt - 1 - i, 1)),
                  pl.BlockSpec((tm, dff), rev),
                  pl.BlockSpec((FFN_K, dff), lambda i: (0, 0)), pl.BlockSpec((1, dff), lambda i: (0, 0))],
        out_specs=[pl.BlockSpec((tm, 2 * dff), rev),
                   pl.BlockSpec((FFN_K, dff), lambda i: (0, 0)), pl.BlockSpec((1, dff), lambda i: (0, 0))],
        out_shape=[jax.ShapeDtypeStruct((t, 2 * dff), BF16),
                   jax.ShapeDtypeStruct((FFN_K, dff), F32), jax.ShapeDtypeStruct((1, dff), F32)],
        scratch_shapes=[pltpu.VMEM((tm + hal, dff), F32), pltpu.VMEM((tm + hal, dff), F32)],
        compiler_params=_cparams(("arbitrary",)),
    )(up, up, up, dy, w, b)


def _rms_bwd(du, h, g, dres, name):
    t, d = h.shape
    tm = ROW_TILE

    def body(du_ref, h_ref, g_ref, dres_ref, dh_ref, dg_ref):
        @pl.when(pl.program_id(0) == 0)
        def _():
            dg_ref[...] = jnp.zeros_like(dg_ref)

        x = h_ref[...]
        r = lax.rsqrt(jnp.mean(x * x, axis=-1, keepdims=True) + RMS_EPS)
        n = x * r
        du_ = du_ref[...]
        dg_ref[...] += jnp.sum(du_ * n, axis=0, keepdims=True)
        dn = du_ * g_ref[...]
        dh_ref[...] = dres_ref[...] + r * (dn - n * jnp.mean(dn * n, axis=-1, keepdims=True))

    row = pl.BlockSpec((tm, d), lambda i: (i, 0))
    vec = pl.BlockSpec((1, d), lambda i: (0, 0))
    return pl.pallas_call(
        body, name=name, grid=(t // tm,),
        in_specs=[row, row, vec, row], out_specs=[row, vec],
        out_shape=[jax.ShapeDtypeStruct((t, d), F32), jax.ShapeDtypeStruct((1, d), F32)],
        compiler_params=_cparams(("arbitrary",)),
    )(du, h, g, dres)


def _merge_bwd(p7, br_gla, br_conf, dmerged, d):
    t = p7.shape[0]
    tm = ROW_TILE

    def body(gg_ref, gc_ref, a_ref, b_ref, dm_ref, da_ref, db_ref, dgg_ref, dgc_ref):
        dm = dm_ref[...]
        sg = _sigmoid(gg_ref[...])
        sc = _sigmoid(gc_ref[...])
        da_ref[...] = (dm * sg).astype(BF16)
        db_ref[...] = (dm * sc).astype(BF16)
        dgg_ref[...] = (dm * a_ref[...] * sg * (1.0 - sg)).astype(BF16)
        dgc_ref[...] = (dm * b_ref[...] * sc * (1.0 - sc)).astype(BF16)

    row = pl.BlockSpec((tm, d), lambda i: (i, 0))
    outs = pl.pallas_call(
        body, name="merge_bwd", grid=(t // tm,),
        in_specs=[pl.BlockSpec((tm, d), lambda i: (i, 5)), pl.BlockSpec((tm, d), lambda i: (i, 6)), row, row, row],
        out_specs=[row, row, row, row],
        out_shape=[jax.ShapeDtypeStruct((t, d), BF16)] * 4,
        compiler_params=_cparams(("parallel",)),
    )(p7, p7, br_gla, br_conf, dmerged)
    return outs


def _conf_bwd(p7, cc, dsc, conv_w, ln_g, ln_b, d):
    t = p7.shape[0]
    tm = ROW_TILE
    nt = t // tm

    def body(c1_ref, c2_ref, c1p_ref, c2p_ref, cc_ref, dsc_ref, w_ref, g_ref, bb_ref,
             dc1_ref, dc2_ref, dw_ref, db_ref, dg_ref, dbb_ref, ext, dext):
        i = pl.program_id(0)
        tile = nt - 1 - i

        @pl.when(i == 0)
        def _():
            dext[tm:tm + HALO, :] = jnp.zeros((HALO, d), F32)
            dw_ref[...] = jnp.zeros_like(dw_ref)
            db_ref[...] = jnp.zeros_like(db_ref)
            dg_ref[...] = jnp.zeros_like(dg_ref)
            dbb_ref[...] = jnp.zeros_like(dbb_ref)

        c1 = c1_ref[...]
        s2 = _sigmoid(c2_ref[...])
        ext[HALO:HALO + tm, :] = c1 * s2
        ext[0:HALO, :] = c1p_ref[...] * _sigmoid(c2p_ref[...]) * jnp.where(tile > 0, 1.0, 0.0)
        cc_ = cc_ref[...]
        mu = jnp.mean(cc_, axis=-1, keepdims=True)
        xc = cc_ - mu
        rstd = lax.rsqrt(jnp.mean(xc * xc, axis=-1, keepdims=True) + LN_EPS)
        xh = xc * rstd
        g = g_ref[...]
        cn = xh * g + bb_ref[...]
        dcn = dsc_ref[...] * _dsilu(cn, _sigmoid(cn))
        dg_ref[...] += jnp.sum(dcn * xh, axis=0, keepdims=True)
        dbb_ref[...] += jnp.sum(dcn, axis=0, keepdims=True)
        dxh = dcn * g
        dcc = rstd * (dxh - jnp.mean(dxh, axis=-1, keepdims=True) - xh * jnp.mean(dxh * xh, axis=-1, keepdims=True))
        dext[0:tm, :] = dcc
        db_ref[...] += jnp.sum(dcc, axis=0, keepdims=True)
        first = HALO - (CONF_K - 1)
        for s0 in range(0, d, LANES):
            ls = slice(s0, s0 + LANES)
            dl = dext[0:tm, ls]
            for j in range(CONF_K):
                dw_ref[j:j + 1, ls] += jnp.sum(dl * ext[first + j:first + j + tm, ls], axis=0, keepdims=True)
        dstrips = _conv_taps(dext, w_ref, CONF_K, 0, tm, d, reverse=True)
        for n, s0 in enumerate(range(0, d, LANES)):
            ls = slice(s0, s0 + LANES)
            dc = dstrips[n]
            s2l = s2[:, ls]
            dc1_ref[:, ls] = (dc * s2l).astype(BF16)
            dc2_ref[:, ls] = (dc * c1[:, ls] * s2l * (1.0 - s2l)).astype(BF16)
        dext[tm:tm + HALO, :] = dext[0:HALO, :]

    rev = lambda i: (nt - 1 - i, 0)
    prev = lambda col: (lambda i: (jnp.maximum((nt - 1 - i) * (tm // HALO) - 1, 0), col))
    vec = pl.BlockSpec((1, d), lambda i: (0, 0))
    return pl.pallas_call(
        body, name="conf_bwd", grid=(nt,),
        in_specs=[pl.BlockSpec((tm, d), lambda i: (nt - 1 - i, 3)), pl.BlockSpec((tm, d), lambda i: (nt - 1 - i, 4)),
                  pl.BlockSpec((HALO, d), prev(3)), pl.BlockSpec((HALO, d), prev(4)),
                  pl.BlockSpec((tm, d), rev), pl.BlockSpec((tm, d), rev),
                  pl.BlockSpec((CONF_K, d), lambda i: (0, 0)), vec, vec],
        out_specs=[pl.BlockSpec((tm, d), rev), pl.BlockSpec((tm, d), rev),
                   pl.BlockSpec((CONF_K, d), lambda i: (0, 0)), vec, vec, vec],
        out_shape=[jax.ShapeDtypeStruct((t, d), BF16), jax.ShapeDtypeStruct((t, d), BF16),
                   jax.ShapeDtypeStruct((CONF_K, d), F32)] + [jax.ShapeDtypeStruct((1, d), F32)] * 3,
        scratch_shapes=[pltpu.VMEM((tm + HALO, d), F32), pltpu.VMEM((tm + HALO, d), F32)],
        compiler_params=_cparams(("arbitrary",)),
    )(p7, p7, p7, p7, cc, dsc, conv_w, ln_g, ln_b)


def _gla_bwd(p7, log_a, alr, wau, b_alpha, gla_g, o, states, dy, d):
    t = p7.shape[0]
    dk_all = d // 2
    dkh, dvh = dk_all // HEADS, d // HEADS
    cb = ROW_TILE
    ncb = cb // CHUNK
    nb = t // cb
    scale = dkh ** -0.5

    def body(qk_ref, v_ref, r_ref, la_ref, alr_ref, wau_ref, ba_ref, g_ref, o_ref, s_ref, dy_ref,
             dqk_ref, dv_ref, dr_ref, dz_ref, dg_ref, dba_ref, dst_scr, dla_scr):
        i = pl.program_id(0)
        blk = nb - 1 - i

        @pl.when(i == 0)
        def _():
            dst_scr[...] = jnp.zeros_like(dst_scr)
            dg_ref[...] = jnp.zeros_like(dg_ref)
            dba_ref[...] = jnp.zeros_like(dba_ref)

        tri = _tri(CHUNK)
        tri_f = tri.astype(F32)
        triu_f = _tri(CHUNK, upper=True).astype(F32)
        for c in reversed(range(ncb)):
            rows = slice(c * CHUNK, (c + 1) * CHUNK)
            eb, ekl, ebl_inv, gam = _chunk_decays(la_ref[rows, :], tri_f)
            for h in range(HEADS):
                ks = slice(h * dkh, (h + 1) * dkh)
                kcols = slice(dk_all + h * dkh, dk_all + (h + 1) * dkh)
                vs = slice(h * dvh, (h + 1) * dvh)
                q = qk_ref[rows, ks] * scale
                k = qk_ref[rows, kcols]
                v = v_ref[rows, vs].astype(BF16)
                ebh, eklh, eih, gamh = eb[:, ks], ekl[:, ks], ebl_inv[:, ks], gam[:, ks]
                qb = q * ebh
                kh = k * eklh
                qc = qb * eih
                qb_b, kh_b, qc_b = qb.astype(BF16), kh.astype(BF16), qc.astype(BF16)
                ov = o_ref[rows, vs]
                r = r_ref[rows, vs]
                dyv = dy_ref[rows, vs]
                sig = _sigmoid(r)
                rr = lax.rsqrt(jnp.mean(ov * ov, axis=-1, keepdims=True) + RMS_EPS)
                n = ov * rr
                g = g_ref[:, vs]
                dr_ref[rows, vs] = (dyv * n * g * _dsilu(r, sig)).astype(BF16)
                don = dyv * (r * sig)
                dg_ref[:, vs] += jnp.sum(don * n, axis=0, keepdims=True)
                dn = don * g
                do = (rr * (dn - n * jnp.mean(dn * n, axis=-1, keepdims=True))).astype(BF16)
                st_b = s_ref[c, h]
                a = jnp.where(tri, _dot_nt(qc_b, kh_b), 0.0).astype(BF16)
                da = jnp.where(tri, _dot_nt(do, v), 0.0).astype(BF16)
                dst = dst_scr[h]
                dst_b = dst.astype(BF16)
                dv_ref[rows, vs] = (_dot_tn(a, do) + _dot_nt(kh_b, dst_b)).astype(BF16)
                dqb = _dot(do, st_b)
                dqc = _dot(da, kh_b)
                dkh_ = _dot_tn(da, qc_b) + _dot(v, dst_b)
                dgam = jnp.sum(st_b.astype(F32) * dst, axis=0, keepdims=True)
                dst_scr[h] = dst * gamh + _dot_tn(do, qb_b)
                dqk_ref[rows, ks] = ((dqb * ebh + dqc * (ebh * eih)) * scale).astype(BF16)
                dqk_ref[rows, kcols] = (dkh_ * eklh).astype(BF16)
                qq = dqc * qc
                kk = dkh_ * kh
                db = dqb * qb + qq - kk
                dbl = jnp.sum(kk - qq, axis=0, keepdims=True) + dgam * gamh
                dla_scr[rows, ks] = jnp.dot(triu_f, db, preferred_element_type=F32,
                                            precision=lax.Precision.HIGHEST) + dbl
        z = jnp.dot(alr_ref[...].astype(BF16), wau_ref[...].astype(BF16), preferred_element_type=F32) + ba_ref[...]
        dz = dla_scr[...] * (1.0 / GATE_TAU) * _sigmoid(-z) * _row_mask(blk, cb)
        dba_ref[...] += jnp.sum(dz, axis=0, keepdims=True)
        dz_ref[...] = dz.astype(BF16)

    rev = lambda i: (nb - 1 - i, 0)
    row = pl.BlockSpec((cb, d), rev)
    return pl.pallas_call(
        body, name="gla_bwd", grid=(nb,),
        in_specs=[row, pl.BlockSpec((cb, d), lambda i: (nb - 1 - i, 1)), pl.BlockSpec((cb, d), lambda i: (nb - 1 - i, 2)),
                  pl.BlockSpec((cb, dk_all), rev), pl.BlockSpec((cb, RANK_PAD), rev),
                  pl.BlockSpec((RANK_PAD, dk_all), lambda i: (0, 0)), pl.BlockSpec((1, dk_all), lambda i: (0, 0)),
                  pl.BlockSpec((1, d), lambda i: (0, 0)), row,
                  pl.BlockSpec((ncb, HEADS, dvh, dkh), lambda i: (nb - 1 - i, 0, 0, 0)), row],
        out_specs=[row, row, row, pl.BlockSpec((cb, dk_all), rev),
                   pl.BlockSpec((1, d), lambda i: (0, 0)), pl.BlockSpec((1, dk_all), lambda i: (0, 0))],
        out_shape=[jax.ShapeDtypeStruct((t, d), BF16)] * 3 + [jax.ShapeDtypeStruct((t, dk_all), BF16),
                   jax.ShapeDtypeStruct((1, d), F32), jax.ShapeDtypeStruct((1, dk_all), F32)],
        scratch_shapes=[pltpu.VMEM((HEADS, dvh, dkh), F32), pltpu.VMEM((cb, dk_all), F32)],
        compiler_params=_cparams(("arbitrary",)),
    )(p7, p7, p7, log_a, alr, wau, b_alpha, gla_g, o, states, dy)


def _input_grad(du_a, du_b, h0, g1, dh1):
    t, d = h0.shape
    tm = ROW_TILE
    s = t - HEAD_ROWS

    def body(dua_ref, dub_ref, h_ref, g_ref, dres_ref, gx_ref, gm_ref, dg_ref):
        i = pl.program_id(0)

        @pl.when(i == 0)
        def _():
            dg_ref[...] = jnp.zeros_like(dg_ref)

        x = h_ref[...]
        r = lax.rsqrt(jnp.mean(x * x, axis=-1, keepdims=True) + RMS_EPS)
        n = x * r
        du_ = dua_ref[...] + dub_ref[...]
        dg_ref[...] += jnp.sum(du_ * n, axis=0, keepdims=True)
        dn = du_ * g_ref[...]
        dh = dres_ref[...] + r * (dn - n * jnp.mean(dn * n, axis=-1, keepdims=True))

        @pl.when(i == 0)
        def _():
            gm_ref[...] = dh[PAD:HEAD_ROWS, :]
            gx_ref[...] = jnp.zeros_like(gx_ref)

        @pl.when(i > 0)
        def _():
            gx_ref[...] = dh

    row = pl.BlockSpec((tm, d), lambda i: (i, 0))
    vec = pl.BlockSpec((1, d), lambda i: (0, 0))
    return pl.pallas_call(
        body, name="input_grad", grid=(t // tm,),
        in_specs=[row, row, row, vec, row],
        out_specs=[pl.BlockSpec((tm, d), lambda i: (jnp.maximum(i - 1, 0), 0)),
                   pl.BlockSpec((N_META, d), lambda i: (0, 0)), vec],
        out_shape=[jax.ShapeDtypeStruct((s, d), F32), jax.ShapeDtypeStruct((N_META, d), F32),
                   jax.ShapeDtypeStruct((1, d), F32)],
        compiler_params=_cparams(("arbitrary",)),
    )(du_a, du_b, h0, g1, dh1)


def _local_step(x, target, w):
    s, d = x.shape
    dk_all = d // 2
    dff = w["w_down"].shape[0]
    w_in = w["w_in"]
    lo, hi = 3 * d, 3 * d + GLA_RANK
    wq = jnp.concatenate([w_in[:, :lo], w_in[:, hi:]], axis=1)
    w_alr = jnp.pad(w_in[:, lo:hi], ((0, 0), (0, RANK_PAD - GLA_RANK)))
    wau = jnp.pad(w["w_alpha_up"], ((0, RANK_PAD - GLA_RANK), (0, 0)))

    h0, u1 = _prep(x, w["meta_tokens"], w["norm_mix_g"])
    p7 = _matmul(u1, wq, dims="nn", name="proj")
    alr = _matmul(u1, w_alr, dims="nn", name="proj_alr")
    log_a = _log_gate(alr, wau, w["b_alpha"])
    o, y_gla, states = _gla_fwd(p7, log_a, w["gla_norm_g"], d)
    br_gla = _matmul(y_gla, w["w_gla_o"], dims="nn", name="gla_out")
    cc, s_c = _conf_fwd(p7, w["conf_dw_w"], w["conf_dw_b"], w["conf_ln_g"], w["conf_ln_b"], d)
    br_conf = _matmul(s_c, w["w_conf_o"], dims="nn", name="conf_out")
    merged = _merge(p7, br_gla, br_conf, d)
    h1 = _matmul(merged, w["w_out"], dims="nn", name="mix_out", add=h0)
    u2 = _rms_fwd(h1, w["norm_ffn_g"], "norm_ffn")
    up = _matmul(u2, w["w_up"], dims="nn", name="ffn_up")
    y = _ffn_mid(up, w["ffn_dw_w"], w["ffn_dw_b"], dff)
    h2 = _matmul(y, w["w_down"], dims="nn", name="ffn_down", add=h1)
    loss, d_gf, dh2 = _loss_head(h2, target, w["final_norm_g"])

    g = {"final_norm_g": d_gf}
    dy = _matmul(dh2, w["w_down"], dims="nt", name="d_ffn_y")
    g["w_down"] = _matmul(y, dh2, dims="tn", name="dw_down")
    dup, g["ffn_dw_w"], g["ffn_dw_b"] = _ffn_mid_bwd(up, dy, w["ffn_dw_w"], w["ffn_dw_b"], dff)
    du2 = _matmul(dup, w["w_up"], dims="nt", name="d_u2")
    g["w_up"] = _matmul(u2, dup, dims="tn", name="dw_up")
    dh1, g["norm_ffn_g"] = _rms_bwd(du2, h1, w["norm_ffn_g"], dh2, "norm_ffn_bwd")
    dmerged = _matmul(dh1, w["w_out"], dims="nt", name="d_merged")
    g["w_out"] = _matmul(merged, dh1, dims="tn", name="dw_out")
    d_br_gla, d_br_conf, dgg, dgc = _merge_bwd(p7, br_gla, br_conf, dmerged, d)
    dsc = _matmul(d_br_conf, w["w_conf_o"], dims="nt", name="d_conf_s")
    g["w_conf_o"] = _matmul(s_c, d_br_conf, dims="tn", name="dw_conf_o")
    dc1, dc2, g["conf_dw_w"], g["conf_dw_b"], g["conf_ln_g"], g["conf_ln_b"] = _conf_bwd(
        p7, cc, dsc, w["conf_dw_w"], w["conf_ln_g"], w["conf_ln_b"], d)
    dyg = _matmul(d_br_gla, w["w_gla_o"], dims="nt", name="d_gla_y")
    g["w_gla_o"] = _matmul(y_gla, d_br_gla, dims="tn", name="dw_gla_o")
    dqk, dv, dr, dz, g["gla_norm_g"], g["b_alpha"] = _gla_bwd(
        p7, log_a, alr, wau, w["b_alpha"], w["gla_norm_g"], o, states, dyg, d)
    dalr = _matmul(dz, wau, dims="nt", name="d_alr", out_dtype=BF16)
    g["w_alpha_up"] = _matmul(alr, dz, dims="tn", name="dw_alpha_up")[:GLA_RANK]
    dp7 = jnp.concatenate([dqk, dv, dr, dc1, dc2, dgg, dgc], axis=1)
    du1a = _matmul(dp7, wq, dims="nt", name="d_u1")
    du1b = _matmul(dalr, w_alr, dims="nt", name="d_u1_alr")
    dwq = _matmul(u1, dp7, dims="tn", name="dw_in")
    dw_alr = _matmul(u1, dalr, dims="tn", name="dw_in_alr")
    g["w_in"] = jnp.concatenate([dwq[:, :lo], dw_alr[:, :GLA_RANK], dwq[:, lo:]], axis=1)
    grad_x, g["meta_tokens"], g["norm_mix_g"] = _input_grad(du1a, du1b, h0, w["norm_mix_g"], dh1)
    return loss, grad_x, g


HBM_SPEC = pl.BlockSpec(memory_space=pltpu.HBM)
FLIPS = ((1, 0), (0, 1), (1, 1))


def _place():
    x, y, c = lax.axis_index("x"), lax.axis_index("y"), lax.axis_index("c")
    return x, y, c


def _gather_weights(big, small):
    _, r, _ = big.shape
    rs = small.shape[0]

    def body(big_ref, small_ref, ball_ref, sall_ref, send_sems, recv_sems, s_send, s_recv, local_sems):
        x, y, c = _place()
        me = 2 * x + y
        sib = (x, y, 1 - c)
        chips = [(x ^ fx, y ^ fy) for fx, fy in FLIPS]
        own_b = pltpu.make_async_copy(big_ref, ball_ref.at[me], local_sems.at[0])
        own_s = pltpu.make_async_copy(small_ref, sall_ref.at[me], local_sems.at[1])
        own_b.start()
        own_s.start()

        def half(chip, hc, k, to, src=None):
            slot = ball_ref.at[2 * chip[0] + chip[1], hc]
            return pltpu.make_async_remote_copy(src_ref=slot if src is None else src, dst_ref=slot,
                                                send_sem=send_sems.at[k], recv_sem=recv_sems.at[k],
                                                device_id=to, device_id_type=MESH)

        def smallcp(chip, k, to, src=None):
            slot = sall_ref.at[2 * chip[0] + chip[1]]
            return pltpu.make_async_remote_copy(src_ref=slot if src is None else src, dst_ref=slot,
                                                send_sem=s_send.at[k], recv_sem=s_recv.at[k],
                                                device_id=to, device_id_type=MESH)

        first = [half((x, y), c, j, (*chip, c), src=big_ref.at[c]) for j, chip in enumerate(chips)]
        first += [smallcp((x, y), j, (*chip, c), src=small_ref) for j, chip in enumerate(chips)]
        for cp in first:
            cp.start()
        passed = [half(chip, c, 3 + j, sib) for j, chip in enumerate(chips)]
        for j, chip in enumerate(chips):
            half(chip, c, j, (x, y, c)).wait_recv()
            passed[j].start()
        for j, chip in enumerate(chips):
            half(chip, 1 - c, 3 + j, (x, y, c)).wait_recv()
            smallcp(chip, j, (x, y, c)).wait_recv()
        for cp in first + passed:
            cp.wait_send()
        own_b.wait()
        own_s.wait()

    return pl.pallas_call(
        body, name="gather_weights",
        in_specs=[HBM_SPEC, HBM_SPEC], out_specs=[HBM_SPEC, HBM_SPEC],
        out_shape=[jax.ShapeDtypeStruct((4, 2, r, LANES), big.dtype), jax.ShapeDtypeStruct((4, rs, LANES), small.dtype)],
        scratch_shapes=[pltpu.SemaphoreType.DMA((6,)), pltpu.SemaphoreType.DMA((6,)),
                        pltpu.SemaphoreType.DMA((3,)), pltpu.SemaphoreType.DMA((3,)), pltpu.SemaphoreType.DMA((2,))],
    )(big, small)


def _swap_with_sibling(send, name):
    def body(src_ref, dst_ref, send_sem, recv_sem):
        x, y, c = _place()
        cp = pltpu.make_async_remote_copy(src_ref=src_ref, dst_ref=dst_ref, send_sem=send_sem, recv_sem=recv_sem,
                                          device_id=(x, y, 1 - c), device_id_type=MESH)
        cp.start()
        cp.wait()

    return pl.pallas_call(
        body, name=name, in_specs=[HBM_SPEC], out_specs=HBM_SPEC,
        out_shape=jax.ShapeDtypeStruct(send.shape, send.dtype),
        scratch_shapes=[pltpu.SemaphoreType.DMA(()), pltpu.SemaphoreType.DMA(())],
    )(send)


def _exchange_chip_partials(cp_all):
    _, r, _ = cp_all.shape

    def body(src_ref, dst_ref, send_sems, recv_sems):
        x, y, c = _place()
        cps = []
        for j, (fx, fy) in enumerate(FLIPS):
            tx, ty = x ^ fx, y ^ fy
            cps.append(pltpu.make_async_remote_copy(
                src_ref=src_ref.at[2 * tx + ty], dst_ref=dst_ref.at[j], send_sem=send_sems.at[j],
                recv_sem=recv_sems.at[j], device_id=(tx, ty, c), device_id_type=MESH))
        for cp in cps:
            cp.start()
        for cp in cps:
            cp.wait()

    return pl.pallas_call(
        body, name="exchange_chip_partials", in_specs=[HBM_SPEC], out_specs=HBM_SPEC,
        out_shape=jax.ShapeDtypeStruct((3, r, LANES), cp_all.dtype),
        scratch_shapes=[pltpu.SemaphoreType.DMA((3,)), pltpu.SemaphoreType.DMA((3,))],
    )(cp_all)


def _gather_small(block):
    m, n = block.shape

    def body(x_ref, out_ref, send_sems, recv_sems, local_sem):
        x, y, c = _place()
        me, sibling = (x, y, c), (x, y, 1 - c)
        chips = [(x ^ fx, y ^ fy) for fx, fy in FLIPS]

        def rows(px, py, pc):
            return out_ref.at[pl.ds((4 * px + 2 * py + pc) * m, m), :]

        def copy(k, blk, to, src=None):
            return pltpu.make_async_remote_copy(
                src_ref=rows(*blk) if src is None else src, dst_ref=rows(*blk),
                send_sem=send_sems.at[k], recv_sem=recv_sems.at[k], device_id=to, device_id_type=MESH)

        mine = pltpu.make_async_copy(x_ref, rows(*me), local_sem)
        mine.start()
        first = [copy(0, me, sibling, src=x_ref)]
        first += [copy(1 + j, me, (*chip, c), src=x_ref) for j, chip in enumerate(chips)]
        for cp in first:
            cp.start()
        passed = [copy(4 + j, (*chip, c), sibling) for j, chip in enumerate(chips)]
        for j, chip in enumerate(chips):
            copy(1 + j, (*chip, c), me).wait_recv()
            passed[j].start()
        copy(0, sibling, me).wait_recv()
        for j, chip in enumerate(chips):
            copy(4 + j, (*chip, 1 - c), me).wait_recv()
        for cp in first + passed:
            cp.wait_send()
        mine.wait()

    out = pl.pallas_call(
        body, name="gather_small",
        out_shape=jax.ShapeDtypeStruct((8 * m, n), block.dtype),
        in_specs=[pl.BlockSpec(memory_space=pltpu.VMEM)],
        out_specs=pl.BlockSpec(memory_space=pltpu.VMEM),
        scratch_shapes=[pltpu.SemaphoreType.DMA((7,)), pltpu.SemaphoreType.DMA((7,)), pltpu.SemaphoreType.DMA],
    )(block)
    return out.reshape(8, m, n)


def _rows_tile(r):
    for n in (1, 2, 3, 4, 6, 8, 12, 16, 24, 32, 48, 64):
        if r % n == 0 and (r // n) % 16 == 0 and (r // n) * LANES * 4 <= 4 * 1024 * 1024:
            return r // n
    return r


def _add_pairs(a, b, name, out_dtype):
    n, r, _ = a.shape
    tr = _rows_tile(r)

    def body(a_ref, b_ref, o_ref):
        o_ref[...] = (a_ref[...].astype(F32) + b_ref[...].astype(F32)).astype(out_dtype)

    spec = pl.BlockSpec((1, tr, LANES), lambda i, j: (i, j, 0))
    return pl.pallas_call(
        body, name=name, grid=(n, r // tr), in_specs=[spec, spec], out_specs=spec,
        out_shape=jax.ShapeDtypeStruct((n, r, LANES), out_dtype),
        compiler_params=_cparams(("parallel", "parallel")),
    )(a, b)


def _sum_partials(own, others):
    r = own.shape[0]
    tr = _rows_tile(r)

    def body(a_ref, b_ref, o_ref):
        acc = a_ref[...].astype(F32)
        for j in range(3):
            acc = acc + b_ref[j].astype(F32)
        o_ref[...] = acc

    return pl.pallas_call(
        body, name="sum_partials", grid=(r // tr,),
        in_specs=[pl.BlockSpec((tr, LANES), lambda i: (i, 0)), pl.BlockSpec((3, tr, LANES), lambda i: (0, i, 0))],
        out_specs=pl.BlockSpec((tr, LANES), lambda i: (i, 0)),
        out_shape=jax.ShapeDtypeStruct((r, LANES), F32),
        compiler_params=_cparams(("parallel",)),
    )(own, others)


def _sum_devices(blocks):
    n, m, _ = blocks.shape

    def body(b_ref, o_ref):
        acc = b_ref[0]
        for j in range(1, n):
            acc = acc + b_ref[j]
        o_ref[...] = acc

    return pl.pallas_call(
        body, name="sum_devices", out_shape=jax.ShapeDtypeStruct((m, LANES), F32),
        in_specs=[pl.BlockSpec(memory_space=pltpu.VMEM)], out_specs=pl.BlockSpec(memory_space=pltpu.VMEM),
    )(blocks)


def _adamw(w, g, m, v, name):
    rws, cols = w.shape
    tr = rws
    for cand in (256, 128, 64, 32, 16, 8):
        if rws % cand == 0 and cand * cols * 4 <= 2 * 1024 * 1024:
            tr = cand
            break
    c1 = 1.0 - ADAM_B1 ** ADAM_STEP
    c2 = 1.0 - ADAM_B2 ** ADAM_STEP

    def body(w_ref, g_ref, m_ref, v_ref, d_ref, nm_ref, nv_ref):
        gv = g_ref[...]
        nm = ADAM_B1 * m_ref[...] + (1.0 - ADAM_B1) * gv
        nv = ADAM_B2 * v_ref[...] + (1.0 - ADAM_B2) * (gv * gv)
        m_hat = nm / c1
        v_hat = nv / c2
        d_ref[...] = -ADAM_LR * (m_hat / (jnp.sqrt(v_hat) + ADAM_EPS) + ADAM_WD * w_ref[...])
        nm_ref[...] = nm
        nv_ref[...] = nv

    spec = pl.BlockSpec((tr, cols), lambda i: (i, 0))
    return pl.pallas_call(
        body, name=name, grid=(rws // tr,), in_specs=[spec] * 4, out_specs=[spec] * 3,
        out_shape=[jax.ShapeDtypeStruct((rws, cols), F32)] * 3,
        compiler_params=_cparams(("parallel",)),
    )(w, g, m, v)


WEIGHTS = (
    ("meta_tokens", (16, 1024), 1), ("norm_mix_g", (1024,), None), ("w_in", (1024, 7184), 1),
    ("w_alpha_up", (16, 512), 1), ("b_alpha", (512,), None), ("gla_norm_g", (1024,), None),
    ("w_gla_o", (1024, 1024), 0), ("conf_dw_w", (31, 1024), 1), ("conf_dw_b", (1024,), None),
    ("conf_ln_g", (1024,), None), ("conf_ln_b", (1024,), None), ("w_conf_o", (1024, 1024), 0),
    ("w_out", (1024, 1024), 0), ("norm_ffn_g", (1024,), None), ("w_up", (1024, 5632), 1),
    ("ffn_dw_w", (3, 2816), 1), ("ffn_dw_b", (2816,), None), ("w_down", (2816, 1024), 0),
    ("final_norm_g", (1024,), None),
)
BIG = ("w_in", "w_up", "w_down", "w_gla_o", "w_conf_o", "w_out")
SMALL_SHARDED = ("meta_tokens", "w_alpha_up", "conf_dw_w", "ffn_dw_w")
REPLICATED = tuple(n for n, _, ax in WEIGHTS if ax is None)
SHAPES = {n: s for n, s, _ in WEIGHTS}
AXIS = {n: ax for n, _, ax in WEIGHTS}
N_CHIPS = 4


def _shard_shape(name):
    s = list(SHAPES[name])
    s[AXIS[name]] //= N_CHIPS
    return tuple(s)


def _pack(parts, mult):
    flat = jnp.concatenate([p.reshape(-1) for p in parts])
    pad = (-flat.shape[0]) % mult
    return jnp.pad(flat, (0, pad))


def _unpack(flat, names, shape_of):
    out, off = {}, 0
    for n in names:
        shp = shape_of(n)
        size = math.prod(shp)
        out[n] = flat[off:off + size].reshape(shp)
        off += size
    return out


def _full_from_shards(per_chip, name):
    return jnp.concatenate(per_chip, axis=AXIS[name])


def _shards_of(full, name):
    return jnp.split(full, N_CHIPS, axis=AXIS[name])


def kernel(x, meta_tokens, norm_mix_g, w_in, w_alpha_up, b_alpha, gla_norm_g, w_gla_o, conf_dw_w, conf_dw_b, conf_ln_g, conf_ln_b, w_conf_o, w_out, norm_ffn_g, w_up, ffn_dw_w, ffn_dw_b, w_down, final_norm_g, loss_target, m_meta_tokens, m_norm_mix_g, m_w_in, m_w_alpha_up, m_b_alpha, m_gla_norm_g, m_w_gla_o, m_conf_dw_w, m_conf_dw_b, m_conf_ln_g, m_conf_ln_b, m_w_conf_o, m_w_out, m_norm_ffn_g, m_w_up, m_ffn_dw_w, m_ffn_dw_b, m_w_down, m_final_norm_g, v_meta_tokens, v_norm_mix_g, v_w_in, v_w_alpha_up, v_b_alpha, v_gla_norm_g, v_w_gla_o, v_conf_dw_w, v_conf_dw_b, v_conf_ln_g, v_conf_ln_b, v_w_conf_o, v_w_out, v_norm_ffn_g, v_w_up, v_ffn_dw_w, v_ffn_dw_b, v_w_down, v_final_norm_g):
    names = [n for n, _, _ in WEIGHTS]
    w_args = (meta_tokens, norm_mix_g, w_in, w_alpha_up, b_alpha, gla_norm_g, w_gla_o, conf_dw_w, conf_dw_b, conf_ln_g,
              conf_ln_b, w_conf_o, w_out, norm_ffn_g, w_up, ffn_dw_w, ffn_dw_b, w_down, final_norm_g)
    m_args = (m_meta_tokens, m_norm_mix_g, m_w_in, m_w_alpha_up, m_b_alpha, m_gla_norm_g, m_w_gla_o, m_conf_dw_w,
              m_conf_dw_b, m_conf_ln_g, m_conf_ln_b, m_w_conf_o, m_w_out, m_norm_ffn_g, m_w_up, m_ffn_dw_w, m_ffn_dw_b,
              m_w_down, m_final_norm_g)
    v_args = (v_meta_tokens, v_norm_mix_g, v_w_in, v_w_alpha_up, v_b_alpha, v_gla_norm_g, v_w_gla_o, v_conf_dw_w,
              v_conf_dw_b, v_conf_ln_g, v_conf_ln_b, v_w_conf_o, v_w_out, v_norm_ffn_g, v_w_up, v_ffn_dw_w, v_ffn_dw_b,
              v_w_down, v_final_norm_g)
    in_shape = {n: a.shape for n, a in zip(names, w_args)}
    local = {n: a.reshape(_shard_shape(n) if AXIS[n] is not None else SHAPES[n]) for n, a in zip(names, w_args)}
    m_loc = {n: a.reshape(local[n].shape) for n, a in zip(names, m_args)}
    v_loc = {n: a.reshape(local[n].shape) for n, a in zip(names, v_args)}

    big = _pack([local[n].astype(BF16) for n in BIG], 2 * 16 * LANES).reshape(2, -1, LANES)
    small = _pack([local[n] for n in SMALL_SHARDED], 8 * LANES).reshape(-1, LANES)
    big_all, small_all = _gather_weights(big, small)
    big_all = big_all.reshape(N_CHIPS, -1)
    small_all = small_all.reshape(N_CHIPS, -1)
    per_chip_big = [_unpack(big_all[t], BIG, _shard_shape) for t in range(N_CHIPS)]
    per_chip_small = [_unpack(small_all[t], SMALL_SHARDED, _shard_shape) for t in range(N_CHIPS)]
    full = {}
    for n in BIG:
        full[n] = _full_from_shards([per_chip_big[t][n] for t in range(N_CHIPS)], n)
    for n in SMALL_SHARDED:
        full[n] = _full_from_shards([per_chip_small[t][n] for t in range(N_CHIPS)], n)
    for n in REPLICATED:
        full[n] = local[n].reshape(1, -1)

    loss_part, grad_x, grads = _local_step(x[0], loss_target[0], full)

    rep = _pack([grads[n] for n in REPLICATED] + [loss_part], 8 * LANES).reshape(-1, LANES)
    rep_sum = _sum_devices(_gather_small(rep)).reshape(-1)
    rep_grads = _unpack(rep_sum, REPLICATED, lambda n: SHAPES[n])
    loss = rep_sum[sum(math.prod(SHAPES[n]) for n in REPLICATED)]

    sharded = BIG + SMALL_SHARDED
    c = lax.axis_index("c")
    me = 2 * lax.axis_index("x") + lax.axis_index("y")
    by_chip = [_pack([_shards_of(grads[n], n)[t].astype(BF16) for n in sharded], 2 * 16 * LANES).reshape(2, -1, LANES)
               for t in range(N_CHIPS)]
    gp = jnp.stack(by_chip, axis=1)
    mine_half = lax.dynamic_index_in_dim(gp, c, 0, keepdims=False)
    other_half = lax.dynamic_index_in_dim(gp, 1 - c, 0, keepdims=False)
    from_sibling = _swap_with_sibling(other_half, "swap_core_halves")
    chip_part = _add_pairs(mine_half, from_sibling, "add_core_halves", BF16)
    from_chips = _exchange_chip_partials(chip_part)
    own = lax.dynamic_index_in_dim(chip_part, me, 0, keepdims=False)
    red_half = _sum_partials(own, from_chips)
    sib_half = _swap_with_sibling(red_half, "swap_reduced_halves")
    lo_half = jnp.where(c == 0, red_half, sib_half)
    hi_half = jnp.where(c == 0, sib_half, red_half)
    red = jnp.concatenate([lo_half.reshape(-1), hi_half.reshape(-1)])
    shard_grads = _unpack(red, sharded, _shard_shape)

    g_loc = {**shard_grads, **rep_grads}
    delta, new_m, new_v = {}, {}, {}
    for n in ("w_in", "w_up", "w_down"):
        delta[n], new_m[n], new_v[n] = _adamw(local[n], g_loc[n], m_loc[n], v_loc[n], "adamw_" + n)
    sq = ("w_gla_o", "w_conf_o", "w_out")
    stk = lambda dct: jnp.concatenate([dct[n] for n in sq], axis=0)
    d3, m3, v3 = _adamw(stk(local), stk(g_loc), stk(m_loc), stk(v_loc), "adamw_square")
    rows3 = local["w_out"].shape[0]
    for j, n in enumerate(sq):
        delta[n], new_m[n], new_v[n] = (a[j * rows3:(j + 1) * rows3] for a in (d3, m3, v3))
    rest = SMALL_SHARDED + REPLICATED
    pk = lambda dct: _pack([dct[n] for n in rest], 8 * LANES).reshape(-1, LANES)
    ds, ms, vs = _adamw(pk(local), pk(g_loc), pk(m_loc), pk(v_loc), "adamw_small")
    shape_loc = lambda n: local[n].shape
    for dct, flat in ((delta, ds), (new_m, ms), (new_v, vs)):
        dct.update(_unpack(flat.reshape(-1), rest, shape_loc))

    outs = [loss, grad_x[None]]
    for dct in (g_loc, delta, new_m, new_v):
        outs += [dct[n].reshape(in_shape[n]) for n in names]
    return tuple(outs)
```

```python
import functools
import math

import jax
import jax.numpy as jnp
from jax import lax
from jax.experimental import pallas as pl
from jax.experimental.pallas import tpu as pltpu

F32 = jnp.float32
BF16 = jnp.bfloat16

N_META = 16
PAD = 240
HEAD_ROWS = PAD + N_META
HEADS = 4
GLA_RANK = 16
RANK_PAD = 128
GATE_TAU = 16.0
CHUNK = 64
CONF_K = 31
FFN_K = 3
RMS_EPS = 1e-6
LN_EPS = 1e-5
ADAM_LR, ADAM_B1, ADAM_B2, ADAM_EPS, ADAM_WD, ADAM_STEP = 0.001, 0.9, 0.999, 1e-08, 0.01, 10

ROW_TILE = 256
HALO = 32
LANES = 128
V7X_VMEM_LIMIT = 56 * 1024 * 1024
MESH = pl.DeviceIdType.MESH


def _cparams(sem):
    return pltpu.CompilerParams(dimension_semantics=sem, vmem_limit_bytes=V7X_VMEM_LIMIT)


def _sigmoid(x):
    return 1.0 / (1.0 + jnp.exp(-x))


def _pick(n, prefs):
    for p in prefs:
        if n % p == 0:
            return p
    return n


def _matmul(a, b, *, dims, name, tm=None, tn=None, tk=None, out_dtype=F32, add=None):
    if dims == "nn":
        (m, k), (_, n) = a.shape, b.shape
    elif dims == "nt":
        (m, k), (n, _) = a.shape, b.shape
    else:
        (k, m), (_, n) = a.shape, b.shape
    tm = tm or _pick(m, (768, 1024, 1408, 512, 256, 128))
    tn = tn or _pick(n, (1024, 1408, 512, 256, 128))
    tk = tk or _pick(k, (1024, 768, 1408, 512, 256, 128))
    nk = k // tk
    assert m % tm == 0 and n % tn == 0 and k % tk == 0, (name, m, n, k, tm, tn, tk)
    a_spec = {"nn": pl.BlockSpec((tm, tk), lambda i, j, kk: (i, kk)),
              "nt": pl.BlockSpec((tm, tk), lambda i, j, kk: (i, kk)),
              "tn": pl.BlockSpec((tk, tm), lambda i, j, kk: (kk, i))}[dims]
    b_spec = {"nn": pl.BlockSpec((tk, tn), lambda i, j, kk: (kk, j)),
              "nt": pl.BlockSpec((tn, tk), lambda i, j, kk: (j, kk)),
              "tn": pl.BlockSpec((tk, tn), lambda i, j, kk: (kk, j))}[dims]
    contract = {"nn": (((1,), (0,)), ((), ())), "nt": (((1,), (1,)), ((), ())), "tn": (((0,), (0,)), ((), ()))}[dims]
    o_spec = pl.BlockSpec((tm, tn), lambda i, j, kk: (i, j))
    has_add = add is not None

    def body(*refs):
        if has_add:
            a_ref, b_ref, add_ref, o_ref, acc_ref = refs
        else:
            a_ref, b_ref, o_ref, acc_ref = refs
            add_ref = None
        kk = pl.program_id(2)
        prod = lax.dot_general(a_ref[...].astype(BF16), b_ref[...].astype(BF16), contract,
                               preferred_element_type=F32)

        @pl.when(kk == 0)
        def _():
            acc_ref[...] = prod + add_ref[...].astype(F32) if has_add else prod

        @pl.when(kk > 0)
        def _():
            acc_ref[...] += prod

        @pl.when(kk == nk - 1)
        def _():
            o_ref[...] = acc_ref[...].astype(out_dtype)

    in_specs = [a_spec, b_spec] + ([o_spec] if has_add else [])
    args = (a, b) + ((add,) if has_add else ())
    return pl.pallas_call(
        body, name=name, grid=(m // tm, n // tn, nk),
        in_specs=in_specs, out_specs=o_spec,
        out_shape=jax.ShapeDtypeStruct((m, n), out_dtype),
        scratch_shapes=[pltpu.VMEM((tm, tn), F32)],
        compiler_params=_cparams(("parallel", "parallel", "arbitrary")),
    )(*args)


def _row_mask(tile_index, rows):
    r = tile_index * rows + lax.broadcasted_iota(jnp.int32, (rows, 1), 0)
    return (r >= PAD).astype(F32)


def _prep(x, meta, g1):
    s, d = x.shape
    t = HEAD_ROWS + s
    tm = ROW_TILE

    def body(x_ref, meta_ref, g_ref, h_ref, u_ref):
        i = pl.program_id(0)

        @pl.when(i == 0)
        def _():
            h_ref[0:PAD, :] = jnp.zeros((PAD, d), F32)
            h_ref[PAD:HEAD_ROWS, :] = meta_ref[...]

        @pl.when(i > 0)
        def _():
            h_ref[...] = x_ref[...]

        h = h_ref[...]
        r = lax.rsqrt(jnp.mean(h * h, axis=-1, keepdims=True) + RMS_EPS)
        u_ref[...] = (h * r * g_ref[...]).astype(BF16)

    return pl.pallas_call(
        body, name="prep", grid=(t // tm,),
        in_specs=[pl.BlockSpec((tm, d), lambda i: (jnp.maximum(i - 1, 0), 0)),
                  pl.BlockSpec((N_META, d), lambda i: (0, 0)),
                  pl.BlockSpec((1, d), lambda i: (0, 0))],
        out_specs=[pl.BlockSpec((tm, d), lambda i: (i, 0)), pl.BlockSpec((tm, d), lambda i: (i, 0))],
        out_shape=[jax.ShapeDtypeStruct((t, d), F32), jax.ShapeDtypeStruct((t, d), BF16)],
        compiler_params=_cparams(("parallel",)),
    )(x, meta, g1)


def _log_gate(alr, wau, b_alpha):
    t = alr.shape[0]
    dk = wau.shape[1]
    tm = ROW_TILE

    def body(alr_ref, w_ref, b_ref, o_ref):
        z = jnp.dot(alr_ref[...].astype(BF16), w_ref[...].astype(BF16), preferred_element_type=F32) + b_ref[...]
        ls = jnp.minimum(z, 0.0) - jnp.log(1.0 + jnp.exp(-jnp.abs(z)))
        o_ref[...] = ls * (1.0 / GATE_TAU) * _row_mask(pl.program_id(0), tm)

    return pl.pallas_call(
        body, name="log_gate", grid=(t // tm,),
        in_specs=[pl.BlockSpec((tm, RANK_PAD), lambda i: (i, 0)),
                  pl.BlockSpec((RANK_PAD, dk), lambda i: (0, 0)),
                  pl.BlockSpec((1, dk), lambda i: (0, 0))],
        out_specs=pl.BlockSpec((tm, dk), lambda i: (i, 0)),
        out_shape=jax.ShapeDtypeStruct((t, dk), F32),
        compiler_params=_cparams(("parallel",)),
    )(alr, wau, b_alpha)


def _tri(n, upper=False):
    r = lax.broadcasted_iota(jnp.int32, (n, n), 0)
    c = lax.broadcasted_iota(jnp.int32, (n, n), 1)
    return (r <= c) if upper else (r >= c)


_NT = (((1,), (1,)), ((), ()))
_TN = (((0,), (0,)), ((), ()))


def _dot(a, b):
    return jnp.dot(a, b, preferred_element_type=F32)


def _dot_nt(a, b):
    return lax.dot_general(a, b, _NT, preferred_element_type=F32)


def _dot_tn(a, b):
    return lax.dot_general(a, b, _TN, preferred_element_type=F32)


def _chunk_decays(la, tri_f32):
    b = jnp.dot(tri_f32, la, preferred_element_type=F32, precision=lax.Precision.HIGHEST)
    bl = b[CHUNK - 1:CHUNK, :]
    return jnp.exp(b), jnp.exp(bl - b), jnp.exp(-bl), jnp.exp(bl)


def _gla_fwd(p7, log_a, gla_g, d):
    t = p7.shape[0]
    dk_all = d // 2
    dkh, dvh = dk_all // HEADS, d // HEADS
    cb = ROW_TILE
    ncb = cb // CHUNK
    scale = dkh ** -0.5

    def body(qk_ref, v_ref, r_ref, la_ref, g_ref, o_ref, y_ref, s_ref, st_scr):
        @pl.when(pl.program_id(0) == 0)
        def _():
            st_scr[...] = jnp.zeros_like(st_scr)

        tri = _tri(CHUNK)
        tri_f = tri.astype(F32)
        for c in range(ncb):
            rows = slice(c * CHUNK, (c + 1) * CHUNK)
            eb, ekl, ebl_inv, gam = _chunk_decays(la_ref[rows, :], tri_f)
            for h in range(HEADS):
                ks = slice(h * dkh, (h + 1) * dkh)
                vs = slice(h * dvh, (h + 1) * dvh)
                q = qk_ref[rows, ks] * scale
                k = qk_ref[rows, dk_all + h * dkh:dk_all + (h + 1) * dkh]
                v = v_ref[rows, vs].astype(BF16)
                qb = q * eb[:, ks]
                kh = (k * ekl[:, ks]).astype(BF16)
                qc = (qb * ebl_inv[:, ks]).astype(BF16)
                a = jnp.where(tri, _dot_nt(qc, kh), 0.0)
                st = st_scr[h]
                st_b = st.astype(BF16)
                s_ref[c, h] = st_b
                o = _dot_nt(qb.astype(BF16), st_b) + _dot(a.astype(BF16), v)
                st_scr[h] = st * gam[:, ks] + _dot_tn(v, kh)
                o_ref[rows, vs] = o
                rr = lax.rsqrt(jnp.mean(o * o, axis=-1, keepdims=True) + RMS_EPS)
                r = r_ref[rows, vs]
                y_ref[rows, vs] = (o * rr * g_ref[:, vs] * (r * _sigmoid(r))).astype(BF16)

    return pl.pallas_call(
        body, name="gla_fwd", grid=(t // cb,),
        in_specs=[pl.BlockSpec((cb, d), lambda i: (i, 0)),
                  pl.BlockSpec((cb, d), lambda i: (i, 1)),
                  pl.BlockSpec((cb, d), lambda i: (i, 2)),
                  pl.BlockSpec((cb, dk_all), lambda i: (i, 0)),
                  pl.BlockSpec((1, d), lambda i: (0, 0))],
        out_specs=[pl.BlockSpec((cb, d), lambda i: (i, 0)),
                   pl.BlockSpec((cb, d), lambda i: (i, 0)),
                   pl.BlockSpec((ncb, HEADS, dvh, dkh), lambda i: (i, 0, 0, 0))],
        out_shape=[jax.ShapeDtypeStruct((t, d), F32), jax.ShapeDtypeStruct((t, d), BF16),
                   jax.ShapeDtypeStruct((t // CHUNK, HEADS, dvh, dkh), BF16)],
        scratch_shapes=[pltpu.VMEM((HEADS, dvh, dkh), F32)],
        compiler_params=_cparams(("arbitrary",)),
    )(p7, p7, p7, log_a, gla_g)


SUBLANES = 8


def _tap_phases(n_taps, first):
    phases = {}
    for j in range(n_taps):
        e = first + j
        phases.setdefault(e % SUBLANES, []).append((j, e - e % SUBLANES))
    return phases


CONV_ROWS = 64


def _shifted_windows(ext_ref, shf_ref, phases, rows, ls):
    for p, taps in phases.items():
        if p:
            span = max(off for _, off in taps) + rows
            shf_ref[p, 0:span, :] = ext_ref[p:p + span, ls]

    def window(p, start, n):
        return shf_ref[p, start:start + n, :] if p else ext_ref[start:start + n, ls]

    return window


def _conv_strip(ext_ref, shf_ref, w_ref, n_taps, first, rows, ls, emit, reverse=False):
    phases = _tap_phases(n_taps, first)
    window = _shifted_windows(ext_ref, shf_ref, phases, rows, ls)
    for r0 in range(0, rows, CONV_ROWS):
        acc = None
        for p, taps in phases.items():
            for j, off in taps:
                wj = w_ref[(n_taps - 1 - j) if reverse else j, ls]
                term = window(p, off + r0, CONV_ROWS) * wj
                acc = term if acc is None else acc + term
        emit(r0, acc)


def _corr_strip(dl_ref, ext_ref, shf_ref, acc_ref, n_taps, first, rows, ls):
    phases = _tap_phases(n_taps, first)
    window = _shifted_windows(ext_ref, shf_ref, phases, rows, ls)
    for r0 in range(0, rows, CONV_ROWS):
        dl = dl_ref[r0:r0 + CONV_ROWS, ls]
        for p, taps in phases.items():
            for j, off in taps:
                prod = dl * window(p, off + r0, CONV_ROWS)
                acc_ref[SUBLANES * j:SUBLANES * (j + 1), ls] += jnp.sum(
                    prod.reshape(CONV_ROWS // SUBLANES, SUBLANES, prod.shape[-1]), axis=0)


def _conf_fwd(p7, conv_w, conv_b, ln_g, ln_b, d):
    t = p7.shape[0]
    tm = ROW_TILE

    def body(c1_ref, c2_ref, w_ref, b_ref, g_ref, bb_ref, cc_ref, sc_ref, ext, shf):
        @pl.when(pl.program_id(0) == 0)
        def _():
            ext[0:HALO, :] = jnp.zeros((HALO, d), F32)

        ext[HALO:HALO + tm, :] = c1_ref[...] * _sigmoid(c2_ref[...])
        for s0 in range(0, d, LANES):
            ls = slice(s0, s0 + LANES)

            def emit(r0, acc, ls=ls):
                cc_ref[r0:r0 + CONV_ROWS, ls] = acc + b_ref[:, ls]

            _conv_strip(ext, shf, w_ref, CONF_K, HALO - (CONF_K - 1), tm, ls, emit)
        ext[0:HALO, :] = ext[tm:tm + HALO, :]
        g = g_ref[...]
        bb = bb_ref[...]
        rows_per_step = 2 * SUBLANES

        def rows_step(k, carry):
            rs = pl.ds(pl.multiple_of(k * rows_per_step, rows_per_step), rows_per_step)
            cc = cc_ref[rs, :]
            xc = cc - jnp.mean(cc, axis=-1, keepdims=True)
            rstd = lax.rsqrt(jnp.mean(xc * xc, axis=-1, keepdims=True) + LN_EPS)
            cn = xc * rstd * g + bb
            sc_ref[rs, :] = (cn * _sigmoid(cn)).astype(BF16)
            return carry

        lax.fori_loop(0, tm // rows_per_step, rows_step, 0)

    vec = pl.BlockSpec((1, d), lambda i: (0, 0))
    return pl.pallas_call(
        body, name="conf_fwd", grid=(t // tm,),
        in_specs=[pl.BlockSpec((tm, d), lambda i: (i, 3)), pl.BlockSpec((tm, d), lambda i: (i, 4)),
                  pl.BlockSpec((CONF_K, d), lambda i: (0, 0)), vec, vec, vec],
        out_specs=[pl.BlockSpec((tm, d), lambda i: (i, 0)), pl.BlockSpec((tm, d), lambda i: (i, 0))],
        out_shape=[jax.ShapeDtypeStruct((t, d), F32), jax.ShapeDtypeStruct((t, d), BF16)],
        scratch_shapes=[pltpu.VMEM((tm + HALO, d), F32), pltpu.VMEM((SUBLANES, tm + HALO, LANES), F32)],
        compiler_params=_cparams(("arbitrary",)),
    )(p7, p7, conv_w, conv_b, ln_g, ln_b)


def _merge(p7, br_gla, br_conf, d):
    t = p7.shape[0]
    tm = ROW_TILE

    def body(gg_ref, gc_ref, a_ref, b_ref, o_ref):
        o_ref[...] = (_sigmoid(gg_ref[...]) * a_ref[...] + _sigmoid(gc_ref[...]) * b_ref[...]).astype(BF16)

    row = pl.BlockSpec((tm, d), lambda i: (i, 0))
    return pl.pallas_call(
        body, name="merge", grid=(t // tm,),
        in_specs=[pl.BlockSpec((tm, d), lambda i: (i, 5)), pl.BlockSpec((tm, d), lambda i: (i, 6)), row, row],
        out_specs=row, out_shape=jax.ShapeDtypeStruct((t, d), BF16),
        compiler_params=_cparams(("parallel",)),
    )(p7, p7, br_gla, br_conf)


def _rms_fwd(h, g, name):
    t, d = h.shape
    tm = ROW_TILE

    def body(h_ref, g_ref, u_ref):
        x = h_ref[...]
        r = lax.rsqrt(jnp.mean(x * x, axis=-1, keepdims=True) + RMS_EPS)
        u_ref[...] = (x * r * g_ref[...]).astype(BF16)

    return pl.pallas_call(
        body, name=name, grid=(t // tm,),
        in_specs=[pl.BlockSpec((tm, d), lambda i: (i, 0)), pl.BlockSpec((1, d), lambda i: (0, 0))],
        out_specs=pl.BlockSpec((tm, d), lambda i: (i, 0)), out_shape=jax.ShapeDtypeStruct((t, d), BF16),
        compiler_params=_cparams(("parallel",)),
    )(h, g)


def _ffn_mid(up, w, b, dff):
    t = up.shape[0]
    tm = ROW_TILE
    hal = 8

    def body(a_ref, bv_ref, w_ref, b_ref, y_ref, ext, shf):
        i = pl.program_id(0)

        @pl.when(i == 0)
        def _():
            ext[0:hal, :] = jnp.zeros((hal, dff), F32)

        ext[hal:hal + tm, :] = a_ref[...] * _row_mask(i, tm)
        for s0 in range(0, dff, LANES):
            ls = slice(s0, s0 + LANES)

            def emit(r0, acc, ls=ls):
                rb = slice(r0, r0 + CONV_ROWS)
                ac = acc + b_ref[:, ls]
                y_ref[rb, ls] = (ac * _sigmoid(ac) * bv_ref[rb, ls]).astype(BF16)

            _conv_strip(ext, shf, w_ref, FFN_K, hal - (FFN_K - 1), tm, ls, emit)
        ext[0:hal, :] = ext[tm:tm + hal, :]

    return pl.pallas_call(
        body, name="ffn_mid", grid=(t // tm,),
        in_specs=[pl.BlockSpec((tm, dff), lambda i: (i, 0)), pl.BlockSpec((tm, dff), lambda i: (i, 1)),
                  pl.BlockSpec((FFN_K, dff), lambda i: (0, 0)), pl.BlockSpec((1, dff), lambda i: (0, 0))],
        out_specs=pl.BlockSpec((tm, dff), lambda i: (i, 0)), out_shape=jax.ShapeDtypeStruct((t, dff), BF16),
        scratch_shapes=[pltpu.VMEM((tm + hal, dff), F32), pltpu.VMEM((SUBLANES, tm + hal, LANES), F32)],
        compiler_params=_cparams(("arbitrary",)),
    )(up, up, w, b)


def _loss_head(h2, target, gf):
    t, d = h2.shape
    tm = ROW_TILE

    def body(h_ref, tg_ref, g_ref, loss_ref, dg_ref, dh_ref):
        i = pl.program_id(0)

        @pl.when(i == 0)
        def _():
            loss_ref[...] = jnp.zeros_like(loss_ref)
            dg_ref[...] = jnp.zeros_like(dg_ref)
            dh_ref[...] = jnp.zeros_like(dh_ref)

        @pl.when(i > 0)
        def _():
            h = h_ref[...]
            g = g_ref[...]
            r = lax.rsqrt(jnp.mean(h * h, axis=-1, keepdims=True) + RMS_EPS)
            n = h * r
            err = n * g - tg_ref[...]
            loss_ref[...] += 0.5 * jnp.sum(jnp.mean(err * err, axis=-1, keepdims=True), axis=0, keepdims=True)
            dout = err * (1.0 / d)
            dg_ref[...] += jnp.sum(dout * n, axis=0, keepdims=True)
            dn = dout * g
            dh_ref[...] = r * (dn - n * jnp.mean(dn * n, axis=-1, keepdims=True))

    return pl.pallas_call(
        body, name="loss_head", grid=(t // tm,),
        in_specs=[pl.BlockSpec((tm, d), lambda i: (i, 0)),
                  pl.BlockSpec((tm, d), lambda i: (jnp.maximum(i - 1, 0), 0)),
                  pl.BlockSpec((1, d), lambda i: (0, 0))],
        out_specs=[pl.BlockSpec((1, 1), lambda i: (0, 0)), pl.BlockSpec((1, d), lambda i: (0, 0)),
                   pl.BlockSpec((tm, d), lambda i: (i, 0))],
        out_shape=[jax.ShapeDtypeStruct((1, 1), F32), jax.ShapeDtypeStruct((1, d), F32),
                   jax.ShapeDtypeStruct((t, d), F32)],
        compiler_params=_cparams(("arbitrary",)),
    )(h2, target, gf)


def _dsilu(x, sig):
    return sig * (1.0 + x * (1.0 - sig))


def _ffn_mid_bwd(up, dy, w, b, dff):
    t = up.shape[0]
    tm = ROW_TILE
    hal = 8
    nt = t // tm

    def body(a_ref, ap_ref, bv_ref, dy_ref, w_ref, b_ref, dup_ref, dw_ref, db_ref, ext, dext, dw_acc, db_acc, shf):
        i = pl.program_id(0)
        tile = nt - 1 - i

        @pl.when(i == 0)
        def _():
            dext[tm:tm + hal, :] = jnp.zeros((hal, dff), F32)
            dw_acc[...] = jnp.zeros_like(dw_acc)
            db_acc[...] = jnp.zeros_like(db_acc)

        ext[0:hal, :] = ap_ref[...] * jnp.where(tile > 0, 1.0, 0.0)
        ext[hal:hal + tm, :] = a_ref[...] * _row_mask(tile, tm)
        first = hal - (FFN_K - 1)
        for s0 in range(0, dff, LANES):
            ls = slice(s0, s0 + LANES)

            def emit_fwd(r0, acc, ls=ls, s0=s0):
                rb = slice(r0, r0 + CONV_ROWS)
                ac = acc + b_ref[:, ls]
                sig = _sigmoid(ac)
                dyv = dy_ref[rb, ls]
                dup_ref[rb, dff + s0:dff + s0 + LANES] = (dyv * ac * sig).astype(BF16)
                dac = dyv * bv_ref[rb, ls] * _dsilu(ac, sig)
                dext[rb, ls] = dac
                db_acc[:, ls] += jnp.sum(dac.reshape(CONV_ROWS // SUBLANES, SUBLANES, LANES), axis=0)

            _conv_strip(ext, shf, w_ref, FFN_K, first, tm, ls, emit_fwd)
            _corr_strip(dext, ext, shf, dw_acc, FFN_K, first, tm, ls)

            def emit_bwd(r0, da, ls=ls):
                rb = slice(r0, r0 + CONV_ROWS)
                mask = ((tile * tm + r0 + lax.broadcasted_iota(jnp.int32, (CONV_ROWS, 1), 0)) >= PAD).astype(F32)
                dup_ref[rb, ls] = (da * mask).astype(BF16)

            _conv_strip(dext, shf, w_ref, FFN_K, 0, tm, ls, emit_bwd, reverse=True)
        dext[tm:tm + hal, :] = dext[0:hal, :]

        @pl.when(i == nt - 1)
        def _():
            db_ref[...] = jnp.sum(db_acc[...], axis=0, keepdims=True)
            for j in range(FFN_K):
                dw_ref[j:j + 1, :] = jnp.sum(dw_acc[SUBLANES * j:SUBLANES * (j + 1), :], axis=0, keepdims=True)

    rev = lambda i: (nt - 1 - i, 0)
    return pl.pallas_call(
        body, name="ffn_mid_bwd", grid=(nt,),
        in_specs=[pl.BlockSpec((tm, dff), rev),
                  pl.BlockSpec((hal, dff), lambda i: (jnp.maximum((nt - 1 - i) * (tm // hal) - 1, 0), 0)),
                  pl.BlockSpec((tm, dff), lambda i: (nt - 1 - i, 1)),
                  pl.BlockSpec((tm, dff), rev),
                  pl.BlockSpec((FFN_K, dff), lambda i: (0, 0)), pl.BlockSpec((1, dff), lambda i: (0, 0))],
        out_specs=[pl.BlockSpec((tm, 2 * dff), rev),
                   pl.BlockSpec((FFN_K, dff), lambda i: (0, 0)), pl.BlockSpec((1, dff), lambda i: (0, 0))],
        out_shape=[jax.ShapeDtypeStruct((t, 2 * dff), BF16),
                   jax.ShapeDtypeStruct((FFN_K, dff), F32), jax.ShapeDtypeStruct((1, dff), F32)],
        scratch_shapes=[pltpu.VMEM((tm + hal, dff), F32), pltpu.VMEM((tm + hal, dff), F32),
                        pltpu.VMEM((SUBLANES * FFN_K, dff), F32), pltpu.VMEM((SUBLANES, dff), F32),
                        pltpu.VMEM((SUBLANES, tm + hal, LANES), F32)],
        compiler_params=_cparams(("arbitrary",)),
    )(up, up, up, dy, w, b)


def _rms_bwd(du, h, g, dres, name):
    t, d = h.shape
    tm = ROW_TILE

    def body(du_ref, h_ref, g_ref, dres_ref, dh_ref, dg_ref):
        @pl.when(pl.program_id(0) == 0)
        def _():
            dg_ref[...] = jnp.zeros_like(dg_ref)

        x = h_ref[...]
        r = lax.rsqrt(jnp.mean(x * x, axis=-1, keepdims=True) + RMS_EPS)
        n = x * r
        du_ = du_ref[...]
        dg_ref[...] += jnp.sum(du_ * n, axis=0, keepdims=True)
        dn = du_ * g_ref[...]
        dh_ref[...] = dres_ref[...] + r * (dn - n * jnp.mean(dn * n, axis=-1, keepdims=True))

    row = pl.BlockSpec((tm, d), lambda i: (i, 0))
    vec = pl.BlockSpec((1, d), lambda i: (0, 0))
    return pl.pallas_call(
        body, name=name, grid=(t // tm,),
        in_specs=[row, row, vec, row], out_specs=[row, vec],
        out_shape=[jax.ShapeDtypeStruct((t, d), F32), jax.ShapeDtypeStruct((1, d), F32)],
        compiler_params=_cparams(("arbitrary",)),
    )(du, h, g, dres)


def _merge_bwd(p7, br_gla, br_conf, dmerged, d):
    t = p7.shape[0]
    tm = ROW_TILE

    def body(gg_ref, gc_ref, a_ref, b_ref, dm_ref, da_ref, db_ref, dgg_ref, dgc_ref):
        dm = dm_ref[...]
        sg = _sigmoid(gg_ref[...])
        sc = _sigmoid(gc_ref[...])
        da_ref[...] = (dm * sg).astype(BF16)
        db_ref[...] = (dm * sc).astype(BF16)
        dgg_ref[...] = (dm * a_ref[...] * sg * (1.0 - sg)).astype(BF16)
        dgc_ref[...] = (dm * b_ref[...] * sc * (1.0 - sc)).astype(BF16)

    row = pl.BlockSpec((tm, d), lambda i: (i, 0))
    outs = pl.pallas_call(
        body, name="merge_bwd", grid=(t // tm,),
        in_specs=[pl.BlockSpec((tm, d), lambda i: (i, 5)), pl.BlockSpec((tm, d), lambda i: (i, 6)), row, row, row],
        out_specs=[row, row, row, row],
        out_shape=[jax.ShapeDtypeStruct((t, d), BF16)] * 4,
        compiler_params=_cparams(("parallel",)),
    )(p7, p7, br_gla, br_conf, dmerged)
    return outs


def _conf_bwd(p7, cc, dsc, conv_w, ln_g, ln_b, d):
    t = p7.shape[0]
    tm = ROW_TILE
    nt = t // tm

    def body(c1_ref, c2_ref, c1p_ref, c2p_ref, cc_ref, dsc_ref, w_ref, g_ref, bb_ref,
             dc1_ref, dc2_ref, dw_ref, db_ref, dg_ref, dbb_ref, ext, dext, dw_acc, shf):
        i = pl.program_id(0)
        tile = nt - 1 - i

        @pl.when(i == 0)
        def _():
            dext[tm:tm + HALO, :] = jnp.zeros((HALO, d), F32)
            dw_acc[...] = jnp.zeros_like(dw_acc)
            db_ref[...] = jnp.zeros_like(db_ref)
            dg_ref[...] = jnp.zeros_like(dg_ref)
            dbb_ref[...] = jnp.zeros_like(dbb_ref)

        ext[0:HALO, :] = c1p_ref[...] * _sigmoid(c2p_ref[...]) * jnp.where(tile > 0, 1.0, 0.0)
        g = g_ref[...]
        bb = bb_ref[...]

        def rows_step(k, sums):
            sdg, sdbb, sdb = sums
            rs = pl.ds(pl.multiple_of(k * SUBLANES, SUBLANES), SUBLANES)
            ext[pl.ds(pl.multiple_of(HALO + k * SUBLANES, SUBLANES), SUBLANES), :] = c1_ref[rs, :] * _sigmoid(c2_ref[rs, :])
            cc_ = cc_ref[rs, :]
            xc = cc_ - jnp.mean(cc_, axis=-1, keepdims=True)
            rstd = lax.rsqrt(jnp.mean(xc * xc, axis=-1, keepdims=True) + LN_EPS)
            xh = xc * rstd
            cn = xh * g + bb
            dcn = dsc_ref[rs, :] * _dsilu(cn, _sigmoid(cn))
            dxh = dcn * g
            dcc = rstd * (dxh - jnp.mean(dxh, axis=-1, keepdims=True)
                          - xh * jnp.mean(dxh * xh, axis=-1, keepdims=True))
            dext[rs, :] = dcc
            return sdg + dcn * xh, sdbb + dcn, sdb + dcc

        zero = jnp.zeros((SUBLANES, d), F32)
        sdg, sdbb, sdb = lax.fori_loop(0, tm // SUBLANES, rows_step, (zero, zero, zero))
        dg_ref[...] += jnp.sum(sdg, axis=0, keepdims=True)
        dbb_ref[...] += jnp.sum(sdbb, axis=0, keepdims=True)
        db_ref[...] += jnp.sum(sdb, axis=0, keepdims=True)
        for s0 in range(0, d, LANES):
            ls = slice(s0, s0 + LANES)
            _corr_strip(dext, ext, shf, dw_acc, CONF_K, HALO - (CONF_K - 1), tm, ls)

            def emit(r0, dc, ls=ls):
                rb = slice(r0, r0 + CONV_ROWS)
                s2l = _sigmoid(c2_ref[rb, ls])
                dc1_ref[rb, ls] = (dc * s2l).astype(BF16)
                dc2_ref[rb, ls] = (dc * c1_ref[rb, ls] * s2l * (1.0 - s2l)).astype(BF16)

            _conv_strip(dext, shf, w_ref, CONF_K, 0, tm, ls, emit, reverse=True)
        dext[tm:tm + HALO, :] = dext[0:HALO, :]

        @pl.when(i == nt - 1)
        def _():
            for j in range(CONF_K):
                dw_ref[j:j + 1, :] = jnp.sum(dw_acc[SUBLANES * j:SUBLANES * (j + 1), :], axis=0, keepdims=True)

    rev = lambda i: (nt - 1 - i, 0)
    prev = lambda col: (lambda i: (jnp.maximum((nt - 1 - i) * (tm // HALO) - 1, 0), col))
    vec = pl.BlockSpec((1, d), lambda i: (0, 0))
    return pl.pallas_call(
        body, name="conf_bwd", grid=(nt,),
        in_specs=[pl.BlockSpec((tm, d), lambda i: (nt - 1 - i, 3)), pl.BlockSpec((tm, d), lambda i: (nt - 1 - i, 4)),
                  pl.BlockSpec((HALO, d), prev(3)), pl.BlockSpec((HALO, d), prev(4)),
                  pl.BlockSpec((tm, d), rev), pl.BlockSpec((tm, d), rev),
                  pl.BlockSpec((CONF_K, d), lambda i: (0, 0)), vec, vec],
        out_specs=[pl.BlockSpec((tm, d), rev), pl.BlockSpec((tm, d), rev),
                   pl.BlockSpec((CONF_K, d), lambda i: (0, 0)), vec, vec, vec],
        out_shape=[jax.ShapeDtypeStruct((t, d), BF16), jax.ShapeDtypeStruct((t, d), BF16),
                   jax.ShapeDtypeStruct((CONF_K, d), F32)] + [jax.ShapeDtypeStruct((1, d), F32)] * 3,
        scratch_shapes=[pltpu.VMEM((tm + HALO, d), F32), pltpu.VMEM((tm + HALO, d), F32),
                        pltpu.VMEM((SUBLANES * CONF_K, d), F32), pltpu.VMEM((SUBLANES, tm + HALO, LANES), F32)],
        compiler_params=_cparams(("arbitrary",)),
    )(p7, p7, p7, p7, cc, dsc, conv_w, ln_g, ln_b)


def _gla_bwd(p7, log_a, alr, wau, b_alpha, gla_g, o, states, dy, d):
    t = p7.shape[0]
    dk_all = d // 2
    dkh, dvh = dk_all // HEADS, d // HEADS
    cb = ROW_TILE
    ncb = cb // CHUNK
    nb = t // cb
    scale = dkh ** -0.5

    def body(qk_ref, v_ref, r_ref, la_ref, alr_ref, wau_ref, ba_ref, g_ref, o_ref, s_ref, dy_ref,
             dqk_ref, dv_ref, dr_ref, dz_ref, dg_ref, dba_ref, dst_scr, dla_scr):
        i = pl.program_id(0)
        blk = nb - 1 - i

        @pl.when(i == 0)
        def _():
            dst_scr[...] = jnp.zeros_like(dst_scr)
            dg_ref[...] = jnp.zeros_like(dg_ref)
            dba_ref[...] = jnp.zeros_like(dba_ref)

        tri = _tri(CHUNK)
        tri_f = tri.astype(F32)
        triu_f = _tri(CHUNK, upper=True).astype(F32)
        for c in reversed(range(ncb)):
            rows = slice(c * CHUNK, (c + 1) * CHUNK)
            eb, ekl, ebl_inv, gam = _chunk_decays(la_ref[rows, :], tri_f)
            for h in range(HEADS):
                ks = slice(h * dkh, (h + 1) * dkh)
                kcols = slice(dk_all + h * dkh, dk_all + (h + 1) * dkh)
                vs = slice(h * dvh, (h + 1) * dvh)
                q = qk_ref[rows, ks] * scale
                k = qk_ref[rows, kcols]
                v = v_ref[rows, vs].astype(BF16)
                ebh, eklh, eih, gamh = eb[:, ks], ekl[:, ks], ebl_inv[:, ks], gam[:, ks]
                qb = q * ebh
                kh = k * eklh
                qc = qb * eih
                qb_b, kh_b, qc_b = qb.astype(BF16), kh.astype(BF16), qc.astype(BF16)
                ov = o_ref[rows, vs]
                r = r_ref[rows, vs]
                dyv = dy_ref[rows, vs]
                sig = _sigmoid(r)
                rr = lax.rsqrt(jnp.mean(ov * ov, axis=-1, keepdims=True) + RMS_EPS)
                n = ov * rr
                g = g_ref[:, vs]
                dr_ref[rows, vs] = (dyv * n * g * _dsilu(r, sig)).astype(BF16)
                don = dyv * (r * sig)
                dg_ref[:, vs] += jnp.sum(don * n, axis=0, keepdims=True)
                dn = don * g
                do = (rr * (dn - n * jnp.mean(dn * n, axis=-1, keepdims=True))).astype(BF16)
                st_b = s_ref[c, h]
                a = jnp.where(tri, _dot_nt(qc_b, kh_b), 0.0).astype(BF16)
                da = jnp.where(tri, _dot_nt(do, v), 0.0).astype(BF16)
                dst = dst_scr[h]
                dst_b = dst.astype(BF16)
                dv_ref[rows, vs] = (_dot_tn(a, do) + _dot_nt(kh_b, dst_b)).astype(BF16)
                dqb = _dot(do, st_b)
                dqc = _dot(da, kh_b)
                dkh_ = _dot_tn(da, qc_b) + _dot(v, dst_b)
                dgam = jnp.sum(st_b.astype(F32) * dst, axis=0, keepdims=True)
                dst_scr[h] = dst * gamh + _dot_tn(do, qb_b)
                dqk_ref[rows, ks] = ((dqb * ebh + dqc * (ebh * eih)) * scale).astype(BF16)
                dqk_ref[rows, kcols] = (dkh_ * eklh).astype(BF16)
                qq = dqc * qc
                kk = dkh_ * kh
                db = dqb * qb + qq - kk
                dbl = jnp.sum(kk - qq, axis=0, keepdims=True) + dgam * gamh
                dla_scr[rows, ks] = jnp.dot(triu_f, db, preferred_element_type=F32,
                                            precision=lax.Precision.HIGHEST) + dbl
        z = jnp.dot(alr_ref[...].astype(BF16), wau_ref[...].astype(BF16), preferred_element_type=F32) + ba_ref[...]
        dz = dla_scr[...] * (1.0 / GATE_TAU) * _sigmoid(-z) * _row_mask(blk, cb)
        dba_ref[...] += jnp.sum(dz, axis=0, keepdims=True)
        dz_ref[...] = dz.astype(BF16)

    rev = lambda i: (nb - 1 - i, 0)
    row = pl.BlockSpec((cb, d), rev)
    return pl.pallas_call(
        body, name="gla_bwd", grid=(nb,),
        in_specs=[row, pl.BlockSpec((cb, d), lambda i: (nb - 1 - i, 1)), pl.BlockSpec((cb, d), lambda i: (nb - 1 - i, 2)),
                  pl.BlockSpec((cb, dk_all), rev), pl.BlockSpec((cb, RANK_PAD), rev),
                  pl.BlockSpec((RANK_PAD, dk_all), lambda i: (0, 0)), pl.BlockSpec((1, dk_all), lambda i: (0, 0)),
                  pl.BlockSpec((1, d), lambda i: (0, 0)), row,
                  pl.BlockSpec((ncb, HEADS, dvh, dkh), lambda i: (nb - 1 - i, 0, 0, 0)), row],
        out_specs=[row, row, row, pl.BlockSpec((cb, dk_all), rev),
                   pl.BlockSpec((1, d), lambda i: (0, 0)), pl.BlockSpec((1, dk_all), lambda i: (0, 0))],
        out_shape=[jax.ShapeDtypeStruct((t, d), BF16)] * 3 + [jax.ShapeDtypeStruct((t, dk_all), BF16),
                   jax.ShapeDtypeStruct((1, d), F32), jax.ShapeDtypeStruct((1, dk_all), F32)],
        scratch_shapes=[pltpu.VMEM((HEADS, dvh, dkh), F32), pltpu.VMEM((cb, dk_all), F32)],
        compiler_params=_cparams(("arbitrary",)),
    )(p7, p7, p7, log_a, alr, wau, b_alpha, gla_g, o, states, dy)


def _input_grad(du_a, du_b, h0, g1, dh1):
    t, d = h0.shape
    tm = ROW_TILE
    s = t - HEAD_ROWS

    def body(dua_ref, dub_ref, h_ref, g_ref, dres_ref, gx_ref, gm_ref, dg_ref):
        i = pl.program_id(0)

        @pl.when(i == 0)
        def _():
            dg_ref[...] = jnp.zeros_like(dg_ref)

        x = h_ref[...]
        r = lax.rsqrt(jnp.mean(x * x, axis=-1, keepdims=True) + RMS_EPS)
        n = x * r
        du_ = dua_ref[...] + dub_ref[...]
        dg_ref[...] += jnp.sum(du_ * n, axis=0, keepdims=True)
        dn = du_ * g_ref[...]
        dh = dres_ref[...] + r * (dn - n * jnp.mean(dn * n, axis=-1, keepdims=True))

        @pl.when(i == 0)
        def _():
            gm_ref[...] = dh[PAD:HEAD_ROWS, :]
            gx_ref[...] = jnp.zeros_like(gx_ref)

        @pl.when(i > 0)
        def _():
            gx_ref[...] = dh

    row = pl.BlockSpec((tm, d), lambda i: (i, 0))
    vec = pl.BlockSpec((1, d), lambda i: (0, 0))
    return pl.pallas_call(
        body, name="input_grad", grid=(t // tm,),
        in_specs=[row, row, row, vec, row],
        out_specs=[pl.BlockSpec((tm, d), lambda i: (jnp.maximum(i - 1, 0), 0)),
                   pl.BlockSpec((N_META, d), lambda i: (0, 0)), vec],
        out_shape=[jax.ShapeDtypeStruct((s, d), F32), jax.ShapeDtypeStruct((N_META, d), F32),
                   jax.ShapeDtypeStruct((1, d), F32)],
        compiler_params=_cparams(("arbitrary",)),
    )(du_a, du_b, h0, g1, dh1)


def _local_step(x, target, w):
    s, d = x.shape
    dk_all = d // 2
    dff = w["w_down"].shape[0]
    w_in = w["w_in"]
    lo, hi = 3 * d, 3 * d + GLA_RANK
    wq = jnp.concatenate([w_in[:, :lo], w_in[:, hi:]], axis=1)
    w_alr = jnp.pad(w_in[:, lo:hi], ((0, 0), (0, RANK_PAD - GLA_RANK)))
    wau = jnp.pad(w["w_alpha_up"], ((0, RANK_PAD - GLA_RANK), (0, 0)))

    h0, u1 = _prep(x, w["meta_tokens"], w["norm_mix_g"])
    p7 = _matmul(u1, wq, dims="nn", name="proj")
    alr = _matmul(u1, w_alr, dims="nn", name="proj_alr")
    log_a = _log_gate(alr, wau, w["b_alpha"])
    o, y_gla, states = _gla_fwd(p7, log_a, w["gla_norm_g"], d)
    br_gla = _matmul(y_gla, w["w_gla_o"], dims="nn", name="gla_out")
    cc, s_c = _conf_fwd(p7, w["conf_dw_w"], w["conf_dw_b"], w["conf_ln_g"], w["conf_ln_b"], d)
    br_conf = _matmul(s_c, w["w_conf_o"], dims="nn", name="conf_out")
    merged = _merge(p7, br_gla, br_conf, d)
    h1 = _matmul(merged, w["w_out"], dims="nn", name="mix_out", add=h0)
    u2 = _rms_fwd(h1, w["norm_ffn_g"], "norm_ffn")
    up = _matmul(u2, w["w_up"], dims="nn", name="ffn_up")
    y = _ffn_mid(up, w["ffn_dw_w"], w["ffn_dw_b"], dff)
    h2 = _matmul(y, w["w_down"], dims="nn", name="ffn_down", add=h1)
    loss, d_gf, dh2 = _loss_head(h2, target, w["final_norm_g"])

    g = {"final_norm_g": d_gf}
    dy = _matmul(dh2, w["w_down"], dims="nt", name="d_ffn_y")
    g["w_down"] = _matmul(y, dh2, dims="tn", name="dw_down")
    dup, g["ffn_dw_w"], g["ffn_dw_b"] = _ffn_mid_bwd(up, dy, w["ffn_dw_w"], w["ffn_dw_b"], dff)
    du2 = _matmul(dup, w["w_up"], dims="nt", name="d_u2")
    g["w_up"] = _matmul(u2, dup, dims="tn", name="dw_up")
    dh1, g["norm_ffn_g"] = _rms_bwd(du2, h1, w["norm_ffn_g"], dh2, "norm_ffn_bwd")
    dmerged = _matmul(dh1, w["w_out"], dims="nt", name="d_merged")
    g["w_out"] = _matmul(merged, dh1, dims="tn", name="dw_out")
    d_br_gla, d_br_conf, dgg, dgc = _merge_bwd(p7, br_gla, br_conf, dmerged, d)
    dsc = _matmul(d_br_conf, w["w_conf_o"], dims="nt", name="d_conf_s")
    g["w_conf_o"] = _matmul(s_c, d_br_conf, dims="tn", name="dw_conf_o")
    dc1, dc2, g["conf_dw_w"], g["conf_dw_b"], g["conf_ln_g"], g["conf_ln_b"] = _conf_bwd(
        p7, cc, dsc, w["conf_dw_w"], w["conf_ln_g"], w["conf_ln_b"], d)
    dyg = _matmul(d_br_gla, w["w_gla_o"], dims="nt", name="d_gla_y")
    g["w_gla_o"] = _matmul(y_gla, d_br_gla, dims="tn", name="dw_gla_o")
    dqk, dv, dr, dz, g["gla_norm_g"], g["b_alpha"] = _gla_bwd(
        p7, log_a, alr, wau, w["b_alpha"], w["gla_norm_g"], o, states, dyg, d)
    dalr = _matmul(dz, wau, dims="nt", name="d_alr", out_dtype=BF16)
    g["w_alpha_up"] = _matmul(alr, dz, dims="tn", name="dw_alpha_up")[:GLA_RANK]
    dp7 = jnp.concatenate([dqk, dv, dr, dc1, dc2, dgg, dgc], axis=1)
    du1a = _matmul(dp7, wq, dims="nt", name="d_u1")
    du1b = _matmul(dalr, w_alr, dims="nt", name="d_u1_alr")
    dwq = _matmul(u1, dp7, dims="tn", name="dw_in")
    dw_alr = _matmul(u1, dalr, dims="tn", name="dw_in_alr")
    g["w_in"] = jnp.concatenate([dwq[:, :lo], dw_alr[:, :GLA_RANK], dwq[:, lo:]], axis=1)
    grad_x, g["meta_tokens"], g["norm_mix_g"] = _input_grad(du1a, du1b, h0, w["norm_mix_g"], dh1)
    return loss, grad_x, g


HBM_SPEC = pl.BlockSpec(memory_space=pltpu.HBM)
FLIPS = ((1, 0), (0, 1), (1, 1))


def _place():
    x, y, c = lax.axis_index("x"), lax.axis_index("y"), lax.axis_index("c")
    return x, y, c


def _gather_weights(big, small):
    _, r, _ = big.shape
    rs = small.shape[0]

    def body(big_ref, small_ref, ball_ref, sall_ref, send_sems, recv_sems, s_send, s_recv, local_sems):
        x, y, c = _place()
        me = 2 * x + y
        sib = (x, y, 1 - c)
        chips = [(x ^ fx, y ^ fy) for fx, fy in FLIPS]
        own_b = pltpu.make_async_copy(big_ref, ball_ref.at[me], local_sems.at[0])
        own_s = pltpu.make_async_copy(small_ref, sall_ref.at[me], local_sems.at[1])
        own_b.start()
        own_s.start()

        def half(chip, hc, k, to, src=None):
            slot = ball_ref.at[2 * chip[0] + chip[1], hc]
            return pltpu.make_async_remote_copy(src_ref=slot if src is None else src, dst_ref=slot,
                                                send_sem=send_sems.at[k], recv_sem=recv_sems.at[k],
                                                device_id=to, device_id_type=MESH)

        def smallcp(chip, k, to, src=None):
            slot = sall_ref.at[2 * chip[0] + chip[1]]
            return pltpu.make_async_remote_copy(src_ref=slot if src is None else src, dst_ref=slot,
                                                send_sem=s_send.at[k], recv_sem=s_recv.at[k],
                                                device_id=to, device_id_type=MESH)

        first = [half((x, y), c, j, (*chip, c), src=big_ref.at[c]) for j, chip in enumerate(chips)]
        first += [smallcp((x, y), j, (*chip, c), src=small_ref) for j, chip in enumerate(chips)]
        for cp in first:
            cp.start()
        passed = [half(chip, c, 3 + j, sib) for j, chip in enumerate(chips)]
        for j, chip in enumerate(chips):
            half(chip, c, j, (x, y, c)).wait_recv()
            passed[j].start()
        for j, chip in enumerate(chips):
            half(chip, 1 - c, 3 + j, (x, y, c)).wait_recv()
            smallcp(chip, j, (x, y, c)).wait_recv()
        for cp in first + passed:
            cp.wait_send()
        own_b.wait()
        own_s.wait()

    return pl.pallas_call(
        body, name="gather_weights",
        in_specs=[HBM_SPEC, HBM_SPEC], out_specs=[HBM_SPEC, HBM_SPEC],
        out_shape=[jax.ShapeDtypeStruct((4, 2, r, LANES), big.dtype), jax.ShapeDtypeStruct((4, rs, LANES), small.dtype)],
        scratch_shapes=[pltpu.SemaphoreType.DMA((6,)), pltpu.SemaphoreType.DMA((6,)),
                        pltpu.SemaphoreType.DMA((3,)), pltpu.SemaphoreType.DMA((3,)), pltpu.SemaphoreType.DMA((2,))],
    )(big, small)


def _swap_with_sibling(send, name):
    def body(src_ref, dst_ref, send_sem, recv_sem):
        x, y, c = _place()
        cp = pltpu.make_async_remote_copy(src_ref=src_ref, dst_ref=dst_ref, send_sem=send_sem, recv_sem=recv_sem,
                                          device_id=(x, y, 1 - c), device_id_type=MESH)
        cp.start()
        cp.wait()

    return pl.pallas_call(
        body, name=name, in_specs=[HBM_SPEC], out_specs=HBM_SPEC,
        out_shape=jax.ShapeDtypeStruct(send.shape, send.dtype),
        scratch_shapes=[pltpu.SemaphoreType.DMA(()), pltpu.SemaphoreType.DMA(())],
    )(send)


def _exchange_chip_partials(cp_all):
    _, r, _ = cp_all.shape

    def body(src_ref, dst_ref, send_sems, recv_sems):
        x, y, c = _place()
        cps = []
        for j, (fx, fy) in enumerate(FLIPS):
            tx, ty = x ^ fx, y ^ fy
            cps.append(pltpu.make_async_remote_copy(
                src_ref=src_ref.at[2 * tx + ty], dst_ref=dst_ref.at[j], send_sem=send_sems.at[j],
                recv_sem=recv_sems.at[j], device_id=(tx, ty, c), device_id_type=MESH))
        for cp in cps:
            cp.start()
        for cp in cps:
            cp.wait()

    return pl.pallas_call(
        body, name="exchange_chip_partials", in_specs=[HBM_SPEC], out_specs=HBM_SPEC,
        out_shape=jax.ShapeDtypeStruct((3, r, LANES), cp_all.dtype),
        scratch_shapes=[pltpu.SemaphoreType.DMA((3,)), pltpu.SemaphoreType.DMA((3,))],
    )(cp_all)


def _gather_small(block):
    m, n = block.shape

    def body(x_ref, out_ref, send_sems, recv_sems, local_sem):
        x, y, c = _place()
        me, sibling = (x, y, c), (x, y, 1 - c)
        chips = [(x ^ fx, y ^ fy) for fx, fy in FLIPS]

        def rows(px, py, pc):
            return out_ref.at[pl.ds((4 * px + 2 * py + pc) * m, m), :]

        def copy(k, blk, to, src=None):
            return pltpu.make_async_remote_copy(
                src_ref=rows(*blk) if src is None else src, dst_ref=rows(*blk),
                send_sem=send_sems.at[k], recv_sem=recv_sems.at[k], device_id=to, device_id_type=MESH)

        mine = pltpu.make_async_copy(x_ref, rows(*me), local_sem)
        mine.start()
        first = [copy(0, me, sibling, src=x_ref)]
        first += [copy(1 + j, me, (*chip, c), src=x_ref) for j, chip in enumerate(chips)]
        for cp in first:
            cp.start()
        passed = [copy(4 + j, (*chip, c), sibling) for j, chip in enumerate(chips)]
        for j, chip in enumerate(chips):
            copy(1 + j, (*chip, c), me).wait_recv()
            passed[j].start()
        copy(0, sibling, me).wait_recv()
        for j, chip in enumerate(chips):
            copy(4 + j, (*chip, 1 - c), me).wait_recv()
        for cp in first + passed:
            cp.wait_send()
        mine.wait()

    out = pl.pallas_call(
        body, name="gather_small",
        out_shape=jax.ShapeDtypeStruct((8 * m, n), block.dtype),
        in_specs=[pl.BlockSpec(memory_space=pltpu.VMEM)],
        out_specs=pl.BlockSpec(memory_space=pltpu.VMEM),
        scratch_shapes=[pltpu.SemaphoreType.DMA((7,)), pltpu.SemaphoreType.DMA((7,)), pltpu.SemaphoreType.DMA],
    )(block)
    return out.reshape(8, m, n)


def _rows_tile(r):
    for n in (1, 2, 3, 4, 6, 8, 12, 16, 24, 32, 48, 64):
        if r % n == 0 and (r // n) % 16 == 0 and (r // n) * LANES * 4 <= 4 * 1024 * 1024:
            return r // n
    return r


def _add_pairs(a, b, name, out_dtype):
    n, r, _ = a.shape
    tr = _rows_tile(r)

    def body(a_ref, b_ref, o_ref):
        o_ref[...] = (a_ref[...].astype(F32) + b_ref[...].astype(F32)).astype(out_dtype)

    spec = pl.BlockSpec((1, tr, LANES), lambda i, j: (i, j, 0))
    return pl.pallas_call(
        body, name=name, grid=(n, r // tr), in_specs=[spec, spec], out_specs=spec,
        out_shape=jax.ShapeDtypeStruct((n, r, LANES), out_dtype),
        compiler_params=_cparams(("parallel", "parallel")),
    )(a, b)


def _sum_partials(own, others):
    r = own.shape[0]
    tr = _rows_tile(r)

    def body(a_ref, b_ref, o_ref):
        acc = a_ref[...].astype(F32)
        for j in range(3):
            acc = acc + b_ref[j].astype(F32)
        o_ref[...] = acc

    return pl.pallas_call(
        body, name="sum_partials", grid=(r // tr,),
        in_specs=[pl.BlockSpec((tr, LANES), lambda i: (i, 0)), pl.BlockSpec((3, tr, LANES), lambda i: (0, i, 0))],
        out_specs=pl.BlockSpec((tr, LANES), lambda i: (i, 0)),
        out_shape=jax.ShapeDtypeStruct((r, LANES), F32),
        compiler_params=_cparams(("parallel",)),
    )(own, others)


def _sum_devices(blocks):
    n, m, _ = blocks.shape

    def body(b_ref, o_ref):
        acc = b_ref[0]
        for j in range(1, n):
            acc = acc + b_ref[j]
        o_ref[...] = acc

    return pl.pallas_call(
        body, name="sum_devices", out_shape=jax.ShapeDtypeStruct((m, LANES), F32),
        in_specs=[pl.BlockSpec(memory_space=pltpu.VMEM)], out_specs=pl.BlockSpec(memory_space=pltpu.VMEM),
    )(blocks)


def _adamw(w, g, m, v, name):
    rws, cols = w.shape
    tr = rws
    for cand in (256, 128, 64, 32, 16, 8):
        if rws % cand == 0 and cand * cols * 4 <= 2 * 1024 * 1024:
            tr = cand
            break
    c1 = 1.0 - ADAM_B1 ** ADAM_STEP
    c2 = 1.0 - ADAM_B2 ** ADAM_STEP

    def body(w_ref, g_ref, m_ref, v_ref, d_ref, nm_ref, nv_ref):
        gv = g_ref[...]
        nm = ADAM_B1 * m_ref[...] + (1.0 - ADAM_B1) * gv
        nv = ADAM_B2 * v_ref[...] + (1.0 - ADAM_B2) * (gv * gv)
        m_hat = nm / c1
        v_hat = nv / c2
        d_ref[...] = -ADAM_LR * (m_hat / (jnp.sqrt(v_hat) + ADAM_EPS) + ADAM_WD * w_ref[...])
        nm_ref[...] = nm
        nv_ref[...] = nv

    spec = pl.BlockSpec((tr, cols), lambda i: (i, 0))
    return pl.pallas_call(
        body, name=name, grid=(rws // tr,), in_specs=[spec] * 4, out_specs=[spec] * 3,
        out_shape=[jax.ShapeDtypeStruct((rws, cols), F32)] * 3,
        compiler_params=_cparams(("parallel",)),
    )(w, g, m, v)


WEIGHTS = (
    ("meta_tokens", (16, 1024), 1), ("norm_mix_g", (1024,), None), ("w_in", (1024, 7184), 1),
    ("w_alpha_up", (16, 512), 1), ("b_alpha", (512,), None), ("gla_norm_g", (1024,), None),
    ("w_gla_o", (1024, 1024), 0), ("conf_dw_w", (31, 1024), 1), ("conf_dw_b", (1024,), None),
    ("conf_ln_g", (1024,), None), ("conf_ln_b", (1024,), None), ("w_conf_o", (1024, 1024), 0),
    ("w_out", (1024, 1024), 0), ("norm_ffn_g", (1024,), None), ("w_up", (1024, 5632), 1),
    ("ffn_dw_w", (3, 2816), 1), ("ffn_dw_b", (2816,), None), ("w_down", (2816, 1024), 0),
    ("final_norm_g", (1024,), None),
)
BIG = ("w_in", "w_up", "w_down", "w_gla_o", "w_conf_o", "w_out")
SMALL_SHARDED = ("meta_tokens", "w_alpha_up", "conf_dw_w", "ffn_dw_w")
REPLICATED = tuple(n for n, _, ax in WEIGHTS if ax is None)
SHAPES = {n: s for n, s, _ in WEIGHTS}
AXIS = {n: ax for n, _, ax in WEIGHTS}
N_CHIPS = 4


def _shard_shape(name):
    s = list(SHAPES[name])
    s[AXIS[name]] //= N_CHIPS
    return tuple(s)


def _pack(parts, mult):
    flat = jnp.concatenate([p.reshape(-1) for p in parts])
    pad = (-flat.shape[0]) % mult
    return jnp.pad(flat, (0, pad))


def _unpack(flat, names, shape_of):
    out, off = {}, 0
    for n in names:
        shp = shape_of(n)
        size = math.prod(shp)
        out[n] = flat[off:off + size].reshape(shp)
        off += size
    return out


def _full_from_shards(per_chip, name):
    return jnp.concatenate(per_chip, axis=AXIS[name])


def _shards_of(full, name):
    return jnp.split(full, N_CHIPS, axis=AXIS[name])


def kernel(x, meta_tokens, norm_mix_g, w_in, w_alpha_up, b_alpha, gla_norm_g, w_gla_o, conf_dw_w, conf_dw_b, conf_ln_g, conf_ln_b, w_conf_o, w_out, norm_ffn_g, w_up, ffn_dw_w, ffn_dw_b, w_down, final_norm_g, loss_target, m_meta_tokens, m_norm_mix_g, m_w_in, m_w_alpha_up, m_b_alpha, m_gla_norm_g, m_w_gla_o, m_conf_dw_w, m_conf_dw_b, m_conf_ln_g, m_conf_ln_b, m_w_conf_o, m_w_out, m_norm_ffn_g, m_w_up, m_ffn_dw_w, m_ffn_dw_b, m_w_down, m_final_norm_g, v_meta_tokens, v_norm_mix_g, v_w_in, v_w_alpha_up, v_b_alpha, v_gla_norm_g, v_w_gla_o, v_conf_dw_w, v_conf_dw_b, v_conf_ln_g, v_conf_ln_b, v_w_conf_o, v_w_out, v_norm_ffn_g, v_w_up, v_ffn_dw_w, v_ffn_dw_b, v_w_down, v_final_norm_g):
    names = [n for n, _, _ in WEIGHTS]
    w_args = (meta_tokens, norm_mix_g, w_in, w_alpha_up, b_alpha, gla_norm_g, w_gla_o, conf_dw_w, conf_dw_b, conf_ln_g,
              conf_ln_b, w_conf_o, w_out, norm_ffn_g, w_up, ffn_dw_w, ffn_dw_b, w_down, final_norm_g)
    m_args = (m_meta_tokens, m_norm_mix_g, m_w_in, m_w_alpha_up, m_b_alpha, m_gla_norm_g, m_w_gla_o, m_conf_dw_w,
              m_conf_dw_b, m_conf_ln_g, m_conf_ln_b, m_w_conf_o, m_w_out, m_norm_ffn_g, m_w_up, m_ffn_dw_w, m_ffn_dw_b,
              m_w_down, m_final_norm_g)
    v_args = (v_meta_tokens, v_norm_mix_g, v_w_in, v_w_alpha_up, v_b_alpha, v_gla_norm_g, v_w_gla_o, v_conf_dw_w,
              v_conf_dw_b, v_conf_ln_g, v_conf_ln_b, v_w_conf_o, v_w_out, v_norm_ffn_g, v_w_up, v_ffn_dw_w, v_ffn_dw_b,
              v_w_down, v_final_norm_g)
    in_shape = {n: a.shape for n, a in zip(names, w_args)}
    local = {n: a.reshape(_shard_shape(n) if AXIS[n] is not None else SHAPES[n]) for n, a in zip(names, w_args)}
    m_loc = {n: a.reshape(local[n].shape) for n, a in zip(names, m_args)}
    v_loc = {n: a.reshape(local[n].shape) for n, a in zip(names, v_args)}

    big = _pack([local[n].astype(BF16) for n in BIG], 2 * 16 * LANES).reshape(2, -1, LANES)
    small = _pack([local[n] for n in SMALL_SHARDED], 8 * LANES).reshape(-1, LANES)
    big_all, small_all = _gather_weights(big, small)
    big_all = big_all.reshape(N_CHIPS, -1)
    small_all = small_all.reshape(N_CHIPS, -1)
    per_chip_big = [_unpack(big_all[t], BIG, _shard_shape) for t in range(N_CHIPS)]
    per_chip_small = [_unpack(small_all[t], SMALL_SHARDED, _shard_shape) for t in range(N_CHIPS)]
    full = {}
    for n in BIG:
        full[n] = _full_from_shards([per_chip_big[t][n] for t in range(N_CHIPS)], n)
    for n in SMALL_SHARDED:
        full[n] = _full_from_shards([per_chip_small[t][n] for t in range(N_CHIPS)], n)
    for n in REPLICATED:
        full[n] = local[n].reshape(1, -1)

    loss_part, grad_x, grads = _local_step(x[0], loss_target[0], full)

    rep = _pack([grads[n] for n in REPLICATED] + [loss_part], 8 * LANES).reshape(-1, LANES)
    rep_sum = _sum_devices(_gather_small(rep)).reshape(-1)
    rep_grads = _unpack(rep_sum, REPLICATED, lambda n: SHAPES[n])
    loss = rep_sum[sum(math.prod(SHAPES[n]) for n in REPLICATED)]

    sharded = BIG + SMALL_SHARDED
    c = lax.axis_index("c")
    me = 2 * lax.axis_index("x") + lax.axis_index("y")
    by_chip = [_pack([_shards_of(grads[n], n)[t].astype(BF16) for n in sharded], 2 * 16 * LANES).reshape(2, -1, LANES)
               for t in range(N_CHIPS)]
    gp = jnp.stack(by_chip, axis=1)
    mine_half = lax.dynamic_index_in_dim(gp, c, 0, keepdims=False)
    other_half = lax.dynamic_index_in_dim(gp, 1 - c, 0, keepdims=False)
    from_sibling = _swap_with_sibling(other_half, "swap_core_halves")
    chip_part = _add_pairs(mine_half, from_sibling, "add_core_halves", BF16)
    from_chips = _exchange_chip_partials(chip_part)
    own = lax.dynamic_index_in_dim(chip_part, me, 0, keepdims=False)
    red_half = _sum_partials(own, from_chips)
    sib_half = _swap_with_sibling(red_half, "swap_reduced_halves")
    lo_half = jnp.where(c == 0, red_half, sib_half)
    hi_half = jnp.where(c == 0, sib_half, red_half)
    red = jnp.concatenate([lo_half.reshape(-1), hi_half.reshape(-1)])
    shard_grads = _unpack(red, sharded, _shard_shape)

    g_loc = {**shard_grads, **rep_grads}
    delta, new_m, new_v = {}, {}, {}
    for n in ("w_in", "w_up", "w_down"):
        delta[n], new_m[n], new_v[n] = _adamw(local[n], g_loc[n], m_loc[n], v_loc[n], "adamw_" + n)
    sq = ("w_gla_o", "w_conf_o", "w_out")
    stk = lambda dct: jnp.concatenate([dct[n] for n in sq], axis=0)
    d3, m3, v3 = _adamw(stk(local), stk(g_loc), stk(m_loc), stk(v_loc), "adamw_square")
    rows3 = local["w_out"].shape[0]
    for j, n in enumerate(sq):
        delta[n], new_m[n], new_v[n] = (a[j * rows3:(j + 1) * rows3] for a in (d3, m3, v3))
    rest = SMALL_SHARDED + REPLICATED
    pk = lambda dct: _pack([dct[n] for n in rest], 8 * LANES).reshape(-1, LANES)
    ds, ms, vs = _adamw(pk(local), pk(g_loc), pk(m_loc), pk(v_loc), "adamw_small")
    shape_loc = lambda n: local[n].shape
    for dct, flat in ((delta, ds), (new_m, ms), (new_v, vs)):
        dct.update(_unpack(flat.reshape(-1), rest, shape_loc))

    outs = [loss, grad_x[None]]
    for dct in (g_loc, delta, new_m, new_v):
        outs += [dct[n].reshape(in_shape[n]) for n in names]
    return tuple(outs)
```

```python
import functools
import math

import jax
import jax.numpy as jnp
from jax import lax
from jax.experimental import pallas as pl
from jax.experimental.pallas import tpu as pltpu

F32 = jnp.float32
BF16 = jnp.bfloat16

N_META = 16
PAD = 240
HEAD_ROWS = PAD + N_META
HEADS = 4
GLA_RANK = 16
RANK_PAD = 128
GATE_TAU = 16.0
CHUNK = 64
CONF_K = 31
FFN_K = 3
RMS_EPS = 1e-6
LN_EPS = 1e-5
ADAM_LR, ADAM_B1, ADAM_B2, ADAM_EPS, ADAM_WD, ADAM_STEP = 0.001, 0.9, 0.999, 1e-08, 0.01, 10

ROW_TILE = 256
HALO = 32
LANES = 128
V7X_VMEM_LIMIT = 56 * 1024 * 1024
MESH = pl.DeviceIdType.MESH


def _cparams(sem):
    return pltpu.CompilerParams(dimension_semantics=sem, vmem_limit_bytes=V7X_VMEM_LIMIT)


def _sigmoid(x):
    return 1.0 / (1.0 + jnp.exp(-x))


def _pick(n, prefs):
    for p in prefs:
        if n % p == 0:
            return p
    return n


def _matmul(a, b, *, dims, name, tm=None, tn=None, tk=None, out_dtype=F32, add=None, chips=None):
    if chips == "b":
        nc, r, cs = b.shape
        b_shape = (r, nc * cs)
    else:
        b_shape = b.shape
    if dims == "nn":
        (m, k), (_, n) = a.shape, b_shape
    elif dims == "nt":
        (m, k), (n, _) = a.shape, b_shape
    else:
        (k, m), (_, n) = a.shape, b_shape
    tm = tm or _pick(m, (768, 1024, 1408, 512, 256, 128))
    tn = tn or _pick(n, (1024, 1408, 512, 256, 128))
    tk = tk or _pick(k, (1024, 768, 1408, 512, 256, 128))
    if chips == "b":
        tn, tk = (cs, tk) if dims == "nn" else (tn, cs)
    if chips == "out":
        tn, tm = n // N_CHIPS, m
    nk = k // tk
    assert m % tm == 0 and n % tn == 0 and k % tk == 0, (name, m, n, k, tm, tn, tk)
    a_spec = {"nn": pl.BlockSpec((tm, tk), lambda i, j, kk: (i, kk)),
              "nt": pl.BlockSpec((tm, tk), lambda i, j, kk: (i, kk)),
              "tn": pl.BlockSpec((tk, tm), lambda i, j, kk: (kk, i))}[dims]
    if chips == "b":
        b_spec = {"nn": pl.BlockSpec((None, tk, tn), lambda i, j, kk: (j, kk, 0)),
                  "nt": pl.BlockSpec((None, tn, tk), lambda i, j, kk: (kk, j, 0))}[dims]
    else:
        b_spec = {"nn": pl.BlockSpec((tk, tn), lambda i, j, kk: (kk, j)),
                  "nt": pl.BlockSpec((tn, tk), lambda i, j, kk: (j, kk)),
                  "tn": pl.BlockSpec((tk, tn), lambda i, j, kk: (kk, j))}[dims]
    contract = {"nn": (((1,), (0,)), ((), ())), "nt": (((1,), (1,)), ((), ())), "tn": (((0,), (0,)), ((), ()))}[dims]
    if chips == "out":
        o_spec = pl.BlockSpec((None, tm, tn), lambda i, j, kk: (j, 0, 0))
        out_struct = jax.ShapeDtypeStruct((N_CHIPS, m, tn), out_dtype)
    else:
        o_spec = pl.BlockSpec((tm, tn), lambda i, j, kk: (i, j))
        out_struct = jax.ShapeDtypeStruct((m, n), out_dtype)
    has_add = add is not None

    def body(*refs):
        if has_add:
            a_ref, b_ref, add_ref, o_ref, acc_ref = refs
        else:
            a_ref, b_ref, o_ref, acc_ref = refs
            add_ref = None
        kk = pl.program_id(2)
        prod = lax.dot_general(a_ref[...].astype(BF16), b_ref[...].astype(BF16), contract,
                               preferred_element_type=F32)

        @pl.when(kk == 0)
        def _():
            acc_ref[...] = prod + add_ref[...].astype(F32) if has_add else prod

        @pl.when(kk > 0)
        def _():
            acc_ref[...] += prod

        @pl.when(kk == nk - 1)
        def _():
            o_ref[...] = acc_ref[...].astype(out_dtype)

    in_specs = [a_spec, b_spec] + ([o_spec] if has_add else [])
    args = (a, b) + ((add,) if has_add else ())
    return pl.pallas_call(
        body, name=name, grid=(m // tm, n // tn, nk),
        in_specs=in_specs, out_specs=o_spec,
        out_shape=out_struct,
        scratch_shapes=[pltpu.VMEM((tm, tn), F32)],
        compiler_params=_cparams(("parallel", "parallel", "arbitrary")),
    )(*args)


def _row_mask(tile_index, rows):
    r = tile_index * rows + lax.broadcasted_iota(jnp.int32, (rows, 1), 0)
    return (r >= PAD).astype(F32)


def _prep(x, meta, g1):
    s, d = x.shape
    t = HEAD_ROWS + s
    tm = ROW_TILE

    def body(x_ref, meta_ref, g_ref, h_ref, u_ref):
        i = pl.program_id(0)

        @pl.when(i == 0)
        def _():
            h_ref[0:PAD, :] = jnp.zeros((PAD, d), F32)
            h_ref[PAD:HEAD_ROWS, :] = meta_ref[...]

        @pl.when(i > 0)
        def _():
            h_ref[...] = x_ref[...]

        h = h_ref[...]
        r = lax.rsqrt(jnp.mean(h * h, axis=-1, keepdims=True) + RMS_EPS)
        u_ref[...] = (h * r * g_ref[...]).astype(BF16)

    return pl.pallas_call(
        body, name="prep", grid=(t // tm,),
        in_specs=[pl.BlockSpec((tm, d), lambda i: (jnp.maximum(i - 1, 0), 0)),
                  pl.BlockSpec((N_META, d), lambda i: (0, 0)),
                  pl.BlockSpec((1, d), lambda i: (0, 0))],
        out_specs=[pl.BlockSpec((tm, d), lambda i: (i, 0)), pl.BlockSpec((tm, d), lambda i: (i, 0))],
        out_shape=[jax.ShapeDtypeStruct((t, d), F32), jax.ShapeDtypeStruct((t, d), BF16)],
        compiler_params=_cparams(("parallel",)),
    )(x, meta, g1)


def _log_gate(alr, wau, b_alpha):
    t = alr.shape[0]
    dk = wau.shape[1]
    tm = ROW_TILE

    def body(alr_ref, w_ref, b_ref, o_ref):
        z = jnp.dot(alr_ref[...].astype(BF16), w_ref[...].astype(BF16), preferred_element_type=F32) + b_ref[...]
        ls = jnp.minimum(z, 0.0) - jnp.log(1.0 + jnp.exp(-jnp.abs(z)))
        o_ref[...] = ls * (1.0 / GATE_TAU) * _row_mask(pl.program_id(0), tm)

    return pl.pallas_call(
        body, name="log_gate", grid=(t // tm,),
        in_specs=[pl.BlockSpec((tm, RANK_PAD), lambda i: (i, 0)),
                  pl.BlockSpec((RANK_PAD, dk), lambda i: (0, 0)),
                  pl.BlockSpec((1, dk), lambda i: (0, 0))],
        out_specs=pl.BlockSpec((tm, dk), lambda i: (i, 0)),
        out_shape=jax.ShapeDtypeStruct((t, dk), F32),
        compiler_params=_cparams(("parallel",)),
    )(alr, wau, b_alpha)


def _tri(n, upper=False):
    r = lax.broadcasted_iota(jnp.int32, (n, n), 0)
    c = lax.broadcasted_iota(jnp.int32, (n, n), 1)
    return (r <= c) if upper else (r >= c)


_NT = (((1,), (1,)), ((), ()))
_TN = (((0,), (0,)), ((), ()))


def _dot(a, b):
    return jnp.dot(a, b, preferred_element_type=F32)


def _dot_nt(a, b):
    return lax.dot_general(a, b, _NT, preferred_element_type=F32)


def _dot_tn(a, b):
    return lax.dot_general(a, b, _TN, preferred_element_type=F32)


def _chunk_decays(la, tri_f32):
    b = jnp.dot(tri_f32, la, preferred_element_type=F32, precision=lax.Precision.HIGHEST)
    bl = b[CHUNK - 1:CHUNK, :]
    return jnp.exp(b), jnp.exp(bl - b), jnp.exp(-bl), jnp.exp(bl)


def _gla_fwd(p7, log_a, gla_g, d):
    t = p7.shape[0]
    dk_all = d // 2
    dkh, dvh = dk_all // HEADS, d // HEADS
    cb = ROW_TILE
    ncb = cb // CHUNK
    scale = dkh ** -0.5

    def body(qk_ref, v_ref, r_ref, la_ref, g_ref, o_ref, y_ref, s_ref, st_scr):
        @pl.when(pl.program_id(0) == 0)
        def _():
            st_scr[...] = jnp.zeros_like(st_scr)

        tri = _tri(CHUNK)
        tri_f = tri.astype(F32)
        for c in range(ncb):
            rows = slice(c * CHUNK, (c + 1) * CHUNK)
            eb, ekl, ebl_inv, gam = _chunk_decays(la_ref[rows, :], tri_f)
            for h in range(HEADS):
                ks = slice(h * dkh, (h + 1) * dkh)
                vs = slice(h * dvh, (h + 1) * dvh)
                q = qk_ref[rows, ks] * scale
                k = qk_ref[rows, dk_all + h * dkh:dk_all + (h + 1) * dkh]
                v = v_ref[rows, vs].astype(BF16)
                qb = q * eb[:, ks]
                kh = (k * ekl[:, ks]).astype(BF16)
                qc = (qb * ebl_inv[:, ks]).astype(BF16)
                a = jnp.where(tri, _dot_nt(qc, kh), 0.0)
                st = st_scr[h]
                st_b = st.astype(BF16)
                s_ref[c, h] = st_b
                o = _dot_nt(qb.astype(BF16), st_b) + _dot(a.astype(BF16), v)
                st_scr[h] = st * gam[:, ks] + _dot_tn(v, kh)
                o_ref[rows, vs] = o
                rr = lax.rsqrt(jnp.mean(o * o, axis=-1, keepdims=True) + RMS_EPS)
                r = r_ref[rows, vs]
                y_ref[rows, vs] = (o * rr * g_ref[:, vs] * (r * _sigmoid(r))).astype(BF16)

    return pl.pallas_call(
        body, name="gla_fwd", grid=(t // cb,),
        in_specs=[pl.BlockSpec((cb, d), lambda i: (i, 0)),
                  pl.BlockSpec((cb, d), lambda i: (i, 1)),
                  pl.BlockSpec((cb, d), lambda i: (i, 2)),
                  pl.BlockSpec((cb, dk_all), lambda i: (i, 0)),
                  pl.BlockSpec((1, d), lambda i: (0, 0))],
        out_specs=[pl.BlockSpec((cb, d), lambda i: (i, 0)),
                   pl.BlockSpec((cb, d), lambda i: (i, 0)),
                   pl.BlockSpec((ncb, HEADS, dvh, dkh), lambda i: (i, 0, 0, 0))],
        out_shape=[jax.ShapeDtypeStruct((t, d), F32), jax.ShapeDtypeStruct((t, d), BF16),
                   jax.ShapeDtypeStruct((t // CHUNK, HEADS, dvh, dkh), BF16)],
        scratch_shapes=[pltpu.VMEM((HEADS, dvh, dkh), F32)],
        compiler_params=_cparams(("arbitrary",)),
    )(p7, p7, p7, log_a, gla_g)


SUBLANES = 8


def _tap_phases(n_taps, first):
    phases = {}
    for j in range(n_taps):
        e = first + j
        phases.setdefault(e % SUBLANES, []).append((j, e - e % SUBLANES))
    return phases


CONV_ROWS = 64


def _shifted_windows(ext_ref, shf_ref, phases, rows, ls):
    for p, taps in phases.items():
        if p:
            span = max(off for _, off in taps) + rows
            shf_ref[p, 0:span, :] = ext_ref[p:p + span, ls]

    def window(p, start, n):
        return shf_ref[p, start:start + n, :] if p else ext_ref[start:start + n, ls]

    return window


def _conv_strip(ext_ref, shf_ref, w_ref, n_taps, first, rows, ls, emit, reverse=False):
    phases = _tap_phases(n_taps, first)
    window = _shifted_windows(ext_ref, shf_ref, phases, rows, ls)
    for r0 in range(0, rows, CONV_ROWS):
        acc = None
        for p, taps in phases.items():
            for j, off in taps:
                wj = w_ref[(n_taps - 1 - j) if reverse else j, ls]
                term = window(p, off + r0, CONV_ROWS) * wj
                acc = term if acc is None else acc + term
        emit(r0, acc)


def _corr_strip(dl_ref, ext_ref, shf_ref, acc_ref, n_taps, first, rows, ls):
    phases = _tap_phases(n_taps, first)
    window = _shifted_windows(ext_ref, shf_ref, phases, rows, ls)
    for r0 in range(0, rows, CONV_ROWS):
        dl = dl_ref[r0:r0 + CONV_ROWS, ls]
        for p, taps in phases.items():
            for j, off in taps:
                prod = dl * window(p, off + r0, CONV_ROWS)
                acc_ref[SUBLANES * j:SUBLANES * (j + 1), ls] += jnp.sum(
                    prod.reshape(CONV_ROWS // SUBLANES, SUBLANES, prod.shape[-1]), axis=0)


def _conf_fwd(p7, conv_w, conv_b, ln_g, ln_b, d):
    t = p7.shape[0]
    tm = ROW_TILE

    def body(c1_ref, c2_ref, w_ref, b_ref, g_ref, bb_ref, cc_ref, sc_ref, ext, shf):
        @pl.when(pl.program_id(0) == 0)
        def _():
            ext[0:HALO, :] = jnp.zeros((HALO, d), F32)

        ext[HALO:HALO + tm, :] = c1_ref[...] * _sigmoid(c2_ref[...])
        for s0 in range(0, d, LANES):
            ls = slice(s0, s0 + LANES)

            def emit(r0, acc, ls=ls):
                cc_ref[r0:r0 + CONV_ROWS, ls] = acc + b_ref[:, ls]

            _conv_strip(ext, shf, w_ref, CONF_K, HALO - (CONF_K - 1), tm, ls, emit)
        ext[0:HALO, :] = ext[tm:tm + HALO, :]
        g = g_ref[...]
        bb = bb_ref[...]
        rows_per_step = 2 * SUBLANES

        def rows_step(k, carry):
            rs = pl.ds(pl.multiple_of(k * rows_per_step, rows_per_step), rows_per_step)
            cc = cc_ref[rs, :]
            xc = cc - jnp.mean(cc, axis=-1, keepdims=True)
            rstd = lax.rsqrt(jnp.mean(xc * xc, axis=-1, keepdims=True) + LN_EPS)
            cn = xc * rstd * g + bb
            sc_ref[rs, :] = (cn * _sigmoid(cn)).astype(BF16)
            return carry

        lax.fori_loop(0, tm // rows_per_step, rows_step, 0, unroll=4)

    vec = pl.BlockSpec((1, d), lambda i: (0, 0))
    return pl.pallas_call(
        body, name="conf_fwd", grid=(t // tm,),
        in_specs=[pl.BlockSpec((tm, d), lambda i: (i, 3)), pl.BlockSpec((tm, d), lambda i: (i, 4)),
                  pl.BlockSpec((CONF_K, d), lambda i: (0, 0)), vec, vec, vec],
        out_specs=[pl.BlockSpec((tm, d), lambda i: (i, 0)), pl.BlockSpec((tm, d), lambda i: (i, 0))],
        out_shape=[jax.ShapeDtypeStruct((t, d), F32), jax.ShapeDtypeStruct((t, d), BF16)],
        scratch_shapes=[pltpu.VMEM((tm + HALO, d), F32), pltpu.VMEM((SUBLANES, tm + HALO, LANES), F32)],
        compiler_params=_cparams(("arbitrary",)),
    )(p7, p7, conv_w, conv_b, ln_g, ln_b)


def _merge(p7, br_gla, br_conf, d):
    t = p7.shape[0]
    tm = ROW_TILE

    def body(gg_ref, gc_ref, a_ref, b_ref, o_ref):
        o_ref[...] = (_sigmoid(gg_ref[...]) * a_ref[...] + _sigmoid(gc_ref[...]) * b_ref[...]).astype(BF16)

    row = pl.BlockSpec((tm, d), lambda i: (i, 0))
    return pl.pallas_call(
        body, name="merge", grid=(t // tm,),
        in_specs=[pl.BlockSpec((tm, d), lambda i: (i, 5)), pl.BlockSpec((tm, d), lambda i: (i, 6)), row, row],
        out_specs=row, out_shape=jax.ShapeDtypeStruct((t, d), BF16),
        compiler_params=_cparams(("parallel",)),
    )(p7, p7, br_gla, br_conf)


def _rms_fwd(h, g, name):
    t, d = h.shape
    tm = ROW_TILE

    def body(h_ref, g_ref, u_ref):
        x = h_ref[...]
        r = lax.rsqrt(jnp.mean(x * x, axis=-1, keepdims=True) + RMS_EPS)
        u_ref[...] = (x * r * g_ref[...]).astype(BF16)

    return pl.pallas_call(
        body, name=name, grid=(t // tm,),
        in_specs=[pl.BlockSpec((tm, d), lambda i: (i, 0)), pl.BlockSpec((1, d), lambda i: (0, 0))],
        out_specs=pl.BlockSpec((tm, d), lambda i: (i, 0)), out_shape=jax.ShapeDtypeStruct((t, d), BF16),
        compiler_params=_cparams(("parallel",)),
    )(h, g)


def _ffn_mid(up, w, b, dff):
    t = up.shape[0]
    tm = ROW_TILE
    hal = 8

    def body(a_ref, bv_ref, w_ref, b_ref, y_ref, ext, shf):
        i = pl.program_id(0)

        @pl.when(i == 0)
        def _():
            ext[0:hal, :] = jnp.zeros((hal, dff), F32)

        ext[hal:hal + tm, :] = a_ref[...] * _row_mask(i, tm)
        for s0 in range(0, dff, LANES):
            ls = slice(s0, s0 + LANES)

            def emit(r0, acc, ls=ls):
                rb = slice(r0, r0 + CONV_ROWS)
                ac = acc + b_ref[:, ls]
                y_ref[rb, ls] = (ac * _sigmoid(ac) * bv_ref[rb, ls]).astype(BF16)

            _conv_strip(ext, shf, w_ref, FFN_K, hal - (FFN_K - 1), tm, ls, emit)
        ext[0:hal, :] = ext[tm:tm + hal, :]

    return pl.pallas_call(
        body, name="ffn_mid", grid=(t // tm,),
        in_specs=[pl.BlockSpec((tm, dff), lambda i: (i, 0)), pl.BlockSpec((tm, dff), lambda i: (i, 1)),
                  pl.BlockSpec((FFN_K, dff), lambda i: (0, 0)), pl.BlockSpec((1, dff), lambda i: (0, 0))],
        out_specs=pl.BlockSpec((tm, dff), lambda i: (i, 0)), out_shape=jax.ShapeDtypeStruct((t, dff), BF16),
        scratch_shapes=[pltpu.VMEM((tm + hal, dff), F32), pltpu.VMEM((SUBLANES, tm + hal, LANES), F32)],
        compiler_params=_cparams(("arbitrary",)),
    )(up, up, w, b)


def _loss_head(h2, target, gf):
    t, d = h2.shape
    tm = ROW_TILE

    def body(h_ref, tg_ref, g_ref, loss_ref, dg_ref, dh_ref):
        i = pl.program_id(0)

        @pl.when(i == 0)
        def _():
            loss_ref[...] = jnp.zeros_like(loss_ref)
            dg_ref[...] = jnp.zeros_like(dg_ref)
            dh_ref[...] = jnp.zeros_like(dh_ref)

        @pl.when(i > 0)
        def _():
            h = h_ref[...]
            g = g_ref[...]
            r = lax.rsqrt(jnp.mean(h * h, axis=-1, keepdims=True) + RMS_EPS)
            n = h * r
            err = n * g - tg_ref[...]
            loss_ref[...] += 0.5 * jnp.sum(jnp.mean(err * err, axis=-1, keepdims=True), axis=0, keepdims=True)
            dout = err * (1.0 / d)
            dg_ref[...] += jnp.sum(dout * n, axis=0, keepdims=True)
            dn = dout * g
            dh_ref[...] = r * (dn - n * jnp.mean(dn * n, axis=-1, keepdims=True))

    return pl.pallas_call(
        body, name="loss_head", grid=(t // tm,),
        in_specs=[pl.BlockSpec((tm, d), lambda i: (i, 0)),
                  pl.BlockSpec((tm, d), lambda i: (jnp.maximum(i - 1, 0), 0)),
                  pl.BlockSpec((1, d), lambda i: (0, 0))],
        out_specs=[pl.BlockSpec((1, 1), lambda i: (0, 0)), pl.BlockSpec((1, d), lambda i: (0, 0)),
                   pl.BlockSpec((tm, d), lambda i: (i, 0))],
        out_shape=[jax.ShapeDtypeStruct((1, 1), F32), jax.ShapeDtypeStruct((1, d), F32),
                   jax.ShapeDtypeStruct((t, d), F32)],
        compiler_params=_cparams(("arbitrary",)),
    )(h2, target, gf)


def _dsilu(x, sig):
    return sig * (1.0 + x * (1.0 - sig))


def _ffn_mid_bwd(up, dy, w, b, dff):
    t = up.shape[0]
    tm = ROW_TILE
    hal = 8
    nt = t // tm

    def body(a_ref, ap_ref, bv_ref, dy_ref, w_ref, b_ref, dup_ref, dw_ref, db_ref, ext, dext, dw_acc, db_acc, shf):
        i = pl.program_id(0)
        tile = nt - 1 - i

        @pl.when(i == 0)
        def _():
            dext[tm:tm + hal, :] = jnp.zeros((hal, dff), F32)
            dw_acc[...] = jnp.zeros_like(dw_acc)
            db_acc[...] = jnp.zeros_like(db_acc)

        ext[0:hal, :] = ap_ref[...] * jnp.where(tile > 0, 1.0, 0.0)
        ext[hal:hal + tm, :] = a_ref[...] * _row_mask(tile, tm)
        first = hal - (FFN_K - 1)
        for s0 in range(0, dff, LANES):
            ls = slice(s0, s0 + LANES)

            def emit_fwd(r0, acc, ls=ls, s0=s0):
                rb = slice(r0, r0 + CONV_ROWS)
                ac = acc + b_ref[:, ls]
                sig = _sigmoid(ac)
                dyv = dy_ref[rb, ls]
                dup_ref[rb, dff + s0:dff + s0 + LANES] = (dyv * ac * sig).astype(BF16)
                dac = dyv * bv_ref[rb, ls] * _dsilu(ac, sig)
                dext[rb, ls] = dac
                db_acc[:, ls] += jnp.sum(dac.reshape(CONV_ROWS // SUBLANES, SUBLANES, LANES), axis=0)

            _conv_strip(ext, shf, w_ref, FFN_K, first, tm, ls, emit_fwd)
            _corr_strip(dext, ext, shf, dw_acc, FFN_K, first, tm, ls)

            def emit_bwd(r0, da, ls=ls):
                rb = slice(r0, r0 + CONV_ROWS)
                mask = ((tile * tm + r0 + lax.broadcasted_iota(jnp.int32, (CONV_ROWS, 1), 0)) >= PAD).astype(F32)
                dup_ref[rb, ls] = (da * mask).astype(BF16)

            _conv_strip(dext, shf, w_ref, FFN_K, 0, tm, ls, emit_bwd, reverse=True)
        dext[tm:tm + hal, :] = dext[0:hal, :]

        @pl.when(i == nt - 1)
        def _():
            db_ref[...] = jnp.sum(db_acc[...], axis=0, keepdims=True)
            for j in range(FFN_K):
                dw_ref[j:j + 1, :] = jnp.sum(dw_acc[SUBLANES * j:SUBLANES * (j + 1), :], axis=0, keepdims=True)

    rev = lambda i: (nt - 1 - i, 0)
    return pl.pallas_call(
        body, name="ffn_mid_bwd", grid=(nt,),
        in_specs=[pl.BlockSpec((tm, dff), rev),
                  pl.BlockSpec((hal, dff), lambda i: (jnp.maximum((nt - 1 - i) * (tm // hal) - 1, 0), 0)),
                  pl.BlockSpec((tm, dff), lambda i: (nt - 1 - i, 1)),
                  pl.BlockSpec((tm, dff), rev),
                  pl.BlockSpec((FFN_K, dff), lambda i: (0, 0)), pl.BlockSpec((1, dff), lambda i: (0, 0))],
        out_specs=[pl.BlockSpec((tm, 2 * dff), rev),
                   pl.BlockSpec((FFN_K, dff), lambda i: (0, 0)), pl.BlockSpec((1, dff), lambda i: (0, 0))],
        out_shape=[jax.ShapeDtypeStruct((t, 2 * dff), BF16),
                   jax.ShapeDtypeStruct((FFN_K, dff), F32), jax.ShapeDtypeStruct((1, dff), F32)],
        scratch_shapes=[pltpu.VMEM((tm + hal, dff), F32), pltpu.VMEM((tm + hal, dff), F32),
                        pltpu.VMEM((SUBLANES * FFN_K, dff), F32), pltpu.VMEM((SUBLANES, dff), F32),
                        pltpu.VMEM((SUBLANES, tm + hal, LANES), F32)],
        compiler_params=_cparams(("arbitrary",)),
    )(up, up, up, dy, w, b)


def _rms_bwd(du, h, g, dres, name):
    t, d = h.shape
    tm = ROW_TILE

    def body(du_ref, h_ref, g_ref, dres_ref, dh_ref, dg_ref):
        @pl.when(pl.program_id(0) == 0)
        def _():
            dg_ref[...] = jnp.zeros_like(dg_ref)

        x = h_ref[...]
        r = lax.rsqrt(jnp.mean(x * x, axis=-1, keepdims=True) + RMS_EPS)
        n = x * r
        du_ = du_ref[...]
        dg_ref[...] += jnp.sum(du_ * n, axis=0, keepdims=True)
        dn = du_ * g_ref[...]
        dh_ref[...] = dres_ref[...] + r * (dn - n * jnp.mean(dn * n, axis=-1, keepdims=True))

    row = pl.BlockSpec((tm, d), lambda i: (i, 0))
    vec = pl.BlockSpec((1, d), lambda i: (0, 0))
    return pl.pallas_call(
        body, name=name, grid=(t // tm,),
        in_specs=[row, row, vec, row], out_specs=[row, vec],
        out_shape=[jax.ShapeDtypeStruct((t, d), F32), jax.ShapeDtypeStruct((1, d), F32)],
        compiler_params=_cparams(("arbitrary",)),
    )(du, h, g, dres)


def _merge_bwd(p7, br_gla, br_conf, dmerged, d):
    t = p7.shape[0]
    tm = ROW_TILE

    def body(gg_ref, gc_ref, a_ref, b_ref, dm_ref, da_ref, db_ref, dgg_ref, dgc_ref):
        dm = dm_ref[...]
        sg = _sigmoid(gg_ref[...])
        sc = _sigmoid(gc_ref[...])
        da_ref[...] = (dm * sg).astype(BF16)
        db_ref[...] = (dm * sc).astype(BF16)
        dgg_ref[...] = (dm * a_ref[...] * sg * (1.0 - sg)).astype(BF16)
        dgc_ref[...] = (dm * b_ref[...] * sc * (1.0 - sc)).astype(BF16)

    row = pl.BlockSpec((tm, d), lambda i: (i, 0))
    outs = pl.pallas_call(
        body, name="merge_bwd", grid=(t // tm,),
        in_specs=[pl.BlockSpec((tm, d), lambda i: (i, 5)), pl.BlockSpec((tm, d), lambda i: (i, 6)), row, row, row],
        out_specs=[row, row, row, row],
        out_shape=[jax.ShapeDtypeStruct((t, d), BF16)] * 4,
        compiler_params=_cparams(("parallel",)),
    )(p7, p7, br_gla, br_conf, dmerged)
    return outs


def _conf_bwd(p7, cc, dsc, conv_w, ln_g, ln_b, d):
    t = p7.shape[0]
    tm = ROW_TILE
    nt = t // tm

    def body(c1_ref, c2_ref, c1p_ref, c2p_ref, cc_ref, dsc_ref, w_ref, g_ref, bb_ref,
             dc1_ref, dc2_ref, dw_ref, db_ref, dg_ref, dbb_ref, ext, dext, dw_acc, shf):
        i = pl.program_id(0)
        tile = nt - 1 - i

        @pl.when(i == 0)
        def _():
            dext[tm:tm + HALO, :] = jnp.zeros((HALO, d), F32)
            dw_acc[...] = jnp.zeros_like(dw_acc)
            db_ref[...] = jnp.zeros_like(db_ref)
            dg_ref[...] = jnp.zeros_like(dg_ref)
            dbb_ref[...] = jnp.zeros_like(dbb_ref)

        ext[0:HALO, :] = c1p_ref[...] * _sigmoid(c2p_ref[...]) * jnp.where(tile > 0, 1.0, 0.0)
        g = g_ref[...]
        bb = bb_ref[...]

        groups = 4

        def rows_step(k, sums):
            sdg, sdbb, sdb = sums
            for u in range(groups):
                r0 = (k * groups + u) * SUBLANES
                rs = pl.ds(pl.multiple_of(r0, SUBLANES), SUBLANES)
                ext[pl.ds(pl.multiple_of(HALO + r0, SUBLANES), SUBLANES), :] = c1_ref[rs, :] * _sigmoid(c2_ref[rs, :])
                cc_ = cc_ref[rs, :]
                xc = cc_ - jnp.mean(cc_, axis=-1, keepdims=True)
                rstd = lax.rsqrt(jnp.mean(xc * xc, axis=-1, keepdims=True) + LN_EPS)
                xh = xc * rstd
                cn = xh * g + bb
                dcn = dsc_ref[rs, :] * _dsilu(cn, _sigmoid(cn))
                dxh = dcn * g
                dcc = rstd * (dxh - jnp.mean(dxh, axis=-1, keepdims=True)
                              - xh * jnp.mean(dxh * xh, axis=-1, keepdims=True))
                dext[rs, :] = dcc
                sdg, sdbb, sdb = sdg + dcn * xh, sdbb + dcn, sdb + dcc
            return sdg, sdbb, sdb

        zero = jnp.zeros((SUBLANES, d), F32)
        sdg, sdbb, sdb = lax.fori_loop(0, tm // (groups * SUBLANES), rows_step, (zero, zero, zero))
        dg_ref[...] += jnp.sum(sdg, axis=0, keepdims=True)
        dbb_ref[...] += jnp.sum(sdbb, axis=0, keepdims=True)
        db_ref[...] += jnp.sum(sdb, axis=0, keepdims=True)
        for s0 in range(0, d, LANES):
            ls = slice(s0, s0 + LANES)
            _corr_strip(dext, ext, shf, dw_acc, CONF_K, HALO - (CONF_K - 1), tm, ls)

            def emit(r0, dc, ls=ls):
                rb = slice(r0, r0 + CONV_ROWS)
                s2l = _sigmoid(c2_ref[rb, ls])
                dc1_ref[rb, ls] = (dc * s2l).astype(BF16)
                dc2_ref[rb, ls] = (dc * c1_ref[rb, ls] * s2l * (1.0 - s2l)).astype(BF16)

            _conv_strip(dext, shf, w_ref, CONF_K, 0, tm, ls, emit, reverse=True)
        dext[tm:tm + HALO, :] = dext[0:HALO, :]

        @pl.when(i == nt - 1)
        def _():
            for j in range(CONF_K):
                dw_ref[j:j + 1, :] = jnp.sum(dw_acc[SUBLANES * j:SUBLANES * (j + 1), :], axis=0, keepdims=True)

    rev = lambda i: (nt - 1 - i, 0)
    prev = lambda col: (lambda i: (jnp.maximum((nt - 1 - i) * (tm // HALO) - 1, 0), col))
    vec = pl.BlockSpec((1, d), lambda i: (0, 0))
    return pl.pallas_call(
        body, name="conf_bwd", grid=(nt,),
        in_specs=[pl.BlockSpec((tm, d), lambda i: (nt - 1 - i, 3)), pl.BlockSpec((tm, d), lambda i: (nt - 1 - i, 4)),
                  pl.BlockSpec((HALO, d), prev(3)), pl.BlockSpec((HALO, d), prev(4)),
                  pl.BlockSpec((tm, d), rev), pl.BlockSpec((tm, d), rev),
                  pl.BlockSpec((CONF_K, d), lambda i: (0, 0)), vec, vec],
        out_specs=[pl.BlockSpec((tm, d), rev), pl.BlockSpec((tm, d), rev),
                   pl.BlockSpec((CONF_K, d), lambda i: (0, 0)), vec, vec, vec],
        out_shape=[jax.ShapeDtypeStruct((t, d), BF16), jax.ShapeDtypeStruct((t, d), BF16),
                   jax.ShapeDtypeStruct((CONF_K, d), F32)] + [jax.ShapeDtypeStruct((1, d), F32)] * 3,
        scratch_shapes=[pltpu.VMEM((tm + HALO, d), F32), pltpu.VMEM((tm + HALO, d), F32),
                        pltpu.VMEM((SUBLANES * CONF_K, d), F32), pltpu.VMEM((SUBLANES, tm + HALO, LANES), F32)],
        compiler_params=_cparams(("arbitrary",)),
    )(p7, p7, p7, p7, cc, dsc, conv_w, ln_g, ln_b)


def _gla_bwd(p7, log_a, alr, wau, b_alpha, gla_g, o, states, dy, d):
    t = p7.shape[0]
    dk_all = d // 2
    dkh, dvh = dk_all // HEADS, d // HEADS
    cb = ROW_TILE
    ncb = cb // CHUNK
    nb = t // cb
    scale = dkh ** -0.5

    def body(qk_ref, v_ref, r_ref, la_ref, alr_ref, wau_ref, ba_ref, g_ref, o_ref, s_ref, dy_ref,
             dqk_ref, dv_ref, dr_ref, dz_ref, dg_ref, dba_ref, dst_scr, dla_scr):
        i = pl.program_id(0)
        blk = nb - 1 - i

        @pl.when(i == 0)
        def _():
            dst_scr[...] = jnp.zeros_like(dst_scr)
            dg_ref[...] = jnp.zeros_like(dg_ref)
            dba_ref[...] = jnp.zeros_like(dba_ref)

        tri = _tri(CHUNK)
        tri_f = tri.astype(F32)
        triu_f = _tri(CHUNK, upper=True).astype(F32)
        for c in reversed(range(ncb)):
            rows = slice(c * CHUNK, (c + 1) * CHUNK)
            eb, ekl, ebl_inv, gam = _chunk_decays(la_ref[rows, :], tri_f)
            for h in range(HEADS):
                ks = slice(h * dkh, (h + 1) * dkh)
                kcols = slice(dk_all + h * dkh, dk_all + (h + 1) * dkh)
                vs = slice(h * dvh, (h + 1) * dvh)
                q = qk_ref[rows, ks] * scale
                k = qk_ref[rows, kcols]
                v = v_ref[rows, vs].astype(BF16)
                ebh, eklh, eih, gamh = eb[:, ks], ekl[:, ks], ebl_inv[:, ks], gam[:, ks]
                qb = q * ebh
                kh = k * eklh
                qc = qb * eih
                qb_b, kh_b, qc_b = qb.astype(BF16), kh.astype(BF16), qc.astype(BF16)
                ov = o_ref[rows, vs]
                r = r_ref[rows, vs]
                dyv = dy_ref[rows, vs]
                sig = _sigmoid(r)
                rr = lax.rsqrt(jnp.mean(ov * ov, axis=-1, keepdims=True) + RMS_EPS)
                n = ov * rr
                g = g_ref[:, vs]
                dr_ref[rows, vs] = (dyv * n * g * _dsilu(r, sig)).astype(BF16)
                don = dyv * (r * sig)
                dg_ref[:, vs] += jnp.sum(don * n, axis=0, keepdims=True)
                dn = don * g
                do = (rr * (dn - n * jnp.mean(dn * n, axis=-1, keepdims=True))).astype(BF16)
                st_b = s_ref[c, h]
                a = jnp.where(tri, _dot_nt(qc_b, kh_b), 0.0).astype(BF16)
                da = jnp.where(tri, _dot_nt(do, v), 0.0).astype(BF16)
                dst = dst_scr[h]
                dst_b = dst.astype(BF16)
                dv_ref[rows, vs] = (_dot_tn(a, do) + _dot_nt(kh_b, dst_b)).astype(BF16)
                dqb = _dot(do, st_b)
                dqc = _dot(da, kh_b)
                dkh_ = _dot_tn(da, qc_b) + _dot(v, dst_b)
                dgam = jnp.sum(st_b.astype(F32) * dst, axis=0, keepdims=True)
                dst_scr[h] = dst * gamh + _dot_tn(do, qb_b)
                dqk_ref[rows, ks] = ((dqb * ebh + dqc * (ebh * eih)) * scale).astype(BF16)
                dqk_ref[rows, kcols] = (dkh_ * eklh).astype(BF16)
                qq = dqc * qc
                kk = dkh_ * kh
                db = dqb * qb + qq - kk
                dbl = jnp.sum(kk - qq, axis=0, keepdims=True) + dgam * gamh
                dla_scr[rows, ks] = jnp.dot(triu_f, db, preferred_element_type=F32,
                                            precision=lax.Precision.HIGHEST) + dbl
        z = jnp.dot(alr_ref[...].astype(BF16), wau_ref[...].astype(BF16), preferred_element_type=F32) + ba_ref[...]
        dz = dla_scr[...] * (1.0 / GATE_TAU) * _sigmoid(-z) * _row_mask(blk, cb)
        dba_ref[...] += jnp.sum(dz, axis=0, keepdims=True)
        dz_ref[...] = dz.astype(BF16)

    rev = lambda i: (nb - 1 - i, 0)
    row = pl.BlockSpec((cb, d), rev)
    return pl.pallas_call(
        body, name="gla_bwd", grid=(nb,),
        in_specs=[row, pl.BlockSpec((cb, d), lambda i: (nb - 1 - i, 1)), pl.BlockSpec((cb, d), lambda i: (nb - 1 - i, 2)),
                  pl.BlockSpec((cb, dk_all), rev), pl.BlockSpec((cb, RANK_PAD), rev),
                  pl.BlockSpec((RANK_PAD, dk_all), lambda i: (0, 0)), pl.BlockSpec((1, dk_all), lambda i: (0, 0)),
                  pl.BlockSpec((1, d), lambda i: (0, 0)), row,
                  pl.BlockSpec((ncb, HEADS, dvh, dkh), lambda i: (nb - 1 - i, 0, 0, 0)), row],
        out_specs=[row, row, row, pl.BlockSpec((cb, dk_all), rev),
                   pl.BlockSpec((1, d), lambda i: (0, 0)), pl.BlockSpec((1, dk_all), lambda i: (0, 0))],
        out_shape=[jax.ShapeDtypeStruct((t, d), BF16)] * 3 + [jax.ShapeDtypeStruct((t, dk_all), BF16),
                   jax.ShapeDtypeStruct((1, d), F32), jax.ShapeDtypeStruct((1, dk_all), F32)],
        scratch_shapes=[pltpu.VMEM((HEADS, dvh, dkh), F32), pltpu.VMEM((cb, dk_all), F32)],
        compiler_params=_cparams(("arbitrary",)),
    )(p7, p7, p7, log_a, alr, wau, b_alpha, gla_g, o, states, dy)


def _input_grad(du_a, du_b, h0, g1, dh1):
    t, d = h0.shape
    tm = ROW_TILE
    s = t - HEAD_ROWS

    def body(dua_ref, dub_ref, h_ref, g_ref, dres_ref, gx_ref, gm_ref, dg_ref):
        i = pl.program_id(0)

        @pl.when(i == 0)
        def _():
            dg_ref[...] = jnp.zeros_like(dg_ref)

        x = h_ref[...]
        r = lax.rsqrt(jnp.mean(x * x, axis=-1, keepdims=True) + RMS_EPS)
        n = x * r
        du_ = dua_ref[...] + dub_ref[...]
        dg_ref[...] += jnp.sum(du_ * n, axis=0, keepdims=True)
        dn = du_ * g_ref[...]
        dh = dres_ref[...] + r * (dn - n * jnp.mean(dn * n, axis=-1, keepdims=True))

        @pl.when(i == 0)
        def _():
            gm_ref[...] = dh[PAD:HEAD_ROWS, :]
            gx_ref[...] = jnp.zeros_like(gx_ref)

        @pl.when(i > 0)
        def _():
            gx_ref[...] = dh

    row = pl.BlockSpec((tm, d), lambda i: (i, 0))
    vec = pl.BlockSpec((1, d), lambda i: (0, 0))
    return pl.pallas_call(
        body, name="input_grad", grid=(t // tm,),
        in_specs=[row, row, row, vec, row],
        out_specs=[pl.BlockSpec((tm, d), lambda i: (jnp.maximum(i - 1, 0), 0)),
                   pl.BlockSpec((N_META, d), lambda i: (0, 0)), vec],
        out_shape=[jax.ShapeDtypeStruct((s, d), F32), jax.ShapeDtypeStruct((N_META, d), F32),
                   jax.ShapeDtypeStruct((1, d), F32)],
        compiler_params=_cparams(("arbitrary",)),
    )(du_a, du_b, h0, g1, dh1)


def _local_step(x, target, w):
    s, d = x.shape
    dk_all = d // 2
    dff = w["w_down"].shape[0]
    w_in = w["w_in"]
    lo, hi = 3 * d, 3 * d + GLA_RANK
    wq = jnp.concatenate([w_in[:, :lo], w_in[:, hi:]], axis=1)
    w_alr = jnp.pad(w_in[:, lo:hi], ((0, 0), (0, RANK_PAD - GLA_RANK)))
    wau = jnp.pad(w["w_alpha_up"], ((0, RANK_PAD - GLA_RANK), (0, 0)))

    h0, u1 = _prep(x, w["meta_tokens"], w["norm_mix_g"])
    p7 = _matmul(u1, wq, dims="nn", name="proj")
    alr = _matmul(u1, w_alr, dims="nn", name="proj_alr")
    log_a = _log_gate(alr, wau, w["b_alpha"])
    o, y_gla, states = _gla_fwd(p7, log_a, w["gla_norm_g"], d)
    br_gla = _matmul(y_gla, w["w_gla_o"], dims="nn", name="gla_out")
    cc, s_c = _conf_fwd(p7, w["conf_dw_w"], w["conf_dw_b"], w["conf_ln_g"], w["conf_ln_b"], d)
    br_conf = _matmul(s_c, w["w_conf_o"], dims="nn", name="conf_out")
    merged = _merge(p7, br_gla, br_conf, d)
    h1 = _matmul(merged, w["w_out"], dims="nn", name="mix_out", add=h0)
    u2 = _rms_fwd(h1, w["norm_ffn_g"], "norm_ffn")
    up = _matmul(u2, w["w_up"], dims="nn", name="ffn_up", chips="b")
    y = _ffn_mid(up, w["ffn_dw_w"], w["ffn_dw_b"], dff)
    h2 = _matmul(y, w["w_down"], dims="nn", name="ffn_down", add=h1)
    loss, d_gf, dh2 = _loss_head(h2, target, w["final_norm_g"])

    g = {"final_norm_g": d_gf}
    dy = _matmul(dh2, w["w_down"], dims="nt", name="d_ffn_y")
    g["w_down"] = _matmul(y, dh2, dims="tn", name="dw_down", out_dtype=BF16)
    dup, g["ffn_dw_w"], g["ffn_dw_b"] = _ffn_mid_bwd(up, dy, w["ffn_dw_w"], w["ffn_dw_b"], dff)
    du2 = _matmul(dup, w["w_up"], dims="nt", name="d_u2", chips="b")
    g["w_up"] = _matmul(u2, dup, dims="tn", name="dw_up", out_dtype=BF16, chips="out")
    dh1, g["norm_ffn_g"] = _rms_bwd(du2, h1, w["norm_ffn_g"], dh2, "norm_ffn_bwd")
    dmerged = _matmul(dh1, w["w_out"], dims="nt", name="d_merged")
    g["w_out"] = _matmul(merged, dh1, dims="tn", name="dw_out", out_dtype=BF16)
    d_br_gla, d_br_conf, dgg, dgc = _merge_bwd(p7, br_gla, br_conf, dmerged, d)
    dsc = _matmul(d_br_conf, w["w_conf_o"], dims="nt", name="d_conf_s")
    g["w_conf_o"] = _matmul(s_c, d_br_conf, dims="tn", name="dw_conf_o", out_dtype=BF16)
    dc1, dc2, g["conf_dw_w"], g["conf_dw_b"], g["conf_ln_g"], g["conf_ln_b"] = _conf_bwd(
        p7, cc, dsc, w["conf_dw_w"], w["conf_ln_g"], w["conf_ln_b"], d)
    dyg = _matmul(d_br_gla, w["w_gla_o"], dims="nt", name="d_gla_y")
    g["w_gla_o"] = _matmul(y_gla, d_br_gla, dims="tn", name="dw_gla_o", out_dtype=BF16)
    dqk, dv, dr, dz, g["gla_norm_g"], g["b_alpha"] = _gla_bwd(
        p7, log_a, alr, wau, w["b_alpha"], w["gla_norm_g"], o, states, dyg, d)
    dalr = _matmul(dz, wau, dims="nt", name="d_alr", out_dtype=BF16)
    g["w_alpha_up"] = _matmul(alr, dz, dims="tn", name="dw_alpha_up")[:GLA_RANK]
    dp7 = jnp.concatenate([dqk, dv, dr, dc1, dc2, dgg, dgc], axis=1)
    du1a = _matmul(dp7, wq, dims="nt", name="d_u1")
    du1b = _matmul(dalr, w_alr, dims="nt", name="d_u1_alr")
    dwq = _matmul(u1, dp7, dims="tn", name="dw_in", out_dtype=BF16)
    dw_alr = _matmul(u1, dalr, dims="tn", name="dw_in_alr", out_dtype=BF16)
    g["w_in"] = _to_chip_major(jnp.concatenate([dwq[:, :lo], dw_alr[:, :GLA_RANK], dwq[:, lo:]], axis=1))
    grad_x, g["meta_tokens"], g["norm_mix_g"] = _input_grad(du1a, du1b, h0, w["norm_mix_g"], dh1)
    return loss, grad_x, g


HBM_SPEC = pl.BlockSpec(memory_space=pltpu.HBM)
FLIPS = ((1, 0), (0, 1), (1, 1))


def _place():
    x, y, c = lax.axis_index("x"), lax.axis_index("y"), lax.axis_index("c")
    return x, y, c


def _half_rows(ref, h, lead=()):
    rh = ref.shape[-2] // 2
    return ref.at[(*lead, pl.ds(pl.multiple_of(h * rh, 2 * SUBLANES), rh), slice(None))]


def _gather_big(shards):
    n = len(shards)

    def body(*refs):
        ins, outs = refs[:n], refs[n:2 * n]
        send_sems, recv_sems, local_sems = refs[2 * n:]
        x, y, c = _place()
        me = 2 * x + y
        chips = [(x ^ fx, y ^ fy) for fx, fy in FLIPS]
        own = [pltpu.make_async_copy(ins[k], outs[k].at[me], local_sems.at[k]) for k in range(n)]
        for cp in own:
            cp.start()

        def copy(k, sem, chip, h, to, src=None):
            slot = _half_rows(outs[k], h, lead=(2 * chip[0] + chip[1],))
            return pltpu.make_async_remote_copy(src_ref=slot if src is None else src, dst_ref=slot,
                                                send_sem=send_sems.at[sem], recv_sem=recv_sems.at[sem],
                                                device_id=to, device_id_type=MESH)

        first = [copy(k, 3 * k + j, (x, y), c, (*chip, c), src=_half_rows(ins[k], c))
                 for k in range(n) for j, chip in enumerate(chips)]
        for cp in first:
            cp.start()
        passed = []
        for k in range(n):
            for j, chip in enumerate(chips):
                copy(k, 3 * k + j, chip, c, (x, y, c)).wait_recv()
                fwd = copy(k, 3 * n + 3 * k + j, chip, c, (x, y, 1 - c))
                fwd.start()
                passed.append(fwd)
        for k in range(n):
            for j, chip in enumerate(chips):
                copy(k, 3 * n + 3 * k + j, chip, 1 - c, (x, y, c)).wait_recv()
        for cp in first + passed:
            cp.wait_send()
        for cp in own:
            cp.wait()

    return pl.pallas_call(
        body, name="gather_weights",
        in_specs=[HBM_SPEC] * n, out_specs=[HBM_SPEC] * n,
        out_shape=[jax.ShapeDtypeStruct((N_CHIPS, *s.shape), s.dtype) for s in shards],
        scratch_shapes=[pltpu.SemaphoreType.DMA((6 * n,)), pltpu.SemaphoreType.DMA((6 * n,)),
                        pltpu.SemaphoreType.DMA((n,))],
    )(*shards)


def _swap_core_halves(grads):
    n = len(grads)

    def body(*refs):
        ins, outs = refs[:n], refs[n:2 * n]
        send_sems, recv_sems = refs[2 * n:]
        x, y, c = _place()
        cps = [pltpu.make_async_remote_copy(
            src_ref=_half_rows(ins[k], 1 - c, lead=(slice(None),)), dst_ref=outs[k], send_sem=send_sems.at[k],
            recv_sem=recv_sems.at[k], device_id=(x, y, 1 - c), device_id_type=MESH) for k in range(n)]
        for cp in cps:
            cp.start()
        for cp in cps:
            cp.wait()

    return pl.pallas_call(
        body, name="swap_core_halves", in_specs=[HBM_SPEC] * n, out_specs=[HBM_SPEC] * n,
        out_shape=[jax.ShapeDtypeStruct((g.shape[0], g.shape[1] // 2, g.shape[2]), g.dtype) for g in grads],
        scratch_shapes=[pltpu.SemaphoreType.DMA((n,)), pltpu.SemaphoreType.DMA((n,))],
    )(*grads)


def _exchange_chip_partials(parts):
    n = len(parts)

    def body(*refs):
        ins, outs = refs[:n], refs[n:2 * n]
        send_sems, recv_sems = refs[2 * n:]
        x, y, c = _place()
        cps = []
        for k in range(n):
            for j, (fx, fy) in enumerate(FLIPS):
                tx, ty = x ^ fx, y ^ fy
                cps.append(pltpu.make_async_remote_copy(
                    src_ref=ins[k].at[2 * tx + ty], dst_ref=outs[k].at[j], send_sem=send_sems.at[3 * k + j],
                    recv_sem=recv_sems.at[3 * k + j], device_id=(tx, ty, c), device_id_type=MESH))
        for cp in cps:
            cp.start()
        for cp in cps:
            cp.wait()

    return pl.pallas_call(
        body, name="exchange_chip_partials", in_specs=[HBM_SPEC] * n, out_specs=[HBM_SPEC] * n,
        out_shape=[jax.ShapeDtypeStruct((3, *p.shape[1:]), p.dtype) for p in parts],
        scratch_shapes=[pltpu.SemaphoreType.DMA((3 * n,)), pltpu.SemaphoreType.DMA((3 * n,))],
    )(*parts)


def _join_core_halves(halves):
    n = len(halves)

    def body(*refs):
        ins, outs = refs[:n], refs[n:2 * n]
        send_sems, recv_sems, local_sems = refs[2 * n:]
        x, y, c = _place()
        own = [pltpu.make_async_copy(ins[k], _half_rows(outs[k], c), local_sems.at[k]) for k in range(n)]
        cps = [pltpu.make_async_remote_copy(
            src_ref=ins[k], dst_ref=_half_rows(outs[k], c), send_sem=send_sems.at[k],
            recv_sem=recv_sems.at[k], device_id=(x, y, 1 - c), device_id_type=MESH) for k in range(n)]
        for cp in own + cps:
            cp.start()
        for k in range(n):
            cps[k].wait_send()
            pltpu.make_async_remote_copy(
                src_ref=ins[k], dst_ref=_half_rows(outs[k], 1 - c), send_sem=send_sems.at[k],
                recv_sem=recv_sems.at[k], device_id=(x, y, 1 - c), device_id_type=MESH).wait_recv()
        for cp in own:
            cp.wait()

    return pl.pallas_call(
        body, name="join_core_halves", in_specs=[HBM_SPEC] * n, out_specs=[HBM_SPEC] * n,
        out_shape=[jax.ShapeDtypeStruct((2 * h.shape[0], h.shape[1]), h.dtype) for h in halves],
        scratch_shapes=[pltpu.SemaphoreType.DMA((n,)), pltpu.SemaphoreType.DMA((n,)), pltpu.SemaphoreType.DMA((n,))],
    )(*halves)


def _gather_small(block, name):
    m, n = block.shape

    def body(x_ref, out_ref, send_sems, recv_sems, local_sem):
        x, y, c = _place()
        me, sibling = (x, y, c), (x, y, 1 - c)
        chips = [(x ^ fx, y ^ fy) for fx, fy in FLIPS]

        def rows(px, py, pc):
            return out_ref.at[pl.ds((4 * px + 2 * py + pc) * m, m), :]

        def copy(k, blk, to, src=None):
            return pltpu.make_async_remote_copy(
                src_ref=rows(*blk) if src is None else src, dst_ref=rows(*blk),
                send_sem=send_sems.at[k], recv_sem=recv_sems.at[k], device_id=to, device_id_type=MESH)

        mine = pltpu.make_async_copy(x_ref, rows(*me), local_sem)
        mine.start()
        first = [copy(0, me, sibling, src=x_ref)]
        first += [copy(1 + j, me, (*chip, c), src=x_ref) for j, chip in enumerate(chips)]
        for cp in first:
            cp.start()
        passed = [copy(4 + j, (*chip, c), sibling) for j, chip in enumerate(chips)]
        for j, chip in enumerate(chips):
            copy(1 + j, (*chip, c), me).wait_recv()
            passed[j].start()
        copy(0, sibling, me).wait_recv()
        for j, chip in enumerate(chips):
            copy(4 + j, (*chip, 1 - c), me).wait_recv()
        for cp in first + passed:
            cp.wait_send()
        mine.wait()

    out = pl.pallas_call(
        body, name=name,
        out_shape=jax.ShapeDtypeStruct((8 * m, n), block.dtype),
        in_specs=[pl.BlockSpec(memory_space=pltpu.VMEM)],
        out_specs=pl.BlockSpec(memory_space=pltpu.VMEM),
        scratch_shapes=[pltpu.SemaphoreType.DMA((7,)), pltpu.SemaphoreType.DMA((7,)), pltpu.SemaphoreType.DMA],
    )(block)
    return out.reshape(8, m, n)


def _add_core_halves(core, grad, from_sibling, name):
    nc, r, cols = grad.shape
    rh = r // 2

    def body(core_ref, g_ref, s_ref, o_ref):
        o_ref[...] = (g_ref[...].astype(F32) + s_ref[...].astype(F32)).astype(BF16)

    spec = pl.BlockSpec((1, rh, cols), lambda t, core_ref: (t, 0, 0))
    return pl.pallas_call(
        body, name=name,
        grid_spec=pltpu.PrefetchScalarGridSpec(
            num_scalar_prefetch=1, grid=(nc,),
            in_specs=[pl.BlockSpec((1, rh, cols), lambda t, core_ref: (t, core_ref[0], 0)), spec], out_specs=spec),
        out_shape=jax.ShapeDtypeStruct((nc, rh, cols), BF16),
        compiler_params=_cparams(("parallel",)),
    )(core, grad, from_sibling)


def _sum_chip_partials(chip, parts, others, name):
    _, rh, cols = parts.shape
    tr = _pick(rh, (128, 176, 64, 32, 16, 8))

    def body(chip_ref, a_ref, b_ref, o_ref):
        acc = a_ref[0].astype(F32)
        for j in range(3):
            acc = acc + b_ref[j].astype(F32)
        o_ref[...] = acc

    return pl.pallas_call(
        body, name=name,
        grid_spec=pltpu.PrefetchScalarGridSpec(
            num_scalar_prefetch=1, grid=(rh // tr,),
            in_specs=[pl.BlockSpec((1, tr, cols), lambda i, chip_ref: (chip_ref[0], i, 0)),
                      pl.BlockSpec((3, tr, cols), lambda i, chip_ref: (0, i, 0))],
            out_specs=pl.BlockSpec((tr, cols), lambda i, chip_ref: (i, 0))),
        out_shape=jax.ShapeDtypeStruct((rh, cols), F32),
        compiler_params=_cparams(("parallel",)),
    )(chip, parts, others)


def _sum_devices(blocks):
    n, m, _ = blocks.shape

    def body(b_ref, o_ref):
        acc = b_ref[0]
        for j in range(1, n):
            acc = acc + b_ref[j]
        o_ref[...] = acc

    return pl.pallas_call(
        body, name="sum_devices", out_shape=jax.ShapeDtypeStruct((m, LANES), F32),
        in_specs=[pl.BlockSpec(memory_space=pltpu.VMEM)], out_specs=pl.BlockSpec(memory_space=pltpu.VMEM),
    )(blocks)


def _adamw(w, g, m, v, name):
    rws, cols = w.shape
    tr = rws
    for cand in (256, 128, 64, 32, 16, 8):
        if rws % cand == 0 and cand * cols * 4 <= 2 * 1024 * 1024:
            tr = cand
            break
    c1 = 1.0 - ADAM_B1 ** ADAM_STEP
    c2 = 1.0 - ADAM_B2 ** ADAM_STEP

    def body(w_ref, g_ref, m_ref, v_ref, d_ref, nm_ref, nv_ref):
        gv = g_ref[...]
        nm = ADAM_B1 * m_ref[...] + (1.0 - ADAM_B1) * gv
        nv = ADAM_B2 * v_ref[...] + (1.0 - ADAM_B2) * (gv * gv)
        m_hat = nm / c1
        v_hat = nv / c2
        d_ref[...] = -ADAM_LR * (m_hat / (jnp.sqrt(v_hat) + ADAM_EPS) + ADAM_WD * w_ref[...])
        nm_ref[...] = nm
        nv_ref[...] = nv

    spec = pl.BlockSpec((tr, cols), lambda i: (i, 0))
    return pl.pallas_call(
        body, name=name, grid=(rws // tr,), in_specs=[spec] * 4, out_specs=[spec] * 3,
        out_shape=[jax.ShapeDtypeStruct((rws, cols), F32)] * 3,
        compiler_params=_cparams(("parallel",)),
    )(w, g, m, v)


WEIGHTS = (
    ("meta_tokens", (16, 1024), 1), ("norm_mix_g", (1024,), None), ("w_in", (1024, 7184), 1),
    ("w_alpha_up", (16, 512), 1), ("b_alpha", (512,), None), ("gla_norm_g", (1024,), None),
    ("w_gla_o", (1024, 1024), 0), ("conf_dw_w", (31, 1024), 1), ("conf_dw_b", (1024,), None),
    ("conf_ln_g", (1024,), None), ("conf_ln_b", (1024,), None), ("w_conf_o", (1024, 1024), 0),
    ("w_out", (1024, 1024), 0), ("norm_ffn_g", (1024,), None), ("w_up", (1024, 5632), 1),
    ("ffn_dw_w", (3, 2816), 1), ("ffn_dw_b", (2816,), None), ("w_down", (2816, 1024), 0),
    ("final_norm_g", (1024,), None),
)
BIG = ("w_in", "w_up", "w_down", "w_gla_o", "w_conf_o", "w_out")
SMALL_SHARDED = ("meta_tokens", "w_alpha_up", "conf_dw_w", "ffn_dw_w")
REPLICATED = tuple(n for n, _, ax in WEIGHTS if ax is None)
SHAPES = {n: s for n, s, _ in WEIGHTS}
AXIS = {n: ax for n, _, ax in WEIGHTS}
N_CHIPS = 4


def _shard_shape(name):
    s = list(SHAPES[name])
    s[AXIS[name]] //= N_CHIPS
    return tuple(s)


def _pack(parts, mult):
    flat = jnp.concatenate([p.reshape(-1) for p in parts])
    pad = (-flat.shape[0]) % mult
    return jnp.pad(flat, (0, pad))


def _unpack(flat, names, shape_of):
    out, off = {}, 0
    for n in names:
        shp = shape_of(n)
        size = math.prod(shp)
        out[n] = flat[off:off + size].reshape(shp)
        off += size
    return out


def _to_chip_major(full):
    r, cols = full.shape
    return full.reshape(r, N_CHIPS, cols // N_CHIPS).transpose(1, 0, 2)


def _from_chip_major(slabs):
    nc, r, cs = slabs.shape
    return slabs.transpose(1, 0, 2).reshape(r, nc * cs)


def kernel(x, meta_tokens, norm_mix_g, w_in, w_alpha_up, b_alpha, gla_norm_g, w_gla_o, conf_dw_w, conf_dw_b, conf_ln_g, conf_ln_b, w_conf_o, w_out, norm_ffn_g, w_up, ffn_dw_w, ffn_dw_b, w_down, final_norm_g, loss_target, m_meta_tokens, m_norm_mix_g, m_w_in, m_w_alpha_up, m_b_alpha, m_gla_norm_g, m_w_gla_o, m_conf_dw_w, m_conf_dw_b, m_conf_ln_g, m_conf_ln_b, m_w_conf_o, m_w_out, m_norm_ffn_g, m_w_up, m_ffn_dw_w, m_ffn_dw_b, m_w_down, m_final_norm_g, v_meta_tokens, v_norm_mix_g, v_w_in, v_w_alpha_up, v_b_alpha, v_gla_norm_g, v_w_gla_o, v_conf_dw_w, v_conf_dw_b, v_conf_ln_g, v_conf_ln_b, v_w_conf_o, v_w_out, v_norm_ffn_g, v_w_up, v_ffn_dw_w, v_ffn_dw_b, v_w_down, v_final_norm_g):
    names = [n for n, _, _ in WEIGHTS]
    w_args = (meta_tokens, norm_mix_g, w_in, w_alpha_up, b_alpha, gla_norm_g, w_gla_o, conf_dw_w, conf_dw_b, conf_ln_g,
              conf_ln_b, w_conf_o, w_out, norm_ffn_g, w_up, ffn_dw_w, ffn_dw_b, w_down, final_norm_g)
    m_args = (m_meta_tokens, m_norm_mix_g, m_w_in, m_w_alpha_up, m_b_alpha, m_gla_norm_g, m_w_gla_o, m_conf_dw_w,
              m_conf_dw_b, m_conf_ln_g, m_conf_ln_b, m_w_conf_o, m_w_out, m_norm_ffn_g, m_w_up, m_ffn_dw_w, m_ffn_dw_b,
              m_w_down, m_final_norm_g)
    v_args = (v_meta_tokens, v_norm_mix_g, v_w_in, v_w_alpha_up, v_b_alpha, v_gla_norm_g, v_w_gla_o, v_conf_dw_w,
              v_conf_dw_b, v_conf_ln_g, v_conf_ln_b, v_w_conf_o, v_w_out, v_norm_ffn_g, v_w_up, v_ffn_dw_w, v_ffn_dw_b,
              v_w_down, v_final_norm_g)
    in_shape = {n: a.shape for n, a in zip(names, w_args)}
    local = {n: a.reshape(_shard_shape(n) if AXIS[n] is not None else SHAPES[n]) for n, a in zip(names, w_args)}
    m_loc = {n: a.reshape(local[n].shape) for n, a in zip(names, m_args)}
    v_loc = {n: a.reshape(local[n].shape) for n, a in zip(names, v_args)}

    me = 2 * lax.axis_index("x") + lax.axis_index("y")
    core = lax.axis_index("c").reshape(1)

    gathered = dict(zip(BIG, _gather_big([local[n].astype(BF16) for n in BIG])))
    small = _pack([local[n] for n in SMALL_SHARDED], 8 * LANES).reshape(-1, LANES)
    small_all = _gather_small(small, "gather_small_weights")[::2].reshape(N_CHIPS, -1)
    per_chip_small = [_unpack(small_all[t], SMALL_SHARDED, _shard_shape) for t in range(N_CHIPS)]
    full = {n: jnp.concatenate([per_chip_small[t][n] for t in range(N_CHIPS)], axis=1) for n in SMALL_SHARDED}
    full["w_in"] = _from_chip_major(gathered["w_in"])
    full["w_up"] = gathered["w_up"]
    for n in ("w_down", "w_gla_o", "w_conf_o", "w_out"):
        full[n] = gathered[n].reshape(-1, gathered[n].shape[-1])
    for n in REPLICATED:
        full[n] = local[n].reshape(1, -1)

    loss_part, grad_x, grads = _local_step(x[0], loss_target[0], full)

    smalls = REPLICATED + SMALL_SHARDED
    rep = _pack([grads[n] for n in smalls] + [loss_part], 8 * LANES).reshape(-1, LANES)
    rep_sum = _sum_devices(_gather_small(rep, "gather_small_grads")).reshape(-1)
    g_loc = _unpack(rep_sum, smalls, lambda n: SHAPES[n])
    loss = rep_sum[sum(math.prod(SHAPES[n]) for n in smalls)]
    for n in SMALL_SHARDED:
        width = _shard_shape(n)[1]
        g_loc[n] = lax.dynamic_slice_in_dim(g_loc[n], me * width, width, axis=1)

    gm = [grads[n] if grads[n].ndim == 3 else grads[n].reshape(N_CHIPS, -1, grads[n].shape[-1]) for n in BIG]
    from_sibling = _swap_core_halves(gm)
    chip_part = [_add_core_halves(core, g, s, "add_core_halves_" + n) for n, g, s in zip(BIG, gm, from_sibling)]
    from_chips = _exchange_chip_partials(chip_part)
    reduced = [_sum_chip_partials(me.reshape(1), p, o, "sum_chip_partials_" + n)
               for n, p, o in zip(BIG, chip_part, from_chips)]
    g_loc.update(zip(BIG, _join_core_halves(reduced)))

    delta, new_m, new_v = {}, {}, {}
    for n in BIG:
        delta[n], new_m[n], new_v[n] = _adamw(local[n], g_loc[n], m_loc[n], v_loc[n], "adamw_" + n)
    rest = SMALL_SHARDED + REPLICATED
    pk = lambda dct: _pack([dct[n] for n in rest], 8 * LANES).reshape(-1, LANES)
    ds, ms, vs = _adamw(pk(local), pk(g_loc), pk(m_loc), pk(v_loc), "adamw_small")
    shape_loc = lambda n: local[n].shape
    for dct, flat in ((delta, ds), (new_m, ms), (new_v, vs)):
        dct.update(_unpack(flat.reshape(-1), rest, shape_loc))

    outs = [loss, grad_x[None]]
    for dct in (g_loc, delta, new_m, new_v):
        outs += [dct[n].reshape(in_shape[n]) for n in names]
    return tuple(outs)
```

```python
import functools
import math

import jax
import jax.numpy as jnp
from jax import lax
from jax.experimental import pallas as pl
from jax.experimental.pallas import tpu as pltpu

F32 = jnp.float32
BF16 = jnp.bfloat16

N_META = 16
PAD = 240
HEAD_ROWS = PAD + N_META
HEADS = 4
GLA_RANK = 16
RANK_PAD = 128
GATE_TAU = 16.0
CHUNK = 64
CONF_K = 31
FFN_K = 3
RMS_EPS = 1e-6
LN_EPS = 1e-5
ADAM_LR, ADAM_B1, ADAM_B2, ADAM_EPS, ADAM_WD, ADAM_STEP = 0.001, 0.9, 0.999, 1e-08, 0.01, 10

ROW_TILE = 256
HALO = 32
LANES = 128
V7X_VMEM_LIMIT = 56 * 1024 * 1024
MESH = pl.DeviceIdType.MESH


def _cparams(sem):
    return pltpu.CompilerParams(dimension_semantics=sem, vmem_limit_bytes=V7X_VMEM_LIMIT)


def _sigmoid(x):
    return 1.0 / (1.0 + jnp.exp(-x))


def _pick(n, prefs):
    for p in prefs:
        if n % p == 0:
            return p
    return n


def _matmul(a, b, *, dims, name, tm=None, tn=None, tk=None, out_dtype=F32, add=None, chips=None, hosted=None):
    if chips == "b":
        nc, r, cs = b.shape
        b_shape = (r, nc * cs)
    else:
        b_shape = b.shape
    if dims == "nn":
        (m, k), (_, n) = a.shape, b_shape
    elif dims == "nt":
        (m, k), (n, _) = a.shape, b_shape
    else:
        (k, m), (_, n) = a.shape, b_shape
    tm = tm or _pick(m, (768, 1024, 1408, 512, 256, 128))
    tn = tn or _pick(n, (1024, 1408, 512, 256, 128))
    tk = tk or _pick(k, (1024, 768, 1408, 512, 256, 128))
    if chips == "b":
        tn, tk = (cs, tk) if dims == "nn" else (tn, cs)
    if chips == "out":
        tn, tm = n // N_CHIPS, m
    nk = k // tk
    assert m % tm == 0 and n % tn == 0 and k % tk == 0, (name, m, n, k, tm, tn, tk)
    a_spec = {"nn": pl.BlockSpec((tm, tk), lambda i, j, kk: (i, kk)),
              "nt": pl.BlockSpec((tm, tk), lambda i, j, kk: (i, kk)),
              "tn": pl.BlockSpec((tk, tm), lambda i, j, kk: (kk, i))}[dims]
    if chips == "b":
        b_spec = {"nn": pl.BlockSpec((None, tk, tn), lambda i, j, kk: (j, kk, 0)),
                  "nt": pl.BlockSpec((None, tn, tk), lambda i, j, kk: (kk, j, 0))}[dims]
    else:
        b_spec = {"nn": pl.BlockSpec((tk, tn), lambda i, j, kk: (kk, j)),
                  "nt": pl.BlockSpec((tn, tk), lambda i, j, kk: (j, kk)),
                  "tn": pl.BlockSpec((tk, tn), lambda i, j, kk: (kk, j))}[dims]
    contract = {"nn": (((1,), (0,)), ((), ())), "nt": (((1,), (1,)), ((), ())), "tn": (((0,), (0,)), ((), ()))}[dims]
    if chips == "out":
        o_spec = pl.BlockSpec((None, tm, tn), lambda i, j, kk: (j, 0, 0))
        out_struct = jax.ShapeDtypeStruct((N_CHIPS, m, tn), out_dtype)
    else:
        o_spec = pl.BlockSpec((tm, tn), lambda i, j, kk: (i, j))
        out_struct = jax.ShapeDtypeStruct((m, n), out_dtype)
    has_add = add is not None

    def body(*refs):
        refs = list(refs)
        a_ref, b_ref = refs[:2]
        add_ref = refs[2] if has_add else None
        n_in = 2 + has_add
        h_ins = refs[n_in:n_in + n_hin]
        o_ref = refs[n_in + n_hin]
        h_outs = refs[n_in + n_hin + 1:n_in + n_hin + 1 + n_hout]
        acc_ref = refs[n_in + n_hin + 1 + n_hout]
        h_sems = refs[n_in + n_hin + 2 + n_hout:]
        i, j, kk = pl.program_id(0), pl.program_id(1), pl.program_id(2)
        if hosted is not None:
            @pl.when((i == 0) & (j == 0) & (kk == 0))
            def _():
                hosted.start(h_ins, h_outs, h_sems)

        prod = lax.dot_general(a_ref[...].astype(BF16), b_ref[...].astype(BF16), contract,
                               preferred_element_type=F32)

        @pl.when(kk == 0)
        def _():
            acc_ref[...] = prod + add_ref[...].astype(F32) if has_add else prod

        @pl.when(kk > 0)
        def _():
            acc_ref[...] += prod

        @pl.when(kk == nk - 1)
        def _():
            o_ref[...] = acc_ref[...].astype(out_dtype)

        if hosted is not None:
            @pl.when((i == m // tm - 1) & (j == n // tn - 1) & (kk == nk - 1))
            def _():
                hosted.finish(h_ins, h_outs, h_sems)

    n_hin = len(hosted.operands) if hosted is not None else 0
    n_hout = len(hosted.out_shapes) if hosted is not None else 0
    in_specs = [a_spec, b_spec] + ([o_spec] if has_add else []) + [HBM_SPEC] * n_hin
    args = (a, b) + ((add,) if has_add else ()) + (tuple(hosted.operands) if hosted is not None else ())
    outs = pl.pallas_call(
        body, name=name, grid=(m // tm, n // tn, nk),
        in_specs=in_specs, out_specs=[o_spec] + [HBM_SPEC] * n_hout,
        out_shape=[out_struct] + (list(hosted.out_shapes) if hosted is not None else []),
        scratch_shapes=[pltpu.VMEM((tm, tn), F32)] + (list(hosted.sem_shapes) if hosted is not None else []),
        compiler_params=_cparams(("arbitrary",) * 3 if hosted is not None else ("parallel", "parallel", "arbitrary")),
    )(*args)
    return outs[0] if hosted is None else (outs[0], list(outs[1:]))


def _row_mask(tile_index, rows):
    r = tile_index * rows + lax.broadcasted_iota(jnp.int32, (rows, 1), 0)
    return (r >= PAD).astype(F32)


def _prep(x, meta, g1):
    s, d = x.shape
    t = HEAD_ROWS + s
    tm = ROW_TILE

    def body(x_ref, meta_ref, g_ref, h_ref, u_ref):
        i = pl.program_id(0)

        @pl.when(i == 0)
        def _():
            h_ref[0:PAD, :] = jnp.zeros((PAD, d), F32)
            h_ref[PAD:HEAD_ROWS, :] = meta_ref[...]

        @pl.when(i > 0)
        def _():
            h_ref[...] = x_ref[...]

        h = h_ref[...]
        r = lax.rsqrt(jnp.mean(h * h, axis=-1, keepdims=True) + RMS_EPS)
        u_ref[...] = (h * r * g_ref[...]).astype(BF16)

    return pl.pallas_call(
        body, name="prep", grid=(t // tm,),
        in_specs=[pl.BlockSpec((tm, d), lambda i: (jnp.maximum(i - 1, 0), 0)),
                  pl.BlockSpec((N_META, d), lambda i: (0, 0)),
                  pl.BlockSpec((1, d), lambda i: (0, 0))],
        out_specs=[pl.BlockSpec((tm, d), lambda i: (i, 0)), pl.BlockSpec((tm, d), lambda i: (i, 0))],
        out_shape=[jax.ShapeDtypeStruct((t, d), F32), jax.ShapeDtypeStruct((t, d), BF16)],
        compiler_params=_cparams(("parallel",)),
    )(x, meta, g1)


def _log_gate(alr, wau, b_alpha):
    t = alr.shape[0]
    dk = wau.shape[1]
    tm = ROW_TILE

    def body(alr_ref, w_ref, b_ref, o_ref):
        z = jnp.dot(alr_ref[...].astype(BF16), w_ref[...].astype(BF16), preferred_element_type=F32) + b_ref[...]
        ls = jnp.minimum(z, 0.0) - jnp.log(1.0 + jnp.exp(-jnp.abs(z)))
        o_ref[...] = ls * (1.0 / GATE_TAU) * _row_mask(pl.program_id(0), tm)

    return pl.pallas_call(
        body, name="log_gate", grid=(t // tm,),
        in_specs=[pl.BlockSpec((tm, RANK_PAD), lambda i: (i, 0)),
                  pl.BlockSpec((RANK_PAD, dk), lambda i: (0, 0)),
                  pl.BlockSpec((1, dk), lambda i: (0, 0))],
        out_specs=pl.BlockSpec((tm, dk), lambda i: (i, 0)),
        out_shape=jax.ShapeDtypeStruct((t, dk), F32),
        compiler_params=_cparams(("parallel",)),
    )(alr, wau, b_alpha)


def _tri(n, upper=False):
    r = lax.broadcasted_iota(jnp.int32, (n, n), 0)
    c = lax.broadcasted_iota(jnp.int32, (n, n), 1)
    return (r <= c) if upper else (r >= c)


_NT = (((1,), (1,)), ((), ()))
_TN = (((0,), (0,)), ((), ()))


def _dot(a, b):
    return jnp.dot(a, b, preferred_element_type=F32)


def _dot_nt(a, b):
    return lax.dot_general(a, b, _NT, preferred_element_type=F32)


def _dot_tn(a, b):
    return lax.dot_general(a, b, _TN, preferred_element_type=F32)


def _chunk_decays(la, tri_f32):
    b = jnp.dot(tri_f32, la, preferred_element_type=F32, precision=lax.Precision.HIGHEST)
    bl = b[CHUNK - 1:CHUNK, :]
    return jnp.exp(b), jnp.exp(bl - b), jnp.exp(-bl), jnp.exp(bl)


def _gla_fwd(p7, log_a, gla_g, d):
    t = p7.shape[0]
    dk_all = d // 2
    dkh, dvh = dk_all // HEADS, d // HEADS
    cb = ROW_TILE
    ncb = cb // CHUNK
    scale = dkh ** -0.5

    def body(qk_ref, v_ref, r_ref, la_ref, g_ref, o_ref, y_ref, s_ref, st_scr):
        @pl.when(pl.program_id(0) == 0)
        def _():
            st_scr[...] = jnp.zeros_like(st_scr)

        tri = _tri(CHUNK)
        tri_f = tri.astype(F32)
        for c in range(ncb):
            rows = slice(c * CHUNK, (c + 1) * CHUNK)
            eb, ekl, ebl_inv, gam = _chunk_decays(la_ref[rows, :], tri_f)
            for h in range(HEADS):
                ks = slice(h * dkh, (h + 1) * dkh)
                vs = slice(h * dvh, (h + 1) * dvh)
                q = qk_ref[rows, ks] * scale
                k = qk_ref[rows, dk_all + h * dkh:dk_all + (h + 1) * dkh]
                v = v_ref[rows, vs].astype(BF16)
                qb = q * eb[:, ks]
                kh = (k * ekl[:, ks]).astype(BF16)
                qc = (qb * ebl_inv[:, ks]).astype(BF16)
                a = jnp.where(tri, _dot_nt(qc, kh), 0.0)
                st = st_scr[h]
                st_b = st.astype(BF16)
                s_ref[c, h] = st_b
                o = _dot_nt(qb.astype(BF16), st_b) + _dot(a.astype(BF16), v)
                st_scr[h] = st * gam[:, ks] + _dot_tn(v, kh)
                o_ref[rows, vs] = o
                rr = lax.rsqrt(jnp.mean(o * o, axis=-1, keepdims=True) + RMS_EPS)
                r = r_ref[rows, vs]
                y_ref[rows, vs] = (o * rr * g_ref[:, vs] * (r * _sigmoid(r))).astype(BF16)

    return pl.pallas_call(
        body, name="gla_fwd", grid=(t // cb,),
        in_specs=[pl.BlockSpec((cb, d), lambda i: (i, 0)),
                  pl.BlockSpec((cb, d), lambda i: (i, 1)),
                  pl.BlockSpec((cb, d), lambda i: (i, 2)),
                  pl.BlockSpec((cb, dk_all), lambda i: (i, 0)),
                  pl.BlockSpec((1, d), lambda i: (0, 0))],
        out_specs=[pl.BlockSpec((cb, d), lambda i: (i, 0)),
                   pl.BlockSpec((cb, d), lambda i: (i, 0)),
                   pl.BlockSpec((ncb, HEADS, dvh, dkh), lambda i: (i, 0, 0, 0))],
        out_shape=[jax.ShapeDtypeStruct((t, d), F32), jax.ShapeDtypeStruct((t, d), BF16),
                   jax.ShapeDtypeStruct((t // CHUNK, HEADS, dvh, dkh), BF16)],
        scratch_shapes=[pltpu.VMEM((HEADS, dvh, dkh), F32)],
        compiler_params=_cparams(("arbitrary",)),
    )(p7, p7, p7, log_a, gla_g)


SUBLANES = 8


def _tap_phases(n_taps, first):
    phases = {}
    for j in range(n_taps):
        e = first + j
        phases.setdefault(e % SUBLANES, []).append((j, e - e % SUBLANES))
    return phases


CONV_ROWS = 64


def _shifted_windows(ext_ref, shf_ref, phases, rows, ls):
    for p, taps in phases.items():
        if p:
            span = max(off for _, off in taps) + rows
            shf_ref[p, 0:span, :] = ext_ref[p:p + span, ls]

    def window(p, start, n):
        return shf_ref[p, start:start + n, :] if p else ext_ref[start:start + n, ls]

    return window


def _conv_strip(ext_ref, shf_ref, w_ref, n_taps, first, rows, ls, emit, reverse=False):
    phases = _tap_phases(n_taps, first)
    window = _shifted_windows(ext_ref, shf_ref, phases, rows, ls)
    for r0 in range(0, rows, CONV_ROWS):
        acc = None
        for p, taps in phases.items():
            for j, off in taps:
                wj = w_ref[(n_taps - 1 - j) if reverse else j, ls]
                term = window(p, off + r0, CONV_ROWS) * wj
                acc = term if acc is None else acc + term
        emit(r0, acc)


def _corr_strip(dl_ref, ext_ref, shf_ref, acc_ref, n_taps, first, rows, ls):
    phases = _tap_phases(n_taps, first)
    window = _shifted_windows(ext_ref, shf_ref, phases, rows, ls)
    for r0 in range(0, rows, CONV_ROWS):
        dl = dl_ref[r0:r0 + CONV_ROWS, ls]
        for p, taps in phases.items():
            for j, off in taps:
                prod = dl * window(p, off + r0, CONV_ROWS)
                acc_ref[SUBLANES * j:SUBLANES * (j + 1), ls] += jnp.sum(
                    prod.reshape(CONV_ROWS // SUBLANES, SUBLANES, prod.shape[-1]), axis=0)


def _conf_fwd(p7, conv_w, conv_b, ln_g, ln_b, d):
    t = p7.shape[0]
    tm = ROW_TILE

    def body(c1_ref, c2_ref, w_ref, b_ref, g_ref, bb_ref, cc_ref, sc_ref, ext, shf):
        @pl.when(pl.program_id(0) == 0)
        def _():
            ext[0:HALO, :] = jnp.zeros((HALO, d), F32)

        ext[HALO:HALO + tm, :] = c1_ref[...] * _sigmoid(c2_ref[...])
        for s0 in range(0, d, LANES):
            ls = slice(s0, s0 + LANES)

            def emit(r0, acc, ls=ls):
                cc_ref[r0:r0 + CONV_ROWS, ls] = acc + b_ref[:, ls]

            _conv_strip(ext, shf, w_ref, CONF_K, HALO - (CONF_K - 1), tm, ls, emit)
        ext[0:HALO, :] = ext[tm:tm + HALO, :]
        g = g_ref[...]
        bb = bb_ref[...]
        rows_per_step = 2 * SUBLANES

        def rows_step(k, carry):
            rs = pl.ds(pl.multiple_of(k * rows_per_step, rows_per_step), rows_per_step)
            cc = cc_ref[rs, :]
            xc = cc - jnp.mean(cc, axis=-1, keepdims=True)
            rstd = lax.rsqrt(jnp.mean(xc * xc, axis=-1, keepdims=True) + LN_EPS)
            cn = xc * rstd * g + bb
            sc_ref[rs, :] = (cn * _sigmoid(cn)).astype(BF16)
            return carry

        lax.fori_loop(0, tm // rows_per_step, rows_step, 0, unroll=4)

    vec = pl.BlockSpec((1, d), lambda i: (0, 0))
    return pl.pallas_call(
        body, name="conf_fwd", grid=(t // tm,),
        in_specs=[pl.BlockSpec((tm, d), lambda i: (i, 3)), pl.BlockSpec((tm, d), lambda i: (i, 4)),
                  pl.BlockSpec((CONF_K, d), lambda i: (0, 0)), vec, vec, vec],
        out_specs=[pl.BlockSpec((tm, d), lambda i: (i, 0)), pl.BlockSpec((tm, d), lambda i: (i, 0))],
        out_shape=[jax.ShapeDtypeStruct((t, d), F32), jax.ShapeDtypeStruct((t, d), BF16)],
        scratch_shapes=[pltpu.VMEM((tm + HALO, d), F32), pltpu.VMEM((SUBLANES, tm + HALO, LANES), F32)],
        compiler_params=_cparams(("arbitrary",)),
    )(p7, p7, conv_w, conv_b, ln_g, ln_b)


def _merge(p7, br_gla, br_conf, d):
    t = p7.shape[0]
    tm = ROW_TILE

    def body(gg_ref, gc_ref, a_ref, b_ref, o_ref):
        o_ref[...] = (_sigmoid(gg_ref[...]) * a_ref[...] + _sigmoid(gc_ref[...]) * b_ref[...]).astype(BF16)

    row = pl.BlockSpec((tm, d), lambda i: (i, 0))
    return pl.pallas_call(
        body, name="merge", grid=(t // tm,),
        in_specs=[pl.BlockSpec((tm, d), lambda i: (i, 5)), pl.BlockSpec((tm, d), lambda i: (i, 6)), row, row],
        out_specs=row, out_shape=jax.ShapeDtypeStruct((t, d), BF16),
        compiler_params=_cparams(("parallel",)),
    )(p7, p7, br_gla, br_conf)


def _rms_fwd(h, g, name):
    t, d = h.shape
    tm = ROW_TILE

    def body(h_ref, g_ref, u_ref):
        x = h_ref[...]
        r = lax.rsqrt(jnp.mean(x * x, axis=-1, keepdims=True) + RMS_EPS)
        u_ref[...] = (x * r * g_ref[...]).astype(BF16)

    return pl.pallas_call(
        body, name=name, grid=(t // tm,),
        in_specs=[pl.BlockSpec((tm, d), lambda i: (i, 0)), pl.BlockSpec((1, d), lambda i: (0, 0))],
        out_specs=pl.BlockSpec((tm, d), lambda i: (i, 0)), out_shape=jax.ShapeDtypeStruct((t, d), BF16),
        compiler_params=_cparams(("parallel",)),
    )(h, g)


def _ffn_mid(up, w, b, dff):
    t = up.shape[0]
    tm = ROW_TILE
    hal = 8

    def body(a_ref, bv_ref, w_ref, b_ref, y_ref, ext, shf):
        i = pl.program_id(0)

        @pl.when(i == 0)
        def _():
            ext[0:hal, :] = jnp.zeros((hal, dff), F32)

        ext[hal:hal + tm, :] = a_ref[...] * _row_mask(i, tm)
        for s0 in range(0, dff, LANES):
            ls = slice(s0, s0 + LANES)

            def emit(r0, acc, ls=ls):
                rb = slice(r0, r0 + CONV_ROWS)
                ac = acc + b_ref[:, ls]
                y_ref[rb, ls] = (ac * _sigmoid(ac) * bv_ref[rb, ls]).astype(BF16)

            _conv_strip(ext, shf, w_ref, FFN_K, hal - (FFN_K - 1), tm, ls, emit)
        ext[0:hal, :] = ext[tm:tm + hal, :]

    return pl.pallas_call(
        body, name="ffn_mid", grid=(t // tm,),
        in_specs=[pl.BlockSpec((tm, dff), lambda i: (i, 0)), pl.BlockSpec((tm, dff), lambda i: (i, 1)),
                  pl.BlockSpec((FFN_K, dff), lambda i: (0, 0)), pl.BlockSpec((1, dff), lambda i: (0, 0))],
        out_specs=pl.BlockSpec((tm, dff), lambda i: (i, 0)), out_shape=jax.ShapeDtypeStruct((t, dff), BF16),
        scratch_shapes=[pltpu.VMEM((tm + hal, dff), F32), pltpu.VMEM((SUBLANES, tm + hal, LANES), F32)],
        compiler_params=_cparams(("arbitrary",)),
    )(up, up, w, b)


def _loss_head(h2, target, gf):
    t, d = h2.shape
    tm = ROW_TILE

    def body(h_ref, tg_ref, g_ref, loss_ref, dg_ref, dh_ref):
        i = pl.program_id(0)

        @pl.when(i == 0)
        def _():
            loss_ref[...] = jnp.zeros_like(loss_ref)
            dg_ref[...] = jnp.zeros_like(dg_ref)
            dh_ref[...] = jnp.zeros_like(dh_ref)

        @pl.when(i > 0)
        def _():
            h = h_ref[...]
            g = g_ref[...]
            r = lax.rsqrt(jnp.mean(h * h, axis=-1, keepdims=True) + RMS_EPS)
            n = h * r
            err = n * g - tg_ref[...]
            loss_ref[...] += 0.5 * jnp.sum(jnp.mean(err * err, axis=-1, keepdims=True), axis=0, keepdims=True)
            dout = err * (1.0 / d)
            dg_ref[...] += jnp.sum(dout * n, axis=0, keepdims=True)
            dn = dout * g
            dh_ref[...] = r * (dn - n * jnp.mean(dn * n, axis=-1, keepdims=True))

    return pl.pallas_call(
        body, name="loss_head", grid=(t // tm,),
        in_specs=[pl.BlockSpec((tm, d), lambda i: (i, 0)),
                  pl.BlockSpec((tm, d), lambda i: (jnp.maximum(i - 1, 0), 0)),
                  pl.BlockSpec((1, d), lambda i: (0, 0))],
        out_specs=[pl.BlockSpec((1, 1), lambda i: (0, 0)), pl.BlockSpec((1, d), lambda i: (0, 0)),
                   pl.BlockSpec((tm, d), lambda i: (i, 0))],
        out_shape=[jax.ShapeDtypeStruct((1, 1), F32), jax.ShapeDtypeStruct((1, d), F32),
                   jax.ShapeDtypeStruct((t, d), F32)],
        compiler_params=_cparams(("arbitrary",)),
    )(h2, target, gf)


def _dsilu(x, sig):
    return sig * (1.0 + x * (1.0 - sig))


def _ffn_mid_bwd(up, dy, w, b, dff):
    t = up.shape[0]
    tm = ROW_TILE
    hal = 8
    nt = t // tm

    def body(a_ref, ap_ref, bv_ref, dy_ref, w_ref, b_ref, dup_ref, dw_ref, db_ref, ext, dext, dw_acc, db_acc, shf):
        i = pl.program_id(0)
        tile = nt - 1 - i

        @pl.when(i == 0)
        def _():
            dext[tm:tm + hal, :] = jnp.zeros((hal, dff), F32)
            dw_acc[...] = jnp.zeros_like(dw_acc)
            db_acc[...] = jnp.zeros_like(db_acc)

        ext[0:hal, :] = ap_ref[...] * jnp.where(tile > 0, 1.0, 0.0)
        ext[hal:hal + tm, :] = a_ref[...] * _row_mask(tile, tm)
        first = hal - (FFN_K - 1)
        for s0 in range(0, dff, LANES):
            ls = slice(s0, s0 + LANES)

            def emit_fwd(r0, acc, ls=ls, s0=s0):
                rb = slice(r0, r0 + CONV_ROWS)
                ac = acc + b_ref[:, ls]
                sig = _sigmoid(ac)
                dyv = dy_ref[rb, ls]
                dup_ref[rb, dff + s0:dff + s0 + LANES] = (dyv * ac * sig).astype(BF16)
                dac = dyv * bv_ref[rb, ls] * _dsilu(ac, sig)
                dext[rb, ls] = dac
                db_acc[:, ls] += jnp.sum(dac.reshape(CONV_ROWS // SUBLANES, SUBLANES, LANES), axis=0)

            _conv_strip(ext, shf, w_ref, FFN_K, first, tm, ls, emit_fwd)
            _corr_strip(dext, ext, shf, dw_acc, FFN_K, first, tm, ls)

            def emit_bwd(r0, da, ls=ls):
                rb = slice(r0, r0 + CONV_ROWS)
                mask = ((tile * tm + r0 + lax.broadcasted_iota(jnp.int32, (CONV_ROWS, 1), 0)) >= PAD).astype(F32)
                dup_ref[rb, ls] = (da * mask).astype(BF16)

            _conv_strip(dext, shf, w_ref, FFN_K, 0, tm, ls, emit_bwd, reverse=True)
        dext[tm:tm + hal, :] = dext[0:hal, :]

        @pl.when(i == nt - 1)
        def _():
            db_ref[...] = jnp.sum(db_acc[...], axis=0, keepdims=True)
            for j in range(FFN_K):
                dw_ref[j:j + 1, :] = jnp.sum(dw_acc[SUBLANES * j:SUBLANES * (j + 1), :], axis=0, keepdims=True)

    rev = lambda i: (nt - 1 - i, 0)
    return pl.pallas_call(
        body, name="ffn_mid_bwd", grid=(nt,),
        in_specs=[pl.BlockSpec((tm, dff), rev),
                  pl.BlockSpec((hal, dff), lambda i: (jnp.maximum((nt - 1 - i) * (tm // hal) - 1, 0), 0)),
                  pl.BlockSpec((tm, dff), lambda i: (nt - 1 - i, 1)),
                  pl.BlockSpec((tm, dff), rev),
                  pl.BlockSpec((FFN_K, dff), lambda i: (0, 0)), pl.BlockSpec((1, dff), lambda i: (0, 0))],
        out_specs=[pl.BlockSpec((tm, 2 * dff), rev),
                   pl.BlockSpec((FFN_K, dff), lambda i: (0, 0)), pl.BlockSpec((1, dff), lambda i: (0, 0))],
        out_shape=[jax.ShapeDtypeStruct((t, 2 * dff), BF16),
                   jax.ShapeDtypeStruct((FFN_K, dff), F32), jax.ShapeDtypeStruct((1, dff), F32)],
        scratch_shapes=[pltpu.VMEM((tm + hal, dff), F32), pltpu.VMEM((tm + hal, dff), F32),
                        pltpu.VMEM((SUBLANES * FFN_K, dff), F32), pltpu.VMEM((SUBLANES, dff), F32),
                        pltpu.VMEM((SUBLANES, tm + hal, LANES), F32)],
        compiler_params=_cparams(("arbitrary",)),
    )(up, up, up, dy, w, b)


def _rms_bwd(du, h, g, dres, name):
    t, d = h.shape
    tm = ROW_TILE

    def body(du_ref, h_ref, g_ref, dres_ref, dh_ref, dg_ref):
        @pl.when(pl.program_id(0) == 0)
        def _():
            dg_ref[...] = jnp.zeros_like(dg_ref)

        x = h_ref[...]
        r = lax.rsqrt(jnp.mean(x * x, axis=-1, keepdims=True) + RMS_EPS)
        n = x * r
        du_ = du_ref[...]
        dg_ref[...] += jnp.sum(du_ * n, axis=0, keepdims=True)
        dn = du_ * g_ref[...]
        dh_ref[...] = dres_ref[...] + r * (dn - n * jnp.mean(dn * n, axis=-1, keepdims=True))

    row = pl.BlockSpec((tm, d), lambda i: (i, 0))
    vec = pl.BlockSpec((1, d), lambda i: (0, 0))
    return pl.pallas_call(
        body, name=name, grid=(t // tm,),
        in_specs=[row, row, vec, row], out_specs=[row, vec],
        out_shape=[jax.ShapeDtypeStruct((t, d), F32), jax.ShapeDtypeStruct((1, d), F32)],
        compiler_params=_cparams(("arbitrary",)),
    )(du, h, g, dres)


def _merge_bwd(p7, br_gla, br_conf, dmerged, d):
    t = p7.shape[0]
    tm = ROW_TILE

    def body(gg_ref, gc_ref, a_ref, b_ref, dm_ref, da_ref, db_ref, dgg_ref, dgc_ref):
        dm = dm_ref[...]
        sg = _sigmoid(gg_ref[...])
        sc = _sigmoid(gc_ref[...])
        da_ref[...] = (dm * sg).astype(BF16)
        db_ref[...] = (dm * sc).astype(BF16)
        dgg_ref[...] = (dm * a_ref[...] * sg * (1.0 - sg)).astype(BF16)
        dgc_ref[...] = (dm * b_ref[...] * sc * (1.0 - sc)).astype(BF16)

    row = pl.BlockSpec((tm, d), lambda i: (i, 0))
    outs = pl.pallas_call(
        body, name="merge_bwd", grid=(t // tm,),
        in_specs=[pl.BlockSpec((tm, d), lambda i: (i, 5)), pl.BlockSpec((tm, d), lambda i: (i, 6)), row, row, row],
        out_specs=[row, row, row, row],
        out_shape=[jax.ShapeDtypeStruct((t, d), BF16)] * 4,
        compiler_params=_cparams(("parallel",)),
    )(p7, p7, br_gla, br_conf, dmerged)
    return outs


def _conf_bwd(p7, cc, dsc, conv_w, ln_g, ln_b, d):
    t = p7.shape[0]
    tm = ROW_TILE
    nt = t // tm

    def body(c1_ref, c2_ref, c1p_ref, c2p_ref, cc_ref, dsc_ref, w_ref, g_ref, bb_ref,
             dc1_ref, dc2_ref, dw_ref, db_ref, dg_ref, dbb_ref, ext, dext, dw_acc, shf):
        i = pl.program_id(0)
        tile = nt - 1 - i

        @pl.when(i == 0)
        def _():
            dext[tm:tm + HALO, :] = jnp.zeros((HALO, d), F32)
            dw_acc[...] = jnp.zeros_like(dw_acc)
            db_ref[...] = jnp.zeros_like(db_ref)
            dg_ref[...] = jnp.zeros_like(dg_ref)
            dbb_ref[...] = jnp.zeros_like(dbb_ref)

        ext[0:HALO, :] = c1p_ref[...] * _sigmoid(c2p_ref[...]) * jnp.where(tile > 0, 1.0, 0.0)
        g = g_ref[...]
        bb = bb_ref[...]

        groups = 4

        def rows_step(k, sums):
            sdg, sdbb, sdb = sums
            for u in range(groups):
                r0 = (k * groups + u) * SUBLANES
                rs = pl.ds(pl.multiple_of(r0, SUBLANES), SUBLANES)
                ext[pl.ds(pl.multiple_of(HALO + r0, SUBLANES), SUBLANES), :] = c1_ref[rs, :] * _sigmoid(c2_ref[rs, :])
                cc_ = cc_ref[rs, :]
                xc = cc_ - jnp.mean(cc_, axis=-1, keepdims=True)
                rstd = lax.rsqrt(jnp.mean(xc * xc, axis=-1, keepdims=True) + LN_EPS)
                xh = xc * rstd
                cn = xh * g + bb
                dcn = dsc_ref[rs, :] * _dsilu(cn, _sigmoid(cn))
                dxh = dcn * g
                dcc = rstd * (dxh - jnp.mean(dxh, axis=-1, keepdims=True)
                              - xh * jnp.mean(dxh * xh, axis=-1, keepdims=True))
                dext[rs, :] = dcc
                sdg, sdbb, sdb = sdg + dcn * xh, sdbb + dcn, sdb + dcc
            return sdg, sdbb, sdb

        zero = jnp.zeros((SUBLANES, d), F32)
        sdg, sdbb, sdb = lax.fori_loop(0, tm // (groups * SUBLANES), rows_step, (zero, zero, zero))
        dg_ref[...] += jnp.sum(sdg, axis=0, keepdims=True)
        dbb_ref[...] += jnp.sum(sdbb, axis=0, keepdims=True)
        db_ref[...] += jnp.sum(sdb, axis=0, keepdims=True)
        for s0 in range(0, d, LANES):
            ls = slice(s0, s0 + LANES)
            _corr_strip(dext, ext, shf, dw_acc, CONF_K, HALO - (CONF_K - 1), tm, ls)

            def emit(r0, dc, ls=ls):
                rb = slice(r0, r0 + CONV_ROWS)
                s2l = _sigmoid(c2_ref[rb, ls])
                dc1_ref[rb, ls] = (dc * s2l).astype(BF16)
                dc2_ref[rb, ls] = (dc * c1_ref[rb, ls] * s2l * (1.0 - s2l)).astype(BF16)

            _conv_strip(dext, shf, w_ref, CONF_K, 0, tm, ls, emit, reverse=True)
        dext[tm:tm + HALO, :] = dext[0:HALO, :]

        @pl.when(i == nt - 1)
        def _():
            for j in range(CONF_K):
                dw_ref[j:j + 1, :] = jnp.sum(dw_acc[SUBLANES * j:SUBLANES * (j + 1), :], axis=0, keepdims=True)

    rev = lambda i: (nt - 1 - i, 0)
    prev = lambda col: (lambda i: (jnp.maximum((nt - 1 - i) * (tm // HALO) - 1, 0), col))
    vec = pl.BlockSpec((1, d), lambda i: (0, 0))
    return pl.pallas_call(
        body, name="conf_bwd", grid=(nt,),
        in_specs=[pl.BlockSpec((tm, d), lambda i: (nt - 1 - i, 3)), pl.BlockSpec((tm, d), lambda i: (nt - 1 - i, 4)),
                  pl.BlockSpec((HALO, d), prev(3)), pl.BlockSpec((HALO, d), prev(4)),
                  pl.BlockSpec((tm, d), rev), pl.BlockSpec((tm, d), rev),
                  pl.BlockSpec((CONF_K, d), lambda i: (0, 0)), vec, vec],
        out_specs=[pl.BlockSpec((tm, d), rev), pl.BlockSpec((tm, d), rev),
                   pl.BlockSpec((CONF_K, d), lambda i: (0, 0)), vec, vec, vec],
        out_shape=[jax.ShapeDtypeStruct((t, d), BF16), jax.ShapeDtypeStruct((t, d), BF16),
                   jax.ShapeDtypeStruct((CONF_K, d), F32)] + [jax.ShapeDtypeStruct((1, d), F32)] * 3,
        scratch_shapes=[pltpu.VMEM((tm + HALO, d), F32), pltpu.VMEM((tm + HALO, d), F32),
                        pltpu.VMEM((SUBLANES * CONF_K, d), F32), pltpu.VMEM((SUBLANES, tm + HALO, LANES), F32)],
        compiler_params=_cparams(("arbitrary",)),
    )(p7, p7, p7, p7, cc, dsc, conv_w, ln_g, ln_b)


def _gla_bwd(p7, log_a, alr, wau, b_alpha, gla_g, o, states, dy, d):
    t = p7.shape[0]
    dk_all = d // 2
    dkh, dvh = dk_all // HEADS, d // HEADS
    cb = ROW_TILE
    ncb = cb // CHUNK
    nb = t // cb
    scale = dkh ** -0.5

    def body(qk_ref, v_ref, r_ref, la_ref, alr_ref, wau_ref, ba_ref, g_ref, o_ref, s_ref, dy_ref,
             dqk_ref, dv_ref, dr_ref, dz_ref, dg_ref, dba_ref, dst_scr, dla_scr):
        i = pl.program_id(0)
        blk = nb - 1 - i

        @pl.when(i == 0)
        def _():
            dst_scr[...] = jnp.zeros_like(dst_scr)
            dg_ref[...] = jnp.zeros_like(dg_ref)
            dba_ref[...] = jnp.zeros_like(dba_ref)

        tri = _tri(CHUNK)
        tri_f = tri.astype(F32)
        triu_f = _tri(CHUNK, upper=True).astype(F32)
        for c in reversed(range(ncb)):
            rows = slice(c * CHUNK, (c + 1) * CHUNK)
            eb, ekl, ebl_inv, gam = _chunk_decays(la_ref[rows, :], tri_f)
            for h in range(HEADS):
                ks = slice(h * dkh, (h + 1) * dkh)
                kcols = slice(dk_all + h * dkh, dk_all + (h + 1) * dkh)
                vs = slice(h * dvh, (h + 1) * dvh)
                q = qk_ref[rows, ks] * scale
                k = qk_ref[rows, kcols]
                v = v_ref[rows, vs].astype(BF16)
                ebh, eklh, eih, gamh = eb[:, ks], ekl[:, ks], ebl_inv[:, ks], gam[:, ks]
                qb = q * ebh
                kh = k * eklh
                qc = qb * eih
                qb_b, kh_b, qc_b = qb.astype(BF16), kh.astype(BF16), qc.astype(BF16)
                ov = o_ref[rows, vs]
                r = r_ref[rows, vs]
                dyv = dy_ref[rows, vs]
                sig = _sigmoid(r)
                rr = lax.rsqrt(jnp.mean(ov * ov, axis=-1, keepdims=True) + RMS_EPS)
                n = ov * rr
                g = g_ref[:, vs]
                dr_ref[rows, vs] = (dyv * n * g * _dsilu(r, sig)).astype(BF16)
                don = dyv * (r * sig)
                dg_ref[:, vs] += jnp.sum(don * n, axis=0, keepdims=True)
                dn = don * g
                do = (rr * (dn - n * jnp.mean(dn * n, axis=-1, keepdims=True))).astype(BF16)
                st_b = s_ref[c, h]
                a = jnp.where(tri, _dot_nt(qc_b, kh_b), 0.0).astype(BF16)
                da = jnp.where(tri, _dot_nt(do, v), 0.0).astype(BF16)
                dst = dst_scr[h]
                dst_b = dst.astype(BF16)
                dv_ref[rows, vs] = (_dot_tn(a, do) + _dot_nt(kh_b, dst_b)).astype(BF16)
                dqb = _dot(do, st_b)
                dqc = _dot(da, kh_b)
                dkh_ = _dot_tn(da, qc_b) + _dot(v, dst_b)
                dgam = jnp.sum(st_b.astype(F32) * dst, axis=0, keepdims=True)
                dst_scr[h] = dst * gamh + _dot_tn(do, qb_b)
                dqk_ref[rows, ks] = ((dqb * ebh + dqc * (ebh * eih)) * scale).astype(BF16)
                dqk_ref[rows, kcols] = (dkh_ * eklh).astype(BF16)
                qq = dqc * qc
                kk = dkh_ * kh
                db = dqb * qb + qq - kk
                dbl = jnp.sum(kk - qq, axis=0, keepdims=True) + dgam * gamh
                dla_scr[rows, ks] = jnp.dot(triu_f, db, preferred_element_type=F32,
                                            precision=lax.Precision.HIGHEST) + dbl
        z = jnp.dot(alr_ref[...].astype(BF16), wau_ref[...].astype(BF16), preferred_element_type=F32) + ba_ref[...]
        dz = dla_scr[...] * (1.0 / GATE_TAU) * _sigmoid(-z) * _row_mask(blk, cb)
        dba_ref[...] += jnp.sum(dz, axis=0, keepdims=True)
        dz_ref[...] = dz.astype(BF16)

    rev = lambda i: (nb - 1 - i, 0)
    row = pl.BlockSpec((cb, d), rev)
    return pl.pallas_call(
        body, name="gla_bwd", grid=(nb,),
        in_specs=[row, pl.BlockSpec((cb, d), lambda i: (nb - 1 - i, 1)), pl.BlockSpec((cb, d), lambda i: (nb - 1 - i, 2)),
                  pl.BlockSpec((cb, dk_all), rev), pl.BlockSpec((cb, RANK_PAD), rev),
                  pl.BlockSpec((RANK_PAD, dk_all), lambda i: (0, 0)), pl.BlockSpec((1, dk_all), lambda i: (0, 0)),
                  pl.BlockSpec((1, d), lambda i: (0, 0)), row,
                  pl.BlockSpec((ncb, HEADS, dvh, dkh), lambda i: (nb - 1 - i, 0, 0, 0)), row],
        out_specs=[row, row, row, pl.BlockSpec((cb, dk_all), rev),
                   pl.BlockSpec((1, d), lambda i: (0, 0)), pl.BlockSpec((1, dk_all), lambda i: (0, 0))],
        out_shape=[jax.ShapeDtypeStruct((t, d), BF16)] * 3 + [jax.ShapeDtypeStruct((t, dk_all), BF16),
                   jax.ShapeDtypeStruct((1, d), F32), jax.ShapeDtypeStruct((1, dk_all), F32)],
        scratch_shapes=[pltpu.VMEM((HEADS, dvh, dkh), F32), pltpu.VMEM((cb, dk_all), F32)],
        compiler_params=_cparams(("arbitrary",)),
    )(p7, p7, p7, log_a, alr, wau, b_alpha, gla_g, o, states, dy)


def _input_grad(du_a, du_b, h0, g1, dh1):
    t, d = h0.shape
    tm = ROW_TILE
    s = t - HEAD_ROWS

    def body(dua_ref, dub_ref, h_ref, g_ref, dres_ref, gx_ref, gm_ref, dg_ref):
        i = pl.program_id(0)

        @pl.when(i == 0)
        def _():
            dg_ref[...] = jnp.zeros_like(dg_ref)

        x = h_ref[...]
        r = lax.rsqrt(jnp.mean(x * x, axis=-1, keepdims=True) + RMS_EPS)
        n = x * r
        du_ = dua_ref[...] + dub_ref[...]
        dg_ref[...] += jnp.sum(du_ * n, axis=0, keepdims=True)
        dn = du_ * g_ref[...]
        dh = dres_ref[...] + r * (dn - n * jnp.mean(dn * n, axis=-1, keepdims=True))

        @pl.when(i == 0)
        def _():
            gm_ref[...] = dh[PAD:HEAD_ROWS, :]
            gx_ref[...] = jnp.zeros_like(gx_ref)

        @pl.when(i > 0)
        def _():
            gx_ref[...] = dh

    row = pl.BlockSpec((tm, d), lambda i: (i, 0))
    vec = pl.BlockSpec((1, d), lambda i: (0, 0))
    return pl.pallas_call(
        body, name="input_grad", grid=(t // tm,),
        in_specs=[row, row, row, vec, row],
        out_specs=[pl.BlockSpec((tm, d), lambda i: (jnp.maximum(i - 1, 0), 0)),
                   pl.BlockSpec((N_META, d), lambda i: (0, 0)), vec],
        out_shape=[jax.ShapeDtypeStruct((s, d), F32), jax.ShapeDtypeStruct((N_META, d), F32),
                   jax.ShapeDtypeStruct((1, d), F32)],
        compiler_params=_cparams(("arbitrary",)),
    )(du_a, du_b, h0, g1, dh1)


REST = ("w_up", "w_down", "w_gla_o", "w_conf_o", "w_out")


def _chip_partials(grads, names, tag):
    core = lax.axis_index("c").reshape(1)
    from_sibling = _swap_core_halves(grads, "swap_core_halves_" + tag)
    return [_add_core_halves(core, g, s, "add_core_halves_" + n) for n, g, s in zip(names, grads, from_sibling)]


def _local_step(x, target, w, rest_shards):
    s, d = x.shape
    dk_all = d // 2
    w_in = w["w_in"]
    lo, hi = 3 * d, 3 * d + GLA_RANK
    wq = jnp.concatenate([w_in[:, :lo], w_in[:, hi:]], axis=1)
    w_alr = jnp.pad(w_in[:, lo:hi], ((0, 0), (0, RANK_PAD - GLA_RANK)))
    wau = jnp.pad(w["w_alpha_up"], ((0, RANK_PAD - GLA_RANK), (0, 0)))

    h0, u1 = _prep(x, w["meta_tokens"], w["norm_mix_g"])
    shards = [rest_shards[n] for n in REST]
    p7, gathered = _matmul(u1, wq, dims="nn", name="proj", hosted=_gather_plan(shards))
    w = dict(w)
    for n, slabs in zip(REST, _with_own_slab(gathered, shards)):
        w[n] = slabs if n == "w_up" else slabs.reshape(-1, slabs.shape[-1])
    dff = w["w_down"].shape[0]
    alr = _matmul(u1, w_alr, dims="nn", name="proj_alr")
    log_a = _log_gate(alr, wau, w["b_alpha"])
    o, y_gla, states = _gla_fwd(p7, log_a, w["gla_norm_g"], d)
    br_gla = _matmul(y_gla, w["w_gla_o"], dims="nn", name="gla_out")
    cc, s_c = _conf_fwd(p7, w["conf_dw_w"], w["conf_dw_b"], w["conf_ln_g"], w["conf_ln_b"], d)
    br_conf = _matmul(s_c, w["w_conf_o"], dims="nn", name="conf_out")
    merged = _merge(p7, br_gla, br_conf, d)
    h1 = _matmul(merged, w["w_out"], dims="nn", name="mix_out", add=h0)
    u2 = _rms_fwd(h1, w["norm_ffn_g"], "norm_ffn")
    up = _matmul(u2, w["w_up"], dims="nn", name="ffn_up", chips="b")
    y = _ffn_mid(up, w["ffn_dw_w"], w["ffn_dw_b"], dff)
    h2 = _matmul(y, w["w_down"], dims="nn", name="ffn_down", add=h1)
    loss, d_gf, dh2 = _loss_head(h2, target, w["final_norm_g"])

    g = {"final_norm_g": d_gf}
    dy = _matmul(dh2, w["w_down"], dims="nt", name="d_ffn_y")
    g["w_down"] = _matmul(y, dh2, dims="tn", name="dw_down", out_dtype=BF16)
    dup, g["ffn_dw_w"], g["ffn_dw_b"] = _ffn_mid_bwd(up, dy, w["ffn_dw_w"], w["ffn_dw_b"], dff)
    du2 = _matmul(dup, w["w_up"], dims="nt", name="d_u2", chips="b")
    g["w_up"] = _matmul(u2, dup, dims="tn", name="dw_up", out_dtype=BF16, chips="out")
    dh1, g["norm_ffn_g"] = _rms_bwd(du2, h1, w["norm_ffn_g"], dh2, "norm_ffn_bwd")
    dmerged = _matmul(dh1, w["w_out"], dims="nt", name="d_merged")
    g["w_out"] = _matmul(merged, dh1, dims="tn", name="dw_out", out_dtype=BF16)
    d_br_gla, d_br_conf, dgg, dgc = _merge_bwd(p7, br_gla, br_conf, dmerged, d)
    dsc = _matmul(d_br_conf, w["w_conf_o"], dims="nt", name="d_conf_s")
    g["w_conf_o"] = _matmul(s_c, d_br_conf, dims="tn", name="dw_conf_o", out_dtype=BF16)
    dc1, dc2, g["conf_dw_w"], g["conf_dw_b"], g["conf_ln_g"], g["conf_ln_b"] = _conf_bwd(
        p7, cc, dsc, w["conf_dw_w"], w["conf_ln_g"], w["conf_ln_b"], d)
    dyg = _matmul(d_br_gla, w["w_gla_o"], dims="nt", name="d_gla_y")
    g["w_gla_o"] = _matmul(y_gla, d_br_gla, dims="tn", name="dw_gla_o", out_dtype=BF16)
    dqk, dv, dr, dz, g["gla_norm_g"], g["b_alpha"] = _gla_bwd(
        p7, log_a, alr, wau, w["b_alpha"], w["gla_norm_g"], o, states, dyg, d)
    dalr = _matmul(dz, wau, dims="nt", name="d_alr", out_dtype=BF16)
    g["w_alpha_up"] = _matmul(alr, dz, dims="tn", name="dw_alpha_up")[:GLA_RANK]
    dp7 = jnp.concatenate([dqk, dv, dr, dc1, dc2, dgg, dgc], axis=1)
    rest_grads = [g.pop(n) for n in REST]
    rest_grads = [a if a.ndim == 3 else a.reshape(N_CHIPS, -1, a.shape[-1]) for a in rest_grads]
    parts = dict(zip(REST, _chip_partials(rest_grads, REST, "rest")))
    dwq, arrived = _matmul(u1, dp7, dims="tn", name="dw_in", out_dtype=BF16,
                           hosted=_exchange_plan([parts[n] for n in REST]))
    from_chips = dict(zip(REST, arrived))
    dw_alr = _matmul(u1, dalr, dims="tn", name="dw_in_alr", out_dtype=BF16)
    dw_in = _to_chip_major(jnp.concatenate([dwq[:, :lo], dw_alr[:, :GLA_RANK], dwq[:, lo:]], axis=1))
    parts["w_in"] = _chip_partials([dw_in], ["w_in"], "w_in")[0]
    du1a, arrived = _matmul(dp7, wq, dims="nt", name="d_u1", hosted=_exchange_plan([parts["w_in"]]))
    from_chips["w_in"] = arrived[0]
    du1b = _matmul(dalr, w_alr, dims="nt", name="d_u1_alr")
    grad_x, g["meta_tokens"], g["norm_mix_g"] = _input_grad(du1a, du1b, h0, w["norm_mix_g"], dh1)
    return loss, grad_x, g, parts, from_chips


HBM_SPEC = pl.BlockSpec(memory_space=pltpu.HBM)
FLIPS = ((1, 0), (0, 1), (1, 1))


def _place():
    x, y, c = lax.axis_index("x"), lax.axis_index("y"), lax.axis_index("c")
    return x, y, c


def _half_rows(ref, h, lead=()):
    rh = ref.shape[-2] // 2
    return ref.at[(*lead, pl.ds(pl.multiple_of(h * rh, 2 * SUBLANES), rh), slice(None))]


class _Hosted:
    def __init__(self, operands, out_shapes, sem_shapes, start, finish):
        self.operands, self.out_shapes, self.sem_shapes = operands, out_shapes, sem_shapes
        self.start, self.finish = start, finish


def _run_hosted(h, name):
    n_in, n_out = len(h.operands), len(h.out_shapes)

    def body(*refs):
        ins, outs, sems = refs[:n_in], refs[n_in:n_in + n_out], refs[n_in + n_out:]
        h.start(ins, outs, sems)
        h.finish(ins, outs, sems)

    return pl.pallas_call(
        body, name=name, in_specs=[HBM_SPEC] * n_in, out_specs=[HBM_SPEC] * n_out,
        out_shape=list(h.out_shapes), scratch_shapes=list(h.sem_shapes),
    )(*h.operands)


def _gather_plan(shards):
    n = len(shards)

    def copies(ins, outs, sems, kinds):
        send_sems, recv_sems = sems
        x, y, c = _place()
        chips = [(x ^ fx, y ^ fy) for fx, fy in FLIPS]
        me, sibling = (x, y, c), (x, y, 1 - c)

        def copy(k, sem, chip, h, to, src=None):
            slot = _half_rows(outs[k], h, lead=(2 * chip[0] + chip[1],))
            return pltpu.make_async_remote_copy(src_ref=slot if src is None else src, dst_ref=slot,
                                                send_sem=send_sems.at[sem], recv_sem=recv_sems.at[sem],
                                                device_id=to, device_id_type=MESH)

        make = {
            "first": lambda k, j, chip: copy(k, 3 * k + j, (x, y), c, (*chip, c), src=_half_rows(ins[k], c)),
            "landed": lambda k, j, chip: copy(k, 3 * k + j, chip, c, me),
            "passed": lambda k, j, chip: copy(k, 3 * n + 3 * k + j, chip, c, sibling),
            "from_sibling": lambda k, j, chip: copy(k, 3 * n + 3 * k + j, chip, 1 - c, me),
        }
        return [[make[kind](k, j, chip) for k in range(n) for j, chip in enumerate(chips)] for kind in kinds]

    def start(ins, outs, sems):
        for cp in copies(ins, outs, sems, ["first"])[0]:
            cp.start()

    def finish(ins, outs, sems):
        first, landed, passed, from_sibling = copies(ins, outs, sems, ["first", "landed", "passed", "from_sibling"])
        for arrived, fwd in zip(landed, passed):
            arrived.wait_recv()
            fwd.start()
        for cp in from_sibling:
            cp.wait_recv()
        for cp in first + passed:
            cp.wait_send()

    return _Hosted(list(shards), [jax.ShapeDtypeStruct((N_CHIPS, *s.shape), s.dtype) for s in shards],
                   [pltpu.SemaphoreType.DMA((6 * n,)), pltpu.SemaphoreType.DMA((6 * n,))], start, finish)


def _with_own_slab(gathered, shards):
    me = 2 * lax.axis_index("x") + lax.axis_index("y")
    return [lax.dynamic_update_index_in_dim(o, s, me, 0) for o, s in zip(gathered, shards)]


def _swap_core_halves(grads, name):
    n = len(grads)

    def body(*refs):
        ins, outs = refs[:n], refs[n:2 * n]
        send_sems, recv_sems = refs[2 * n:]
        x, y, c = _place()
        cps = [pltpu.make_async_remote_copy(
            src_ref=_half_rows(ins[k], 1 - c, lead=(slice(None),)), dst_ref=outs[k], send_sem=send_sems.at[k],
            recv_sem=recv_sems.at[k], device_id=(x, y, 1 - c), device_id_type=MESH) for k in range(n)]
        for cp in cps:
            cp.start()
        for cp in cps:
            cp.wait()

    return pl.pallas_call(
        body, name=name, in_specs=[HBM_SPEC] * n, out_specs=[HBM_SPEC] * n,
        out_shape=[jax.ShapeDtypeStruct((g.shape[0], g.shape[1] // 2, g.shape[2]), g.dtype) for g in grads],
        scratch_shapes=[pltpu.SemaphoreType.DMA((n,)), pltpu.SemaphoreType.DMA((n,))],
    )(*grads)


def _exchange_plan(parts):
    n = len(parts)

    def copies(ins, outs, sems):
        send_sems, recv_sems = sems
        x, y, c = _place()
        cps = []
        for k in range(n):
            for j, (fx, fy) in enumerate(FLIPS):
                tx, ty = x ^ fx, y ^ fy
                cps.append(pltpu.make_async_remote_copy(
                    src_ref=ins[k].at[2 * tx + ty], dst_ref=outs[k].at[j], send_sem=send_sems.at[3 * k + j],
                    recv_sem=recv_sems.at[3 * k + j], device_id=(tx, ty, c), device_id_type=MESH))
        return cps

    def start(ins, outs, sems):
        for cp in copies(ins, outs, sems):
            cp.start()

    def finish(ins, outs, sems):
        for cp in copies(ins, outs, sems):
            cp.wait()

    return _Hosted(list(parts), [jax.ShapeDtypeStruct((3, *p.shape[1:]), p.dtype) for p in parts],
                   [pltpu.SemaphoreType.DMA((3 * n,)), pltpu.SemaphoreType.DMA((3 * n,))], start, finish)


def _join_core_halves(fulls):
    n = len(fulls)

    def body(*refs):
        bufs = refs[n:2 * n]
        send_sems, recv_sems = refs[2 * n:]
        x, y, c = _place()
        cps = [pltpu.make_async_remote_copy(
            src_ref=_half_rows(bufs[k], c), dst_ref=_half_rows(bufs[k], c), send_sem=send_sems.at[k],
            recv_sem=recv_sems.at[k], device_id=(x, y, 1 - c), device_id_type=MESH) for k in range(n)]
        for cp in cps:
            cp.start()
        for k in range(n):
            cps[k].wait_send()
            pltpu.make_async_remote_copy(
                src_ref=_half_rows(bufs[k], c), dst_ref=_half_rows(bufs[k], 1 - c), send_sem=send_sems.at[k],
                recv_sem=recv_sems.at[k], device_id=(x, y, 1 - c), device_id_type=MESH).wait_recv()

    return pl.pallas_call(
        body, name="join_core_halves", in_specs=[HBM_SPEC] * n, out_specs=[HBM_SPEC] * n,
        out_shape=[jax.ShapeDtypeStruct(f.shape, f.dtype) for f in fulls],
        input_output_aliases={k: k for k in range(n)},
        scratch_shapes=[pltpu.SemaphoreType.DMA((n,)), pltpu.SemaphoreType.DMA((n,))],
    )(*fulls)


def _gather_small(block, name):
    m, n = block.shape

    def body(x_ref, out_ref, send_sems, recv_sems, local_sem):
        x, y, c = _place()
        me, sibling = (x, y, c), (x, y, 1 - c)
        chips = [(x ^ fx, y ^ fy) for fx, fy in FLIPS]

        def rows(px, py, pc):
            return out_ref.at[pl.ds((4 * px + 2 * py + pc) * m, m), :]

        def copy(k, blk, to, src=None):
            return pltpu.make_async_remote_copy(
                src_ref=rows(*blk) if src is None else src, dst_ref=rows(*blk),
                send_sem=send_sems.at[k], recv_sem=recv_sems.at[k], device_id=to, device_id_type=MESH)

        mine = pltpu.make_async_copy(x_ref, rows(*me), local_sem)
        mine.start()
        first = [copy(0, me, sibling, src=x_ref)]
        first += [copy(1 + j, me, (*chip, c), src=x_ref) for j, chip in enumerate(chips)]
        for cp in first:
            cp.start()
        passed = [copy(4 + j, (*chip, c), sibling) for j, chip in enumerate(chips)]
        for j, chip in enumerate(chips):
            copy(1 + j, (*chip, c), me).wait_recv()
            passed[j].start()
        copy(0, sibling, me).wait_recv()
        for j, chip in enumerate(chips):
            copy(4 + j, (*chip, 1 - c), me).wait_recv()
        for cp in first + passed:
            cp.wait_send()
        mine.wait()

    out = pl.pallas_call(
        body, name=name,
        out_shape=jax.ShapeDtypeStruct((8 * m, n), block.dtype),
        in_specs=[pl.BlockSpec(memory_space=pltpu.VMEM)],
        out_specs=pl.BlockSpec(memory_space=pltpu.VMEM),
        scratch_shapes=[pltpu.SemaphoreType.DMA((7,)), pltpu.SemaphoreType.DMA((7,)), pltpu.SemaphoreType.DMA],
    )(block)
    return out.reshape(8, m, n)


def _add_core_halves(core, grad, from_sibling, name):
    nc, r, cols = grad.shape
    rh = r // 2

    def body(core_ref, g_ref, s_ref, o_ref):
        o_ref[...] = (g_ref[...].astype(F32) + s_ref[...].astype(F32)).astype(BF16)

    spec = pl.BlockSpec((1, rh, cols), lambda t, core_ref: (t, 0, 0))
    return pl.pallas_call(
        body, name=name,
        grid_spec=pltpu.PrefetchScalarGridSpec(
            num_scalar_prefetch=1, grid=(nc,),
            in_specs=[pl.BlockSpec((1, rh, cols), lambda t, core_ref: (t, core_ref[0], 0)), spec], out_specs=spec),
        out_shape=jax.ShapeDtypeStruct((nc, rh, cols), BF16),
        compiler_params=_cparams(("parallel",)),
    )(core, grad, from_sibling)


def _sum_chip_partials(place, parts, others, name):
    _, rh, cols = parts.shape
    tr = _pick(rh, (128, 176, 64, 32, 16, 8))
    nb = rh // tr

    def body(place_ref, a_ref, b_ref, o_ref):
        acc = a_ref[0].astype(F32)
        for j in range(3):
            acc = acc + b_ref[j].astype(F32)
        o_ref[...] = acc

    return pl.pallas_call(
        body, name=name,
        grid_spec=pltpu.PrefetchScalarGridSpec(
            num_scalar_prefetch=1, grid=(nb,),
            in_specs=[pl.BlockSpec((1, tr, cols), lambda i, place_ref: (place_ref[0], i, 0)),
                      pl.BlockSpec((3, tr, cols), lambda i, place_ref: (0, i, 0))],
            out_specs=pl.BlockSpec((tr, cols), lambda i, place_ref: (place_ref[1] * nb + i, 0))),
        out_shape=jax.ShapeDtypeStruct((2 * rh, cols), F32),
        compiler_params=_cparams(("parallel",)),
    )(place, parts, others)


def _sum_devices(blocks):
    n, m, _ = blocks.shape

    def body(b_ref, o_ref):
        acc = b_ref[0]
        for j in range(1, n):
            acc = acc + b_ref[j]
        o_ref[...] = acc

    return pl.pallas_call(
        body, name="sum_devices", out_shape=jax.ShapeDtypeStruct((m, LANES), F32),
        in_specs=[pl.BlockSpec(memory_space=pltpu.VMEM)], out_specs=pl.BlockSpec(memory_space=pltpu.VMEM),
    )(blocks)


def _adamw(w, g, m, v, name):
    rws, cols = w.shape
    tr = rws
    for cand in (256, 128, 64, 32, 16, 8):
        if rws % cand == 0 and cand * cols * 4 <= 2 * 1024 * 1024:
            tr = cand
            break
    c1 = 1.0 - ADAM_B1 ** ADAM_STEP
    c2 = 1.0 - ADAM_B2 ** ADAM_STEP

    def body(w_ref, g_ref, m_ref, v_ref, d_ref, nm_ref, nv_ref):
        gv = g_ref[...]
        nm = ADAM_B1 * m_ref[...] + (1.0 - ADAM_B1) * gv
        nv = ADAM_B2 * v_ref[...] + (1.0 - ADAM_B2) * (gv * gv)
        m_hat = nm / c1
        v_hat = nv / c2
        d_ref[...] = -ADAM_LR * (m_hat / (jnp.sqrt(v_hat) + ADAM_EPS) + ADAM_WD * w_ref[...])
        nm_ref[...] = nm
        nv_ref[...] = nv

    spec = pl.BlockSpec((tr, cols), lambda i: (i, 0))
    return pl.pallas_call(
        body, name=name, grid=(rws // tr,), in_specs=[spec] * 4, out_specs=[spec] * 3,
        out_shape=[jax.ShapeDtypeStruct((rws, cols), F32)] * 3,
        compiler_params=_cparams(("parallel",)),
    )(w, g, m, v)


WEIGHTS = (
    ("meta_tokens", (16, 1024), 1), ("norm_mix_g", (1024,), None), ("w_in", (1024, 7184), 1),
    ("w_alpha_up", (16, 512), 1), ("b_alpha", (512,), None), ("gla_norm_g", (1024,), None),
    ("w_gla_o", (1024, 1024), 0), ("conf_dw_w", (31, 1024), 1), ("conf_dw_b", (1024,), None),
    ("conf_ln_g", (1024,), None), ("conf_ln_b", (1024,), None), ("w_conf_o", (1024, 1024), 0),
    ("w_out", (1024, 1024), 0), ("norm_ffn_g", (1024,), None), ("w_up", (1024, 5632), 1),
    ("ffn_dw_w", (3, 2816), 1), ("ffn_dw_b", (2816,), None), ("w_down", (2816, 1024), 0),
    ("final_norm_g", (1024,), None),
)
BIG = ("w_in", "w_up", "w_down", "w_gla_o", "w_conf_o", "w_out")
SMALL_SHARDED = ("meta_tokens", "w_alpha_up", "conf_dw_w", "ffn_dw_w")
REPLICATED = tuple(n for n, _, ax in WEIGHTS if ax is None)
SHAPES = {n: s for n, s, _ in WEIGHTS}
AXIS = {n: ax for n, _, ax in WEIGHTS}
N_CHIPS = 4


def _shard_shape(name):
    s = list(SHAPES[name])
    s[AXIS[name]] //= N_CHIPS
    return tuple(s)


def _pack(parts, mult):
    flat = jnp.concatenate([p.reshape(-1) for p in parts])
    pad = (-flat.shape[0]) % mult
    return jnp.pad(flat, (0, pad))


def _unpack(flat, names, shape_of):
    out, off = {}, 0
    for n in names:
        shp = shape_of(n)
        size = math.prod(shp)
        out[n] = flat[off:off + size].reshape(shp)
        off += size
    return out


def _to_chip_major(full):
    r, cols = full.shape
    return full.reshape(r, N_CHIPS, cols // N_CHIPS).transpose(1, 0, 2)


def _from_chip_major(slabs):
    nc, r, cs = slabs.shape
    return slabs.transpose(1, 0, 2).reshape(r, nc * cs)


def kernel(x, meta_tokens, norm_mix_g, w_in, w_alpha_up, b_alpha, gla_norm_g, w_gla_o, conf_dw_w, conf_dw_b, conf_ln_g, conf_ln_b, w_conf_o, w_out, norm_ffn_g, w_up, ffn_dw_w, ffn_dw_b, w_down, final_norm_g, loss_target, m_meta_tokens, m_norm_mix_g, m_w_in, m_w_alpha_up, m_b_alpha, m_gla_norm_g, m_w_gla_o, m_conf_dw_w, m_conf_dw_b, m_conf_ln_g, m_conf_ln_b, m_w_conf_o, m_w_out, m_norm_ffn_g, m_w_up, m_ffn_dw_w, m_ffn_dw_b, m_w_down, m_final_norm_g, v_meta_tokens, v_norm_mix_g, v_w_in, v_w_alpha_up, v_b_alpha, v_gla_norm_g, v_w_gla_o, v_conf_dw_w, v_conf_dw_b, v_conf_ln_g, v_conf_ln_b, v_w_conf_o, v_w_out, v_norm_ffn_g, v_w_up, v_ffn_dw_w, v_ffn_dw_b, v_w_down, v_final_norm_g):
    names = [n for n, _, _ in WEIGHTS]
    w_args = (meta_tokens, norm_mix_g, w_in, w_alpha_up, b_alpha, gla_norm_g, w_gla_o, conf_dw_w, conf_dw_b, conf_ln_g,
              conf_ln_b, w_conf_o, w_out, norm_ffn_g, w_up, ffn_dw_w, ffn_dw_b, w_down, final_norm_g)
    m_args = (m_meta_tokens, m_norm_mix_g, m_w_in, m_w_alpha_up, m_b_alpha, m_gla_norm_g, m_w_gla_o, m_conf_dw_w,
              m_conf_dw_b, m_conf_ln_g, m_conf_ln_b, m_w_conf_o, m_w_out, m_norm_ffn_g, m_w_up, m_ffn_dw_w, m_ffn_dw_b,
              m_w_down, m_final_norm_g)
    v_args = (v_meta_tokens, v_norm_mix_g, v_w_in, v_w_alpha_up, v_b_alpha, v_gla_norm_g, v_w_gla_o, v_conf_dw_w,
              v_conf_dw_b, v_conf_ln_g, v_conf_ln_b, v_w_conf_o, v_w_out, v_norm_ffn_g, v_w_up, v_ffn_dw_w, v_ffn_dw_b,
              v_w_down, v_final_norm_g)
    in_shape = {n: a.shape for n, a in zip(names, w_args)}
    local = {n: a.reshape(_shard_shape(n) if AXIS[n] is not None else SHAPES[n]) for n, a in zip(names, w_args)}
    m_loc = {n: a.reshape(local[n].shape) for n, a in zip(names, m_args)}
    v_loc = {n: a.reshape(local[n].shape) for n, a in zip(names, v_args)}

    me = 2 * lax.axis_index("x") + lax.axis_index("y")

    w_in_shard = local["w_in"].astype(BF16)
    w_in_all = _with_own_slab(_run_hosted(_gather_plan([w_in_shard]), "gather_w_in"), [w_in_shard])[0]
    small = _pack([local[n] for n in SMALL_SHARDED], 8 * LANES).reshape(-1, LANES)
    small_all = _gather_small(small, "gather_small_weights")[::2].reshape(N_CHIPS, -1)
    per_chip_small = [_unpack(small_all[t], SMALL_SHARDED, _shard_shape) for t in range(N_CHIPS)]
    full = {n: jnp.concatenate([per_chip_small[t][n] for t in range(N_CHIPS)], axis=1) for n in SMALL_SHARDED}
    full["w_in"] = _from_chip_major(w_in_all)
    for n in REPLICATED:
        full[n] = local[n].reshape(1, -1)

    loss_part, grad_x, grads, chip_part, from_chips = _local_step(
        x[0], loss_target[0], full, {n: local[n].astype(BF16) for n in REST})

    smalls = REPLICATED + SMALL_SHARDED
    rep = _pack([grads[n] for n in smalls] + [loss_part], 8 * LANES).reshape(-1, LANES)
    rep_sum = _sum_devices(_gather_small(rep, "gather_small_grads")).reshape(-1)
    g_loc = _unpack(rep_sum, smalls, lambda n: SHAPES[n])
    loss = rep_sum[sum(math.prod(SHAPES[n]) for n in smalls)]
    for n in SMALL_SHARDED:
        width = _shard_shape(n)[1]
        g_loc[n] = lax.dynamic_slice_in_dim(g_loc[n], me * width, width, axis=1)

    place = jnp.stack([me, lax.axis_index("c")])
    reduced = [_sum_chip_partials(place, chip_part[n], from_chips[n], "sum_chip_partials_" + n) for n in BIG]
    g_loc.update(zip(BIG, _join_core_halves(reduced)))

    delta, new_m, new_v = {}, {}, {}
    for n in BIG:
        delta[n], new_m[n], new_v[n] = _adamw(local[n], g_loc[n], m_loc[n], v_loc[n], "adamw_" + n)
    rest = SMALL_SHARDED + REPLICATED
    pk = lambda dct: _pack([dct[n] for n in rest], 8 * LANES).reshape(-1, LANES)
    ds, ms, vs = _adamw(pk(local), pk(g_loc), pk(m_loc), pk(v_loc), "adamw_small")
    shape_loc = lambda n: local[n].shape
    for dct, flat in ((delta, ds), (new_m, ms), (new_v, vs)):
        dct.update(_unpack(flat.reshape(-1), rest, shape_loc))

    outs = [loss, grad_x[None]]
    for dct in (g_loc, delta, new_m, new_v):
        outs += [dct[n].reshape(in_shape[n]) for n in names]
    return tuple(outs)
```

```python
import functools
import math

import jax
import jax.numpy as jnp
from jax import lax
from jax.experimental import pallas as pl
from jax.experimental.pallas import tpu as pltpu

F32 = jnp.float32
BF16 = jnp.bfloat16

N_META = 16
PAD = 240
HEAD_ROWS = PAD + N_META
HEADS = 4
GLA_RANK = 16
RANK_PAD = 128
GATE_TAU = 16.0
CHUNK = 64
CONF_K = 31
FFN_K = 3
RMS_EPS = 1e-6
LN_EPS = 1e-5
ADAM_LR, ADAM_B1, ADAM_B2, ADAM_EPS, ADAM_WD, ADAM_STEP = 0.001, 0.9, 0.999, 1e-08, 0.01, 10

ROW_TILE = 256
HALO = 32
LANES = 128
V7X_VMEM_LIMIT = 56 * 1024 * 1024
MESH = pl.DeviceIdType.MESH


def _cparams(sem):
    return pltpu.CompilerParams(dimension_semantics=sem, vmem_limit_bytes=V7X_VMEM_LIMIT)


def _sigmoid(x):
    return 1.0 / (1.0 + jnp.exp(-x))


def _pick(n, prefs):
    for p in prefs:
        if n % p == 0:
            return p
    return n


def _matmul(a, b, *, dims, name, tm=None, tn=None, tk=None, out_dtype=F32, add=None, chips=None, hosted=None):
    if chips == "b":
        nc, r, cs = b.shape
        b_shape = (r, nc * cs)
    else:
        b_shape = b.shape
    if dims == "nn":
        (m, k), (_, n) = a.shape, b_shape
    elif dims == "nt":
        (m, k), (n, _) = a.shape, b_shape
    else:
        (k, m), (_, n) = a.shape, b_shape
    tm = tm or _pick(m, (768, 1024, 1408, 512, 256, 128))
    tn = tn or _pick(n, (1024, 1408, 512, 256, 128))
    tk = tk or _pick(k, (1024, 768, 1408, 512, 256, 128))
    if chips == "b":
        tn, tk = (cs, tk) if dims == "nn" else (tn, cs)
    if chips == "out":
        tn, tm = n // N_CHIPS, m
    nk = k // tk
    assert m % tm == 0 and n % tn == 0 and k % tk == 0, (name, m, n, k, tm, tn, tk)
    a_spec = {"nn": pl.BlockSpec((tm, tk), lambda i, j, kk: (i, kk)),
              "nt": pl.BlockSpec((tm, tk), lambda i, j, kk: (i, kk)),
              "tn": pl.BlockSpec((tk, tm), lambda i, j, kk: (kk, i))}[dims]
    if chips == "b":
        b_spec = {"nn": pl.BlockSpec((None, tk, tn), lambda i, j, kk: (j, kk, 0)),
                  "nt": pl.BlockSpec((None, tn, tk), lambda i, j, kk: (kk, j, 0))}[dims]
    else:
        b_spec = {"nn": pl.BlockSpec((tk, tn), lambda i, j, kk: (kk, j)),
                  "nt": pl.BlockSpec((tn, tk), lambda i, j, kk: (j, kk)),
                  "tn": pl.BlockSpec((tk, tn), lambda i, j, kk: (kk, j))}[dims]
    contract = {"nn": (((1,), (0,)), ((), ())), "nt": (((1,), (1,)), ((), ())), "tn": (((0,), (0,)), ((), ()))}[dims]
    if chips == "out":
        o_spec = pl.BlockSpec((None, tm, tn), lambda i, j, kk: (j, 0, 0))
        out_struct = jax.ShapeDtypeStruct((N_CHIPS, m, tn), out_dtype)
    else:
        o_spec = pl.BlockSpec((tm, tn), lambda i, j, kk: (i, j))
        out_struct = jax.ShapeDtypeStruct((m, n), out_dtype)
    has_add = add is not None

    def body(*refs):
        refs = list(refs)
        a_ref, b_ref = refs[:2]
        add_ref = refs[2] if has_add else None
        n_in = 2 + has_add
        h_ins = refs[n_in:n_in + n_hin]
        o_ref = refs[n_in + n_hin]
        h_outs = refs[n_in + n_hin + 1:n_in + n_hin + 1 + n_hout]
        acc_ref = refs[n_in + n_hin + 1 + n_hout]
        h_sems = refs[n_in + n_hin + 2 + n_hout:]
        i, j, kk = pl.program_id(0), pl.program_id(1), pl.program_id(2)
        if hosted is not None:
            @pl.when((i == 0) & (j == 0) & (kk == 0))
            def _():
                hosted.start(h_ins, h_outs, h_sems)

        prod = lax.dot_general(a_ref[...].astype(BF16), b_ref[...].astype(BF16), contract,
                               preferred_element_type=F32)

        @pl.when(kk == 0)
        def _():
            acc_ref[...] = prod + add_ref[...].astype(F32) if has_add else prod

        @pl.when(kk > 0)
        def _():
            acc_ref[...] += prod

        @pl.when(kk == nk - 1)
        def _():
            o_ref[...] = acc_ref[...].astype(out_dtype)

        if hosted is not None:
            @pl.when((i == m // tm - 1) & (j == n // tn - 1) & (kk == nk - 1))
            def _():
                hosted.finish(h_ins, h_outs, h_sems)

    n_hin = len(hosted.operands) if hosted is not None else 0
    n_hout = len(hosted.out_shapes) if hosted is not None else 0
    in_specs = [a_spec, b_spec] + ([o_spec] if has_add else []) + [HBM_SPEC] * n_hin
    args = (a, b) + ((add,) if has_add else ()) + (tuple(hosted.operands) if hosted is not None else ())
    outs = pl.pallas_call(
        body, name=name, grid=(m // tm, n // tn, nk),
        in_specs=in_specs, out_specs=[o_spec] + [HBM_SPEC] * n_hout,
        out_shape=[out_struct] + (list(hosted.out_shapes) if hosted is not None else []),
        scratch_shapes=[pltpu.VMEM((tm, tn), F32)] + (list(hosted.sem_shapes) if hosted is not None else []),
        compiler_params=_cparams(("arbitrary",) * 3 if hosted is not None else ("parallel", "parallel", "arbitrary")),
    )(*args)
    return outs[0] if hosted is None else (outs[0], list(outs[1:]))


def _row_mask(tile_index, rows):
    r = tile_index * rows + lax.broadcasted_iota(jnp.int32, (rows, 1), 0)
    return (r >= PAD).astype(F32)


def _prep(x, meta, g1):
    s, d = x.shape
    t = HEAD_ROWS + s
    tm = ROW_TILE

    def body(x_ref, meta_ref, g_ref, h_ref, u_ref):
        i = pl.program_id(0)

        @pl.when(i == 0)
        def _():
            h_ref[0:PAD, :] = jnp.zeros((PAD, d), F32)
            h_ref[PAD:HEAD_ROWS, :] = meta_ref[...]

        @pl.when(i > 0)
        def _():
            h_ref[...] = x_ref[...]

        h = h_ref[...]
        r = lax.rsqrt(jnp.mean(h * h, axis=-1, keepdims=True) + RMS_EPS)
        u_ref[...] = (h * r * g_ref[...]).astype(BF16)

    return pl.pallas_call(
        body, name="prep", grid=(t // tm,),
        in_specs=[pl.BlockSpec((tm, d), lambda i: (jnp.maximum(i - 1, 0), 0)),
                  pl.BlockSpec((N_META, d), lambda i: (0, 0)),
                  pl.BlockSpec((1, d), lambda i: (0, 0))],
        out_specs=[pl.BlockSpec((tm, d), lambda i: (i, 0)), pl.BlockSpec((tm, d), lambda i: (i, 0))],
        out_shape=[jax.ShapeDtypeStruct((t, d), F32), jax.ShapeDtypeStruct((t, d), BF16)],
        compiler_params=_cparams(("parallel",)),
    )(x, meta, g1)


def _log_gate(alr, wau, b_alpha):
    t = alr.shape[0]
    dk = wau.shape[1]
    tm = ROW_TILE

    def body(alr_ref, w_ref, b_ref, o_ref):
        z = jnp.dot(alr_ref[...].astype(BF16), w_ref[...].astype(BF16), preferred_element_type=F32) + b_ref[...]
        ls = jnp.minimum(z, 0.0) - jnp.log(1.0 + jnp.exp(-jnp.abs(z)))
        o_ref[...] = ls * (1.0 / GATE_TAU) * _row_mask(pl.program_id(0), tm)

    return pl.pallas_call(
        body, name="log_gate", grid=(t // tm,),
        in_specs=[pl.BlockSpec((tm, RANK_PAD), lambda i: (i, 0)),
                  pl.BlockSpec((RANK_PAD, dk), lambda i: (0, 0)),
                  pl.BlockSpec((1, dk), lambda i: (0, 0))],
        out_specs=pl.BlockSpec((tm, dk), lambda i: (i, 0)),
        out_shape=jax.ShapeDtypeStruct((t, dk), F32),
        compiler_params=_cparams(("parallel",)),
    )(alr, wau, b_alpha)


def _tri(n, upper=False):
    r = lax.broadcasted_iota(jnp.int32, (n, n), 0)
    c = lax.broadcasted_iota(jnp.int32, (n, n), 1)
    return (r <= c) if upper else (r >= c)


_NT = (((1,), (1,)), ((), ()))
_TN = (((0,), (0,)), ((), ()))


def _dot(a, b):
    return jnp.dot(a, b, preferred_element_type=F32)


def _dot_nt(a, b):
    return lax.dot_general(a, b, _NT, preferred_element_type=F32)


def _dot_tn(a, b):
    return lax.dot_general(a, b, _TN, preferred_element_type=F32)


def _chunk_decays(la, tri_f32):
    b = jnp.dot(tri_f32, la, preferred_element_type=F32, precision=lax.Precision.HIGHEST)
    bl = b[CHUNK - 1:CHUNK, :]
    return jnp.exp(b), jnp.exp(bl - b), jnp.exp(-bl), jnp.exp(bl)


def _gla_fwd(p7, log_a, gla_g, d):
    t = p7.shape[0]
    dk_all = d // 2
    dkh, dvh = dk_all // HEADS, d // HEADS
    cb = ROW_TILE
    ncb = cb // CHUNK
    scale = dkh ** -0.5

    def body(qk_ref, v_ref, r_ref, la_ref, g_ref, o_ref, y_ref, s_ref, st_scr):
        @pl.when(pl.program_id(0) == 0)
        def _():
            st_scr[...] = jnp.zeros_like(st_scr)

        tri = _tri(CHUNK)
        tri_f = tri.astype(F32)
        for c in range(ncb):
            rows = slice(c * CHUNK, (c + 1) * CHUNK)
            eb, ekl, ebl_inv, gam = _chunk_decays(la_ref[rows, :], tri_f)
            for h in range(HEADS):
                ks = slice(h * dkh, (h + 1) * dkh)
                vs = slice(h * dvh, (h + 1) * dvh)
                q = qk_ref[rows, ks].astype(F32) * scale
                k = qk_ref[rows, dk_all + h * dkh:dk_all + (h + 1) * dkh].astype(F32)
                v = v_ref[rows, vs].astype(BF16)
                qb = q * eb[:, ks]
                kh = (k * ekl[:, ks]).astype(BF16)
                qc = (qb * ebl_inv[:, ks]).astype(BF16)
                a = jnp.where(tri, _dot_nt(qc, kh), 0.0)
                st = st_scr[h]
                st_b = st.astype(BF16)
                s_ref[c, h] = st_b
                o = _dot_nt(qb.astype(BF16), st_b) + _dot(a.astype(BF16), v)
                st_scr[h] = st * gam[:, ks] + _dot_tn(v, kh)
                o_ref[rows, vs] = o
                rr = lax.rsqrt(jnp.mean(o * o, axis=-1, keepdims=True) + RMS_EPS)
                r = r_ref[rows, vs].astype(F32)
                y_ref[rows, vs] = (o * rr * g_ref[:, vs] * (r * _sigmoid(r))).astype(BF16)

    return pl.pallas_call(
        body, name="gla_fwd", grid=(t // cb,),
        in_specs=[pl.BlockSpec((cb, d), lambda i: (i, 0)),
                  pl.BlockSpec((cb, d), lambda i: (i, 1)),
                  pl.BlockSpec((cb, d), lambda i: (i, 2)),
                  pl.BlockSpec((cb, dk_all), lambda i: (i, 0)),
                  pl.BlockSpec((1, d), lambda i: (0, 0))],
        out_specs=[pl.BlockSpec((cb, d), lambda i: (i, 0)),
                   pl.BlockSpec((cb, d), lambda i: (i, 0)),
                   pl.BlockSpec((ncb, HEADS, dvh, dkh), lambda i: (i, 0, 0, 0))],
        out_shape=[jax.ShapeDtypeStruct((t, d), F32), jax.ShapeDtypeStruct((t, d), BF16),
                   jax.ShapeDtypeStruct((t // CHUNK, HEADS, dvh, dkh), BF16)],
        scratch_shapes=[pltpu.VMEM((HEADS, dvh, dkh), F32)],
        compiler_params=_cparams(("arbitrary",)),
    )(p7, p7, p7, log_a, gla_g)


SUBLANES = 8


def _tap_phases(n_taps, first):
    phases = {}
    for j in range(n_taps):
        e = first + j
        phases.setdefault(e % SUBLANES, []).append((j, e - e % SUBLANES))
    return phases


CONV_ROWS = 64


def _shifted_windows(ext_ref, shf_ref, phases, rows, ls):
    for p, taps in phases.items():
        if p:
            span = max(off for _, off in taps) + rows
            shf_ref[p, 0:span, :] = ext_ref[p:p + span, ls]

    def window(p, start, n):
        return shf_ref[p, start:start + n, :] if p else ext_ref[start:start + n, ls]

    return window


def _conv_strip(ext_ref, shf_ref, w_ref, n_taps, first, rows, ls, emit, reverse=False):
    phases = _tap_phases(n_taps, first)
    window = _shifted_windows(ext_ref, shf_ref, phases, rows, ls)
    for r0 in range(0, rows, CONV_ROWS):
        acc = None
        for p, taps in phases.items():
            for j, off in taps:
                wj = w_ref[(n_taps - 1 - j) if reverse else j, ls]
                term = window(p, off + r0, CONV_ROWS) * wj
                acc = term if acc is None else acc + term
        emit(r0, acc)


def _corr_strip(dl_ref, ext_ref, shf_ref, acc_ref, n_taps, first, rows, ls):
    phases = _tap_phases(n_taps, first)
    window = _shifted_windows(ext_ref, shf_ref, phases, rows, ls)
    for r0 in range(0, rows, CONV_ROWS):
        dl = dl_ref[r0:r0 + CONV_ROWS, ls]
        for p, taps in phases.items():
            for j, off in taps:
                prod = dl * window(p, off + r0, CONV_ROWS)
                acc_ref[SUBLANES * j:SUBLANES * (j + 1), ls] += jnp.sum(
                    prod.reshape(CONV_ROWS // SUBLANES, SUBLANES, prod.shape[-1]), axis=0)


def _conf_fwd(p7, conv_w, conv_b, ln_g, ln_b, d):
    t = p7.shape[0]
    tm = ROW_TILE

    def body(c1_ref, c2_ref, w_ref, b_ref, g_ref, bb_ref, cc_ref, sc_ref, ext, shf):
        @pl.when(pl.program_id(0) == 0)
        def _():
            ext[0:HALO, :] = jnp.zeros((HALO, d), F32)

        ext[HALO:HALO + tm, :] = c1_ref[...].astype(F32) * _sigmoid(c2_ref[...].astype(F32))
        for s0 in range(0, d, LANES):
            ls = slice(s0, s0 + LANES)

            def emit(r0, acc, ls=ls):
                cc_ref[r0:r0 + CONV_ROWS, ls] = acc + b_ref[:, ls]

            _conv_strip(ext, shf, w_ref, CONF_K, HALO - (CONF_K - 1), tm, ls, emit)
        ext[0:HALO, :] = ext[tm:tm + HALO, :]
        g = g_ref[...]
        bb = bb_ref[...]
        rows_per_step = 2 * SUBLANES

        def rows_step(k, carry):
            rs = pl.ds(pl.multiple_of(k * rows_per_step, rows_per_step), rows_per_step)
            cc = cc_ref[rs, :]
            xc = cc - jnp.mean(cc, axis=-1, keepdims=True)
            rstd = lax.rsqrt(jnp.mean(xc * xc, axis=-1, keepdims=True) + LN_EPS)
            cn = xc * rstd * g + bb
            sc_ref[rs, :] = (cn * _sigmoid(cn)).astype(BF16)
            return carry

        lax.fori_loop(0, tm // rows_per_step, rows_step, 0, unroll=4)

    vec = pl.BlockSpec((1, d), lambda i: (0, 0))
    return pl.pallas_call(
        body, name="conf_fwd", grid=(t // tm,),
        in_specs=[pl.BlockSpec((tm, d), lambda i: (i, 3)), pl.BlockSpec((tm, d), lambda i: (i, 4)),
                  pl.BlockSpec((CONF_K, d), lambda i: (0, 0)), vec, vec, vec],
        out_specs=[pl.BlockSpec((tm, d), lambda i: (i, 0)), pl.BlockSpec((tm, d), lambda i: (i, 0))],
        out_shape=[jax.ShapeDtypeStruct((t, d), F32), jax.ShapeDtypeStruct((t, d), BF16)],
        scratch_shapes=[pltpu.VMEM((tm + HALO, d), F32), pltpu.VMEM((SUBLANES, tm + HALO, LANES), F32)],
        compiler_params=_cparams(("arbitrary",)),
    )(p7, p7, conv_w, conv_b, ln_g, ln_b)


def _merge(p7, br_gla, br_conf, d):
    t = p7.shape[0]
    tm = ROW_TILE

    def body(gg_ref, gc_ref, a_ref, b_ref, o_ref):
        o_ref[...] = (_sigmoid(gg_ref[...].astype(F32)) * a_ref[...].astype(F32)
                      + _sigmoid(gc_ref[...].astype(F32)) * b_ref[...].astype(F32)).astype(BF16)

    row = pl.BlockSpec((tm, d), lambda i: (i, 0))
    return pl.pallas_call(
        body, name="merge", grid=(t // tm,),
        in_specs=[pl.BlockSpec((tm, d), lambda i: (i, 5)), pl.BlockSpec((tm, d), lambda i: (i, 6)), row, row],
        out_specs=row, out_shape=jax.ShapeDtypeStruct((t, d), BF16),
        compiler_params=_cparams(("parallel",)),
    )(p7, p7, br_gla, br_conf)


def _rms_fwd(h, g, name):
    t, d = h.shape
    tm = ROW_TILE

    def body(h_ref, g_ref, u_ref):
        x = h_ref[...]
        r = lax.rsqrt(jnp.mean(x * x, axis=-1, keepdims=True) + RMS_EPS)
        u_ref[...] = (x * r * g_ref[...]).astype(BF16)

    return pl.pallas_call(
        body, name=name, grid=(t // tm,),
        in_specs=[pl.BlockSpec((tm, d), lambda i: (i, 0)), pl.BlockSpec((1, d), lambda i: (0, 0))],
        out_specs=pl.BlockSpec((tm, d), lambda i: (i, 0)), out_shape=jax.ShapeDtypeStruct((t, d), BF16),
        compiler_params=_cparams(("parallel",)),
    )(h, g)


def _ffn_mid(up, w, b, dff):
    t = up.shape[0]
    tm = ROW_TILE
    hal = 8

    def body(a_ref, bv_ref, w_ref, b_ref, y_ref, ext, shf):
        i = pl.program_id(0)

        @pl.when(i == 0)
        def _():
            ext[0:hal, :] = jnp.zeros((hal, dff), F32)

        ext[hal:hal + tm, :] = a_ref[...].astype(F32) * _row_mask(i, tm)
        for s0 in range(0, dff, LANES):
            ls = slice(s0, s0 + LANES)

            def emit(r0, acc, ls=ls):
                rb = slice(r0, r0 + CONV_ROWS)
                ac = acc + b_ref[:, ls]
                y_ref[rb, ls] = (ac * _sigmoid(ac) * bv_ref[rb, ls].astype(F32)).astype(BF16)

            _conv_strip(ext, shf, w_ref, FFN_K, hal - (FFN_K - 1), tm, ls, emit)
        ext[0:hal, :] = ext[tm:tm + hal, :]

    return pl.pallas_call(
        body, name="ffn_mid", grid=(t // tm,),
        in_specs=[pl.BlockSpec((tm, dff), lambda i: (i, 0)), pl.BlockSpec((tm, dff), lambda i: (i, 1)),
                  pl.BlockSpec((FFN_K, dff), lambda i: (0, 0)), pl.BlockSpec((1, dff), lambda i: (0, 0))],
        out_specs=pl.BlockSpec((tm, dff), lambda i: (i, 0)), out_shape=jax.ShapeDtypeStruct((t, dff), BF16),
        scratch_shapes=[pltpu.VMEM((tm + hal, dff), F32), pltpu.VMEM((SUBLANES, tm + hal, LANES), F32)],
        compiler_params=_cparams(("arbitrary",)),
    )(up, up, w, b)


def _loss_head(h2, target, gf):
    t, d = h2.shape
    tm = ROW_TILE

    def body(h_ref, tg_ref, g_ref, loss_ref, dg_ref, dh_ref):
        i = pl.program_id(0)

        @pl.when(i == 0)
        def _():
            loss_ref[...] = jnp.zeros_like(loss_ref)
            dg_ref[...] = jnp.zeros_like(dg_ref)
            dh_ref[...] = jnp.zeros_like(dh_ref)

        @pl.when(i > 0)
        def _():
            h = h_ref[...]
            g = g_ref[...]
            r = lax.rsqrt(jnp.mean(h * h, axis=-1, keepdims=True) + RMS_EPS)
            n = h * r
            err = n * g - tg_ref[...]
            loss_ref[...] += 0.5 * jnp.sum(jnp.mean(err * err, axis=-1, keepdims=True), axis=0, keepdims=True)
            dout = err * (1.0 / d)
            dg_ref[...] += jnp.sum(dout * n, axis=0, keepdims=True)
            dn = dout * g
            dh_ref[...] = r * (dn - n * jnp.mean(dn * n, axis=-1, keepdims=True))

    return pl.pallas_call(
        body, name="loss_head", grid=(t // tm,),
        in_specs=[pl.BlockSpec((tm, d), lambda i: (i, 0)),
                  pl.BlockSpec((tm, d), lambda i: (jnp.maximum(i - 1, 0), 0)),
                  pl.BlockSpec((1, d), lambda i: (0, 0))],
        out_specs=[pl.BlockSpec((1, 1), lambda i: (0, 0)), pl.BlockSpec((1, d), lambda i: (0, 0)),
                   pl.BlockSpec((tm, d), lambda i: (i, 0))],
        out_shape=[jax.ShapeDtypeStruct((1, 1), F32), jax.ShapeDtypeStruct((1, d), F32),
                   jax.ShapeDtypeStruct((t, d), F32)],
        compiler_params=_cparams(("arbitrary",)),
    )(h2, target, gf)


def _dsilu(x, sig):
    return sig * (1.0 + x * (1.0 - sig))


def _ffn_mid_bwd(up, dy, w, b, dff):
    t = up.shape[0]
    tm = ROW_TILE
    hal = 8
    prev_rows = 2 * SUBLANES
    nt = t // tm

    def body(a_ref, ap_ref, bv_ref, dy_ref, w_ref, b_ref, dup_ref, dw_ref, db_ref, ext, dext, dw_acc, db_acc, shf):
        i = pl.program_id(0)
        tile = nt - 1 - i

        @pl.when(i == 0)
        def _():
            dext[tm:tm + hal, :] = jnp.zeros((hal, dff), F32)
            dw_acc[...] = jnp.zeros_like(dw_acc)
            db_acc[...] = jnp.zeros_like(db_acc)

        ext[0:hal, :] = ap_ref[prev_rows - hal:prev_rows, :].astype(F32) * jnp.where(tile > 0, 1.0, 0.0)
        ext[hal:hal + tm, :] = a_ref[...].astype(F32) * _row_mask(tile, tm)
        first = hal - (FFN_K - 1)
        for s0 in range(0, dff, LANES):
            ls = slice(s0, s0 + LANES)

            def emit_fwd(r0, acc, ls=ls, s0=s0):
                rb = slice(r0, r0 + CONV_ROWS)
                ac = acc + b_ref[:, ls]
                sig = _sigmoid(ac)
                dyv = dy_ref[rb, ls].astype(F32)
                dup_ref[rb, dff + s0:dff + s0 + LANES] = (dyv * ac * sig).astype(BF16)
                dac = dyv * bv_ref[rb, ls].astype(F32) * _dsilu(ac, sig)
                dext[rb, ls] = dac
                db_acc[:, ls] += jnp.sum(dac.reshape(CONV_ROWS // SUBLANES, SUBLANES, LANES), axis=0)

            _conv_strip(ext, shf, w_ref, FFN_K, first, tm, ls, emit_fwd)
            _corr_strip(dext, ext, shf, dw_acc, FFN_K, first, tm, ls)

            def emit_bwd(r0, da, ls=ls):
                rb = slice(r0, r0 + CONV_ROWS)
                mask = ((tile * tm + r0 + lax.broadcasted_iota(jnp.int32, (CONV_ROWS, 1), 0)) >= PAD).astype(F32)
                dup_ref[rb, ls] = (da * mask).astype(BF16)

            _conv_strip(dext, shf, w_ref, FFN_K, 0, tm, ls, emit_bwd, reverse=True)
        dext[tm:tm + hal, :] = dext[0:hal, :]

        @pl.when(i == nt - 1)
        def _():
            db_ref[...] = jnp.sum(db_acc[...], axis=0, keepdims=True)
            for j in range(FFN_K):
                dw_ref[j:j + 1, :] = jnp.sum(dw_acc[SUBLANES * j:SUBLANES * (j + 1), :], axis=0, keepdims=True)

    rev = lambda i: (nt - 1 - i, 0)
    return pl.pallas_call(
        body, name="ffn_mid_bwd", grid=(nt,),
        in_specs=[pl.BlockSpec((tm, dff), rev),
                  pl.BlockSpec((prev_rows, dff), lambda i: (jnp.maximum((nt - 1 - i) * (tm // prev_rows) - 1, 0), 0)),
                  pl.BlockSpec((tm, dff), lambda i: (nt - 1 - i, 1)),
                  pl.BlockSpec((tm, dff), rev),
                  pl.BlockSpec((FFN_K, dff), lambda i: (0, 0)), pl.BlockSpec((1, dff), lambda i: (0, 0))],
        out_specs=[pl.BlockSpec((tm, 2 * dff), rev),
                   pl.BlockSpec((FFN_K, dff), lambda i: (0, 0)), pl.BlockSpec((1, dff), lambda i: (0, 0))],
        out_shape=[jax.ShapeDtypeStruct((t, 2 * dff), BF16),
                   jax.ShapeDtypeStruct((FFN_K, dff), F32), jax.ShapeDtypeStruct((1, dff), F32)],
        scratch_shapes=[pltpu.VMEM((tm + hal, dff), F32), pltpu.VMEM((tm + hal, dff), F32),
                        pltpu.VMEM((SUBLANES * FFN_K, dff), F32), pltpu.VMEM((SUBLANES, dff), F32),
                        pltpu.VMEM((SUBLANES, tm + hal, LANES), F32)],
        compiler_params=_cparams(("arbitrary",)),
    )(up, up, up, dy, w, b)


def _rms_bwd(du, h, g, dres, name):
    t, d = h.shape
    tm = ROW_TILE

    def body(du_ref, h_ref, g_ref, dres_ref, dh_ref, dg_ref):
        @pl.when(pl.program_id(0) == 0)
        def _():
            dg_ref[...] = jnp.zeros_like(dg_ref)

        x = h_ref[...]
        r = lax.rsqrt(jnp.mean(x * x, axis=-1, keepdims=True) + RMS_EPS)
        n = x * r
        du_ = du_ref[...]
        dg_ref[...] += jnp.sum(du_ * n, axis=0, keepdims=True)
        dn = du_ * g_ref[...]
        dh_ref[...] = dres_ref[...] + r * (dn - n * jnp.mean(dn * n, axis=-1, keepdims=True))

    row = pl.BlockSpec((tm, d), lambda i: (i, 0))
    vec = pl.BlockSpec((1, d), lambda i: (0, 0))
    return pl.pallas_call(
        body, name=name, grid=(t // tm,),
        in_specs=[row, row, vec, row], out_specs=[row, vec],
        out_shape=[jax.ShapeDtypeStruct((t, d), F32), jax.ShapeDtypeStruct((1, d), F32)],
        compiler_params=_cparams(("arbitrary",)),
    )(du, h, g, dres)


def _merge_bwd(p7, br_gla, br_conf, dmerged, d):
    t = p7.shape[0]
    tm = ROW_TILE

    def body(gg_ref, gc_ref, a_ref, b_ref, dm_ref, da_ref, db_ref, dgg_ref, dgc_ref):
        dm = dm_ref[...].astype(F32)
        sg = _sigmoid(gg_ref[...].astype(F32))
        sc = _sigmoid(gc_ref[...].astype(F32))
        da_ref[...] = (dm * sg).astype(BF16)
        db_ref[...] = (dm * sc).astype(BF16)
        dgg_ref[...] = (dm * a_ref[...].astype(F32) * sg * (1.0 - sg)).astype(BF16)
        dgc_ref[...] = (dm * b_ref[...].astype(F32) * sc * (1.0 - sc)).astype(BF16)

    row = pl.BlockSpec((tm, d), lambda i: (i, 0))
    outs = pl.pallas_call(
        body, name="merge_bwd", grid=(t // tm,),
        in_specs=[pl.BlockSpec((tm, d), lambda i: (i, 5)), pl.BlockSpec((tm, d), lambda i: (i, 6)), row, row, row],
        out_specs=[row, row, row, row],
        out_shape=[jax.ShapeDtypeStruct((t, d), BF16)] * 4,
        compiler_params=_cparams(("parallel",)),
    )(p7, p7, br_gla, br_conf, dmerged)
    return outs


def _conf_bwd(p7, cc, dsc, conv_w, ln_g, ln_b, d):
    t = p7.shape[0]
    tm = ROW_TILE
    nt = t // tm

    def body(c1_ref, c2_ref, c1p_ref, c2p_ref, cc_ref, dsc_ref, w_ref, g_ref, bb_ref,
             dc1_ref, dc2_ref, dw_ref, db_ref, dg_ref, dbb_ref, ext, dext, dw_acc, shf):
        i = pl.program_id(0)
        tile = nt - 1 - i

        @pl.when(i == 0)
        def _():
            dext[tm:tm + HALO, :] = jnp.zeros((HALO, d), F32)
            dw_acc[...] = jnp.zeros_like(dw_acc)
            db_ref[...] = jnp.zeros_like(db_ref)
            dg_ref[...] = jnp.zeros_like(dg_ref)
            dbb_ref[...] = jnp.zeros_like(dbb_ref)

        ext[0:HALO, :] = (c1p_ref[...].astype(F32) * _sigmoid(c2p_ref[...].astype(F32))
                          * jnp.where(tile > 0, 1.0, 0.0))
        g = g_ref[...]
        bb = bb_ref[...]

        groups = 2
        grp = 2 * SUBLANES

        def fold(a):
            return a[:SUBLANES] + a[SUBLANES:]

        def rows_step(k, sums):
            sdg, sdbb, sdb = sums
            for u in range(groups):
                r0 = (k * groups + u) * grp
                rs = pl.ds(pl.multiple_of(r0, grp), grp)
                ext[pl.ds(pl.multiple_of(HALO + r0, grp), grp), :] = (
                    c1_ref[rs, :].astype(F32) * _sigmoid(c2_ref[rs, :].astype(F32)))
                cc_ = cc_ref[rs, :]
                xc = cc_ - jnp.mean(cc_, axis=-1, keepdims=True)
                rstd = lax.rsqrt(jnp.mean(xc * xc, axis=-1, keepdims=True) + LN_EPS)
                xh = xc * rstd
                cn = xh * g + bb
                dcn = dsc_ref[rs, :].astype(F32) * _dsilu(cn, _sigmoid(cn))
                dxh = dcn * g
                dcc = rstd * (dxh - jnp.mean(dxh, axis=-1, keepdims=True)
                              - xh * jnp.mean(dxh * xh, axis=-1, keepdims=True))
                dext[rs, :] = dcc
                sdg, sdbb, sdb = sdg + fold(dcn * xh), sdbb + fold(dcn), sdb + fold(dcc)
            return sdg, sdbb, sdb

        zero = jnp.zeros((SUBLANES, d), F32)
        sdg, sdbb, sdb = lax.fori_loop(0, tm // (groups * grp), rows_step, (zero, zero, zero))
        dg_ref[...] += jnp.sum(sdg, axis=0, keepdims=True)
        dbb_ref[...] += jnp.sum(sdbb, axis=0, keepdims=True)
        db_ref[...] += jnp.sum(sdb, axis=0, keepdims=True)
        for s0 in range(0, d, LANES):
            ls = slice(s0, s0 + LANES)
            _corr_strip(dext, ext, shf, dw_acc, CONF_K, HALO - (CONF_K - 1), tm, ls)

            def emit(r0, dc, ls=ls):
                rb = slice(r0, r0 + CONV_ROWS)
                s2l = _sigmoid(c2_ref[rb, ls].astype(F32))
                dc1_ref[rb, ls] = (dc * s2l).astype(BF16)
                dc2_ref[rb, ls] = (dc * c1_ref[rb, ls].astype(F32) * s2l * (1.0 - s2l)).astype(BF16)

            _conv_strip(dext, shf, w_ref, CONF_K, 0, tm, ls, emit, reverse=True)
        dext[tm:tm + HALO, :] = dext[0:HALO, :]

        @pl.when(i == nt - 1)
        def _():
            for j in range(CONF_K):
                dw_ref[j:j + 1, :] = jnp.sum(dw_acc[SUBLANES * j:SUBLANES * (j + 1), :], axis=0, keepdims=True)

    rev = lambda i: (nt - 1 - i, 0)
    prev = lambda col: (lambda i: (jnp.maximum((nt - 1 - i) * (tm // HALO) - 1, 0), col))
    vec = pl.BlockSpec((1, d), lambda i: (0, 0))
    return pl.pallas_call(
        body, name="conf_bwd", grid=(nt,),
        in_specs=[pl.BlockSpec((tm, d), lambda i: (nt - 1 - i, 3)), pl.BlockSpec((tm, d), lambda i: (nt - 1 - i, 4)),
                  pl.BlockSpec((HALO, d), prev(3)), pl.BlockSpec((HALO, d), prev(4)),
                  pl.BlockSpec((tm, d), rev), pl.BlockSpec((tm, d), rev),
                  pl.BlockSpec((CONF_K, d), lambda i: (0, 0)), vec, vec],
        out_specs=[pl.BlockSpec((tm, d), rev), pl.BlockSpec((tm, d), rev),
                   pl.BlockSpec((CONF_K, d), lambda i: (0, 0)), vec, vec, vec],
        out_shape=[jax.ShapeDtypeStruct((t, d), BF16), jax.ShapeDtypeStruct((t, d), BF16),
                   jax.ShapeDtypeStruct((CONF_K, d), F32)] + [jax.ShapeDtypeStruct((1, d), F32)] * 3,
        scratch_shapes=[pltpu.VMEM((tm + HALO, d), F32), pltpu.VMEM((tm + HALO, d), F32),
                        pltpu.VMEM((SUBLANES * CONF_K, d), F32), pltpu.VMEM((SUBLANES, tm + HALO, LANES), F32)],
        compiler_params=_cparams(("arbitrary",)),
    )(p7, p7, p7, p7, cc, dsc, conv_w, ln_g, ln_b)


def _gla_bwd(p7, log_a, alr, wau, b_alpha, gla_g, o, states, dy, d):
    t = p7.shape[0]
    dk_all = d // 2
    dkh, dvh = dk_all // HEADS, d // HEADS
    cb = ROW_TILE
    ncb = cb // CHUNK
    nb = t // cb
    scale = dkh ** -0.5

    def body(qk_ref, v_ref, r_ref, la_ref, alr_ref, wau_ref, ba_ref, g_ref, o_ref, s_ref, dy_ref,
             dqk_ref, dv_ref, dr_ref, dz_ref, dg_ref, dba_ref, dst_scr, dla_scr):
        i = pl.program_id(0)
        blk = nb - 1 - i

        @pl.when(i == 0)
        def _():
            dst_scr[...] = jnp.zeros_like(dst_scr)
            dg_ref[...] = jnp.zeros_like(dg_ref)
            dba_ref[...] = jnp.zeros_like(dba_ref)

        tri = _tri(CHUNK)
        tri_f = tri.astype(F32)
        triu_f = _tri(CHUNK, upper=True).astype(F32)
        for c in reversed(range(ncb)):
            rows = slice(c * CHUNK, (c + 1) * CHUNK)
            eb, ekl, ebl_inv, gam = _chunk_decays(la_ref[rows, :], tri_f)
            for h in range(HEADS):
                ks = slice(h * dkh, (h + 1) * dkh)
                kcols = slice(dk_all + h * dkh, dk_all + (h + 1) * dkh)
                vs = slice(h * dvh, (h + 1) * dvh)
                q = qk_ref[rows, ks].astype(F32) * scale
                k = qk_ref[rows, kcols].astype(F32)
                v = v_ref[rows, vs].astype(BF16)
                ebh, eklh, eih, gamh = eb[:, ks], ekl[:, ks], ebl_inv[:, ks], gam[:, ks]
                qb = q * ebh
                kh = k * eklh
                qc = qb * eih
                qb_b, kh_b, qc_b = qb.astype(BF16), kh.astype(BF16), qc.astype(BF16)
                ov = o_ref[rows, vs]
                r = r_ref[rows, vs].astype(F32)
                dyv = dy_ref[rows, vs].astype(F32)
                sig = _sigmoid(r)
                rr = lax.rsqrt(jnp.mean(ov * ov, axis=-1, keepdims=True) + RMS_EPS)
                n = ov * rr
                g = g_ref[:, vs]
                dr_ref[rows, vs] = (dyv * n * g * _dsilu(r, sig)).astype(BF16)
                don = dyv * (r * sig)
                dg_ref[:, vs] += jnp.sum(don * n, axis=0, keepdims=True)
                dn = don * g
                do = (rr * (dn - n * jnp.mean(dn * n, axis=-1, keepdims=True))).astype(BF16)
                st_b = s_ref[c, h]
                a = jnp.where(tri, _dot_nt(qc_b, kh_b), 0.0).astype(BF16)
                da = jnp.where(tri, _dot_nt(do, v), 0.0).astype(BF16)
                dst = dst_scr[h]
                dst_b = dst.astype(BF16)
                dv_ref[rows, vs] = (_dot_tn(a, do) + _dot_nt(kh_b, dst_b)).astype(BF16)
                dqb = _dot(do, st_b)
                dqc = _dot(da, kh_b)
                dkh_ = _dot_tn(da, qc_b) + _dot(v, dst_b)
                dgam = jnp.sum(st_b.astype(F32) * dst, axis=0, keepdims=True)
                dst_scr[h] = dst * gamh + _dot_tn(do, qb_b)
                dqk_ref[rows, ks] = ((dqb * ebh + dqc * (ebh * eih)) * scale).astype(BF16)
                dqk_ref[rows, kcols] = (dkh_ * eklh).astype(BF16)
                qq = dqc * qc
                kk = dkh_ * kh
                db = dqb * qb + qq - kk
                dbl = jnp.sum(kk - qq, axis=0, keepdims=True) + dgam * gamh
                dla_scr[rows, ks] = jnp.dot(triu_f, db, preferred_element_type=F32,
                                            precision=lax.Precision.HIGHEST) + dbl
        z = jnp.dot(alr_ref[...].astype(BF16), wau_ref[...].astype(BF16), preferred_element_type=F32) + ba_ref[...]
        dz = dla_scr[...] * (1.0 / GATE_TAU) * _sigmoid(-z) * _row_mask(blk, cb)
        dba_ref[...] += jnp.sum(dz, axis=0, keepdims=True)
        dz_ref[...] = dz.astype(BF16)

    rev = lambda i: (nb - 1 - i, 0)
    row = pl.BlockSpec((cb, d), rev)
    return pl.pallas_call(
        body, name="gla_bwd", grid=(nb,),
        in_specs=[row, pl.BlockSpec((cb, d), lambda i: (nb - 1 - i, 1)), pl.BlockSpec((cb, d), lambda i: (nb - 1 - i, 2)),
                  pl.BlockSpec((cb, dk_all), rev), pl.BlockSpec((cb, RANK_PAD), rev),
                  pl.BlockSpec((RANK_PAD, dk_all), lambda i: (0, 0)), pl.BlockSpec((1, dk_all), lambda i: (0, 0)),
                  pl.BlockSpec((1, d), lambda i: (0, 0)), row,
                  pl.BlockSpec((ncb, HEADS, dvh, dkh), lambda i: (nb - 1 - i, 0, 0, 0)), row],
        out_specs=[row, row, row, pl.BlockSpec((cb, dk_all), rev),
                   pl.BlockSpec((1, d), lambda i: (0, 0)), pl.BlockSpec((1, dk_all), lambda i: (0, 0))],
        out_shape=[jax.ShapeDtypeStruct((t, d), BF16)] * 3 + [jax.ShapeDtypeStruct((t, dk_all), BF16),
                   jax.ShapeDtypeStruct((1, d), F32), jax.ShapeDtypeStruct((1, dk_all), F32)],
        scratch_shapes=[pltpu.VMEM((HEADS, dvh, dkh), F32), pltpu.VMEM((cb, dk_all), F32)],
        compiler_params=_cparams(("arbitrary",)),
    )(p7, p7, p7, log_a, alr, wau, b_alpha, gla_g, o, states, dy)


def _input_grad(du_a, du_b, h0, g1, dh1):
    t, d = h0.shape
    tm = ROW_TILE
    s = t - HEAD_ROWS

    def body(dua_ref, dub_ref, h_ref, g_ref, dres_ref, gx_ref, gm_ref, dg_ref):
        i = pl.program_id(0)

        @pl.when(i == 0)
        def _():
            dg_ref[...] = jnp.zeros_like(dg_ref)

        x = h_ref[...]
        r = lax.rsqrt(jnp.mean(x * x, axis=-1, keepdims=True) + RMS_EPS)
        n = x * r
        du_ = dua_ref[...] + dub_ref[...]
        dg_ref[...] += jnp.sum(du_ * n, axis=0, keepdims=True)
        dn = du_ * g_ref[...]
        dh = dres_ref[...] + r * (dn - n * jnp.mean(dn * n, axis=-1, keepdims=True))

        @pl.when(i == 0)
        def _():
            gm_ref[...] = dh[PAD:HEAD_ROWS, :]
            gx_ref[...] = jnp.zeros_like(gx_ref)

        @pl.when(i > 0)
        def _():
            gx_ref[...] = dh

    row = pl.BlockSpec((tm, d), lambda i: (i, 0))
    vec = pl.BlockSpec((1, d), lambda i: (0, 0))
    return pl.pallas_call(
        body, name="input_grad", grid=(t // tm,),
        in_specs=[row, row, row, vec, row],
        out_specs=[pl.BlockSpec((tm, d), lambda i: (jnp.maximum(i - 1, 0), 0)),
                   pl.BlockSpec((N_META, d), lambda i: (0, 0)), vec],
        out_shape=[jax.ShapeDtypeStruct((s, d), F32), jax.ShapeDtypeStruct((N_META, d), F32),
                   jax.ShapeDtypeStruct((1, d), F32)],
        compiler_params=_cparams(("arbitrary",)),
    )(du_a, du_b, h0, g1, dh1)


REST = ("w_up", "w_down", "w_gla_o", "w_conf_o", "w_out")


def _chip_partials(grads, names, tag):
    core = lax.axis_index("c").reshape(1)
    from_sibling = _swap_core_halves(grads, "swap_core_halves_" + tag)
    return [_add_core_halves(core, g, s, "add_core_halves_" + n) for n, g, s in zip(names, grads, from_sibling)]


def _local_step(x, target, w, rest_shards):
    s, d = x.shape
    dk_all = d // 2
    w_in = w["w_in"]
    lo, hi = 3 * d, 3 * d + GLA_RANK
    wq = jnp.concatenate([w_in[:, :lo], w_in[:, hi:]], axis=1)
    w_alr = jnp.pad(w_in[:, lo:hi], ((0, 0), (0, RANK_PAD - GLA_RANK)))
    wau = jnp.pad(w["w_alpha_up"], ((0, RANK_PAD - GLA_RANK), (0, 0)))

    h0, u1 = _prep(x, w["meta_tokens"], w["norm_mix_g"])
    shards = [rest_shards[n] for n in REST]
    p7, gathered = _matmul(u1, wq, dims="nn", name="proj", out_dtype=BF16, hosted=_gather_plan(shards))
    w = dict(w)
    for n, slabs in zip(REST, _with_own_slab(gathered, shards)):
        w[n] = slabs if n == "w_up" else slabs.reshape(-1, slabs.shape[-1])
    dff = w["w_down"].shape[0]
    alr = _matmul(u1, w_alr, dims="nn", name="proj_alr")
    log_a = _log_gate(alr, wau, w["b_alpha"])
    o, y_gla, states = _gla_fwd(p7, log_a, w["gla_norm_g"], d)
    br_gla = _matmul(y_gla, w["w_gla_o"], dims="nn", name="gla_out", out_dtype=BF16)
    cc, s_c = _conf_fwd(p7, w["conf_dw_w"], w["conf_dw_b"], w["conf_ln_g"], w["conf_ln_b"], d)
    br_conf = _matmul(s_c, w["w_conf_o"], dims="nn", name="conf_out", out_dtype=BF16)
    merged = _merge(p7, br_gla, br_conf, d)
    h1 = _matmul(merged, w["w_out"], dims="nn", name="mix_out", add=h0)
    u2 = _rms_fwd(h1, w["norm_ffn_g"], "norm_ffn")
    up = _matmul(u2, w["w_up"], dims="nn", name="ffn_up", out_dtype=BF16, chips="b")
    y = _ffn_mid(up, w["ffn_dw_w"], w["ffn_dw_b"], dff)
    h2 = _matmul(y, w["w_down"], dims="nn", name="ffn_down", add=h1)
    loss, d_gf, dh2 = _loss_head(h2, target, w["final_norm_g"])

    g = {"final_norm_g": d_gf}
    dy = _matmul(dh2, w["w_down"], dims="nt", name="d_ffn_y", out_dtype=BF16)
    g["w_down"] = _matmul(y, dh2, dims="tn", name="dw_down", out_dtype=BF16)
    dup, g["ffn_dw_w"], g["ffn_dw_b"] = _ffn_mid_bwd(up, dy, w["ffn_dw_w"], w["ffn_dw_b"], dff)
    du2 = _matmul(dup, w["w_up"], dims="nt", name="d_u2", chips="b")
    g["w_up"] = _matmul(u2, dup, dims="tn", name="dw_up", out_dtype=BF16, chips="out")
    dh1, g["norm_ffn_g"] = _rms_bwd(du2, h1, w["norm_ffn_g"], dh2, "norm_ffn_bwd")
    dmerged = _matmul(dh1, w["w_out"], dims="nt", name="d_merged", out_dtype=BF16)
    g["w_out"] = _matmul(merged, dh1, dims="tn", name="dw_out", out_dtype=BF16)
    d_br_gla, d_br_conf, dgg, dgc = _merge_bwd(p7, br_gla, br_conf, dmerged, d)
    dsc = _matmul(d_br_conf, w["w_conf_o"], dims="nt", name="d_conf_s", out_dtype=BF16)
    g["w_conf_o"] = _matmul(s_c, d_br_conf, dims="tn", name="dw_conf_o", out_dtype=BF16)
    dc1, dc2, g["conf_dw_w"], g["conf_dw_b"], g["conf_ln_g"], g["conf_ln_b"] = _conf_bwd(
        p7, cc, dsc, w["conf_dw_w"], w["conf_ln_g"], w["conf_ln_b"], d)
    dyg = _matmul(d_br_gla, w["w_gla_o"], dims="nt", name="d_gla_y", out_dtype=BF16)
    g["w_gla_o"] = _matmul(y_gla, d_br_gla, dims="tn", name="dw_gla_o", out_dtype=BF16)
    dqk, dv, dr, dz, g["gla_norm_g"], g["b_alpha"] = _gla_bwd(
        p7, log_a, alr, wau, w["b_alpha"], w["gla_norm_g"], o, states, dyg, d)
    dalr = _matmul(dz, wau, dims="nt", name="d_alr", out_dtype=BF16)
    g["w_alpha_up"] = _matmul(alr, dz, dims="tn", name="dw_alpha_up")[:GLA_RANK]
    dp7 = jnp.concatenate([dqk, dv, dr, dc1, dc2, dgg, dgc], axis=1)
    rest_grads = [g.pop(n) for n in REST]
    rest_grads = [a if a.ndim == 3 else a.reshape(N_CHIPS, -1, a.shape[-1]) for a in rest_grads]
    parts = dict(zip(REST, _chip_partials(rest_grads, REST, "rest")))
    dwq, arrived = _matmul(u1, dp7, dims="tn", name="dw_in", out_dtype=BF16,
                           hosted=_exchange_plan([parts[n] for n in REST]))
    from_chips = dict(zip(REST, arrived))
    dw_alr = _matmul(u1, dalr, dims="tn", name="dw_in_alr", out_dtype=BF16)
    dw_in = _to_chip_major(jnp.concatenate([dwq[:, :lo], dw_alr[:, :GLA_RANK], dwq[:, lo:]], axis=1))
    parts["w_in"] = _chip_partials([dw_in], ["w_in"], "w_in")[0]
    du1a, arrived = _matmul(dp7, wq, dims="nt", name="d_u1", hosted=_exchange_plan([parts["w_in"]]))
    from_chips["w_in"] = arrived[0]
    du1b = _matmul(dalr, w_alr, dims="nt", name="d_u1_alr")
    grad_x, g["meta_tokens"], g["norm_mix_g"] = _input_grad(du1a, du1b, h0, w["norm_mix_g"], dh1)
    return loss, grad_x, g, parts, from_chips


HBM_SPEC = pl.BlockSpec(memory_space=pltpu.HBM)
FLIPS = ((1, 0), (0, 1), (1, 1))


def _place():
    x, y, c = lax.axis_index("x"), lax.axis_index("y"), lax.axis_index("c")
    return x, y, c


def _half_rows(ref, h, lead=()):
    rh = ref.shape[-2] // 2
    return ref.at[(*lead, pl.ds(pl.multiple_of(h * rh, 2 * SUBLANES), rh), slice(None))]


class _Hosted:
    def __init__(self, operands, out_shapes, sem_shapes, start, finish):
        self.operands, self.out_shapes, self.sem_shapes = operands, out_shapes, sem_shapes
        self.start, self.finish = start, finish


def _run_hosted(h, name):
    n_in, n_out = len(h.operands), len(h.out_shapes)

    def body(*refs):
        ins, outs, sems = refs[:n_in], refs[n_in:n_in + n_out], refs[n_in + n_out:]
        h.start(ins, outs, sems)
        h.finish(ins, outs, sems)

    return pl.pallas_call(
        body, name=name, in_specs=[HBM_SPEC] * n_in, out_specs=[HBM_SPEC] * n_out,
        out_shape=list(h.out_shapes), scratch_shapes=list(h.sem_shapes),
    )(*h.operands)


def _gather_plan(shards):
    n = len(shards)

    def copies(ins, outs, sems, kinds):
        send_sems, recv_sems = sems
        x, y, c = _place()
        chips = [(x ^ fx, y ^ fy) for fx, fy in FLIPS]
        me, sibling = (x, y, c), (x, y, 1 - c)

        def copy(k, sem, chip, h, to, src=None):
            slot = _half_rows(outs[k], h, lead=(2 * chip[0] + chip[1],))
            return pltpu.make_async_remote_copy(src_ref=slot if src is None else src, dst_ref=slot,
                                                send_sem=send_sems.at[sem], recv_sem=recv_sems.at[sem],
                                                device_id=to, device_id_type=MESH)

        make = {
            "first": lambda k, j, chip: copy(k, 3 * k + j, (x, y), c, (*chip, c), src=_half_rows(ins[k], c)),
            "landed": lambda k, j, chip: copy(k, 3 * k + j, chip, c, me),
            "passed": lambda k, j, chip: copy(k, 3 * n + 3 * k + j, chip, c, sibling),
            "from_sibling": lambda k, j, chip: copy(k, 3 * n + 3 * k + j, chip, 1 - c, me),
        }
        return [[make[kind](k, j, chip) for k in range(n) for j, chip in enumerate(chips)] for kind in kinds]

    def start(ins, outs, sems):
        for cp in copies(ins, outs, sems, ["first"])[0]:
            cp.start()

    def finish(ins, outs, sems):
        first, landed, passed, from_sibling = copies(ins, outs, sems, ["first", "landed", "passed", "from_sibling"])
        for arrived, fwd in zip(landed, passed):
            arrived.wait_recv()
            fwd.start()
        for cp in from_sibling:
            cp.wait_recv()
        for cp in first + passed:
            cp.wait_send()

    return _Hosted(list(shards), [jax.ShapeDtypeStruct((N_CHIPS, *s.shape), s.dtype) for s in shards],
                   [pltpu.SemaphoreType.DMA((6 * n,)), pltpu.SemaphoreType.DMA((6 * n,))], start, finish)


def _with_own_slab(gathered, shards):
    me = 2 * lax.axis_index("x") + lax.axis_index("y")
    return [lax.dynamic_update_index_in_dim(o, s, me, 0) for o, s in zip(gathered, shards)]


def _swap_core_halves(grads, name):
    n = len(grads)

    def body(*refs):
        ins, outs = refs[:n], refs[n:2 * n]
        send_sems, recv_sems = refs[2 * n:]
        x, y, c = _place()
        cps = [pltpu.make_async_remote_copy(
            src_ref=_half_rows(ins[k], 1 - c, lead=(slice(None),)), dst_ref=outs[k], send_sem=send_sems.at[k],
            recv_sem=recv_sems.at[k], device_id=(x, y, 1 - c), device_id_type=MESH) for k in range(n)]
        for cp in cps:
            cp.start()
        for cp in cps:
            cp.wait()

    return pl.pallas_call(
        body, name=name, in_specs=[HBM_SPEC] * n, out_specs=[HBM_SPEC] * n,
        out_shape=[jax.ShapeDtypeStruct((g.shape[0], g.shape[1] // 2, g.shape[2]), g.dtype) for g in grads],
        scratch_shapes=[pltpu.SemaphoreType.DMA((n,)), pltpu.SemaphoreType.DMA((n,))],
    )(*grads)


def _exchange_plan(parts):
    n = len(parts)

    def copies(ins, outs, sems):
        send_sems, recv_sems = sems
        x, y, c = _place()
        cps = []
        for k in range(n):
            for j, (fx, fy) in enumerate(FLIPS):
                tx, ty = x ^ fx, y ^ fy
                cps.append(pltpu.make_async_remote_copy(
                    src_ref=ins[k].at[2 * tx + ty], dst_ref=outs[k].at[j], send_sem=send_sems.at[3 * k + j],
                    recv_sem=recv_sems.at[3 * k + j], device_id=(tx, ty, c), device_id_type=MESH))
        return cps

    def start(ins, outs, sems):
        for cp in copies(ins, outs, sems):
            cp.start()

    def finish(ins, outs, sems):
        for cp in copies(ins, outs, sems):
            cp.wait()

    return _Hosted(list(parts), [jax.ShapeDtypeStruct((3, *p.shape[1:]), p.dtype) for p in parts],
                   [pltpu.SemaphoreType.DMA((3 * n,)), pltpu.SemaphoreType.DMA((3 * n,))], start, finish)


def _join_core_halves(fulls):
    n = len(fulls)

    def body(*refs):
        bufs = refs[n:2 * n]
        send_sems, recv_sems = refs[2 * n:]
        x, y, c = _place()
        cps = [pltpu.make_async_remote_copy(
            src_ref=_half_rows(bufs[k], c), dst_ref=_half_rows(bufs[k], c), send_sem=send_sems.at[k],
            recv_sem=recv_sems.at[k], device_id=(x, y, 1 - c), device_id_type=MESH) for k in range(n)]
        for cp in cps:
            cp.start()
        for k in range(n):
            cps[k].wait_send()
            pltpu.make_async_remote_copy(
                src_ref=_half_rows(bufs[k], c), dst_ref=_half_rows(bufs[k], 1 - c), send_sem=send_sems.at[k],
                recv_sem=recv_sems.at[k], device_id=(x, y, 1 - c), device_id_type=MESH).wait_recv()

    return pl.pallas_call(
        body, name="join_core_halves", in_specs=[HBM_SPEC] * n, out_specs=[HBM_SPEC] * n,
        out_shape=[jax.ShapeDtypeStruct(f.shape, f.dtype) for f in fulls],
        input_output_aliases={k: k for k in range(n)},
        scratch_shapes=[pltpu.SemaphoreType.DMA((n,)), pltpu.SemaphoreType.DMA((n,))],
    )(*fulls)


def _gather_small(block, name):
    m, n = block.shape

    def body(x_ref, out_ref, send_sems, recv_sems, local_sem):
        x, y, c = _place()
        me, sibling = (x, y, c), (x, y, 1 - c)
        chips = [(x ^ fx, y ^ fy) for fx, fy in FLIPS]

        def rows(px, py, pc):
            return out_ref.at[pl.ds((4 * px + 2 * py + pc) * m, m), :]

        def copy(k, blk, to, src=None):
            return pltpu.make_async_remote_copy(
                src_ref=rows(*blk) if src is None else src, dst_ref=rows(*blk),
                send_sem=send_sems.at[k], recv_sem=recv_sems.at[k], device_id=to, device_id_type=MESH)

        mine = pltpu.make_async_copy(x_ref, rows(*me), local_sem)
        mine.start()
        first = [copy(0, me, sibling, src=x_ref)]
        first += [copy(1 + j, me, (*chip, c), src=x_ref) for j, chip in enumerate(chips)]
        for cp in first:
            cp.start()
        passed = [copy(4 + j, (*chip, c), sibling) for j, chip in enumerate(chips)]
        for j, chip in enumerate(chips):
            copy(1 + j, (*chip, c), me).wait_recv()
            passed[j].start()
        copy(0, sibling, me).wait_recv()
        for j, chip in enumerate(chips):
            copy(4 + j, (*chip, 1 - c), me).wait_recv()
        for cp in first + passed:
            cp.wait_send()
        mine.wait()

    out = pl.pallas_call(
        body, name=name,
        out_shape=jax.ShapeDtypeStruct((8 * m, n), block.dtype),
        in_specs=[pl.BlockSpec(memory_space=pltpu.VMEM)],
        out_specs=pl.BlockSpec(memory_space=pltpu.VMEM),
        scratch_shapes=[pltpu.SemaphoreType.DMA((7,)), pltpu.SemaphoreType.DMA((7,)), pltpu.SemaphoreType.DMA],
    )(block)
    return out.reshape(8, m, n)


def _add_core_halves(core, grad, from_sibling, name):
    nc, r, cols = grad.shape
    rh = r // 2

    def body(core_ref, g_ref, s_ref, o_ref):
        o_ref[...] = (g_ref[...].astype(F32) + s_ref[...].astype(F32)).astype(BF16)

    spec = pl.BlockSpec((1, rh, cols), lambda t, core_ref: (t, 0, 0))
    return pl.pallas_call(
        body, name=name,
        grid_spec=pltpu.PrefetchScalarGridSpec(
            num_scalar_prefetch=1, grid=(nc,),
            in_specs=[pl.BlockSpec((1, rh, cols), lambda t, core_ref: (t, core_ref[0], 0)), spec], out_specs=spec),
        out_shape=jax.ShapeDtypeStruct((nc, rh, cols), BF16),
        compiler_params=_cparams(("parallel",)),
    )(core, grad, from_sibling)


def _sum_chip_partials(place, parts, others, name):
    _, rh, cols = parts.shape
    tr = _pick(rh, (128, 176, 64, 32, 16, 8))
    nb = rh // tr

    def body(place_ref, a_ref, b_ref, o_ref):
        acc = a_ref[0].astype(F32)
        for j in range(3):
            acc = acc + b_ref[j].astype(F32)
        o_ref[...] = acc

    return pl.pallas_call(
        body, name=name,
        grid_spec=pltpu.PrefetchScalarGridSpec(
            num_scalar_prefetch=1, grid=(nb,),
            in_specs=[pl.BlockSpec((1, tr, cols), lambda i, place_ref: (place_ref[0], i, 0)),
                      pl.BlockSpec((3, tr, cols), lambda i, place_ref: (0, i, 0))],
            out_specs=pl.BlockSpec((tr, cols), lambda i, place_ref: (place_ref[1] * nb + i, 0))),
        out_shape=jax.ShapeDtypeStruct((2 * rh, cols), F32),
        compiler_params=_cparams(("parallel",)),
    )(place, parts, others)


def _sum_devices(blocks):
    n, m, _ = blocks.shape

    def body(b_ref, o_ref):
        acc = b_ref[0]
        for j in range(1, n):
            acc = acc + b_ref[j]
        o_ref[...] = acc

    return pl.pallas_call(
        body, name="sum_devices", out_shape=jax.ShapeDtypeStruct((m, LANES), F32),
        in_specs=[pl.BlockSpec(memory_space=pltpu.VMEM)], out_specs=pl.BlockSpec(memory_space=pltpu.VMEM),
    )(blocks)


def _adamw(w, g, m, v, name):
    rws, cols = w.shape
    tr = rws
    for cand in (256, 128, 64, 32, 16, 8):
        if rws % cand == 0 and cand * cols * 4 <= 2 * 1024 * 1024:
            tr = cand
            break
    c1 = 1.0 - ADAM_B1 ** ADAM_STEP
    c2 = 1.0 - ADAM_B2 ** ADAM_STEP

    def body(w_ref, g_ref, m_ref, v_ref, d_ref, nm_ref, nv_ref):
        gv = g_ref[...]
        nm = ADAM_B1 * m_ref[...] + (1.0 - ADAM_B1) * gv
        nv = ADAM_B2 * v_ref[...] + (1.0 - ADAM_B2) * (gv * gv)
        m_hat = nm / c1
        v_hat = nv / c2
        d_ref[...] = -ADAM_LR * (m_hat / (jnp.sqrt(v_hat) + ADAM_EPS) + ADAM_WD * w_ref[...])
        nm_ref[...] = nm
        nv_ref[...] = nv

    spec = pl.BlockSpec((tr, cols), lambda i: (i, 0))
    return pl.pallas_call(
        body, name=name, grid=(rws // tr,), in_specs=[spec] * 4, out_specs=[spec] * 3,
        out_shape=[jax.ShapeDtypeStruct((rws, cols), F32)] * 3,
        compiler_params=_cparams(("parallel",)),
    )(w, g, m, v)


WEIGHTS = (
    ("meta_tokens", (16, 1024), 1), ("norm_mix_g", (1024,), None), ("w_in", (1024, 7184), 1),
    ("w_alpha_up", (16, 512), 1), ("b_alpha", (512,), None), ("gla_norm_g", (1024,), None),
    ("w_gla_o", (1024, 1024), 0), ("conf_dw_w", (31, 1024), 1), ("conf_dw_b", (1024,), None),
    ("conf_ln_g", (1024,), None), ("conf_ln_b", (1024,), None), ("w_conf_o", (1024, 1024), 0),
    ("w_out", (1024, 1024), 0), ("norm_ffn_g", (1024,), None), ("w_up", (1024, 5632), 1),
    ("ffn_dw_w", (3, 2816), 1), ("ffn_dw_b", (2816,), None), ("w_down", (2816, 1024), 0),
    ("final_norm_g", (1024,), None),
)
BIG = ("w_in", "w_up", "w_down", "w_gla_o", "w_conf_o", "w_out")
SMALL_SHARDED = ("meta_tokens", "w_alpha_up", "conf_dw_w", "ffn_dw_w")
REPLICATED = tuple(n for n, _, ax in WEIGHTS if ax is None)
SHAPES = {n: s for n, s, _ in WEIGHTS}
AXIS = {n: ax for n, _, ax in WEIGHTS}
N_CHIPS = 4


def _shard_shape(name):
    s = list(SHAPES[name])
    s[AXIS[name]] //= N_CHIPS
    return tuple(s)


def _pack(parts, mult):
    flat = jnp.concatenate([p.reshape(-1) for p in parts])
    pad = (-flat.shape[0]) % mult
    return jnp.pad(flat, (0, pad))


def _unpack(flat, names, shape_of):
    out, off = {}, 0
    for n in names:
        shp = shape_of(n)
        size = math.prod(shp)
        out[n] = flat[off:off + size].reshape(shp)
        off += size
    return out


def _to_chip_major(full):
    r, cols = full.shape
    return full.reshape(r, N_CHIPS, cols // N_CHIPS).transpose(1, 0, 2)


def _from_chip_major(slabs):
    nc, r, cs = slabs.shape
    return slabs.transpose(1, 0, 2).reshape(r, nc * cs)


def kernel(x, meta_tokens, norm_mix_g, w_in, w_alpha_up, b_alpha, gla_norm_g, w_gla_o, conf_dw_w, conf_dw_b, conf_ln_g, conf_ln_b, w_conf_o, w_out, norm_ffn_g, w_up, ffn_dw_w, ffn_dw_b, w_down, final_norm_g, loss_target, m_meta_tokens, m_norm_mix_g, m_w_in, m_w_alpha_up, m_b_alpha, m_gla_norm_g, m_w_gla_o, m_conf_dw_w, m_conf_dw_b, m_conf_ln_g, m_conf_ln_b, m_w_conf_o, m_w_out, m_norm_ffn_g, m_w_up, m_ffn_dw_w, m_ffn_dw_b, m_w_down, m_final_norm_g, v_meta_tokens, v_norm_mix_g, v_w_in, v_w_alpha_up, v_b_alpha, v_gla_norm_g, v_w_gla_o, v_conf_dw_w, v_conf_dw_b, v_conf_ln_g, v_conf_ln_b, v_w_conf_o, v_w_out, v_norm_ffn_g, v_w_up, v_ffn_dw_w, v_ffn_dw_b, v_w_down, v_final_norm_g):
    names = [n for n, _, _ in WEIGHTS]
    w_args = (meta_tokens, norm_mix_g, w_in, w_alpha_up, b_alpha, gla_norm_g, w_gla_o, conf_dw_w, conf_dw_b, conf_ln_g,
              conf_ln_b, w_conf_o, w_out, norm_ffn_g, w_up, ffn_dw_w, ffn_dw_b, w_down, final_norm_g)
    m_args = (m_meta_tokens, m_norm_mix_g, m_w_in, m_w_alpha_up, m_b_alpha, m_gla_norm_g, m_w_gla_o, m_conf_dw_w,
              m_conf_dw_b, m_conf_ln_g, m_conf_ln_b, m_w_conf_o, m_w_out, m_norm_ffn_g, m_w_up, m_ffn_dw_w, m_ffn_dw_b,
              m_w_down, m_final_norm_g)
    v_args = (v_meta_tokens, v_norm_mix_g, v_w_in, v_w_alpha_up, v_b_alpha, v_gla_norm_g, v_w_gla_o, v_conf_dw_w,
              v_conf_dw_b, v_conf_ln_g, v_conf_ln_b, v_w_conf_o, v_w_out, v_norm_ffn_g, v_w_up, v_ffn_dw_w, v_ffn_dw_b,
              v_w_down, v_final_norm_g)
    in_shape = {n: a.shape for n, a in zip(names, w_args)}
    local = {n: a.reshape(_shard_shape(n) if AXIS[n] is not None else SHAPES[n]) for n, a in zip(names, w_args)}
    m_loc = {n: a.reshape(local[n].shape) for n, a in zip(names, m_args)}
    v_loc = {n: a.reshape(local[n].shape) for n, a in zip(names, v_args)}

    me = 2 * lax.axis_index("x") + lax.axis_index("y")

    w_in_shard = local["w_in"].astype(BF16)
    w_in_all = _with_own_slab(_run_hosted(_gather_plan([w_in_shard]), "gather_w_in"), [w_in_shard])[0]
    small = _pack([local[n] for n in SMALL_SHARDED], 8 * LANES).reshape(-1, LANES)
    small_all = _gather_small(small, "gather_small_weights")[::2].reshape(N_CHIPS, -1)
    per_chip_small = [_unpack(small_all[t], SMALL_SHARDED, _shard_shape) for t in range(N_CHIPS)]
    full = {n: jnp.concatenate([per_chip_small[t][n] for t in range(N_CHIPS)], axis=1) for n in SMALL_SHARDED}
    full["w_in"] = _from_chip_major(w_in_all)
    for n in REPLICATED:
        full[n] = local[n].reshape(1, -1)

    loss_part, grad_x, grads, chip_part, from_chips = _local_step(
        x[0], loss_target[0], full, {n: local[n].astype(BF16) for n in REST})

    smalls = REPLICATED + SMALL_SHARDED
    rep = _pack([grads[n] for n in smalls] + [loss_part], 8 * LANES).reshape(-1, LANES)
    rep_sum = _sum_devices(_gather_small(rep, "gather_small_grads")).reshape(-1)
    g_loc = _unpack(rep_sum, smalls, lambda n: SHAPES[n])
    loss = rep_sum[sum(math.prod(SHAPES[n]) for n in smalls)]
    for n in SMALL_SHARDED:
        width = _shard_shape(n)[1]
        g_loc[n] = lax.dynamic_slice_in_dim(g_loc[n], me * width, width, axis=1)

    place = jnp.stack([me, lax.axis_index("c")])
    reduced = [_sum_chip_partials(place, chip_part[n], from_chips[n], "sum_chip_partials_" + n) for n in BIG]
    g_loc.update(zip(BIG, _join_core_halves(reduced)))

    delta, new_m, new_v = {}, {}, {}
    for n in BIG:
        delta[n], new_m[n], new_v[n] = _adamw(local[n], g_loc[n], m_loc[n], v_loc[n], "adamw_" + n)
    rest = SMALL_SHARDED + REPLICATED
    pk = lambda dct: _pack([dct[n] for n in rest], 8 * LANES).reshape(-1, LANES)
    ds, ms, vs = _adamw(pk(local), pk(g_loc), pk(m_loc), pk(v_loc), "adamw_small")
    shape_loc = lambda n: local[n].shape
    for dct, flat in ((delta, ds), (new_m, ms), (new_v, vs)):
        dct.update(_unpack(flat.reshape(-1), rest, shape_loc))

    outs = [loss, grad_x[None]]
    for dct in (g_loc, delta, new_m, new_v):
        outs += [dct[n].reshape(in_shape[n]) for n in names]
    return tuple(outs)
```

```python
import functools
import math

import jax
import jax.numpy as jnp
from jax import lax
from jax.experimental import pallas as pl
from jax.experimental.pallas import tpu as pltpu

F32 = jnp.float32
BF16 = jnp.bfloat16

N_META = 16
PAD = 240
HEAD_ROWS = PAD + N_META
HEADS = 4
GLA_RANK = 16
RANK_PAD = 128
GATE_TAU = 16.0
CHUNK = 64
CONF_K = 31
FFN_K = 3
RMS_EPS = 1e-6
LN_EPS = 1e-5
ADAM_LR, ADAM_B1, ADAM_B2, ADAM_EPS, ADAM_WD, ADAM_STEP = 0.001, 0.9, 0.999, 1e-08, 0.01, 10

ROW_TILE = 256
HALO = 32
LANES = 128
V7X_VMEM_LIMIT = 56 * 1024 * 1024
MESH = pl.DeviceIdType.MESH


def _cparams(sem):
    return pltpu.CompilerParams(dimension_semantics=sem, vmem_limit_bytes=V7X_VMEM_LIMIT)


def _sigmoid(x):
    return 1.0 / (1.0 + jnp.exp(-x))


def _pick(n, prefs):
    for p in prefs:
        if n % p == 0:
            return p
    return n


MATMUL_TILES = {
    "proj": (2816, 1024, 1024), "gla_out": (1408, 1024, 1024), "conf_out": (1408, 1024, 1024),
    "mix_out": (1408, 1024, 1024), "ffn_up": (1408, 1408, 1024), "ffn_down": (768, 1024, 2816),
    "d_ffn_y": (1408, 1408, 1024), "d_u2": (1408, 1024, 1408), "d_merged": (1408, 1024, 1024),
    "d_conf_s": (1408, 1024, 1024), "d_gla_y": (1408, 1024, 1024), "d_u1": (1408, 1024, 1792),
    "dw_in": (1024, 1024, 2816), "dw_up": (1024, 1408, 1408), "dw_down": (1408, 1024, 1408),
    "dw_out": (1024, 1024, 1408), "dw_conf_o": (1024, 1024, 1408), "dw_gla_o": (1024, 1024, 1408),
}


def _matmul(a, b, *, dims, name, tm=None, tn=None, tk=None, out_dtype=F32, add=None, chips=None, hosted=None):
    if chips == "b":
        nc, r, cs = b.shape
        b_shape = (r, nc * cs)
    else:
        b_shape = b.shape
    if dims == "nn":
        (m, k), (_, n) = a.shape, b_shape
    elif dims == "nt":
        (m, k), (n, _) = a.shape, b_shape
    else:
        (k, m), (_, n) = a.shape, b_shape
    want = MATMUL_TILES.get(name, (None, None, None))
    tm, tn, tk = tm or want[0], tn or want[1], tk or want[2]
    tm = tm if tm and m % tm == 0 else _pick(m, (768, 1024, 1408, 512, 256, 128))
    tn = tn if tn and n % tn == 0 else _pick(n, (1024, 1408, 512, 256, 128))
    tk = tk if tk and k % tk == 0 else _pick(k, (1024, 768, 1408, 512, 256, 128))
    if chips == "b":
        tn, tk = (cs, tk) if dims == "nn" else (tn, cs)
    if chips == "out":
        tn, tm = n // N_CHIPS, m
    nk = k // tk
    assert m % tm == 0 and n % tn == 0 and k % tk == 0, (name, m, n, k, tm, tn, tk)
    a_spec = {"nn": pl.BlockSpec((tm, tk), lambda i, j, kk: (i, kk)),
              "nt": pl.BlockSpec((tm, tk), lambda i, j, kk: (i, kk)),
              "tn": pl.BlockSpec((tk, tm), lambda i, j, kk: (kk, i))}[dims]
    if chips == "b":
        b_spec = {"nn": pl.BlockSpec((None, tk, tn), lambda i, j, kk: (j, kk, 0)),
                  "nt": pl.BlockSpec((None, tn, tk), lambda i, j, kk: (kk, j, 0))}[dims]
    else:
        b_spec = {"nn": pl.BlockSpec((tk, tn), lambda i, j, kk: (kk, j)),
                  "nt": pl.BlockSpec((tn, tk), lambda i, j, kk: (j, kk)),
                  "tn": pl.BlockSpec((tk, tn), lambda i, j, kk: (kk, j))}[dims]
    contract = {"nn": (((1,), (0,)), ((), ())), "nt": (((1,), (1,)), ((), ())), "tn": (((0,), (0,)), ((), ()))}[dims]
    if chips == "out":
        o_spec = pl.BlockSpec((None, tm, tn), lambda i, j, kk: (j, 0, 0))
        out_struct = jax.ShapeDtypeStruct((N_CHIPS, m, tn), out_dtype)
    else:
        o_spec = pl.BlockSpec((tm, tn), lambda i, j, kk: (i, j))
        out_struct = jax.ShapeDtypeStruct((m, n), out_dtype)
    has_add = add is not None

    def body(*refs):
        refs = list(refs)
        a_ref, b_ref = refs[:2]
        add_ref = refs[2] if has_add else None
        n_in = 2 + has_add
        h_ins = refs[n_in:n_in + n_hin]
        o_ref = refs[n_in + n_hin]
        h_outs = refs[n_in + n_hin + 1:n_in + n_hin + 1 + n_hout]
        rest = refs[n_in + n_hin + 1 + n_hout:]
        acc_ref = rest[0] if nk > 1 else None
        h_sems = rest[1:] if nk > 1 else rest
        i, j, kk = pl.program_id(0), pl.program_id(1), pl.program_id(2)
        if hosted is not None:
            @pl.when((i == 0) & (j == 0) & (kk == 0))
            def _():
                hosted.start(h_ins, h_outs, h_sems)

        prod = lax.dot_general(a_ref[...].astype(BF16), b_ref[...].astype(BF16), contract,
                               preferred_element_type=F32)
        if nk == 1:
            o_ref[...] = (prod + add_ref[...].astype(F32) if has_add else prod).astype(out_dtype)
        else:
            @pl.when(kk == 0)
            def _():
                acc_ref[...] = prod + add_ref[...].astype(F32) if has_add else prod

            @pl.when((kk > 0) & (kk < nk - 1))
            def _():
                acc_ref[...] += prod

            @pl.when(kk == nk - 1)
            def _():
                o_ref[...] = (acc_ref[...] + prod).astype(out_dtype)

        if hosted is not None:
            @pl.when((i == m // tm - 1) & (j == n // tn - 1) & (kk == nk - 1))
            def _():
                hosted.finish(h_ins, h_outs, h_sems)

    n_hin = len(hosted.operands) if hosted is not None else 0
    n_hout = len(hosted.out_shapes) if hosted is not None else 0
    in_specs = [a_spec, b_spec] + ([o_spec] if has_add else []) + [HBM_SPEC] * n_hin
    args = (a, b) + ((add,) if has_add else ()) + (tuple(hosted.operands) if hosted is not None else ())
    outs = pl.pallas_call(
        body, name=name, grid=(m // tm, n // tn, nk),
        in_specs=in_specs, out_specs=[o_spec] + [HBM_SPEC] * n_hout,
        out_shape=[out_struct] + (list(hosted.out_shapes) if hosted is not None else []),
        scratch_shapes=([pltpu.VMEM((tm, tn), F32)] if nk > 1 else [])
        + (list(hosted.sem_shapes) if hosted is not None else []),
        compiler_params=_cparams(("arbitrary",) * 3 if hosted is not None else ("parallel", "parallel", "arbitrary")),
    )(*args)
    return outs[0] if hosted is None else (outs[0], list(outs[1:]))


def _row_mask(tile_index, rows):
    r = tile_index * rows + lax.broadcasted_iota(jnp.int32, (rows, 1), 0)
    return (r >= PAD).astype(F32)


def _prep(x, meta, g1):
    s, d = x.shape
    t = HEAD_ROWS + s
    tm = ROW_TILE

    def body(x_ref, meta_ref, g_ref, h_ref, u_ref):
        i = pl.program_id(0)

        @pl.when(i == 0)
        def _():
            h_ref[0:PAD, :] = jnp.zeros((PAD, d), F32)
            h_ref[PAD:HEAD_ROWS, :] = meta_ref[...]

        @pl.when(i > 0)
        def _():
            h_ref[...] = x_ref[...]

        h = h_ref[...]
        r = lax.rsqrt(jnp.mean(h * h, axis=-1, keepdims=True) + RMS_EPS)
        u_ref[...] = (h * r * g_ref[...]).astype(BF16)

    return pl.pallas_call(
        body, name="prep", grid=(t // tm,),
        in_specs=[pl.BlockSpec((tm, d), lambda i: (jnp.maximum(i - 1, 0), 0)),
                  pl.BlockSpec((N_META, d), lambda i: (0, 0)),
                  pl.BlockSpec((1, d), lambda i: (0, 0))],
        out_specs=[pl.BlockSpec((tm, d), lambda i: (i, 0)), pl.BlockSpec((tm, d), lambda i: (i, 0))],
        out_shape=[jax.ShapeDtypeStruct((t, d), F32), jax.ShapeDtypeStruct((t, d), BF16)],
        compiler_params=_cparams(("parallel",)),
    )(x, meta, g1)


def _log_gate(alr, wau, b_alpha):
    t = alr.shape[0]
    dk = wau.shape[1]
    tm = ROW_TILE

    def body(alr_ref, w_ref, b_ref, o_ref):
        z = jnp.dot(alr_ref[...].astype(BF16), w_ref[...].astype(BF16), preferred_element_type=F32) + b_ref[...]
        ls = jnp.minimum(z, 0.0) - jnp.log(1.0 + jnp.exp(-jnp.abs(z)))
        o_ref[...] = ls * (1.0 / GATE_TAU) * _row_mask(pl.program_id(0), tm)

    return pl.pallas_call(
        body, name="log_gate", grid=(t // tm,),
        in_specs=[pl.BlockSpec((tm, RANK_PAD), lambda i: (i, 0)),
                  pl.BlockSpec((RANK_PAD, dk), lambda i: (0, 0)),
                  pl.BlockSpec((1, dk), lambda i: (0, 0))],
        out_specs=pl.BlockSpec((tm, dk), lambda i: (i, 0)),
        out_shape=jax.ShapeDtypeStruct((t, dk), F32),
        compiler_params=_cparams(("parallel",)),
    )(alr, wau, b_alpha)


def _tri(n, upper=False):
    r = lax.broadcasted_iota(jnp.int32, (n, n), 0)
    c = lax.broadcasted_iota(jnp.int32, (n, n), 1)
    return (r <= c) if upper else (r >= c)


_NT = (((1,), (1,)), ((), ()))
_TN = (((0,), (0,)), ((), ()))


def _dot(a, b):
    return jnp.dot(a, b, preferred_element_type=F32)


def _dot_nt(a, b):
    return lax.dot_general(a, b, _NT, preferred_element_type=F32)


def _dot_tn(a, b):
    return lax.dot_general(a, b, _TN, preferred_element_type=F32)


def _chunk_decays(la, tri_f32):
    b = jnp.dot(tri_f32, la, preferred_element_type=F32, precision=lax.Precision.HIGHEST)
    bl = b[CHUNK - 1:CHUNK, :]
    return jnp.exp(b), jnp.exp(bl - b), jnp.exp(-bl), jnp.exp(bl)


def _gla_fwd(p7, log_a, gla_g, d):
    t = p7.shape[0]
    dk_all = d // 2
    dkh, dvh = dk_all // HEADS, d // HEADS
    cb = ROW_TILE
    ncb = cb // CHUNK
    scale = dkh ** -0.5

    def body(qk_ref, v_ref, r_ref, la_ref, g_ref, o_ref, y_ref, s_ref, st_scr):
        @pl.when(pl.program_id(0) == 0)
        def _():
            st_scr[...] = jnp.zeros_like(st_scr)

        tri = _tri(CHUNK)
        tri_f = tri.astype(F32)
        for c in range(ncb):
            rows = slice(c * CHUNK, (c + 1) * CHUNK)
            eb, ekl, ebl_inv, gam = _chunk_decays(la_ref[rows, :], tri_f)
            for h in range(HEADS):
                ks = slice(h * dkh, (h + 1) * dkh)
                vs = slice(h * dvh, (h + 1) * dvh)
                q = qk_ref[rows, ks].astype(F32) * scale
                k = qk_ref[rows, dk_all + h * dkh:dk_all + (h + 1) * dkh].astype(F32)
                v = v_ref[rows, vs].astype(BF16)
                qb = q * eb[:, ks]
                kh = (k * ekl[:, ks]).astype(BF16)
                qc = (qb * ebl_inv[:, ks]).astype(BF16)
                a = jnp.where(tri, _dot_nt(qc, kh), 0.0)
                st = st_scr[h]
                st_b = st.astype(BF16)
                s_ref[c, h] = st_b
                o = _dot_nt(qb.astype(BF16), st_b) + _dot(a.astype(BF16), v)
                st_scr[h] = st * gam[:, ks] + _dot_tn(v, kh)
                o_ref[rows, vs] = o
                rr = lax.rsqrt(jnp.mean(o * o, axis=-1, keepdims=True) + RMS_EPS)
                r = r_ref[rows, vs].astype(F32)
                y_ref[rows, vs] = (o * rr * g_ref[:, vs] * (r * _sigmoid(r))).astype(BF16)

    return pl.pallas_call(
        body, name="gla_fwd", grid=(t // cb,),
        in_specs=[pl.BlockSpec((cb, d), lambda i: (i, 0)),
                  pl.BlockSpec((cb, d), lambda i: (i, 1)),
                  pl.BlockSpec((cb, d), lambda i: (i, 2)),
                  pl.BlockSpec((cb, dk_all), lambda i: (i, 0)),
                  pl.BlockSpec((1, d), lambda i: (0, 0))],
        out_specs=[pl.BlockSpec((cb, d), lambda i: (i, 0)),
                   pl.BlockSpec((cb, d), lambda i: (i, 0)),
                   pl.BlockSpec((ncb, HEADS, dvh, dkh), lambda i: (i, 0, 0, 0))],
        out_shape=[jax.ShapeDtypeStruct((t, d), F32), jax.ShapeDtypeStruct((t, d), BF16),
                   jax.ShapeDtypeStruct((t // CHUNK, HEADS, dvh, dkh), BF16)],
        scratch_shapes=[pltpu.VMEM((HEADS, dvh, dkh), F32)],
        compiler_params=_cparams(("arbitrary",)),
    )(p7, p7, p7, log_a, gla_g)


SUBLANES = 8


def _tap_phases(n_taps, first):
    phases = {}
    for j in range(n_taps):
        e = first + j
        phases.setdefault(e % SUBLANES, []).append((j, e - e % SUBLANES))
    return phases


CONV_ROWS = 64


def _shifted_windows(ext_ref, shf_ref, phases, rows, ls):
    for p, taps in phases.items():
        if p:
            span = max(off for _, off in taps) + rows
            shf_ref[p, 0:span, :] = ext_ref[p:p + span, ls]

    def window(p, start, n):
        return shf_ref[p, start:start + n, :] if p else ext_ref[start:start + n, ls]

    return window


def _conv_strip(ext_ref, shf_ref, w_ref, n_taps, first, rows, ls, emit, reverse=False):
    phases = _tap_phases(n_taps, first)
    window = _shifted_windows(ext_ref, shf_ref, phases, rows, ls)
    for r0 in range(0, rows, CONV_ROWS):
        acc = None
        for p, taps in phases.items():
            for j, off in taps:
                wj = w_ref[(n_taps - 1 - j) if reverse else j, ls]
                term = window(p, off + r0, CONV_ROWS) * wj
                acc = term if acc is None else acc + term
        emit(r0, acc)


def _corr_strip(dl_ref, ext_ref, shf_ref, acc_ref, n_taps, first, rows, ls):
    phases = _tap_phases(n_taps, first)
    window = _shifted_windows(ext_ref, shf_ref, phases, rows, ls)
    for r0 in range(0, rows, CONV_ROWS):
        dl = dl_ref[r0:r0 + CONV_ROWS, ls]
        for p, taps in phases.items():
            for j, off in taps:
                prod = dl * window(p, off + r0, CONV_ROWS)
                acc_ref[SUBLANES * j:SUBLANES * (j + 1), ls] += jnp.sum(
                    prod.reshape(CONV_ROWS // SUBLANES, SUBLANES, prod.shape[-1]), axis=0)


def _conf_fwd(p7, conv_w, conv_b, ln_g, ln_b, d):
    t = p7.shape[0]
    tm = ROW_TILE

    def body(c1_ref, c2_ref, w_ref, b_ref, g_ref, bb_ref, cc_ref, sc_ref, ext, shf):
        @pl.when(pl.program_id(0) == 0)
        def _():
            ext[0:HALO, :] = jnp.zeros((HALO, d), F32)

        ext[HALO:HALO + tm, :] = c1_ref[...].astype(F32) * _sigmoid(c2_ref[...].astype(F32))
        for s0 in range(0, d, LANES):
            ls = slice(s0, s0 + LANES)

            def emit(r0, acc, ls=ls):
                cc_ref[r0:r0 + CONV_ROWS, ls] = acc + b_ref[:, ls]

            _conv_strip(ext, shf, w_ref, CONF_K, HALO - (CONF_K - 1), tm, ls, emit)
        ext[0:HALO, :] = ext[tm:tm + HALO, :]
        g = g_ref[...]
        bb = bb_ref[...]
        rows_per_step = 2 * SUBLANES

        def rows_step(k, carry):
            rs = pl.ds(pl.multiple_of(k * rows_per_step, rows_per_step), rows_per_step)
            cc = cc_ref[rs, :]
            xc = cc - jnp.mean(cc, axis=-1, keepdims=True)
            rstd = lax.rsqrt(jnp.mean(xc * xc, axis=-1, keepdims=True) + LN_EPS)
            cn = xc * rstd * g + bb
            sc_ref[rs, :] = (cn * _sigmoid(cn)).astype(BF16)
            return carry

        lax.fori_loop(0, tm // rows_per_step, rows_step, 0, unroll=4)

    vec = pl.BlockSpec((1, d), lambda i: (0, 0))
    return pl.pallas_call(
        body, name="conf_fwd", grid=(t // tm,),
        in_specs=[pl.BlockSpec((tm, d), lambda i: (i, 3)), pl.BlockSpec((tm, d), lambda i: (i, 4)),
                  pl.BlockSpec((CONF_K, d), lambda i: (0, 0)), vec, vec, vec],
        out_specs=[pl.BlockSpec((tm, d), lambda i: (i, 0)), pl.BlockSpec((tm, d), lambda i: (i, 0))],
        out_shape=[jax.ShapeDtypeStruct((t, d), F32), jax.ShapeDtypeStruct((t, d), BF16)],
        scratch_shapes=[pltpu.VMEM((tm + HALO, d), F32), pltpu.VMEM((SUBLANES, tm + HALO, LANES), F32)],
        compiler_params=_cparams(("arbitrary",)),
    )(p7, p7, conv_w, conv_b, ln_g, ln_b)


def _merge(p7, br_gla, br_conf, d):
    t = p7.shape[0]
    tm = ROW_TILE

    def body(gg_ref, gc_ref, a_ref, b_ref, o_ref):
        o_ref[...] = (_sigmoid(gg_ref[...].astype(F32)) * a_ref[...].astype(F32)
                      + _sigmoid(gc_ref[...].astype(F32)) * b_ref[...].astype(F32)).astype(BF16)

    row = pl.BlockSpec((tm, d), lambda i: (i, 0))
    return pl.pallas_call(
        body, name="merge", grid=(t // tm,),
        in_specs=[pl.BlockSpec((tm, d), lambda i: (i, 5)), pl.BlockSpec((tm, d), lambda i: (i, 6)), row, row],
        out_specs=row, out_shape=jax.ShapeDtypeStruct((t, d), BF16),
        compiler_params=_cparams(("parallel",)),
    )(p7, p7, br_gla, br_conf)


def _rms_fwd(h, g, name):
    t, d = h.shape
    tm = ROW_TILE

    def body(h_ref, g_ref, u_ref):
        x = h_ref[...]
        r = lax.rsqrt(jnp.mean(x * x, axis=-1, keepdims=True) + RMS_EPS)
        u_ref[...] = (x * r * g_ref[...]).astype(BF16)

    return pl.pallas_call(
        body, name=name, grid=(t // tm,),
        in_specs=[pl.BlockSpec((tm, d), lambda i: (i, 0)), pl.BlockSpec((1, d), lambda i: (0, 0))],
        out_specs=pl.BlockSpec((tm, d), lambda i: (i, 0)), out_shape=jax.ShapeDtypeStruct((t, d), BF16),
        compiler_params=_cparams(("parallel",)),
    )(h, g)


def _ffn_mid(up, w, b, dff):
    t = up.shape[0]
    tm = ROW_TILE
    hal = 8

    def body(a_ref, bv_ref, w_ref, b_ref, y_ref, ext, shf):
        i = pl.program_id(0)

        @pl.when(i == 0)
        def _():
            ext[0:hal, :] = jnp.zeros((hal, dff), F32)

        ext[hal:hal + tm, :] = a_ref[...].astype(F32) * _row_mask(i, tm)
        for s0 in range(0, dff, LANES):
            ls = slice(s0, s0 + LANES)

            def emit(r0, acc, ls=ls):
                rb = slice(r0, r0 + CONV_ROWS)
                ac = acc + b_ref[:, ls]
                y_ref[rb, ls] = (ac * _sigmoid(ac) * bv_ref[rb, ls].astype(F32)).astype(BF16)

            _conv_strip(ext, shf, w_ref, FFN_K, hal - (FFN_K - 1), tm, ls, emit)
        ext[0:hal, :] = ext[tm:tm + hal, :]

    return pl.pallas_call(
        body, name="ffn_mid", grid=(t // tm,),
        in_specs=[pl.BlockSpec((tm, dff), lambda i: (i, 0)), pl.BlockSpec((tm, dff), lambda i: (i, 1)),
                  pl.BlockSpec((FFN_K, dff), lambda i: (0, 0)), pl.BlockSpec((1, dff), lambda i: (0, 0))],
        out_specs=pl.BlockSpec((tm, dff), lambda i: (i, 0)), out_shape=jax.ShapeDtypeStruct((t, dff), BF16),
        scratch_shapes=[pltpu.VMEM((tm + hal, dff), F32), pltpu.VMEM((SUBLANES, tm + hal, LANES), F32)],
        compiler_params=_cparams(("arbitrary",)),
    )(up, up, w, b)


def _loss_head(h2, target, gf):
    t, d = h2.shape
    tm = ROW_TILE

    def body(h_ref, tg_ref, g_ref, loss_ref, dg_ref, dh_ref):
        i = pl.program_id(0)

        @pl.when(i == 0)
        def _():
            loss_ref[...] = jnp.zeros_like(loss_ref)
            dg_ref[...] = jnp.zeros_like(dg_ref)
            dh_ref[...] = jnp.zeros_like(dh_ref)

        @pl.when(i > 0)
        def _():
            h = h_ref[...]
            g = g_ref[...]
            r = lax.rsqrt(jnp.mean(h * h, axis=-1, keepdims=True) + RMS_EPS)
            n = h * r
            err = n * g - tg_ref[...]
            loss_ref[...] += 0.5 * jnp.sum(jnp.mean(err * err, axis=-1, keepdims=True), axis=0, keepdims=True)
            dout = err * (1.0 / d)
            dg_ref[...] += jnp.sum(dout * n, axis=0, keepdims=True)
            dn = dout * g
            dh_ref[...] = r * (dn - n * jnp.mean(dn * n, axis=-1, keepdims=True))

    return pl.pallas_call(
        body, name="loss_head", grid=(t // tm,),
        in_specs=[pl.BlockSpec((tm, d), lambda i: (i, 0)),
                  pl.BlockSpec((tm, d), lambda i: (jnp.maximum(i - 1, 0), 0)),
                  pl.BlockSpec((1, d), lambda i: (0, 0))],
        out_specs=[pl.BlockSpec((1, 1), lambda i: (0, 0)), pl.BlockSpec((1, d), lambda i: (0, 0)),
                   pl.BlockSpec((tm, d), lambda i: (i, 0))],
        out_shape=[jax.ShapeDtypeStruct((1, 1), F32), jax.ShapeDtypeStruct((1, d), F32),
                   jax.ShapeDtypeStruct((t, d), F32)],
        compiler_params=_cparams(("arbitrary",)),
    )(h2, target, gf)


def _dsilu(x, sig):
    return sig * (1.0 + x * (1.0 - sig))


def _ffn_mid_bwd(up, dy, w, b, dff):
    t = up.shape[0]
    tm = ROW_TILE
    hal = 8
    prev_rows = 2 * SUBLANES
    nt = t // tm

    def body(a_ref, ap_ref, bv_ref, dy_ref, w_ref, b_ref, dup_ref, dw_ref, db_ref, ext, dext, dw_acc, db_acc, shf):
        i = pl.program_id(0)
        tile = nt - 1 - i

        @pl.when(i == 0)
        def _():
            dext[tm:tm + hal, :] = jnp.zeros((hal, dff), F32)
            dw_acc[...] = jnp.zeros_like(dw_acc)
            db_acc[...] = jnp.zeros_like(db_acc)

        ext[0:hal, :] = ap_ref[prev_rows - hal:prev_rows, :].astype(F32) * jnp.where(tile > 0, 1.0, 0.0)
        ext[hal:hal + tm, :] = a_ref[...].astype(F32) * _row_mask(tile, tm)
        first = hal - (FFN_K - 1)
        for s0 in range(0, dff, LANES):
            ls = slice(s0, s0 + LANES)

            def emit_fwd(r0, acc, ls=ls, s0=s0):
                rb = slice(r0, r0 + CONV_ROWS)
                ac = acc + b_ref[:, ls]
                sig = _sigmoid(ac)
                dyv = dy_ref[rb, ls].astype(F32)
                dup_ref[rb, dff + s0:dff + s0 + LANES] = (dyv * ac * sig).astype(BF16)
                dac = dyv * bv_ref[rb, ls].astype(F32) * _dsilu(ac, sig)
                dext[rb, ls] = dac
                db_acc[:, ls] += jnp.sum(dac.reshape(CONV_ROWS // SUBLANES, SUBLANES, LANES), axis=0)

            _conv_strip(ext, shf, w_ref, FFN_K, first, tm, ls, emit_fwd)
            _corr_strip(dext, ext, shf, dw_acc, FFN_K, first, tm, ls)

            def emit_bwd(r0, da, ls=ls):
                rb = slice(r0, r0 + CONV_ROWS)
                mask = ((tile * tm + r0 + lax.broadcasted_iota(jnp.int32, (CONV_ROWS, 1), 0)) >= PAD).astype(F32)
                dup_ref[rb, ls] = (da * mask).astype(BF16)

            _conv_strip(dext, shf, w_ref, FFN_K, 0, tm, ls, emit_bwd, reverse=True)
        dext[tm:tm + hal, :] = dext[0:hal, :]

        @pl.when(i == nt - 1)
        def _():
            db_ref[...] = jnp.sum(db_acc[...], axis=0, keepdims=True)
            for j in range(FFN_K):
                dw_ref[j:j + 1, :] = jnp.sum(dw_acc[SUBLANES * j:SUBLANES * (j + 1), :], axis=0, keepdims=True)

    rev = lambda i: (nt - 1 - i, 0)
    return pl.pallas_call(
        body, name="ffn_mid_bwd", grid=(nt,),
        in_specs=[pl.BlockSpec((tm, dff), rev),
                  pl.BlockSpec((prev_rows, dff), lambda i: (jnp.maximum((nt - 1 - i) * (tm // prev_rows) - 1, 0), 0)),
                  pl.BlockSpec((tm, dff), lambda i: (nt - 1 - i, 1)),
                  pl.BlockSpec((tm, dff), rev),
                  pl.BlockSpec((FFN_K, dff), lambda i: (0, 0)), pl.BlockSpec((1, dff), lambda i: (0, 0))],
        out_specs=[pl.BlockSpec((tm, 2 * dff), rev),
                   pl.BlockSpec((FFN_K, dff), lambda i: (0, 0)), pl.BlockSpec((1, dff), lambda i: (0, 0))],
        out_shape=[jax.ShapeDtypeStruct((t, 2 * dff), BF16),
                   jax.ShapeDtypeStruct((FFN_K, dff), F32), jax.ShapeDtypeStruct((1, dff), F32)],
        scratch_shapes=[pltpu.VMEM((tm + hal, dff), F32), pltpu.VMEM((tm + hal, dff), F32),
                        pltpu.VMEM((SUBLANES * FFN_K, dff), F32), pltpu.VMEM((SUBLANES, dff), F32),
                        pltpu.VMEM((SUBLANES, tm + hal, LANES), F32)],
        compiler_params=_cparams(("arbitrary",)),
    )(up, up, up, dy, w, b)


def _rms_bwd(du, h, g, dres, name):
    t, d = h.shape
    tm = ROW_TILE

    def body(du_ref, h_ref, g_ref, dres_ref, dh_ref, dg_ref):
        @pl.when(pl.program_id(0) == 0)
        def _():
            dg_ref[...] = jnp.zeros_like(dg_ref)

        x = h_ref[...]
        r = lax.rsqrt(jnp.mean(x * x, axis=-1, keepdims=True) + RMS_EPS)
        n = x * r
        du_ = du_ref[...]
        dg_ref[...] += jnp.sum(du_ * n, axis=0, keepdims=True)
        dn = du_ * g_ref[...]
        dh_ref[...] = dres_ref[...] + r * (dn - n * jnp.mean(dn * n, axis=-1, keepdims=True))

    row = pl.BlockSpec((tm, d), lambda i: (i, 0))
    vec = pl.BlockSpec((1, d), lambda i: (0, 0))
    return pl.pallas_call(
        body, name=name, grid=(t // tm,),
        in_specs=[row, row, vec, row], out_specs=[row, vec],
        out_shape=[jax.ShapeDtypeStruct((t, d), F32), jax.ShapeDtypeStruct((1, d), F32)],
        compiler_params=_cparams(("arbitrary",)),
    )(du, h, g, dres)


def _merge_bwd(p7, br_gla, br_conf, dmerged, d):
    t = p7.shape[0]
    tm = ROW_TILE

    def body(gg_ref, gc_ref, a_ref, b_ref, dm_ref, da_ref, db_ref, dgg_ref, dgc_ref):
        dm = dm_ref[...].astype(F32)
        sg = _sigmoid(gg_ref[...].astype(F32))
        sc = _sigmoid(gc_ref[...].astype(F32))
        da_ref[...] = (dm * sg).astype(BF16)
        db_ref[...] = (dm * sc).astype(BF16)
        dgg_ref[...] = (dm * a_ref[...].astype(F32) * sg * (1.0 - sg)).astype(BF16)
        dgc_ref[...] = (dm * b_ref[...].astype(F32) * sc * (1.0 - sc)).astype(BF16)

    row = pl.BlockSpec((tm, d), lambda i: (i, 0))
    outs = pl.pallas_call(
        body, name="merge_bwd", grid=(t // tm,),
        in_specs=[pl.BlockSpec((tm, d), lambda i: (i, 5)), pl.BlockSpec((tm, d), lambda i: (i, 6)), row, row, row],
        out_specs=[row, row, row, row],
        out_shape=[jax.ShapeDtypeStruct((t, d), BF16)] * 4,
        compiler_params=_cparams(("parallel",)),
    )(p7, p7, br_gla, br_conf, dmerged)
    return outs


def _conf_bwd(p7, cc, dsc, conv_w, ln_g, ln_b, d):
    t = p7.shape[0]
    tm = ROW_TILE
    nt = t // tm

    def body(c1_ref, c2_ref, c1p_ref, c2p_ref, cc_ref, dsc_ref, w_ref, g_ref, bb_ref,
             dc1_ref, dc2_ref, dw_ref, db_ref, dg_ref, dbb_ref, ext, dext, dw_acc, shf):
        i = pl.program_id(0)
        tile = nt - 1 - i

        @pl.when(i == 0)
        def _():
            dext[tm:tm + HALO, :] = jnp.zeros((HALO, d), F32)
            dw_acc[...] = jnp.zeros_like(dw_acc)
            db_ref[...] = jnp.zeros_like(db_ref)
            dg_ref[...] = jnp.zeros_like(dg_ref)
            dbb_ref[...] = jnp.zeros_like(dbb_ref)

        ext[0:HALO, :] = (c1p_ref[...].astype(F32) * _sigmoid(c2p_ref[...].astype(F32))
                          * jnp.where(tile > 0, 1.0, 0.0))
        g = g_ref[...]
        bb = bb_ref[...]

        groups = 2
        grp = 2 * SUBLANES

        def fold(a):
            return a[:SUBLANES] + a[SUBLANES:]

        def rows_step(k, sums):
            sdg, sdbb, sdb = sums
            for u in range(groups):
                r0 = (k * groups + u) * grp
                rs = pl.ds(pl.multiple_of(r0, grp), grp)
                ext[pl.ds(pl.multiple_of(HALO + r0, grp), grp), :] = (
                    c1_ref[rs, :].astype(F32) * _sigmoid(c2_ref[rs, :].astype(F32)))
                cc_ = cc_ref[rs, :]
                xc = cc_ - jnp.mean(cc_, axis=-1, keepdims=True)
                rstd = lax.rsqrt(jnp.mean(xc * xc, axis=-1, keepdims=True) + LN_EPS)
                xh = xc * rstd
                cn = xh * g + bb
                dcn = dsc_ref[rs, :].astype(F32) * _dsilu(cn, _sigmoid(cn))
                dxh = dcn * g
                dcc = rstd * (dxh - jnp.mean(dxh, axis=-1, keepdims=True)
                              - xh * jnp.mean(dxh * xh, axis=-1, keepdims=True))
                dext[rs, :] = dcc
                sdg, sdbb, sdb = sdg + fold(dcn * xh), sdbb + fold(dcn), sdb + fold(dcc)
            return sdg, sdbb, sdb

        zero = jnp.zeros((SUBLANES, d), F32)
        sdg, sdbb, sdb = lax.fori_loop(0, tm // (groups * grp), rows_step, (zero, zero, zero))
        dg_ref[...] += jnp.sum(sdg, axis=0, keepdims=True)
        dbb_ref[...] += jnp.sum(sdbb, axis=0, keepdims=True)
        db_ref[...] += jnp.sum(sdb, axis=0, keepdims=True)
        for s0 in range(0, d, LANES):
            ls = slice(s0, s0 + LANES)
            _corr_strip(dext, ext, shf, dw_acc, CONF_K, HALO - (CONF_K - 1), tm, ls)

            def emit(r0, dc, ls=ls):
                rb = slice(r0, r0 + CONV_ROWS)
                s2l = _sigmoid(c2_ref[rb, ls].astype(F32))
                dc1_ref[rb, ls] = (dc * s2l).astype(BF16)
                dc2_ref[rb, ls] = (dc * c1_ref[rb, ls].astype(F32) * s2l * (1.0 - s2l)).astype(BF16)

            _conv_strip(dext, shf, w_ref, CONF_K, 0, tm, ls, emit, reverse=True)
        dext[tm:tm + HALO, :] = dext[0:HALO, :]

        @pl.when(i == nt - 1)
        def _():
            for j in range(CONF_K):
                dw_ref[j:j + 1, :] = jnp.sum(dw_acc[SUBLANES * j:SUBLANES * (j + 1), :], axis=0, keepdims=True)

    rev = lambda i: (nt - 1 - i, 0)
    prev = lambda col: (lambda i: (jnp.maximum((nt - 1 - i) * (tm // HALO) - 1, 0), col))
    vec = pl.BlockSpec((1, d), lambda i: (0, 0))
    return pl.pallas_call(
        body, name="conf_bwd", grid=(nt,),
        in_specs=[pl.BlockSpec((tm, d), lambda i: (nt - 1 - i, 3)), pl.BlockSpec((tm, d), lambda i: (nt - 1 - i, 4)),
                  pl.BlockSpec((HALO, d), prev(3)), pl.BlockSpec((HALO, d), prev(4)),
                  pl.BlockSpec((tm, d), rev), pl.BlockSpec((tm, d), rev),
                  pl.BlockSpec((CONF_K, d), lambda i: (0, 0)), vec, vec],
        out_specs=[pl.BlockSpec((tm, d), rev), pl.BlockSpec((tm, d), rev),
                   pl.BlockSpec((CONF_K, d), lambda i: (0, 0)), vec, vec, vec],
        out_shape=[jax.ShapeDtypeStruct((t, d), BF16), jax.ShapeDtypeStruct((t, d), BF16),
                   jax.ShapeDtypeStruct((CONF_K, d), F32)] + [jax.ShapeDtypeStruct((1, d), F32)] * 3,
        scratch_shapes=[pltpu.VMEM((tm + HALO, d), F32), pltpu.VMEM((tm + HALO, d), F32),
                        pltpu.VMEM((SUBLANES * CONF_K, d), F32), pltpu.VMEM((SUBLANES, tm + HALO, LANES), F32)],
        compiler_params=_cparams(("arbitrary",)),
    )(p7, p7, p7, p7, cc, dsc, conv_w, ln_g, ln_b)


def _gla_bwd(p7, log_a, alr, wau, b_alpha, gla_g, o, states, dy, d):
    t = p7.shape[0]
    dk_all = d // 2
    dkh, dvh = dk_all // HEADS, d // HEADS
    cb = ROW_TILE
    ncb = cb // CHUNK
    nb = t // cb
    scale = dkh ** -0.5

    def body(qk_ref, v_ref, r_ref, la_ref, alr_ref, wau_ref, ba_ref, g_ref, o_ref, s_ref, dy_ref,
             dqk_ref, dv_ref, dr_ref, dz_ref, dg_ref, dba_ref, dst_scr, dla_scr):
        i = pl.program_id(0)
        blk = nb - 1 - i

        @pl.when(i == 0)
        def _():
            dst_scr[...] = jnp.zeros_like(dst_scr)
            dg_ref[...] = jnp.zeros_like(dg_ref)
            dba_ref[...] = jnp.zeros_like(dba_ref)

        tri = _tri(CHUNK)
        tri_f = tri.astype(F32)
        triu_f = _tri(CHUNK, upper=True).astype(F32)
        for c in reversed(range(ncb)):
            rows = slice(c * CHUNK, (c + 1) * CHUNK)
            eb, ekl, ebl_inv, gam = _chunk_decays(la_ref[rows, :], tri_f)
            for h in range(HEADS):
                ks = slice(h * dkh, (h + 1) * dkh)
                kcols = slice(dk_all + h * dkh, dk_all + (h + 1) * dkh)
                vs = slice(h * dvh, (h + 1) * dvh)
                q = qk_ref[rows, ks].astype(F32) * scale
                k = qk_ref[rows, kcols].astype(F32)
                v = v_ref[rows, vs].astype(BF16)
                ebh, eklh, eih, gamh = eb[:, ks], ekl[:, ks], ebl_inv[:, ks], gam[:, ks]
                qb = q * ebh
                kh = k * eklh
                qc = qb * eih
                qb_b, kh_b, qc_b = qb.astype(BF16), kh.astype(BF16), qc.astype(BF16)
                ov = o_ref[rows, vs]
                r = r_ref[rows, vs].astype(F32)
                dyv = dy_ref[rows, vs].astype(F32)
                sig = _sigmoid(r)
                rr = lax.rsqrt(jnp.mean(ov * ov, axis=-1, keepdims=True) + RMS_EPS)
                n = ov * rr
                g = g_ref[:, vs]
                dr_ref[rows, vs] = (dyv * n * g * _dsilu(r, sig)).astype(BF16)
                don = dyv * (r * sig)
                dg_ref[:, vs] += jnp.sum(don * n, axis=0, keepdims=True)
                dn = don * g
                do = (rr * (dn - n * jnp.mean(dn * n, axis=-1, keepdims=True))).astype(BF16)
                st_b = s_ref[c, h]
                a = jnp.where(tri, _dot_nt(qc_b, kh_b), 0.0).astype(BF16)
                da = jnp.where(tri, _dot_nt(do, v), 0.0).astype(BF16)
                dst = dst_scr[h]
                dst_b = dst.astype(BF16)
                dv_ref[rows, vs] = (_dot_tn(a, do) + _dot_nt(kh_b, dst_b)).astype(BF16)
                dqb = _dot(do, st_b)
                dqc = _dot(da, kh_b)
                dkh_ = _dot_tn(da, qc_b) + _dot(v, dst_b)
                dgam = jnp.sum(st_b.astype(F32) * dst, axis=0, keepdims=True)
                dst_scr[h] = dst * gamh + _dot_tn(do, qb_b)
                dqk_ref[rows, ks] = ((dqb * ebh + dqc * (ebh * eih)) * scale).astype(BF16)
                dqk_ref[rows, kcols] = (dkh_ * eklh).astype(BF16)
                qq = dqc * qc
                kk = dkh_ * kh
                db = dqb * qb + qq - kk
                dbl = jnp.sum(kk - qq, axis=0, keepdims=True) + dgam * gamh
                dla_scr[rows, ks] = jnp.dot(triu_f, db, preferred_element_type=F32,
                                            precision=lax.Precision.HIGHEST) + dbl
        z = jnp.dot(alr_ref[...].astype(BF16), wau_ref[...].astype(BF16), preferred_element_type=F32) + ba_ref[...]
        dz = dla_scr[...] * (1.0 / GATE_TAU) * _sigmoid(-z) * _row_mask(blk, cb)
        dba_ref[...] += jnp.sum(dz, axis=0, keepdims=True)
        dz_ref[...] = dz.astype(BF16)

    rev = lambda i: (nb - 1 - i, 0)
    row = pl.BlockSpec((cb, d), rev)
    return pl.pallas_call(
        body, name="gla_bwd", grid=(nb,),
        in_specs=[row, pl.BlockSpec((cb, d), lambda i: (nb - 1 - i, 1)), pl.BlockSpec((cb, d), lambda i: (nb - 1 - i, 2)),
                  pl.BlockSpec((cb, dk_all), rev), pl.BlockSpec((cb, RANK_PAD), rev),
                  pl.BlockSpec((RANK_PAD, dk_all), lambda i: (0, 0)), pl.BlockSpec((1, dk_all), lambda i: (0, 0)),
                  pl.BlockSpec((1, d), lambda i: (0, 0)), row,
                  pl.BlockSpec((ncb, HEADS, dvh, dkh), lambda i: (nb - 1 - i, 0, 0, 0)), row],
        out_specs=[row, row, row, pl.BlockSpec((cb, dk_all), rev),
                   pl.BlockSpec((1, d), lambda i: (0, 0)), pl.BlockSpec((1, dk_all), lambda i: (0, 0))],
        out_shape=[jax.ShapeDtypeStruct((t, d), BF16)] * 3 + [jax.ShapeDtypeStruct((t, dk_all), BF16),
                   jax.ShapeDtypeStruct((1, d), F32), jax.ShapeDtypeStruct((1, dk_all), F32)],
        scratch_shapes=[pltpu.VMEM((HEADS, dvh, dkh), F32), pltpu.VMEM((cb, dk_all), F32)],
        compiler_params=_cparams(("arbitrary",)),
    )(p7, p7, p7, log_a, alr, wau, b_alpha, gla_g, o, states, dy)


def _input_grad(du_a, du_b, h0, g1, dh1):
    t, d = h0.shape
    tm = ROW_TILE
    s = t - HEAD_ROWS

    def body(dua_ref, dub_ref, h_ref, g_ref, dres_ref, gx_ref, gm_ref, dg_ref):
        i = pl.program_id(0)

        @pl.when(i == 0)
        def _():
            dg_ref[...] = jnp.zeros_like(dg_ref)

        x = h_ref[...]
        r = lax.rsqrt(jnp.mean(x * x, axis=-1, keepdims=True) + RMS_EPS)
        n = x * r
        du_ = dua_ref[...] + dub_ref[...]
        dg_ref[...] += jnp.sum(du_ * n, axis=0, keepdims=True)
        dn = du_ * g_ref[...]
        dh = dres_ref[...] + r * (dn - n * jnp.mean(dn * n, axis=-1, keepdims=True))

        @pl.when(i == 0)
        def _():
            gm_ref[...] = dh[PAD:HEAD_ROWS, :]
            gx_ref[...] = jnp.zeros_like(gx_ref)

        @pl.when(i > 0)
        def _():
            gx_ref[...] = dh

    row = pl.BlockSpec((tm, d), lambda i: (i, 0))
    vec = pl.BlockSpec((1, d), lambda i: (0, 0))
    return pl.pallas_call(
        body, name="input_grad", grid=(t // tm,),
        in_specs=[row, row, row, vec, row],
        out_specs=[pl.BlockSpec((tm, d), lambda i: (jnp.maximum(i - 1, 0), 0)),
                   pl.BlockSpec((N_META, d), lambda i: (0, 0)), vec],
        out_shape=[jax.ShapeDtypeStruct((s, d), F32), jax.ShapeDtypeStruct((N_META, d), F32),
                   jax.ShapeDtypeStruct((1, d), F32)],
        compiler_params=_cparams(("arbitrary",)),
    )(du_a, du_b, h0, g1, dh1)


REST = ("w_up", "w_down", "w_gla_o", "w_conf_o", "w_out")


def _chip_partials(grads, names, tag):
    core = lax.axis_index("c").reshape(1)
    from_sibling = _swap_core_halves(grads, "swap_core_halves_" + tag)
    return [_add_core_halves(core, g, s, "add_core_halves_" + n) for n, g, s in zip(names, grads, from_sibling)]


def _local_step(x, target, w, rest_shards):
    s, d = x.shape
    dk_all = d // 2
    w_in = w["w_in"]
    lo, hi = 3 * d, 3 * d + GLA_RANK
    wq = jnp.concatenate([w_in[:, :lo], w_in[:, hi:]], axis=1)
    w_alr = jnp.pad(w_in[:, lo:hi], ((0, 0), (0, RANK_PAD - GLA_RANK)))
    wau = jnp.pad(w["w_alpha_up"], ((0, RANK_PAD - GLA_RANK), (0, 0)))

    h0, u1 = _prep(x, w["meta_tokens"], w["norm_mix_g"])
    shards = [rest_shards[n] for n in REST]
    p7, gathered = _matmul(u1, wq, dims="nn", name="proj", out_dtype=BF16, hosted=_gather_plan(shards))
    w = dict(w)
    for n, slabs in zip(REST, _with_own_slab(gathered, shards)):
        w[n] = slabs if n == "w_up" else slabs.reshape(-1, slabs.shape[-1])
    dff = w["w_down"].shape[0]
    alr = _matmul(u1, w_alr, dims="nn", name="proj_alr")
    log_a = _log_gate(alr, wau, w["b_alpha"])
    o, y_gla, states = _gla_fwd(p7, log_a, w["gla_norm_g"], d)
    br_gla = _matmul(y_gla, w["w_gla_o"], dims="nn", name="gla_out", out_dtype=BF16)
    cc, s_c = _conf_fwd(p7, w["conf_dw_w"], w["conf_dw_b"], w["conf_ln_g"], w["conf_ln_b"], d)
    br_conf = _matmul(s_c, w["w_conf_o"], dims="nn", name="conf_out", out_dtype=BF16)
    merged = _merge(p7, br_gla, br_conf, d)
    h1 = _matmul(merged, w["w_out"], dims="nn", name="mix_out", add=h0)
    u2 = _rms_fwd(h1, w["norm_ffn_g"], "norm_ffn")
    up = _matmul(u2, w["w_up"], dims="nn", name="ffn_up", out_dtype=BF16, chips="b")
    y = _ffn_mid(up, w["ffn_dw_w"], w["ffn_dw_b"], dff)
    h2 = _matmul(y, w["w_down"], dims="nn", name="ffn_down", add=h1)
    loss, d_gf, dh2 = _loss_head(h2, target, w["final_norm_g"])

    g = {"final_norm_g": d_gf}
    dy = _matmul(dh2, w["w_down"], dims="nt", name="d_ffn_y", out_dtype=BF16)
    g["w_down"] = _matmul(y, dh2, dims="tn", name="dw_down", out_dtype=BF16)
    dup, g["ffn_dw_w"], g["ffn_dw_b"] = _ffn_mid_bwd(up, dy, w["ffn_dw_w"], w["ffn_dw_b"], dff)
    du2 = _matmul(dup, w["w_up"], dims="nt", name="d_u2", chips="b")
    g["w_up"] = _matmul(u2, dup, dims="tn", name="dw_up", out_dtype=BF16, chips="out")
    dh1, g["norm_ffn_g"] = _rms_bwd(du2, h1, w["norm_ffn_g"], dh2, "norm_ffn_bwd")
    dmerged = _matmul(dh1, w["w_out"], dims="nt", name="d_merged", out_dtype=BF16)
    g["w_out"] = _matmul(merged, dh1, dims="tn", name="dw_out", out_dtype=BF16)
    d_br_gla, d_br_conf, dgg, dgc = _merge_bwd(p7, br_gla, br_conf, dmerged, d)
    dsc = _matmul(d_br_conf, w["w_conf_o"], dims="nt", name="d_conf_s", out_dtype=BF16)
    g["w_conf_o"] = _matmul(s_c, d_br_conf, dims="tn", name="dw_conf_o", out_dtype=BF16)
    dc1, dc2, g["conf_dw_w"], g["conf_dw_b"], g["conf_ln_g"], g["conf_ln_b"] = _conf_bwd(
        p7, cc, dsc, w["conf_dw_w"], w["conf_ln_g"], w["conf_ln_b"], d)
    dyg = _matmul(d_br_gla, w["w_gla_o"], dims="nt", name="d_gla_y", out_dtype=BF16)
    g["w_gla_o"] = _matmul(y_gla, d_br_gla, dims="tn", name="dw_gla_o", out_dtype=BF16)
    dqk, dv, dr, dz, g["gla_norm_g"], g["b_alpha"] = _gla_bwd(
        p7, log_a, alr, wau, w["b_alpha"], w["gla_norm_g"], o, states, dyg, d)
    dalr = _matmul(dz, wau, dims="nt", name="d_alr", out_dtype=BF16)
    g["w_alpha_up"] = _matmul(alr, dz, dims="tn", name="dw_alpha_up")[:GLA_RANK]
    dp7 = jnp.concatenate([dqk, dv, dr, dc1, dc2, dgg, dgc], axis=1)
    rest_grads = [g.pop(n) for n in REST]
    rest_grads = [a if a.ndim == 3 else a.reshape(N_CHIPS, -1, a.shape[-1]) for a in rest_grads]
    parts = dict(zip(REST, _chip_partials(rest_grads, REST, "rest")))
    dwq, arrived = _matmul(u1, dp7, dims="tn", name="dw_in", out_dtype=BF16,
                           hosted=_exchange_plan([parts[n] for n in REST]))
    from_chips = dict(zip(REST, arrived))
    dw_alr = _matmul(u1, dalr, dims="tn", name="dw_in_alr", out_dtype=BF16)
    dw_in = _to_chip_major(jnp.concatenate([dwq[:, :lo], dw_alr[:, :GLA_RANK], dwq[:, lo:]], axis=1))
    parts["w_in"] = _chip_partials([dw_in], ["w_in"], "w_in")[0]
    du1a, arrived = _matmul(dp7, wq, dims="nt", name="d_u1", hosted=_exchange_plan([parts["w_in"]]))
    from_chips["w_in"] = arrived[0]
    du1b = _matmul(dalr, w_alr, dims="nt", name="d_u1_alr")
    grad_x, g["meta_tokens"], g["norm_mix_g"] = _input_grad(du1a, du1b, h0, w["norm_mix_g"], dh1)
    return loss, grad_x, g, parts, from_chips


HBM_SPEC = pl.BlockSpec(memory_space=pltpu.HBM)
FLIPS = ((1, 0), (0, 1), (1, 1))


def _place():
    x, y, c = lax.axis_index("x"), lax.axis_index("y"), lax.axis_index("c")
    return x, y, c


def _half_rows(ref, h, lead=()):
    rh = ref.shape[-2] // 2
    return ref.at[(*lead, pl.ds(pl.multiple_of(h * rh, 2 * SUBLANES), rh), slice(None))]


class _Hosted:
    def __init__(self, operands, out_shapes, sem_shapes, start, finish):
        self.operands, self.out_shapes, self.sem_shapes = operands, out_shapes, sem_shapes
        self.start, self.finish = start, finish


def _run_hosted(h, name):
    n_in, n_out = len(h.operands), len(h.out_shapes)

    def body(*refs):
        ins, outs, sems = refs[:n_in], refs[n_in:n_in + n_out], refs[n_in + n_out:]
        h.start(ins, outs, sems)
        h.finish(ins, outs, sems)

    return pl.pallas_call(
        body, name=name, in_specs=[HBM_SPEC] * n_in, out_specs=[HBM_SPEC] * n_out,
        out_shape=list(h.out_shapes), scratch_shapes=list(h.sem_shapes),
    )(*h.operands)


def _gather_plan(shards):
    n = len(shards)

    def copies(ins, outs, sems, kinds):
        send_sems, recv_sems = sems
        x, y, c = _place()
        chips = [(x ^ fx, y ^ fy) for fx, fy in FLIPS]
        me, sibling = (x, y, c), (x, y, 1 - c)

        def copy(k, sem, chip, h, to, src=None):
            slot = _half_rows(outs[k], h, lead=(2 * chip[0] + chip[1],))
            return pltpu.make_async_remote_copy(src_ref=slot if src is None else src, dst_ref=slot,
                                                send_sem=send_sems.at[sem], recv_sem=recv_sems.at[sem],
                                                device_id=to, device_id_type=MESH)

        make = {
            "first": lambda k, j, chip: copy(k, 3 * k + j, (x, y), c, (*chip, c), src=_half_rows(ins[k], c)),
            "landed": lambda k, j, chip: copy(k, 3 * k + j, chip, c, me),
            "passed": lambda k, j, chip: copy(k, 3 * n + 3 * k + j, chip, c, sibling),
            "from_sibling": lambda k, j, chip: copy(k, 3 * n + 3 * k + j, chip, 1 - c, me),
        }
        return [[make[kind](k, j, chip) for k in range(n) for j, chip in enumerate(chips)] for kind in kinds]

    def start(ins, outs, sems):
        for cp in copies(ins, outs, sems, ["first"])[0]:
            cp.start()

    def finish(ins, outs, sems):
        first, landed, passed, from_sibling = copies(ins, outs, sems, ["first", "landed", "passed", "from_sibling"])
        for arrived, fwd in zip(landed, passed):
            arrived.wait_recv()
            fwd.start()
        for cp in from_sibling:
            cp.wait_recv()
        for cp in first + passed:
            cp.wait_send()

    return _Hosted(list(shards), [jax.ShapeDtypeStruct((N_CHIPS, *s.shape), s.dtype) for s in shards],
                   [pltpu.SemaphoreType.DMA((6 * n,)), pltpu.SemaphoreType.DMA((6 * n,))], start, finish)


def _with_own_slab(gathered, shards):
    me = 2 * lax.axis_index("x") + lax.axis_index("y")
    return [lax.dynamic_update_index_in_dim(o, s, me, 0) for o, s in zip(gathered, shards)]


def _swap_core_halves(grads, name):
    n = len(grads)

    def body(*refs):
        ins, outs = refs[:n], refs[n:2 * n]
        send_sems, recv_sems = refs[2 * n:]
        x, y, c = _place()
        cps = [pltpu.make_async_remote_copy(
            src_ref=_half_rows(ins[k], 1 - c, lead=(slice(None),)), dst_ref=outs[k], send_sem=send_sems.at[k],
            recv_sem=recv_sems.at[k], device_id=(x, y, 1 - c), device_id_type=MESH) for k in range(n)]
        for cp in cps:
            cp.start()
        for cp in cps:
            cp.wait()

    return pl.pallas_call(
        body, name=name, in_specs=[HBM_SPEC] * n, out_specs=[HBM_SPEC] * n,
        out_shape=[jax.ShapeDtypeStruct((g.shape[0], g.shape[1] // 2, g.shape[2]), g.dtype) for g in grads],
        scratch_shapes=[pltpu.SemaphoreType.DMA((n,)), pltpu.SemaphoreType.DMA((n,))],
    )(*grads)


def _exchange_plan(parts):
    n = len(parts)

    def copies(ins, outs, sems):
        send_sems, recv_sems = sems
        x, y, c = _place()
        cps = []
        for k in range(n):
            for j, (fx, fy) in enumerate(FLIPS):
                tx, ty = x ^ fx, y ^ fy
                cps.append(pltpu.make_async_remote_copy(
                    src_ref=ins[k].at[2 * tx + ty], dst_ref=outs[k].at[j], send_sem=send_sems.at[3 * k + j],
                    recv_sem=recv_sems.at[3 * k + j], device_id=(tx, ty, c), device_id_type=MESH))
        return cps

    def start(ins, outs, sems):
        for cp in copies(ins, outs, sems):
            cp.start()

    def finish(ins, outs, sems):
        for cp in copies(ins, outs, sems):
            cp.wait()

    return _Hosted(list(parts), [jax.ShapeDtypeStruct((3, *p.shape[1:]), p.dtype) for p in parts],
                   [pltpu.SemaphoreType.DMA((3 * n,)), pltpu.SemaphoreType.DMA((3 * n,))], start, finish)


def _join_core_halves(fulls):
    n = len(fulls)

    def body(*refs):
        bufs = refs[n:2 * n]
        send_sems, recv_sems = refs[2 * n:]
        x, y, c = _place()
        cps = [pltpu.make_async_remote_copy(
            src_ref=_half_rows(bufs[k], c), dst_ref=_half_rows(bufs[k], c), send_sem=send_sems.at[k],
            recv_sem=recv_sems.at[k], device_id=(x, y, 1 - c), device_id_type=MESH) for k in range(n)]
        for cp in cps:
            cp.start()
        for k in range(n):
            cps[k].wait_send()
            pltpu.make_async_remote_copy(
                src_ref=_half_rows(bufs[k], c), dst_ref=_half_rows(bufs[k], 1 - c), send_sem=send_sems.at[k],
                recv_sem=recv_sems.at[k], device_id=(x, y, 1 - c), device_id_type=MESH).wait_recv()

    return pl.pallas_call(
        body, name="join_core_halves", in_specs=[HBM_SPEC] * n, out_specs=[HBM_SPEC] * n,
        out_shape=[jax.ShapeDtypeStruct(f.shape, f.dtype) for f in fulls],
        input_output_aliases={k: k for k in range(n)},
        scratch_shapes=[pltpu.SemaphoreType.DMA((n,)), pltpu.SemaphoreType.DMA((n,))],
    )(*fulls)


def _gather_small(block, name):
    m, n = block.shape

    def body(x_ref, out_ref, send_sems, recv_sems, local_sem):
        x, y, c = _place()
        me, sibling = (x, y, c), (x, y, 1 - c)
        chips = [(x ^ fx, y ^ fy) for fx, fy in FLIPS]

        def rows(px, py, pc):
            return out_ref.at[pl.ds((4 * px + 2 * py + pc) * m, m), :]

        def copy(k, blk, to, src=None):
            return pltpu.make_async_remote_copy(
                src_ref=rows(*blk) if src is None else src, dst_ref=rows(*blk),
                send_sem=send_sems.at[k], recv_sem=recv_sems.at[k], device_id=to, device_id_type=MESH)

        mine = pltpu.make_async_copy(x_ref, rows(*me), local_sem)
        mine.start()
        first = [copy(0, me, sibling, src=x_ref)]
        first += [copy(1 + j, me, (*chip, c), src=x_ref) for j, chip in enumerate(chips)]
        for cp in first:
            cp.start()
        passed = [copy(4 + j, (*chip, c), sibling) for j, chip in enumerate(chips)]
        for j, chip in enumerate(chips):
            copy(1 + j, (*chip, c), me).wait_recv()
            passed[j].start()
        copy(0, sibling, me).wait_recv()
        for j, chip in enumerate(chips):
            copy(4 + j, (*chip, 1 - c), me).wait_recv()
        for cp in first + passed:
            cp.wait_send()
        mine.wait()

    out = pl.pallas_call(
        body, name=name,
        out_shape=jax.ShapeDtypeStruct((8 * m, n), block.dtype),
        in_specs=[pl.BlockSpec(memory_space=pltpu.VMEM)],
        out_specs=pl.BlockSpec(memory_space=pltpu.VMEM),
        scratch_shapes=[pltpu.SemaphoreType.DMA((7,)), pltpu.SemaphoreType.DMA((7,)), pltpu.SemaphoreType.DMA],
    )(block)
    return out.reshape(8, m, n)


def _add_core_halves(core, grad, from_sibling, name):
    nc, r, cols = grad.shape
    rh = r // 2

    def body(core_ref, g_ref, s_ref, o_ref):
        o_ref[...] = (g_ref[...].astype(F32) + s_ref[...].astype(F32)).astype(BF16)

    spec = pl.BlockSpec((1, rh, cols), lambda t, core_ref: (t, 0, 0))
    return pl.pallas_call(
        body, name=name,
        grid_spec=pltpu.PrefetchScalarGridSpec(
            num_scalar_prefetch=1, grid=(nc,),
            in_specs=[pl.BlockSpec((1, rh, cols), lambda t, core_ref: (t, core_ref[0], 0)), spec], out_specs=spec),
        out_shape=jax.ShapeDtypeStruct((nc, rh, cols), BF16),
        compiler_params=_cparams(("parallel",)),
    )(core, grad, from_sibling)


def _sum_chip_partials(place, parts, others, name):
    _, rh, cols = parts.shape
    tr = _pick(rh, (128, 176, 64, 32, 16, 8))
    nb = rh // tr

    def body(place_ref, a_ref, b_ref, o_ref):
        acc = a_ref[0].astype(F32)
        for j in range(3):
            acc = acc + b_ref[j].astype(F32)
        o_ref[...] = acc

    return pl.pallas_call(
        body, name=name,
        grid_spec=pltpu.PrefetchScalarGridSpec(
            num_scalar_prefetch=1, grid=(nb,),
            in_specs=[pl.BlockSpec((1, tr, cols), lambda i, place_ref: (place_ref[0], i, 0)),
                      pl.BlockSpec((3, tr, cols), lambda i, place_ref: (0, i, 0))],
            out_specs=pl.BlockSpec((tr, cols), lambda i, place_ref: (place_ref[1] * nb + i, 0))),
        out_shape=jax.ShapeDtypeStruct((2 * rh, cols), F32),
        compiler_params=_cparams(("parallel",)),
    )(place, parts, others)


def _sum_devices(blocks):
    n, m, _ = blocks.shape

    def body(b_ref, o_ref):
        acc = b_ref[0]
        for j in range(1, n):
            acc = acc + b_ref[j]
        o_ref[...] = acc

    return pl.pallas_call(
        body, name="sum_devices", out_shape=jax.ShapeDtypeStruct((m, LANES), F32),
        in_specs=[pl.BlockSpec(memory_space=pltpu.VMEM)], out_specs=pl.BlockSpec(memory_space=pltpu.VMEM),
    )(blocks)


def _adamw(w, g, m, v, name):
    rws, cols = w.shape
    tr = rws
    for cand in (256, 128, 64, 32, 16, 8):
        if rws % cand == 0 and cand * cols * 4 <= 2 * 1024 * 1024:
            tr = cand
            break
    c1 = 1.0 - ADAM_B1 ** ADAM_STEP
    c2 = 1.0 - ADAM_B2 ** ADAM_STEP

    def body(w_ref, g_ref, m_ref, v_ref, d_ref, nm_ref, nv_ref):
        gv = g_ref[...]
        nm = ADAM_B1 * m_ref[...] + (1.0 - ADAM_B1) * gv
        nv = ADAM_B2 * v_ref[...] + (1.0 - ADAM_B2) * (gv * gv)
        m_hat = nm / c1
        v_hat = nv / c2
        d_ref[...] = -ADAM_LR * (m_hat / (jnp.sqrt(v_hat) + ADAM_EPS) + ADAM_WD * w_ref[...])
        nm_ref[...] = nm
        nv_ref[...] = nv

    spec = pl.BlockSpec((tr, cols), lambda i: (i, 0))
    return pl.pallas_call(
        body, name=name, grid=(rws // tr,), in_specs=[spec] * 4, out_specs=[spec] * 3,
        out_shape=[jax.ShapeDtypeStruct((rws, cols), F32)] * 3,
        compiler_params=_cparams(("parallel",)),
    )(w, g, m, v)


WEIGHTS = (
    ("meta_tokens", (16, 1024), 1), ("norm_mix_g", (1024,), None), ("w_in", (1024, 7184), 1),
    ("w_alpha_up", (16, 512), 1), ("b_alpha", (512,), None), ("gla_norm_g", (1024,), None),
    ("w_gla_o", (1024, 1024), 0), ("conf_dw_w", (31, 1024), 1), ("conf_dw_b", (1024,), None),
    ("conf_ln_g", (1024,), None), ("conf_ln_b", (1024,), None), ("w_conf_o", (1024, 1024), 0),
    ("w_out", (1024, 1024), 0), ("norm_ffn_g", (1024,), None), ("w_up", (1024, 5632), 1),
    ("ffn_dw_w", (3, 2816), 1), ("ffn_dw_b", (2816,), None), ("w_down", (2816, 1024), 0),
    ("final_norm_g", (1024,), None),
)
BIG = ("w_in", "w_up", "w_down", "w_gla_o", "w_conf_o", "w_out")
SMALL_SHARDED = ("meta_tokens", "w_alpha_up", "conf_dw_w", "ffn_dw_w")
REPLICATED = tuple(n for n, _, ax in WEIGHTS if ax is None)
SHAPES = {n: s for n, s, _ in WEIGHTS}
AXIS = {n: ax for n, _, ax in WEIGHTS}
N_CHIPS = 4


def _shard_shape(name):
    s = list(SHAPES[name])
    s[AXIS[name]] //= N_CHIPS
    return tuple(s)


def _pack(parts, mult):
    flat = jnp.concatenate([p.reshape(-1) for p in parts])
    pad = (-flat.shape[0]) % mult
    return jnp.pad(flat, (0, pad))


def _unpack(flat, names, shape_of):
    out, off = {}, 0
    for n in names:
        shp = shape_of(n)
        size = math.prod(shp)
        out[n] = flat[off:off + size].reshape(shp)
        off += size
    return out


def _to_chip_major(full):
    r, cols = full.shape
    return full.reshape(r, N_CHIPS, cols // N_CHIPS).transpose(1, 0, 2)


def _from_chip_major(slabs):
    nc, r, cs = slabs.shape
    return slabs.transpose(1, 0, 2).reshape(r, nc * cs)


def kernel(x, meta_tokens, norm_mix_g, w_in, w_alpha_up, b_alpha, gla_norm_g, w_gla_o, conf_dw_w, conf_dw_b, conf_ln_g, conf_ln_b, w_conf_o, w_out, norm_ffn_g, w_up, ffn_dw_w, ffn_dw_b, w_down, final_norm_g, loss_target, m_meta_tokens, m_norm_mix_g, m_w_in, m_w_alpha_up, m_b_alpha, m_gla_norm_g, m_w_gla_o, m_conf_dw_w, m_conf_dw_b, m_conf_ln_g, m_conf_ln_b, m_w_conf_o, m_w_out, m_norm_ffn_g, m_w_up, m_ffn_dw_w, m_ffn_dw_b, m_w_down, m_final_norm_g, v_meta_tokens, v_norm_mix_g, v_w_in, v_w_alpha_up, v_b_alpha, v_gla_norm_g, v_w_gla_o, v_conf_dw_w, v_conf_dw_b, v_conf_ln_g, v_conf_ln_b, v_w_conf_o, v_w_out, v_norm_ffn_g, v_w_up, v_ffn_dw_w, v_ffn_dw_b, v_w_down, v_final_norm_g):
    names = [n for n, _, _ in WEIGHTS]
    w_args = (meta_tokens, norm_mix_g, w_in, w_alpha_up, b_alpha, gla_norm_g, w_gla_o, conf_dw_w, conf_dw_b, conf_ln_g,
              conf_ln_b, w_conf_o, w_out, norm_ffn_g, w_up, ffn_dw_w, ffn_dw_b, w_down, final_norm_g)
    m_args = (m_meta_tokens, m_norm_mix_g, m_w_in, m_w_alpha_up, m_b_alpha, m_gla_norm_g, m_w_gla_o, m_conf_dw_w,
              m_conf_dw_b, m_conf_ln_g, m_conf_ln_b, m_w_conf_o, m_w_out, m_norm_ffn_g, m_w_up, m_ffn_dw_w, m_ffn_dw_b,
              m_w_down, m_final_norm_g)
    v_args = (v_meta_tokens, v_norm_mix_g, v_w_in, v_w_alpha_up, v_b_alpha, v_gla_norm_g, v_w_gla_o, v_conf_dw_w,
              v_conf_dw_b, v_conf_ln_g, v_conf_ln_b, v_w_conf_o, v_w_out, v_norm_ffn_g, v_w_up, v_ffn_dw_w, v_ffn_dw_b,
              v_w_down, v_final_norm_g)
    in_shape = {n: a.shape for n, a in zip(names, w_args)}
    local = {n: a.reshape(_shard_shape(n) if AXIS[n] is not None else SHAPES[n]) for n, a in zip(names, w_args)}
    m_loc = {n: a.reshape(local[n].shape) for n, a in zip(names, m_args)}
    v_loc = {n: a.reshape(local[n].shape) for n, a in zip(names, v_args)}

    me = 2 * lax.axis_index("x") + lax.axis_index("y")

    w_in_shard = local["w_in"].astype(BF16)
    w_in_all = _with_own_slab(_run_hosted(_gather_plan([w_in_shard]), "gather_w_in"), [w_in_shard])[0]
    small = _pack([local[n] for n in SMALL_SHARDED], 8 * LANES).reshape(-1, LANES)
    small_all = _gather_small(small, "gather_small_weights")[::2].reshape(N_CHIPS, -1)
    per_chip_small = [_unpack(small_all[t], SMALL_SHARDED, _shard_shape) for t in range(N_CHIPS)]
    full = {n: jnp.concatenate([per_chip_small[t][n] for t in range(N_CHIPS)], axis=1) for n in SMALL_SHARDED}
    full["w_in"] = _from_chip_major(w_in_all)
    for n in REPLICATED:
        full[n] = local[n].reshape(1, -1)

    loss_part, grad_x, grads, chip_part, from_chips = _local_step(
        x[0], loss_target[0], full, {n: local[n].astype(BF16) for n in REST})

    smalls = REPLICATED + SMALL_SHARDED
    rep = _pack([grads[n] for n in smalls] + [loss_part], 8 * LANES).reshape(-1, LANES)
    rep_sum = _sum_devices(_gather_small(rep, "gather_small_grads")).reshape(-1)
    g_loc = _unpack(rep_sum, smalls, lambda n: SHAPES[n])
    loss = rep_sum[sum(math.prod(SHAPES[n]) for n in smalls)]
    for n in SMALL_SHARDED:
        width = _shard_shape(n)[1]
        g_loc[n] = lax.dynamic_slice_in_dim(g_loc[n], me * width, width, axis=1)

    place = jnp.stack([me, lax.axis_index("c")])
    reduced = [_sum_chip_partials(place, chip_part[n], from_chips[n], "sum_chip_partials_" + n) for n in BIG]
    g_loc.update(zip(BIG, _join_core_halves(reduced)))

    delta, new_m, new_v = {}, {}, {}
    for n in BIG:
        delta[n], new_m[n], new_v[n] = _adamw(local[n], g_loc[n], m_loc[n], v_loc[n], "adamw_" + n)
    rest = SMALL_SHARDED + REPLICATED
    pk = lambda dct: _pack([dct[n] for n in rest], 8 * LANES).reshape(-1, LANES)
    ds, ms, vs = _adamw(pk(local), pk(g_loc), pk(m_loc), pk(v_loc), "adamw_small")
    shape_loc = lambda n: local[n].shape
    for dct, flat in ((delta, ds), (new_m, ms), (new_v, vs)):
        dct.update(_unpack(flat.reshape(-1), rest, shape_loc))

    outs = [loss, grad_x[None]]
    for dct in (g_loc, delta, new_m, new_v):
        outs += [dct[n].reshape(in_shape[n]) for n in names]
    return tuple(outs)
```

```python
import functools
import math

import jax
import jax.numpy as jnp
from jax import lax
from jax.experimental import pallas as pl
from jax.experimental.pallas import tpu as pltpu

F32 = jnp.float32
BF16 = jnp.bfloat16

N_META = 16
PAD = 240
HEAD_ROWS = PAD + N_META
HEADS = 4
GLA_RANK = 16
RANK_PAD = 128
GATE_TAU = 16.0
CHUNK = 64
CONF_K = 31
FFN_K = 3
RMS_EPS = 1e-6
LN_EPS = 1e-5
ADAM_LR, ADAM_B1, ADAM_B2, ADAM_EPS, ADAM_WD, ADAM_STEP = 0.001, 0.9, 0.999, 1e-08, 0.01, 10

ROW_TILE = 256
HALO = 32
LANES = 128
V7X_VMEM_LIMIT = 56 * 1024 * 1024
MESH = pl.DeviceIdType.MESH


def _cparams(sem):
    return pltpu.CompilerParams(dimension_semantics=sem, vmem_limit_bytes=V7X_VMEM_LIMIT)


def _sigmoid(x):
    return 1.0 / (1.0 + jnp.exp(-x))


def _pick(n, prefs):
    for p in prefs:
        if n % p == 0:
            return p
    return n


MATMUL_TILES = {
    "proj": (2816, 1024, 1024), "gla_out": (1408, 1024, 1024), "conf_out": (1408, 1024, 1024),
    "mix_out": (1408, 1024, 1024), "ffn_up": (1408, 1408, 1024), "ffn_down": (768, 1024, 2816),
    "d_ffn_y": (1408, 1408, 1024), "d_u2": (768, 1024, 1408), "d_merged": (1408, 1024, 1024),
    "d_conf_s": (1408, 1024, 1024), "d_gla_y": (1408, 1024, 1024), "d_u1": (768, 1024, 1792),
    "dw_in": (1024, 1024, 2816), "dw_up": (1024, 1408, 1408), "dw_down": (1408, 1024, 1408),
    "dw_out": (1024, 1024, 1408), "dw_conf_o": (1024, 1024, 1408), "dw_gla_o": (1024, 1024, 1408),
}


class _Epilogue:
    def __init__(self, operands, in_specs, out_shapes, out_specs, fn, init=None):
        self.operands, self.in_specs, self.out_shapes, self.out_specs = operands, in_specs, out_shapes, out_specs
        self.fn, self.init = fn, init


def _matmul(a, b, *, dims, name, tm=None, tn=None, tk=None, out_dtype=F32, add=None, chips=None, hosted=None,
            epilogue=None):
    if chips == "b":
        nc, r, cs = b.shape
        b_shape = (r, nc * cs)
    else:
        b_shape = b.shape
    if dims == "nn":
        (m, k), (_, n) = a.shape, b_shape
    elif dims == "nt":
        (m, k), (n, _) = a.shape, b_shape
    else:
        (k, m), (_, n) = a.shape, b_shape
    want = MATMUL_TILES.get(name, (None, None, None))
    tm, tn, tk = tm or want[0], tn or want[1], tk or want[2]
    tm = tm if tm and m % tm == 0 else _pick(m, (768, 1024, 1408, 512, 256, 128))
    tn = tn if tn and n % tn == 0 else _pick(n, (1024, 1408, 512, 256, 128))
    tk = tk if tk and k % tk == 0 else _pick(k, (1024, 768, 1408, 512, 256, 128))
    if chips == "b":
        tn, tk = (cs, tk) if dims == "nn" else (tn, cs)
    if chips == "out":
        tn, tm = n // N_CHIPS, m
    nk = k // tk
    assert m % tm == 0 and n % tn == 0 and k % tk == 0, (name, m, n, k, tm, tn, tk)
    a_spec = {"nn": pl.BlockSpec((tm, tk), lambda i, j, kk: (i, kk)),
              "nt": pl.BlockSpec((tm, tk), lambda i, j, kk: (i, kk)),
              "tn": pl.BlockSpec((tk, tm), lambda i, j, kk: (kk, i))}[dims]
    if chips == "b":
        b_spec = {"nn": pl.BlockSpec((None, tk, tn), lambda i, j, kk: (j, kk, 0)),
                  "nt": pl.BlockSpec((None, tn, tk), lambda i, j, kk: (kk, j, 0))}[dims]
    else:
        b_spec = {"nn": pl.BlockSpec((tk, tn), lambda i, j, kk: (kk, j)),
                  "nt": pl.BlockSpec((tn, tk), lambda i, j, kk: (j, kk)),
                  "tn": pl.BlockSpec((tk, tn), lambda i, j, kk: (kk, j))}[dims]
    contract = {"nn": (((1,), (0,)), ((), ())), "nt": (((1,), (1,)), ((), ())), "tn": (((0,), (0,)), ((), ()))}[dims]
    if chips == "out":
        o_spec = pl.BlockSpec((None, tm, tn), lambda i, j, kk: (j, 0, 0))
        out_struct = jax.ShapeDtypeStruct((N_CHIPS, m, tn), out_dtype)
    else:
        o_spec = pl.BlockSpec((tm, tn), lambda i, j, kk: (i, j))
        out_struct = jax.ShapeDtypeStruct((m, n), out_dtype)
    has_add = add is not None

    n_hin = len(hosted.operands) if hosted is not None else 0
    n_hout = len(hosted.out_shapes) if hosted is not None else 0
    e_operands = []
    if epilogue is not None:
        e_operands = epilogue.operands(tm) if callable(epilogue.operands) else epilogue.operands
    n_ein = len(e_operands)
    n_eout = len(epilogue.out_shapes) if epilogue is not None else 0

    def body(*refs):
        refs = list(refs)
        a_ref, b_ref = refs[:2]
        add_ref = refs[2] if has_add else None
        pos = 2 + has_add
        h_ins, pos = refs[pos:pos + n_hin], pos + n_hin
        e_ins, pos = refs[pos:pos + n_ein], pos + n_ein
        o_ref, pos = refs[pos], pos + 1
        h_outs, pos = refs[pos:pos + n_hout], pos + n_hout
        e_outs, pos = refs[pos:pos + n_eout], pos + n_eout
        acc_ref = refs[pos] if nk > 1 else None
        h_sems = refs[pos + (nk > 1):]
        i, j, kk = pl.program_id(0), pl.program_id(1), pl.program_id(2)
        first = (i == 0) & (j == 0) & (kk == 0)
        if hosted is not None:
            @pl.when(first)
            def _():
                hosted.start(h_ins, h_outs, h_sems)

        if epilogue is not None and epilogue.init is not None:
            @pl.when(first)
            def _():
                epilogue.init(e_outs)

        def finish(total):
            if epilogue is None:
                o_ref[...] = total.astype(out_dtype)
            else:
                epilogue.fn(total, i, o_ref, e_ins, e_outs)

        prod = lax.dot_general(a_ref[...].astype(BF16), b_ref[...].astype(BF16), contract,
                               preferred_element_type=F32)
        if nk == 1:
            finish(prod + add_ref[...].astype(F32) if has_add else prod)
        else:
            @pl.when(kk == 0)
            def _():
                acc_ref[...] = prod + add_ref[...].astype(F32) if has_add else prod

            @pl.when((kk > 0) & (kk < nk - 1))
            def _():
                acc_ref[...] += prod

            @pl.when(kk == nk - 1)
            def _():
                finish(acc_ref[...] + prod)

        if hosted is not None:
            @pl.when((i == m // tm - 1) & (j == n // tn - 1) & (kk == nk - 1))
            def _():
                hosted.finish(h_ins, h_outs, h_sems)

    in_specs = ([a_spec, b_spec] + ([o_spec] if has_add else []) + [HBM_SPEC] * n_hin
                + (list(epilogue.in_specs(tm)) if epilogue is not None else []))
    args = ((a, b) + ((add,) if has_add else ()) + (tuple(hosted.operands) if hosted is not None else ())
            + tuple(e_operands))
    serial = hosted is not None or epilogue is not None
    outs = pl.pallas_call(
        body, name=name, grid=(m // tm, n // tn, nk),
        in_specs=in_specs,
        out_specs=[o_spec] + [HBM_SPEC] * n_hout + (list(epilogue.out_specs(tm)) if epilogue is not None else []),
        out_shape=([out_struct] + (list(hosted.out_shapes) if hosted is not None else [])
                   + (list(epilogue.out_shapes) if epilogue is not None else [])),
        scratch_shapes=([pltpu.VMEM((tm, tn), F32)] if nk > 1 else [])
        + (list(hosted.sem_shapes) if hosted is not None else []),
        compiler_params=_cparams(("arbitrary",) * 3 if serial else ("parallel", "parallel", "arbitrary")),
    )(*args)
    if not serial:
        return outs[0]
    return outs[0], list(outs[1:1 + n_hout]), list(outs[1 + n_hout:])


def _tile_rows(n):
    return lambda tm: pl.BlockSpec((tm, n), lambda i, j, kk: (i, 0))


def _one_row(n):
    return pl.BlockSpec((1, n), lambda i, j, kk: (0, 0))


def _rms_fwd_epilogue(rows, g):
    n = g.shape[1]

    def fn(total, i, o_ref, ins, outs):
        o_ref[...] = total
        r = lax.rsqrt(jnp.mean(total * total, axis=-1, keepdims=True) + RMS_EPS)
        outs[0][...] = (total * r * ins[0][...]).astype(BF16)

    return _Epilogue([g], lambda tm: [_one_row(n)], [jax.ShapeDtypeStruct((rows, n), BF16)],
                     lambda tm: [_tile_rows(n)(tm)], fn)


def _rms_bwd_epilogue(h, g, dres, low_rank=None):
    n = g.shape[1]
    operands = [h, dres, g] + (list(low_rank) if low_rank is not None else [])

    def in_specs(tm):
        specs = [_tile_rows(n)(tm), _tile_rows(n)(tm), _one_row(n)]
        if low_rank is not None:
            specs += [_tile_rows(low_rank[0].shape[1])(tm),
                      pl.BlockSpec(low_rank[1].shape, lambda i, j, kk: (0, 0))]
        return specs

    def init(outs):
        outs[0][...] = jnp.zeros_like(outs[0])

    def fn(total, i, o_ref, ins, outs):
        if low_rank is not None:
            total = total + _dot_nt(ins[3][...], ins[4][...])
        x = ins[0][...]
        r = lax.rsqrt(jnp.mean(x * x, axis=-1, keepdims=True) + RMS_EPS)
        nrm = x * r
        outs[0][...] += jnp.sum(total * nrm, axis=0, keepdims=True)
        dn = total * ins[2][...]
        o_ref[...] = ins[1][...] + r * (dn - nrm * jnp.mean(dn * nrm, axis=-1, keepdims=True))

    return _Epilogue(operands, in_specs, [jax.ShapeDtypeStruct((1, n), F32)], lambda tm: [_one_row(n)], fn, init)


def _loss_epilogue(target, gf):
    n = gf.shape[1]

    def in_specs(tm):
        nb = tm // ROW_TILE
        return [pl.BlockSpec((ROW_TILE, n), (lambda s: lambda i, j, kk: (jnp.maximum(i * nb + s - 1, 0), 0))(s))
                for s in range(nb)] + [_one_row(n)]

    def init(outs):
        outs[0][...] = jnp.zeros_like(outs[0])
        outs[1][...] = jnp.zeros_like(outs[1])

    def fn(total, i, o_ref, ins, outs):
        nb = len(ins) - 1
        g = ins[nb][...]
        for s in range(nb):
            rows = slice(s * ROW_TILE, (s + 1) * ROW_TILE)
            live = jnp.where(i * nb + s > 0, 1.0, 0.0)
            h = total[rows]
            r = lax.rsqrt(jnp.mean(h * h, axis=-1, keepdims=True) + RMS_EPS)
            nrm = h * r
            err = (nrm * g - ins[s][...]) * live
            outs[0][...] += 0.5 * jnp.sum(jnp.mean(err * err, axis=-1, keepdims=True), axis=0, keepdims=True)
            dout = err * (1.0 / n)
            outs[1][...] += jnp.sum(dout * nrm, axis=0, keepdims=True)
            dn = dout * g
            o_ref[rows, :] = r * (dn - nrm * jnp.mean(dn * nrm, axis=-1, keepdims=True))

    nb_ops = lambda tm: [target] * (tm // ROW_TILE) + [gf]
    return _Epilogue(nb_ops, in_specs, [jax.ShapeDtypeStruct((1, 1), F32), jax.ShapeDtypeStruct((1, n), F32)],
                     lambda tm: [pl.BlockSpec((1, 1), lambda i, j, kk: (0, 0)), _one_row(n)], fn, init)


def _row_mask(tile_index, rows):
    r = tile_index * rows + lax.broadcasted_iota(jnp.int32, (rows, 1), 0)
    return (r >= PAD).astype(F32)


def _prep(x, meta, g1, hosted):
    s, d = x.shape
    t = HEAD_ROWS + s
    tm = ROW_TILE
    nt = t // tm
    n_hin, n_hout = len(hosted.operands), len(hosted.out_shapes)

    def body(*refs):
        x_ref, meta_ref, g_ref = refs[:3]
        h_ins = refs[3:3 + n_hin]
        h_ref, u_ref = refs[3 + n_hin:5 + n_hin]
        h_outs = refs[5 + n_hin:5 + n_hin + n_hout]
        h_sems = refs[5 + n_hin + n_hout:]
        i = pl.program_id(0)

        @pl.when(i == 0)
        def _():
            hosted.start(h_ins, h_outs, h_sems)
            h_ref[0:PAD, :] = jnp.zeros((PAD, d), F32)
            h_ref[PAD:HEAD_ROWS, :] = meta_ref[...]

        @pl.when(i > 0)
        def _():
            h_ref[...] = x_ref[...]

        h = h_ref[...]
        r = lax.rsqrt(jnp.mean(h * h, axis=-1, keepdims=True) + RMS_EPS)
        u_ref[...] = (h * r * g_ref[...]).astype(BF16)

        @pl.when(i == nt - 1)
        def _():
            hosted.finish(h_ins, h_outs, h_sems)

    outs = pl.pallas_call(
        body, name="prep", grid=(nt,),
        in_specs=[pl.BlockSpec((tm, d), lambda i: (jnp.maximum(i - 1, 0), 0)),
                  pl.BlockSpec((N_META, d), lambda i: (0, 0)),
                  pl.BlockSpec((1, d), lambda i: (0, 0))] + [HBM_SPEC] * n_hin,
        out_specs=[pl.BlockSpec((tm, d), lambda i: (i, 0)), pl.BlockSpec((tm, d), lambda i: (i, 0))]
        + [HBM_SPEC] * n_hout,
        out_shape=[jax.ShapeDtypeStruct((t, d), F32), jax.ShapeDtypeStruct((t, d), BF16)] + list(hosted.out_shapes),
        scratch_shapes=list(hosted.sem_shapes),
        compiler_params=_cparams(("arbitrary",)),
    )(x, meta, g1, *hosted.operands)
    return outs[0], outs[1], list(outs[2:])


def _log_gate(alr, wau, b_alpha):
    t = alr.shape[0]
    dk = wau.shape[1]
    tm = ROW_TILE

    def body(alr_ref, w_ref, b_ref, o_ref):
        z = jnp.dot(alr_ref[...].astype(BF16), w_ref[...].astype(BF16), preferred_element_type=F32) + b_ref[...]
        ls = jnp.minimum(z, 0.0) - jnp.log(1.0 + jnp.exp(-jnp.abs(z)))
        o_ref[...] = ls * (1.0 / GATE_TAU) * _row_mask(pl.program_id(0), tm)

    return pl.pallas_call(
        body, name="log_gate", grid=(t // tm,),
        in_specs=[pl.BlockSpec((tm, RANK_PAD), lambda i: (i, 0)),
                  pl.BlockSpec((RANK_PAD, dk), lambda i: (0, 0)),
                  pl.BlockSpec((1, dk), lambda i: (0, 0))],
        out_specs=pl.BlockSpec((tm, dk), lambda i: (i, 0)),
        out_shape=jax.ShapeDtypeStruct((t, dk), F32),
        compiler_params=_cparams(("parallel",)),
    )(alr, wau, b_alpha)


def _tri(n, upper=False):
    r = lax.broadcasted_iota(jnp.int32, (n, n), 0)
    c = lax.broadcasted_iota(jnp.int32, (n, n), 1)
    return (r <= c) if upper else (r >= c)


_NT = (((1,), (1,)), ((), ()))
_TN = (((0,), (0,)), ((), ()))


def _dot(a, b):
    return jnp.dot(a, b, preferred_element_type=F32)


def _dot_nt(a, b):
    return lax.dot_general(a, b, _NT, preferred_element_type=F32)


def _dot_tn(a, b):
    return lax.dot_general(a, b, _TN, preferred_element_type=F32)


def _chunk_decays(la, tri_f32):
    b = jnp.dot(tri_f32, la, preferred_element_type=F32, precision=lax.Precision.HIGHEST)
    bl = b[CHUNK - 1:CHUNK, :]
    return jnp.exp(b), jnp.exp(bl - b), jnp.exp(-bl), jnp.exp(bl)


def _gla_fwd(p7, log_a, gla_g, d):
    t = p7.shape[0]
    dk_all = d // 2
    dkh, dvh = dk_all // HEADS, d // HEADS
    cb = ROW_TILE
    ncb = cb // CHUNK
    scale = dkh ** -0.5

    def body(qk_ref, v_ref, r_ref, la_ref, g_ref, o_ref, y_ref, s_ref, st_scr):
        @pl.when(pl.program_id(0) == 0)
        def _():
            st_scr[...] = jnp.zeros_like(st_scr)

        tri = _tri(CHUNK)
        tri_f = tri.astype(F32)
        for c in range(ncb):
            rows = slice(c * CHUNK, (c + 1) * CHUNK)
            eb, ekl, ebl_inv, gam = _chunk_decays(la_ref[rows, :], tri_f)
            for h in range(HEADS):
                ks = slice(h * dkh, (h + 1) * dkh)
                vs = slice(h * dvh, (h + 1) * dvh)
                q = qk_ref[rows, ks].astype(F32) * scale
                k = qk_ref[rows, dk_all + h * dkh:dk_all + (h + 1) * dkh].astype(F32)
                v = v_ref[rows, vs].astype(BF16)
                qb = q * eb[:, ks]
                kh = (k * ekl[:, ks]).astype(BF16)
                qc = (qb * ebl_inv[:, ks]).astype(BF16)
                a = jnp.where(tri, _dot_nt(qc, kh), 0.0)
                st = st_scr[h]
                st_b = st.astype(BF16)
                s_ref[c, h] = st_b
                o = _dot_nt(qb.astype(BF16), st_b) + _dot(a.astype(BF16), v)
                st_scr[h] = st * gam[:, ks] + _dot_tn(v, kh)
                o_ref[rows, vs] = o
                rr = lax.rsqrt(jnp.mean(o * o, axis=-1, keepdims=True) + RMS_EPS)
                r = r_ref[rows, vs].astype(F32)
                y_ref[rows, vs] = (o * rr * g_ref[:, vs] * (r * _sigmoid(r))).astype(BF16)

    return pl.pallas_call(
        body, name="gla_fwd", grid=(t // cb,),
        in_specs=[pl.BlockSpec((cb, d), lambda i: (i, 0)),
                  pl.BlockSpec((cb, d), lambda i: (i, 1)),
                  pl.BlockSpec((cb, d), lambda i: (i, 2)),
                  pl.BlockSpec((cb, dk_all), lambda i: (i, 0)),
                  pl.BlockSpec((1, d), lambda i: (0, 0))],
        out_specs=[pl.BlockSpec((cb, d), lambda i: (i, 0)),
                   pl.BlockSpec((cb, d), lambda i: (i, 0)),
                   pl.BlockSpec((ncb, HEADS, dvh, dkh), lambda i: (i, 0, 0, 0))],
        out_shape=[jax.ShapeDtypeStruct((t, d), F32), jax.ShapeDtypeStruct((t, d), BF16),
                   jax.ShapeDtypeStruct((t // CHUNK, HEADS, dvh, dkh), BF16)],
        scratch_shapes=[pltpu.VMEM((HEADS, dvh, dkh), F32)],
        compiler_params=_cparams(("arbitrary",)),
    )(p7, p7, p7, log_a, gla_g)


SUBLANES = 8


def _tap_phases(n_taps, first):
    phases = {}
    for j in range(n_taps):
        e = first + j
        phases.setdefault(e % SUBLANES, []).append((j, e - e % SUBLANES))
    return phases


CONV_ROWS = 64


def _shifted_windows(ext_ref, shf_ref, phases, rows, ls):
    for p, taps in phases.items():
        if p:
            span = max(off for _, off in taps) + rows
            shf_ref[p, 0:span, :] = ext_ref[p:p + span, ls]

    def window(p, start, n):
        return shf_ref[p, start:start + n, :] if p else ext_ref[start:start + n, ls]

    return window


def _conv_strip(ext_ref, shf_ref, w_ref, n_taps, first, rows, ls, emit, reverse=False):
    phases = _tap_phases(n_taps, first)
    window = _shifted_windows(ext_ref, shf_ref, phases, rows, ls)
    for r0 in range(0, rows, CONV_ROWS):
        acc = None
        for p, taps in phases.items():
            for j, off in taps:
                wj = w_ref[(n_taps - 1 - j) if reverse else j, ls]
                term = window(p, off + r0, CONV_ROWS) * wj
                acc = term if acc is None else acc + term
        emit(r0, acc)


def _corr_strip(dl_ref, ext_ref, shf_ref, acc_ref, n_taps, first, rows, ls):
    phases = _tap_phases(n_taps, first)
    window = _shifted_windows(ext_ref, shf_ref, phases, rows, ls)
    for r0 in range(0, rows, CONV_ROWS):
        dl = dl_ref[r0:r0 + CONV_ROWS, ls]
        for p, taps in phases.items():
            for j, off in taps:
                prod = dl * window(p, off + r0, CONV_ROWS)
                acc_ref[SUBLANES * j:SUBLANES * (j + 1), ls] += jnp.sum(
                    prod.reshape(CONV_ROWS // SUBLANES, SUBLANES, prod.shape[-1]), axis=0)


def _conf_fwd(p7, conv_w, conv_b, ln_g, ln_b, d):
    t = p7.shape[0]
    tm = ROW_TILE

    def body(c1_ref, c2_ref, w_ref, b_ref, g_ref, bb_ref, cc_ref, sc_ref, ext, shf):
        @pl.when(pl.program_id(0) == 0)
        def _():
            ext[0:HALO, :] = jnp.zeros((HALO, d), F32)

        ext[HALO:HALO + tm, :] = c1_ref[...].astype(F32) * _sigmoid(c2_ref[...].astype(F32))
        for s0 in range(0, d, LANES):
            ls = slice(s0, s0 + LANES)

            def emit(r0, acc, ls=ls):
                cc_ref[r0:r0 + CONV_ROWS, ls] = acc + b_ref[:, ls]

            _conv_strip(ext, shf, w_ref, CONF_K, HALO - (CONF_K - 1), tm, ls, emit)
        ext[0:HALO, :] = ext[tm:tm + HALO, :]
        g = g_ref[...]
        bb = bb_ref[...]
        rows_per_step = 2 * SUBLANES

        def rows_step(k, carry):
            rs = pl.ds(pl.multiple_of(k * rows_per_step, rows_per_step), rows_per_step)
            cc = cc_ref[rs, :]
            xc = cc - jnp.mean(cc, axis=-1, keepdims=True)
            rstd = lax.rsqrt(jnp.mean(xc * xc, axis=-1, keepdims=True) + LN_EPS)
            cn = xc * rstd * g + bb
            sc_ref[rs, :] = (cn * _sigmoid(cn)).astype(BF16)
            return carry

        lax.fori_loop(0, tm // rows_per_step, rows_step, 0, unroll=4)

    vec = pl.BlockSpec((1, d), lambda i: (0, 0))
    return pl.pallas_call(
        body, name="conf_fwd", grid=(t // tm,),
        in_specs=[pl.BlockSpec((tm, d), lambda i: (i, 3)), pl.BlockSpec((tm, d), lambda i: (i, 4)),
                  pl.BlockSpec((CONF_K, d), lambda i: (0, 0)), vec, vec, vec],
        out_specs=[pl.BlockSpec((tm, d), lambda i: (i, 0)), pl.BlockSpec((tm, d), lambda i: (i, 0))],
        out_shape=[jax.ShapeDtypeStruct((t, d), F32), jax.ShapeDtypeStruct((t, d), BF16)],
        scratch_shapes=[pltpu.VMEM((tm + HALO, d), F32), pltpu.VMEM((SUBLANES, tm + HALO, LANES), F32)],
        compiler_params=_cparams(("arbitrary",)),
    )(p7, p7, conv_w, conv_b, ln_g, ln_b)


def _merge(p7, br_gla, br_conf, d):
    t = p7.shape[0]
    tm = ROW_TILE

    def body(gg_ref, gc_ref, a_ref, b_ref, o_ref):
        o_ref[...] = (_sigmoid(gg_ref[...].astype(F32)) * a_ref[...].astype(F32)
                      + _sigmoid(gc_ref[...].astype(F32)) * b_ref[...].astype(F32)).astype(BF16)

    row = pl.BlockSpec((tm, d), lambda i: (i, 0))
    return pl.pallas_call(
        body, name="merge", grid=(t // tm,),
        in_specs=[pl.BlockSpec((tm, d), lambda i: (i, 5)), pl.BlockSpec((tm, d), lambda i: (i, 6)), row, row],
        out_specs=row, out_shape=jax.ShapeDtypeStruct((t, d), BF16),
        compiler_params=_cparams(("parallel",)),
    )(p7, p7, br_gla, br_conf)


def _ffn_mid(up, w, b, dff):
    t = up.shape[0]
    tm = ROW_TILE
    hal = 8

    def body(a_ref, bv_ref, w_ref, b_ref, y_ref, ext, shf):
        i = pl.program_id(0)

        @pl.when(i == 0)
        def _():
            ext[0:hal, :] = jnp.zeros((hal, dff), F32)

        ext[hal:hal + tm, :] = a_ref[...].astype(F32) * _row_mask(i, tm)
        for s0 in range(0, dff, LANES):
            ls = slice(s0, s0 + LANES)

            def emit(r0, acc, ls=ls):
                rb = slice(r0, r0 + CONV_ROWS)
                ac = acc + b_ref[:, ls]
                y_ref[rb, ls] = (ac * _sigmoid(ac) * bv_ref[rb, ls].astype(F32)).astype(BF16)

            _conv_strip(ext, shf, w_ref, FFN_K, hal - (FFN_K - 1), tm, ls, emit)
        ext[0:hal, :] = ext[tm:tm + hal, :]

    return pl.pallas_call(
        body, name="ffn_mid", grid=(t // tm,),
        in_specs=[pl.BlockSpec((tm, dff), lambda i: (i, 0)), pl.BlockSpec((tm, dff), lambda i: (i, 1)),
                  pl.BlockSpec((FFN_K, dff), lambda i: (0, 0)), pl.BlockSpec((1, dff), lambda i: (0, 0))],
        out_specs=pl.BlockSpec((tm, dff), lambda i: (i, 0)), out_shape=jax.ShapeDtypeStruct((t, dff), BF16),
        scratch_shapes=[pltpu.VMEM((tm + hal, dff), F32), pltpu.VMEM((SUBLANES, tm + hal, LANES), F32)],
        compiler_params=_cparams(("arbitrary",)),
    )(up, up, w, b)


def _dsilu(x, sig):
    return sig * (1.0 + x * (1.0 - sig))


def _ffn_mid_bwd(up, dy, w, b, dff):
    t = up.shape[0]
    tm = ROW_TILE
    hal = 8
    prev_rows = 2 * SUBLANES
    nt = t // tm

    def body(a_ref, ap_ref, bv_ref, dy_ref, w_ref, b_ref, dup_ref, dw_ref, db_ref, ext, dext, dw_acc, db_acc, shf):
        i = pl.program_id(0)
        tile = nt - 1 - i

        @pl.when(i == 0)
        def _():
            dext[tm:tm + hal, :] = jnp.zeros((hal, dff), F32)
            dw_acc[...] = jnp.zeros_like(dw_acc)
            db_acc[...] = jnp.zeros_like(db_acc)

        ext[0:hal, :] = ap_ref[prev_rows - hal:prev_rows, :].astype(F32) * jnp.where(tile > 0, 1.0, 0.0)
        ext[hal:hal + tm, :] = a_ref[...].astype(F32) * _row_mask(tile, tm)
        first = hal - (FFN_K - 1)
        for s0 in range(0, dff, LANES):
            ls = slice(s0, s0 + LANES)

            def emit_fwd(r0, acc, ls=ls, s0=s0):
                rb = slice(r0, r0 + CONV_ROWS)
                ac = acc + b_ref[:, ls]
                sig = _sigmoid(ac)
                dyv = dy_ref[rb, ls].astype(F32)
                dup_ref[rb, dff + s0:dff + s0 + LANES] = (dyv * ac * sig).astype(BF16)
                dac = dyv * bv_ref[rb, ls].astype(F32) * _dsilu(ac, sig)
                dext[rb, ls] = dac
                db_acc[:, ls] += jnp.sum(dac.reshape(CONV_ROWS // SUBLANES, SUBLANES, LANES), axis=0)

            _conv_strip(ext, shf, w_ref, FFN_K, first, tm, ls, emit_fwd)
            _corr_strip(dext, ext, shf, dw_acc, FFN_K, first, tm, ls)

            def emit_bwd(r0, da, ls=ls):
                rb = slice(r0, r0 + CONV_ROWS)
                mask = ((tile * tm + r0 + lax.broadcasted_iota(jnp.int32, (CONV_ROWS, 1), 0)) >= PAD).astype(F32)
                dup_ref[rb, ls] = (da * mask).astype(BF16)

            _conv_strip(dext, shf, w_ref, FFN_K, 0, tm, ls, emit_bwd, reverse=True)
        dext[tm:tm + hal, :] = dext[0:hal, :]

        @pl.when(i == nt - 1)
        def _():
            db_ref[...] = jnp.sum(db_acc[...], axis=0, keepdims=True)
            for j in range(FFN_K):
                dw_ref[j:j + 1, :] = jnp.sum(dw_acc[SUBLANES * j:SUBLANES * (j + 1), :], axis=0, keepdims=True)

    rev = lambda i: (nt - 1 - i, 0)
    return pl.pallas_call(
        body, name="ffn_mid_bwd", grid=(nt,),
        in_specs=[pl.BlockSpec((tm, dff), rev),
                  pl.BlockSpec((prev_rows, dff), lambda i: (jnp.maximum((nt - 1 - i) * (tm // prev_rows) - 1, 0), 0)),
                  pl.BlockSpec((tm, dff), lambda i: (nt - 1 - i, 1)),
                  pl.BlockSpec((tm, dff), rev),
                  pl.BlockSpec((FFN_K, dff), lambda i: (0, 0)), pl.BlockSpec((1, dff), lambda i: (0, 0))],
        out_specs=[pl.BlockSpec((tm, 2 * dff), rev),
                   pl.BlockSpec((FFN_K, dff), lambda i: (0, 0)), pl.BlockSpec((1, dff), lambda i: (0, 0))],
        out_shape=[jax.ShapeDtypeStruct((t, 2 * dff), BF16),
                   jax.ShapeDtypeStruct((FFN_K, dff), F32), jax.ShapeDtypeStruct((1, dff), F32)],
        scratch_shapes=[pltpu.VMEM((tm + hal, dff), F32), pltpu.VMEM((tm + hal, dff), F32),
                        pltpu.VMEM((SUBLANES * FFN_K, dff), F32), pltpu.VMEM((SUBLANES, dff), F32),
                        pltpu.VMEM((SUBLANES, tm + hal, LANES), F32)],
        compiler_params=_cparams(("arbitrary",)),
    )(up, up, up, dy, w, b)


def _merge_bwd(p7, br_gla, br_conf, dmerged, d):
    t = p7.shape[0]
    tm = ROW_TILE

    def body(gg_ref, gc_ref, a_ref, b_ref, dm_ref, da_ref, db_ref, dgg_ref, dgc_ref):
        dm = dm_ref[...].astype(F32)
        sg = _sigmoid(gg_ref[...].astype(F32))
        sc = _sigmoid(gc_ref[...].astype(F32))
        da_ref[...] = (dm * sg).astype(BF16)
        db_ref[...] = (dm * sc).astype(BF16)
        dgg_ref[...] = (dm * a_ref[...].astype(F32) * sg * (1.0 - sg)).astype(BF16)
        dgc_ref[...] = (dm * b_ref[...].astype(F32) * sc * (1.0 - sc)).astype(BF16)

    row = pl.BlockSpec((tm, d), lambda i: (i, 0))
    outs = pl.pallas_call(
        body, name="merge_bwd", grid=(t // tm,),
        in_specs=[pl.BlockSpec((tm, d), lambda i: (i, 5)), pl.BlockSpec((tm, d), lambda i: (i, 6)), row, row, row],
        out_specs=[row, row, row, row],
        out_shape=[jax.ShapeDtypeStruct((t, d), BF16)] * 4,
        compiler_params=_cparams(("parallel",)),
    )(p7, p7, br_gla, br_conf, dmerged)
    return outs


def _conf_bwd(p7, cc, dsc, conv_w, ln_g, ln_b, d):
    t = p7.shape[0]
    tm = ROW_TILE
    nt = t // tm

    def body(c1_ref, c2_ref, c1p_ref, c2p_ref, cc_ref, dsc_ref, w_ref, g_ref, bb_ref,
             dc1_ref, dc2_ref, dw_ref, db_ref, dg_ref, dbb_ref, ext, dext, dw_acc, shf):
        i = pl.program_id(0)
        tile = nt - 1 - i

        @pl.when(i == 0)
        def _():
            dext[tm:tm + HALO, :] = jnp.zeros((HALO, d), F32)
            dw_acc[...] = jnp.zeros_like(dw_acc)
            db_ref[...] = jnp.zeros_like(db_ref)
            dg_ref[...] = jnp.zeros_like(dg_ref)
            dbb_ref[...] = jnp.zeros_like(dbb_ref)

        ext[0:HALO, :] = (c1p_ref[...].astype(F32) * _sigmoid(c2p_ref[...].astype(F32))
                          * jnp.where(tile > 0, 1.0, 0.0))
        g = g_ref[...]
        bb = bb_ref[...]

        groups = 4
        grp = 2 * SUBLANES

        def fold(a):
            return a[:SUBLANES] + a[SUBLANES:]

        def rows_step(k, sums):
            sdg, sdbb, sdb = sums
            for u in range(groups):
                r0 = (k * groups + u) * grp
                rs = pl.ds(pl.multiple_of(r0, grp), grp)
                ext[pl.ds(pl.multiple_of(HALO + r0, grp), grp), :] = (
                    c1_ref[rs, :].astype(F32) * _sigmoid(c2_ref[rs, :].astype(F32)))
                cc_ = cc_ref[rs, :]
                xc = cc_ - jnp.mean(cc_, axis=-1, keepdims=True)
                rstd = lax.rsqrt(jnp.mean(xc * xc, axis=-1, keepdims=True) + LN_EPS)
                xh = xc * rstd
                cn = xh * g + bb
                dcn = dsc_ref[rs, :].astype(F32) * _dsilu(cn, _sigmoid(cn))
                dxh = dcn * g
                dcc = rstd * (dxh - jnp.mean(dxh, axis=-1, keepdims=True)
                              - xh * jnp.mean(dxh * xh, axis=-1, keepdims=True))
                dext[rs, :] = dcc
                sdg, sdbb, sdb = sdg + fold(dcn * xh), sdbb + fold(dcn), sdb + fold(dcc)
            return sdg, sdbb, sdb

        zero = jnp.zeros((SUBLANES, d), F32)
        sdg, sdbb, sdb = lax.fori_loop(0, tm // (groups * grp), rows_step, (zero, zero, zero))
        dg_ref[...] += jnp.sum(sdg, axis=0, keepdims=True)
        dbb_ref[...] += jnp.sum(sdbb, axis=0, keepdims=True)
        db_ref[...] += jnp.sum(sdb, axis=0, keepdims=True)
        for s0 in range(0, d, LANES):
            ls = slice(s0, s0 + LANES)
            _corr_strip(dext, ext, shf, dw_acc, CONF_K, HALO - (CONF_K - 1), tm, ls)

            def emit(r0, dc, ls=ls):
                rb = slice(r0, r0 + CONV_ROWS)
                s2l = _sigmoid(c2_ref[rb, ls].astype(F32))
                dc1_ref[rb, ls] = (dc * s2l).astype(BF16)
                dc2_ref[rb, ls] = (dc * c1_ref[rb, ls].astype(F32) * s2l * (1.0 - s2l)).astype(BF16)

            _conv_strip(dext, shf, w_ref, CONF_K, 0, tm, ls, emit, reverse=True)
        dext[tm:tm + HALO, :] = dext[0:HALO, :]

        @pl.when(i == nt - 1)
        def _():
            for j in range(CONF_K):
                dw_ref[j:j + 1, :] = jnp.sum(dw_acc[SUBLANES * j:SUBLANES * (j + 1), :], axis=0, keepdims=True)

    rev = lambda i: (nt - 1 - i, 0)
    prev = lambda col: (lambda i: (jnp.maximum((nt - 1 - i) * (tm // HALO) - 1, 0), col))
    vec = pl.BlockSpec((1, d), lambda i: (0, 0))
    return pl.pallas_call(
        body, name="conf_bwd", grid=(nt,),
        in_specs=[pl.BlockSpec((tm, d), lambda i: (nt - 1 - i, 3)), pl.BlockSpec((tm, d), lambda i: (nt - 1 - i, 4)),
                  pl.BlockSpec((HALO, d), prev(3)), pl.BlockSpec((HALO, d), prev(4)),
                  pl.BlockSpec((tm, d), rev), pl.BlockSpec((tm, d), rev),
                  pl.BlockSpec((CONF_K, d), lambda i: (0, 0)), vec, vec],
        out_specs=[pl.BlockSpec((tm, d), rev), pl.BlockSpec((tm, d), rev),
                   pl.BlockSpec((CONF_K, d), lambda i: (0, 0)), vec, vec, vec],
        out_shape=[jax.ShapeDtypeStruct((t, d), BF16), jax.ShapeDtypeStruct((t, d), BF16),
                   jax.ShapeDtypeStruct((CONF_K, d), F32)] + [jax.ShapeDtypeStruct((1, d), F32)] * 3,
        scratch_shapes=[pltpu.VMEM((tm + HALO, d), F32), pltpu.VMEM((tm + HALO, d), F32),
                        pltpu.VMEM((SUBLANES * CONF_K, d), F32), pltpu.VMEM((SUBLANES, tm + HALO, LANES), F32)],
        compiler_params=_cparams(("arbitrary",)),
    )(p7, p7, p7, p7, cc, dsc, conv_w, ln_g, ln_b)


def _gla_bwd(p7, log_a, alr, wau, b_alpha, gla_g, o, states, dy, d):
    t = p7.shape[0]
    dk_all = d // 2
    dkh, dvh = dk_all // HEADS, d // HEADS
    cb = ROW_TILE
    ncb = cb // CHUNK
    nb = t // cb
    scale = dkh ** -0.5

    def body(qk_ref, v_ref, r_ref, la_ref, alr_ref, wau_ref, ba_ref, g_ref, o_ref, s_ref, dy_ref,
             dqk_ref, dv_ref, dr_ref, dz_ref, dg_ref, dba_ref, dst_scr, dla_scr):
        i = pl.program_id(0)
        blk = nb - 1 - i

        @pl.when(i == 0)
        def _():
            dst_scr[...] = jnp.zeros_like(dst_scr)
            dg_ref[...] = jnp.zeros_like(dg_ref)
            dba_ref[...] = jnp.zeros_like(dba_ref)

        tri = _tri(CHUNK)
        tri_f = tri.astype(F32)
        triu_f = _tri(CHUNK, upper=True).astype(F32)
        for c in reversed(range(ncb)):
            rows = slice(c * CHUNK, (c + 1) * CHUNK)
            eb, ekl, ebl_inv, gam = _chunk_decays(la_ref[rows, :], tri_f)
            for h in range(HEADS):
                ks = slice(h * dkh, (h + 1) * dkh)
                kcols = slice(dk_all + h * dkh, dk_all + (h + 1) * dkh)
                vs = slice(h * dvh, (h + 1) * dvh)
                q = qk_ref[rows, ks].astype(F32) * scale
                k = qk_ref[rows, kcols].astype(F32)
                v = v_ref[rows, vs].astype(BF16)
                ebh, eklh, eih, gamh = eb[:, ks], ekl[:, ks], ebl_inv[:, ks], gam[:, ks]
                qb = q * ebh
                kh = k * eklh
                qc = qb * eih
                qb_b, kh_b, qc_b = qb.astype(BF16), kh.astype(BF16), qc.astype(BF16)
                ov = o_ref[rows, vs]
                r = r_ref[rows, vs].astype(F32)
                dyv = dy_ref[rows, vs].astype(F32)
                sig = _sigmoid(r)
                rr = lax.rsqrt(jnp.mean(ov * ov, axis=-1, keepdims=True) + RMS_EPS)
                n = ov * rr
                g = g_ref[:, vs]
                dr_ref[rows, vs] = (dyv * n * g * _dsilu(r, sig)).astype(BF16)
                don = dyv * (r * sig)
                dg_ref[:, vs] += jnp.sum(don * n, axis=0, keepdims=True)
                dn = don * g
                do = (rr * (dn - n * jnp.mean(dn * n, axis=-1, keepdims=True))).astype(BF16)
                st_b = s_ref[c, h]
                a = jnp.where(tri, _dot_nt(qc_b, kh_b), 0.0).astype(BF16)
                da = jnp.where(tri, _dot_nt(do, v), 0.0).astype(BF16)
                dst = dst_scr[h]
                dst_b = dst.astype(BF16)
                dv_ref[rows, vs] = (_dot_tn(a, do) + _dot_nt(kh_b, dst_b)).astype(BF16)
                dqb = _dot(do, st_b)
                dqc = _dot(da, kh_b)
                dkh_ = _dot_tn(da, qc_b) + _dot(v, dst_b)
                dgam = jnp.sum(st_b.astype(F32) * dst, axis=0, keepdims=True)
                dst_scr[h] = dst * gamh + _dot_tn(do, qb_b)
                dqk_ref[rows, ks] = ((dqb * ebh + dqc * (ebh * eih)) * scale).astype(BF16)
                dqk_ref[rows, kcols] = (dkh_ * eklh).astype(BF16)
                qq = dqc * qc
                kk = dkh_ * kh
                db = dqb * qb + qq - kk
                dbl = jnp.sum(kk - qq, axis=0, keepdims=True) + dgam * gamh
                dla_scr[rows, ks] = jnp.dot(triu_f, db, preferred_element_type=F32,
                                            precision=lax.Precision.HIGHEST) + dbl
        z = jnp.dot(alr_ref[...].astype(BF16), wau_ref[...].astype(BF16), preferred_element_type=F32) + ba_ref[...]
        dz = dla_scr[...] * (1.0 / GATE_TAU) * _sigmoid(-z) * _row_mask(blk, cb)
        dba_ref[...] += jnp.sum(dz, axis=0, keepdims=True)
        dz_ref[...] = dz.astype(BF16)

    rev = lambda i: (nb - 1 - i, 0)
    row = pl.BlockSpec((cb, d), rev)
    return pl.pallas_call(
        body, name="gla_bwd", grid=(nb,),
        in_specs=[row, pl.BlockSpec((cb, d), lambda i: (nb - 1 - i, 1)), pl.BlockSpec((cb, d), lambda i: (nb - 1 - i, 2)),
                  pl.BlockSpec((cb, dk_all), rev), pl.BlockSpec((cb, RANK_PAD), rev),
                  pl.BlockSpec((RANK_PAD, dk_all), lambda i: (0, 0)), pl.BlockSpec((1, dk_all), lambda i: (0, 0)),
                  pl.BlockSpec((1, d), lambda i: (0, 0)), row,
                  pl.BlockSpec((ncb, HEADS, dvh, dkh), lambda i: (nb - 1 - i, 0, 0, 0)), row],
        out_specs=[row, row, row, pl.BlockSpec((cb, dk_all), rev),
                   pl.BlockSpec((1, d), lambda i: (0, 0)), pl.BlockSpec((1, dk_all), lambda i: (0, 0))],
        out_shape=[jax.ShapeDtypeStruct((t, d), BF16)] * 3 + [jax.ShapeDtypeStruct((t, dk_all), BF16),
                   jax.ShapeDtypeStruct((1, d), F32), jax.ShapeDtypeStruct((1, dk_all), F32)],
        scratch_shapes=[pltpu.VMEM((HEADS, dvh, dkh), F32), pltpu.VMEM((cb, dk_all), F32)],
        compiler_params=_cparams(("arbitrary",)),
    )(p7, p7, p7, log_a, alr, wau, b_alpha, gla_g, o, states, dy)


REST = ("w_up", "w_down", "w_gla_o", "w_conf_o", "w_out")


def _chip_partials(grads, names, tag):
    core = lax.axis_index("c").reshape(1)
    from_sibling = _swap_core_halves(grads, "swap_core_halves_" + tag)
    return [_add_core_halves(core, g, s, "add_core_halves_" + n) for n, g, s in zip(names, grads, from_sibling)]


def _local_step(x, target, w, shard):
    s, d = x.shape
    dk_all = d // 2
    wau = jnp.pad(w["w_alpha_up"], ((0, RANK_PAD - GLA_RANK), (0, 0)))

    h0, u1, gathered = _prep(x, w["meta_tokens"], w["norm_mix_g"], _gather_plan([shard["w_in"]]))
    w_in = _from_chip_major(_with_own_slab(gathered, [shard["w_in"]])[0])
    lo, hi = 3 * d, 3 * d + GLA_RANK
    wq = jnp.concatenate([w_in[:, :lo], w_in[:, hi:]], axis=1)
    w_alr = jnp.pad(w_in[:, lo:hi], ((0, 0), (0, RANK_PAD - GLA_RANK)))
    shards = [shard[n] for n in REST]
    p7, gathered, _ = _matmul(u1, wq, dims="nn", name="proj", out_dtype=BF16, hosted=_gather_plan(shards))
    w = dict(w)
    for n, slabs in zip(REST, _with_own_slab(gathered, shards)):
        w[n] = slabs if n == "w_up" else slabs.reshape(-1, slabs.shape[-1])
    dff = w["w_down"].shape[0]
    alr = _matmul(u1, w_alr, dims="nn", name="proj_alr")
    log_a = _log_gate(alr, wau, w["b_alpha"])
    o, y_gla, states = _gla_fwd(p7, log_a, w["gla_norm_g"], d)
    br_gla = _matmul(y_gla, w["w_gla_o"], dims="nn", name="gla_out", out_dtype=BF16)
    cc, s_c = _conf_fwd(p7, w["conf_dw_w"], w["conf_dw_b"], w["conf_ln_g"], w["conf_ln_b"], d)
    br_conf = _matmul(s_c, w["w_conf_o"], dims="nn", name="conf_out", out_dtype=BF16)
    merged = _merge(p7, br_gla, br_conf, d)
    h1, _, (u2,) = _matmul(merged, w["w_out"], dims="nn", name="mix_out", add=h0,
                           epilogue=_rms_fwd_epilogue(h0.shape[0], w["norm_ffn_g"]))
    up = _matmul(u2, w["w_up"], dims="nn", name="ffn_up", out_dtype=BF16, chips="b")
    y = _ffn_mid(up, w["ffn_dw_w"], w["ffn_dw_b"], dff)
    dh2, _, (loss, d_gf) = _matmul(y, w["w_down"], dims="nn", name="ffn_down", add=h1,
                                   epilogue=_loss_epilogue(target, w["final_norm_g"]))

    g = {"final_norm_g": d_gf}
    dy = _matmul(dh2, w["w_down"], dims="nt", name="d_ffn_y", out_dtype=BF16)
    g["w_down"] = _matmul(y, dh2, dims="tn", name="dw_down", out_dtype=BF16)
    dup, g["ffn_dw_w"], g["ffn_dw_b"] = _ffn_mid_bwd(up, dy, w["ffn_dw_w"], w["ffn_dw_b"], dff)
    dh1, _, (g["norm_ffn_g"],) = _matmul(dup, w["w_up"], dims="nt", name="d_u2", chips="b",
                                         epilogue=_rms_bwd_epilogue(h1, w["norm_ffn_g"], dh2))
    g["w_up"] = _matmul(u2, dup, dims="tn", name="dw_up", out_dtype=BF16, chips="out")
    dmerged = _matmul(dh1, w["w_out"], dims="nt", name="d_merged", out_dtype=BF16)
    g["w_out"] = _matmul(merged, dh1, dims="tn", name="dw_out", out_dtype=BF16)
    d_br_gla, d_br_conf, dgg, dgc = _merge_bwd(p7, br_gla, br_conf, dmerged, d)
    dsc = _matmul(d_br_conf, w["w_conf_o"], dims="nt", name="d_conf_s", out_dtype=BF16)
    g["w_conf_o"] = _matmul(s_c, d_br_conf, dims="tn", name="dw_conf_o", out_dtype=BF16)
    dc1, dc2, g["conf_dw_w"], g["conf_dw_b"], g["conf_ln_g"], g["conf_ln_b"] = _conf_bwd(
        p7, cc, dsc, w["conf_dw_w"], w["conf_ln_g"], w["conf_ln_b"], d)
    dyg = _matmul(d_br_gla, w["w_gla_o"], dims="nt", name="d_gla_y", out_dtype=BF16)
    g["w_gla_o"] = _matmul(y_gla, d_br_gla, dims="tn", name="dw_gla_o", out_dtype=BF16)
    dqk, dv, dr, dz, g["gla_norm_g"], g["b_alpha"] = _gla_bwd(
        p7, log_a, alr, wau, w["b_alpha"], w["gla_norm_g"], o, states, dyg, d)
    dalr = _matmul(dz, wau, dims="nt", name="d_alr", out_dtype=BF16)
    g["w_alpha_up"] = _matmul(alr, dz, dims="tn", name="dw_alpha_up")[:GLA_RANK]
    dp7 = jnp.concatenate([dqk, dv, dr, dc1, dc2, dgg, dgc], axis=1)
    rest_grads = [g.pop(n) for n in REST]
    rest_grads = [a if a.ndim == 3 else a.reshape(N_CHIPS, -1, a.shape[-1]) for a in rest_grads]
    parts = dict(zip(REST, _chip_partials(rest_grads, REST, "rest")))
    dwq, arrived, _ = _matmul(u1, dp7, dims="tn", name="dw_in", out_dtype=BF16,
                              hosted=_exchange_plan([parts[n] for n in REST]))
    from_chips = dict(zip(REST, arrived))
    dw_alr = _matmul(u1, dalr, dims="tn", name="dw_in_alr", out_dtype=BF16)
    dw_in = _to_chip_major(jnp.concatenate([dwq[:, :lo], dw_alr[:, :GLA_RANK], dwq[:, lo:]], axis=1))
    parts["w_in"] = _chip_partials([dw_in], ["w_in"], "w_in")[0]
    dh0, arrived, (g["norm_mix_g"],) = _matmul(
        dp7, wq, dims="nt", name="d_u1", hosted=_exchange_plan([parts["w_in"]]),
        epilogue=_rms_bwd_epilogue(h0, w["norm_mix_g"], dh1, low_rank=(dalr, w_alr)))
    from_chips["w_in"] = arrived[0]
    g["meta_tokens"] = dh0[PAD:HEAD_ROWS]
    return loss, dh0[HEAD_ROWS:], g, parts, from_chips


HBM_SPEC = pl.BlockSpec(memory_space=pltpu.HBM)
FLIPS = ((1, 0), (0, 1), (1, 1))


def _place():
    x, y, c = lax.axis_index("x"), lax.axis_index("y"), lax.axis_index("c")
    return x, y, c


def _half_rows(ref, h, lead=()):
    rh = ref.shape[-2] // 2
    return ref.at[(*lead, pl.ds(pl.multiple_of(h * rh, 2 * SUBLANES), rh), slice(None))]


class _Hosted:
    def __init__(self, operands, out_shapes, sem_shapes, start, finish):
        self.operands, self.out_shapes, self.sem_shapes = operands, out_shapes, sem_shapes
        self.start, self.finish = start, finish


def _gather_plan(shards):
    n = len(shards)

    def copies(ins, outs, sems, kinds):
        send_sems, recv_sems = sems
        x, y, c = _place()
        chips = [(x ^ fx, y ^ fy) for fx, fy in FLIPS]
        me, sibling = (x, y, c), (x, y, 1 - c)

        def copy(k, sem, chip, h, to, src=None):
            slot = _half_rows(outs[k], h, lead=(2 * chip[0] + chip[1],))
            return pltpu.make_async_remote_copy(src_ref=slot if src is None else src, dst_ref=slot,
                                                send_sem=send_sems.at[sem], recv_sem=recv_sems.at[sem],
                                                device_id=to, device_id_type=MESH)

        make = {
            "first": lambda k, j, chip: copy(k, 3 * k + j, (x, y), c, (*chip, c), src=_half_rows(ins[k], c)),
            "landed": lambda k, j, chip: copy(k, 3 * k + j, chip, c, me),
            "passed": lambda k, j, chip: copy(k, 3 * n + 3 * k + j, chip, c, sibling),
            "from_sibling": lambda k, j, chip: copy(k, 3 * n + 3 * k + j, chip, 1 - c, me),
        }
        return [[make[kind](k, j, chip) for k in range(n) for j, chip in enumerate(chips)] for kind in kinds]

    def start(ins, outs, sems):
        for cp in copies(ins, outs, sems, ["first"])[0]:
            cp.start()

    def finish(ins, outs, sems):
        first, landed, passed, from_sibling = copies(ins, outs, sems, ["first", "landed", "passed", "from_sibling"])
        for arrived, fwd in zip(landed, passed):
            arrived.wait_recv()
            fwd.start()
        for cp in from_sibling:
            cp.wait_recv()
        for cp in first + passed:
            cp.wait_send()

    return _Hosted(list(shards), [jax.ShapeDtypeStruct((N_CHIPS, *s.shape), s.dtype) for s in shards],
                   [pltpu.SemaphoreType.DMA((6 * n,)), pltpu.SemaphoreType.DMA((6 * n,))], start, finish)


def _with_own_slab(gathered, shards):
    me = 2 * lax.axis_index("x") + lax.axis_index("y")
    return [lax.dynamic_update_index_in_dim(o, s, me, 0) for o, s in zip(gathered, shards)]


def _swap_core_halves(grads, name):
    n = len(grads)

    def body(*refs):
        ins, outs = refs[:n], refs[n:2 * n]
        send_sems, recv_sems = refs[2 * n:]
        x, y, c = _place()
        cps = [pltpu.make_async_remote_copy(
            src_ref=_half_rows(ins[k], 1 - c, lead=(slice(None),)), dst_ref=outs[k], send_sem=send_sems.at[k],
            recv_sem=recv_sems.at[k], device_id=(x, y, 1 - c), device_id_type=MESH) for k in range(n)]
        for cp in cps:
            cp.start()
        for cp in cps:
            cp.wait()

    return pl.pallas_call(
        body, name=name, in_specs=[HBM_SPEC] * n, out_specs=[HBM_SPEC] * n,
        out_shape=[jax.ShapeDtypeStruct((g.shape[0], g.shape[1] // 2, g.shape[2]), g.dtype) for g in grads],
        scratch_shapes=[pltpu.SemaphoreType.DMA((n,)), pltpu.SemaphoreType.DMA((n,))],
    )(*grads)


def _exchange_plan(parts):
    n = len(parts)

    def copies(ins, outs, sems):
        send_sems, recv_sems = sems
        x, y, c = _place()
        cps = []
        for k in range(n):
            for j, (fx, fy) in enumerate(FLIPS):
                tx, ty = x ^ fx, y ^ fy
                cps.append(pltpu.make_async_remote_copy(
                    src_ref=ins[k].at[2 * tx + ty], dst_ref=outs[k].at[j], send_sem=send_sems.at[3 * k + j],
                    recv_sem=recv_sems.at[3 * k + j], device_id=(tx, ty, c), device_id_type=MESH))
        return cps

    def start(ins, outs, sems):
        for cp in copies(ins, outs, sems):
            cp.start()

    def finish(ins, outs, sems):
        for cp in copies(ins, outs, sems):
            cp.wait()

    return _Hosted(list(parts), [jax.ShapeDtypeStruct((3, *p.shape[1:]), p.dtype) for p in parts],
                   [pltpu.SemaphoreType.DMA((3 * n,)), pltpu.SemaphoreType.DMA((3 * n,))], start, finish)


def _join_core_halves(fulls):
    n = len(fulls)

    def body(*refs):
        bufs = refs[n:2 * n]
        send_sems, recv_sems = refs[2 * n:]
        x, y, c = _place()
        cps = [pltpu.make_async_remote_copy(
            src_ref=_half_rows(bufs[k], c), dst_ref=_half_rows(bufs[k], c), send_sem=send_sems.at[k],
            recv_sem=recv_sems.at[k], device_id=(x, y, 1 - c), device_id_type=MESH) for k in range(n)]
        for cp in cps:
            cp.start()
        for k in range(n):
            cps[k].wait_send()
            pltpu.make_async_remote_copy(
                src_ref=_half_rows(bufs[k], c), dst_ref=_half_rows(bufs[k], 1 - c), send_sem=send_sems.at[k],
                recv_sem=recv_sems.at[k], device_id=(x, y, 1 - c), device_id_type=MESH).wait_recv()

    return pl.pallas_call(
        body, name="join_core_halves", in_specs=[HBM_SPEC] * n, out_specs=[HBM_SPEC] * n,
        out_shape=[jax.ShapeDtypeStruct(f.shape, f.dtype) for f in fulls],
        input_output_aliases={k: k for k in range(n)},
        scratch_shapes=[pltpu.SemaphoreType.DMA((n,)), pltpu.SemaphoreType.DMA((n,))],
    )(*fulls)


def _gather_small(block, name):
    m, n = block.shape

    def body(x_ref, out_ref, send_sems, recv_sems, local_sem):
        x, y, c = _place()
        me, sibling = (x, y, c), (x, y, 1 - c)
        chips = [(x ^ fx, y ^ fy) for fx, fy in FLIPS]

        def rows(px, py, pc):
            return out_ref.at[pl.ds((4 * px + 2 * py + pc) * m, m), :]

        def copy(k, blk, to, src=None):
            return pltpu.make_async_remote_copy(
                src_ref=rows(*blk) if src is None else src, dst_ref=rows(*blk),
                send_sem=send_sems.at[k], recv_sem=recv_sems.at[k], device_id=to, device_id_type=MESH)

        mine = pltpu.make_async_copy(x_ref, rows(*me), local_sem)
        mine.start()
        first = [copy(0, me, sibling, src=x_ref)]
        first += [copy(1 + j, me, (*chip, c), src=x_ref) for j, chip in enumerate(chips)]
        for cp in first:
            cp.start()
        passed = [copy(4 + j, (*chip, c), sibling) for j, chip in enumerate(chips)]
        for j, chip in enumerate(chips):
            copy(1 + j, (*chip, c), me).wait_recv()
            passed[j].start()
        copy(0, sibling, me).wait_recv()
        for j, chip in enumerate(chips):
            copy(4 + j, (*chip, 1 - c), me).wait_recv()
        for cp in first + passed:
            cp.wait_send()
        mine.wait()

    out = pl.pallas_call(
        body, name=name,
        out_shape=jax.ShapeDtypeStruct((8 * m, n), block.dtype),
        in_specs=[pl.BlockSpec(memory_space=pltpu.VMEM)],
        out_specs=pl.BlockSpec(memory_space=pltpu.VMEM),
        scratch_shapes=[pltpu.SemaphoreType.DMA((7,)), pltpu.SemaphoreType.DMA((7,)), pltpu.SemaphoreType.DMA],
    )(block)
    return out.reshape(8, m, n)


def _add_core_halves(core, grad, from_sibling, name):
    nc, r, cols = grad.shape
    rh = r // 2

    def body(core_ref, g_ref, s_ref, o_ref):
        o_ref[...] = (g_ref[...].astype(F32) + s_ref[...].astype(F32)).astype(BF16)

    spec = pl.BlockSpec((1, rh, cols), lambda t, core_ref: (t, 0, 0))
    return pl.pallas_call(
        body, name=name,
        grid_spec=pltpu.PrefetchScalarGridSpec(
            num_scalar_prefetch=1, grid=(nc,),
            in_specs=[pl.BlockSpec((1, rh, cols), lambda t, core_ref: (t, core_ref[0], 0)), spec], out_specs=spec),
        out_shape=jax.ShapeDtypeStruct((nc, rh, cols), BF16),
        compiler_params=_cparams(("parallel",)),
    )(core, grad, from_sibling)


def _sum_chip_partials(place, parts, others, name):
    _, rh, cols = parts.shape
    tr = _pick(rh, (128, 176, 64, 32, 16, 8))
    nb = rh // tr

    def body(place_ref, a_ref, b_ref, o_ref):
        acc = a_ref[0].astype(F32)
        for j in range(3):
            acc = acc + b_ref[j].astype(F32)
        o_ref[...] = acc

    return pl.pallas_call(
        body, name=name,
        grid_spec=pltpu.PrefetchScalarGridSpec(
            num_scalar_prefetch=1, grid=(nb,),
            in_specs=[pl.BlockSpec((1, tr, cols), lambda i, place_ref: (place_ref[0], i, 0)),
                      pl.BlockSpec((3, tr, cols), lambda i, place_ref: (0, i, 0))],
            out_specs=pl.BlockSpec((tr, cols), lambda i, place_ref: (place_ref[1] * nb + i, 0))),
        out_shape=jax.ShapeDtypeStruct((2 * rh, cols), F32),
        compiler_params=_cparams(("parallel",)),
    )(place, parts, others)


def _sum_devices(blocks):
    n, m, _ = blocks.shape

    def body(b_ref, o_ref):
        acc = b_ref[0]
        for j in range(1, n):
            acc = acc + b_ref[j]
        o_ref[...] = acc

    return pl.pallas_call(
        body, name="sum_devices", out_shape=jax.ShapeDtypeStruct((m, LANES), F32),
        in_specs=[pl.BlockSpec(memory_space=pltpu.VMEM)], out_specs=pl.BlockSpec(memory_space=pltpu.VMEM),
    )(blocks)


def _adamw(w, g, m, v, name):
    rws, cols = w.shape
    tr = rws
    for cand in (256, 128, 64, 32, 16, 8):
        if rws % cand == 0 and cand * cols * 4 <= 2 * 1024 * 1024:
            tr = cand
            break
    c1 = 1.0 - ADAM_B1 ** ADAM_STEP
    c2 = 1.0 - ADAM_B2 ** ADAM_STEP

    def body(w_ref, g_ref, m_ref, v_ref, d_ref, nm_ref, nv_ref):
        gv = g_ref[...]
        nm = ADAM_B1 * m_ref[...] + (1.0 - ADAM_B1) * gv
        nv = ADAM_B2 * v_ref[...] + (1.0 - ADAM_B2) * (gv * gv)
        m_hat = nm / c1
        v_hat = nv / c2
        d_ref[...] = -ADAM_LR * (m_hat / (jnp.sqrt(v_hat) + ADAM_EPS) + ADAM_WD * w_ref[...])
        nm_ref[...] = nm
        nv_ref[...] = nv

    spec = pl.BlockSpec((tr, cols), lambda i: (i, 0))
    return pl.pallas_call(
        body, name=name, grid=(rws // tr,), in_specs=[spec] * 4, out_specs=[spec] * 3,
        out_shape=[jax.ShapeDtypeStruct((rws, cols), F32)] * 3,
        compiler_params=_cparams(("parallel",)),
    )(w, g, m, v)


WEIGHTS = (
    ("meta_tokens", (16, 1024), 1), ("norm_mix_g", (1024,), None), ("w_in", (1024, 7184), 1),
    ("w_alpha_up", (16, 512), 1), ("b_alpha", (512,), None), ("gla_norm_g", (1024,), None),
    ("w_gla_o", (1024, 1024), 0), ("conf_dw_w", (31, 1024), 1), ("conf_dw_b", (1024,), None),
    ("conf_ln_g", (1024,), None), ("conf_ln_b", (1024,), None), ("w_conf_o", (1024, 1024), 0),
    ("w_out", (1024, 1024), 0), ("norm_ffn_g", (1024,), None), ("w_up", (1024, 5632), 1),
    ("ffn_dw_w", (3, 2816), 1), ("ffn_dw_b", (2816,), None), ("w_down", (2816, 1024), 0),
    ("final_norm_g", (1024,), None),
)
BIG = ("w_in", "w_up", "w_down", "w_gla_o", "w_conf_o", "w_out")
SMALL_SHARDED = ("meta_tokens", "w_alpha_up", "conf_dw_w", "ffn_dw_w")
REPLICATED = tuple(n for n, _, ax in WEIGHTS if ax is None)
SHAPES = {n: s for n, s, _ in WEIGHTS}
AXIS = {n: ax for n, _, ax in WEIGHTS}
N_CHIPS = 4


def _shard_shape(name):
    s = list(SHAPES[name])
    s[AXIS[name]] //= N_CHIPS
    return tuple(s)


def _pack(parts, mult):
    flat = jnp.concatenate([p.reshape(-1) for p in parts])
    pad = (-flat.shape[0]) % mult
    return jnp.pad(flat, (0, pad))


def _unpack(flat, names, shape_of):
    out, off = {}, 0
    for n in names:
        shp = shape_of(n)
        size = math.prod(shp)
        out[n] = flat[off:off + size].reshape(shp)
        off += size
    return out


def _to_chip_major(full):
    r, cols = full.shape
    return full.reshape(r, N_CHIPS, cols // N_CHIPS).transpose(1, 0, 2)


def _from_chip_major(slabs):
    nc, r, cs = slabs.shape
    return slabs.transpose(1, 0, 2).reshape(r, nc * cs)


def kernel(x, meta_tokens, norm_mix_g, w_in, w_alpha_up, b_alpha, gla_norm_g, w_gla_o, conf_dw_w, conf_dw_b, conf_ln_g, conf_ln_b, w_conf_o, w_out, norm_ffn_g, w_up, ffn_dw_w, ffn_dw_b, w_down, final_norm_g, loss_target, m_meta_tokens, m_norm_mix_g, m_w_in, m_w_alpha_up, m_b_alpha, m_gla_norm_g, m_w_gla_o, m_conf_dw_w, m_conf_dw_b, m_conf_ln_g, m_conf_ln_b, m_w_conf_o, m_w_out, m_norm_ffn_g, m_w_up, m_ffn_dw_w, m_ffn_dw_b, m_w_down, m_final_norm_g, v_meta_tokens, v_norm_mix_g, v_w_in, v_w_alpha_up, v_b_alpha, v_gla_norm_g, v_w_gla_o, v_conf_dw_w, v_conf_dw_b, v_conf_ln_g, v_conf_ln_b, v_w_conf_o, v_w_out, v_norm_ffn_g, v_w_up, v_ffn_dw_w, v_ffn_dw_b, v_w_down, v_final_norm_g):
    names = [n for n, _, _ in WEIGHTS]
    w_args = (meta_tokens, norm_mix_g, w_in, w_alpha_up, b_alpha, gla_norm_g, w_gla_o, conf_dw_w, conf_dw_b, conf_ln_g,
              conf_ln_b, w_conf_o, w_out, norm_ffn_g, w_up, ffn_dw_w, ffn_dw_b, w_down, final_norm_g)
    m_args = (m_meta_tokens, m_norm_mix_g, m_w_in, m_w_alpha_up, m_b_alpha, m_gla_norm_g, m_w_gla_o, m_conf_dw_w,
              m_conf_dw_b, m_conf_ln_g, m_conf_ln_b, m_w_conf_o, m_w_out, m_norm_ffn_g, m_w_up, m_ffn_dw_w, m_ffn_dw_b,
              m_w_down, m_final_norm_g)
    v_args = (v_meta_tokens, v_norm_mix_g, v_w_in, v_w_alpha_up, v_b_alpha, v_gla_norm_g, v_w_gla_o, v_conf_dw_w,
              v_conf_dw_b, v_conf_ln_g, v_conf_ln_b, v_w_conf_o, v_w_out, v_norm_ffn_g, v_w_up, v_ffn_dw_w, v_ffn_dw_b,
              v_w_down, v_final_norm_g)
    in_shape = {n: a.shape for n, a in zip(names, w_args)}
    local = {n: a.reshape(_shard_shape(n) if AXIS[n] is not None else SHAPES[n]) for n, a in zip(names, w_args)}
    m_loc = {n: a.reshape(local[n].shape) for n, a in zip(names, m_args)}
    v_loc = {n: a.reshape(local[n].shape) for n, a in zip(names, v_args)}

    me = 2 * lax.axis_index("x") + lax.axis_index("y")

    small = _pack([local[n] for n in SMALL_SHARDED], 8 * LANES).reshape(-1, LANES)
    small_all = _gather_small(small, "gather_small_weights")[::2].reshape(N_CHIPS, -1)
    per_chip_small = [_unpack(small_all[t], SMALL_SHARDED, _shard_shape) for t in range(N_CHIPS)]
    full = {n: jnp.concatenate([per_chip_small[t][n] for t in range(N_CHIPS)], axis=1) for n in SMALL_SHARDED}
    for n in REPLICATED:
        full[n] = local[n].reshape(1, -1)

    loss_part, grad_x, grads, chip_part, from_chips = _local_step(
        x[0], loss_target[0], full, {n: local[n].astype(BF16) for n in BIG})

    smalls = REPLICATED + SMALL_SHARDED
    rep = _pack([grads[n] for n in smalls] + [loss_part], 8 * LANES).reshape(-1, LANES)
    rep_sum = _sum_devices(_gather_small(rep, "gather_small_grads")).reshape(-1)
    g_loc = _unpack(rep_sum, smalls, lambda n: SHAPES[n])
    loss = rep_sum[sum(math.prod(SHAPES[n]) for n in smalls)]
    for n in SMALL_SHARDED:
        width = _shard_shape(n)[1]
        g_loc[n] = lax.dynamic_slice_in_dim(g_loc[n], me * width, width, axis=1)

    place = jnp.stack([me, lax.axis_index("c")])
    reduced = [_sum_chip_partials(place, chip_part[n], from_chips[n], "sum_chip_partials_" + n) for n in BIG]
    g_loc.update(zip(BIG, _join_core_halves(reduced)))

    delta, new_m, new_v = {}, {}, {}
    for n in BIG:
        delta[n], new_m[n], new_v[n] = _adamw(local[n], g_loc[n], m_loc[n], v_loc[n], "adamw_" + n)
    rest = SMALL_SHARDED + REPLICATED
    pk = lambda dct: _pack([dct[n] for n in rest], 8 * LANES).reshape(-1, LANES)
    ds, ms, vs = _adamw(pk(local), pk(g_loc), pk(m_loc), pk(v_loc), "adamw_small")
    shape_loc = lambda n: local[n].shape
    for dct, flat in ((delta, ds), (new_m, ms), (new_v, vs)):
        dct.update(_unpack(flat.reshape(-1), rest, shape_loc))

    outs = [loss, grad_x[None]]
    for dct in (g_loc, delta, new_m, new_v):
        outs += [dct[n].reshape(in_shape[n]) for n in names]
    return tuple(outs)
```

```python
import functools
import math

import jax
import jax.numpy as jnp
from jax import lax
from jax.experimental import pallas as pl
from jax.experimental.pallas import tpu as pltpu

F32 = jnp.float32
BF16 = jnp.bfloat16

N_META = 16
PAD = 240
HEAD_ROWS = PAD + N_META
HEADS = 4
GLA_RANK = 16
RANK_PAD = 128
GATE_TAU = 16.0
CHUNK = 64
CONF_K = 31
FFN_K = 3
RMS_EPS = 1e-6
LN_EPS = 1e-5
ADAM_LR, ADAM_B1, ADAM_B2, ADAM_EPS, ADAM_WD, ADAM_STEP = 0.001, 0.9, 0.999, 1e-08, 0.01, 10

ROW_TILE = 256
HALO = 32
LANES = 128
V7X_VMEM_LIMIT = 56 * 1024 * 1024
MESH = pl.DeviceIdType.MESH


def _cparams(sem):
    return pltpu.CompilerParams(dimension_semantics=sem, vmem_limit_bytes=V7X_VMEM_LIMIT)


def _sigmoid(x):
    return 1.0 / (1.0 + jnp.exp(-x))


def _pick(n, prefs):
    for p in prefs:
        if n % p == 0:
            return p
    return n


MATMUL_TILES = {
    "proj": (2816, 1024, 1024), "gla_out": (1408, 1024, 1024), "conf_out": (1408, 1024, 1024),
    "mix_out": (1408, 1024, 1024), "ffn_up": (1408, 1408, 1024), "ffn_down": (768, 1024, 2816),
    "d_ffn_y": (1408, 1408, 1024), "d_u2": (768, 1024, 1408), "d_merged": (1408, 1024, 1024),
    "d_conf_s": (1408, 1024, 1024), "d_gla_y": (1408, 1024, 1024), "d_u1": (768, 1024, 2048),
    "dw_in": (1024, 1024, 2816), "dw_up": (1024, 1408, 1408), "dw_down": (1408, 1024, 1408),
    "dw_out": (1024, 1024, 1408), "dw_conf_o": (1024, 1024, 1408), "dw_gla_o": (1024, 1024, 1408),
}


class _Epilogue:
    def __init__(self, operands, in_specs, out_shapes, out_specs, fn, init=None):
        self.operands, self.in_specs, self.out_shapes, self.out_specs = operands, in_specs, out_shapes, out_specs
        self.fn, self.init = fn, init


def _matmul(a, b, *, dims, name, tm=None, tn=None, tk=None, out_dtype=F32, add=None, chips=None, hosted=None,
            epilogue=None):
    if chips == "b":
        nc, r, cs = b.shape
        b_shape = (r, nc * cs)
    else:
        b_shape = b.shape
    if dims == "nn":
        (m, k), (_, n) = a.shape, b_shape
    elif dims == "nt":
        (m, k), (n, _) = a.shape, b_shape
    else:
        (k, m), (_, n) = a.shape, b_shape
    want = MATMUL_TILES.get(name, (None, None, None))
    tm, tn, tk = tm or want[0], tn or want[1], tk or want[2]
    tm = tm if tm and m % tm == 0 else _pick(m, (768, 1024, 1408, 512, 256, 128))
    tn = tn if tn and n % tn == 0 else _pick(n, (1024, 1408, 512, 256, 128))
    tk = tk if tk and k % tk == 0 else _pick(k, (1024, 768, 1408, 512, 256, 128))
    if chips == "b":
        tn, tk = (cs, tk) if dims == "nn" else (tn, cs)
    if chips == "out":
        tn, tm = n // N_CHIPS, m
    nk = k // tk
    assert m % tm == 0 and n % tn == 0 and k % tk == 0, (name, m, n, k, tm, tn, tk)
    a_spec = {"nn": pl.BlockSpec((tm, tk), lambda i, j, kk: (i, kk)),
              "nt": pl.BlockSpec((tm, tk), lambda i, j, kk: (i, kk)),
              "tn": pl.BlockSpec((tk, tm), lambda i, j, kk: (kk, i))}[dims]
    if chips == "b":
        b_spec = {"nn": pl.BlockSpec((None, tk, tn), lambda i, j, kk: (j, kk, 0)),
                  "nt": pl.BlockSpec((None, tn, tk), lambda i, j, kk: (kk, j, 0))}[dims]
    else:
        b_spec = {"nn": pl.BlockSpec((tk, tn), lambda i, j, kk: (kk, j)),
                  "nt": pl.BlockSpec((tn, tk), lambda i, j, kk: (j, kk)),
                  "tn": pl.BlockSpec((tk, tn), lambda i, j, kk: (kk, j))}[dims]
    contract = {"nn": (((1,), (0,)), ((), ())), "nt": (((1,), (1,)), ((), ())), "tn": (((0,), (0,)), ((), ()))}[dims]
    if chips == "out":
        o_spec = pl.BlockSpec((None, tm, tn), lambda i, j, kk: (j, 0, 0))
        out_struct = jax.ShapeDtypeStruct((N_CHIPS, m, tn), out_dtype)
    else:
        o_spec = pl.BlockSpec((tm, tn), lambda i, j, kk: (i, j))
        out_struct = jax.ShapeDtypeStruct((m, n), out_dtype)
    has_add = add is not None

    n_hin = len(hosted.operands) if hosted is not None else 0
    n_hout = len(hosted.out_shapes) if hosted is not None else 0
    e_operands = []
    if epilogue is not None:
        e_operands = epilogue.operands(tm) if callable(epilogue.operands) else epilogue.operands
    n_ein = len(e_operands)
    n_eout = len(epilogue.out_shapes) if epilogue is not None else 0

    def body(*refs):
        refs = list(refs)
        a_ref, b_ref = refs[:2]
        add_ref = refs[2] if has_add else None
        pos = 2 + has_add
        h_ins, pos = refs[pos:pos + n_hin], pos + n_hin
        e_ins, pos = refs[pos:pos + n_ein], pos + n_ein
        o_ref, pos = refs[pos], pos + 1
        h_outs, pos = refs[pos:pos + n_hout], pos + n_hout
        e_outs, pos = refs[pos:pos + n_eout], pos + n_eout
        acc_ref = refs[pos] if nk > 1 else None
        h_sems = refs[pos + (nk > 1):]
        i, j, kk = pl.program_id(0), pl.program_id(1), pl.program_id(2)
        first = (i == 0) & (j == 0) & (kk == 0)
        if hosted is not None:
            @pl.when(first)
            def _():
                hosted.start(h_ins, h_outs, h_sems)

        if epilogue is not None and epilogue.init is not None:
            @pl.when(first)
            def _():
                epilogue.init(e_outs)

        def finish(total):
            if epilogue is None:
                o_ref[...] = total.astype(out_dtype)
            else:
                epilogue.fn(total, i, o_ref, e_ins, e_outs)

        prod = lax.dot_general(a_ref[...].astype(BF16), b_ref[...].astype(BF16), contract,
                               preferred_element_type=F32)
        if nk == 1:
            finish(prod + add_ref[...].astype(F32) if has_add else prod)
        else:
            @pl.when(kk == 0)
            def _():
                acc_ref[...] = prod + add_ref[...].astype(F32) if has_add else prod

            @pl.when((kk > 0) & (kk < nk - 1))
            def _():
                acc_ref[...] += prod

            @pl.when(kk == nk - 1)
            def _():
                finish(acc_ref[...] + prod)

        if hosted is not None:
            @pl.when((i == m // tm - 1) & (j == n // tn - 1) & (kk == nk - 1))
            def _():
                hosted.finish(h_ins, h_outs, h_sems)

    in_specs = ([a_spec, b_spec] + ([o_spec] if has_add else []) + [HBM_SPEC] * n_hin
                + (list(epilogue.in_specs(tm)) if epilogue is not None else []))
    args = ((a, b) + ((add,) if has_add else ()) + (tuple(hosted.operands) if hosted is not None else ())
            + tuple(e_operands))
    serial = hosted is not None or epilogue is not None
    outs = pl.pallas_call(
        body, name=name, grid=(m // tm, n // tn, nk),
        in_specs=in_specs,
        out_specs=[o_spec] + [HBM_SPEC] * n_hout + (list(epilogue.out_specs(tm)) if epilogue is not None else []),
        out_shape=([out_struct] + (list(hosted.out_shapes) if hosted is not None else [])
                   + (list(epilogue.out_shapes) if epilogue is not None else [])),
        scratch_shapes=([pltpu.VMEM((tm, tn), F32)] if nk > 1 else [])
        + (list(hosted.sem_shapes) if hosted is not None else []),
        compiler_params=_cparams(("arbitrary",) * 3 if serial else ("parallel", "parallel", "arbitrary")),
    )(*args)
    if not serial:
        return outs[0]
    return outs[0], list(outs[1:1 + n_hout]), list(outs[1 + n_hout:])


def _tile_rows(n):
    return lambda tm: pl.BlockSpec((tm, n), lambda i, j, kk: (i, 0))


def _one_row(n):
    return pl.BlockSpec((1, n), lambda i, j, kk: (0, 0))


def _rms_fwd_epilogue(rows, g):
    n = g.shape[1]

    def fn(total, i, o_ref, ins, outs):
        o_ref[...] = total
        r = lax.rsqrt(jnp.mean(total * total, axis=-1, keepdims=True) + RMS_EPS)
        outs[0][...] = (total * r * ins[0][...]).astype(BF16)

    return _Epilogue([g], lambda tm: [_one_row(n)], [jax.ShapeDtypeStruct((rows, n), BF16)],
                     lambda tm: [_tile_rows(n)(tm)], fn)


def _rms_bwd_epilogue(h, g, dres, low_rank=None):
    n = g.shape[1]
    operands = [h, dres, g] + (list(low_rank) if low_rank is not None else [])

    def in_specs(tm):
        specs = [_tile_rows(n)(tm), _tile_rows(n)(tm), _one_row(n)]
        if low_rank is not None:
            specs += [_tile_rows(low_rank[0].shape[1])(tm),
                      pl.BlockSpec(low_rank[1].shape, lambda i, j, kk: (0, 0))]
        return specs

    def init(outs):
        outs[0][...] = jnp.zeros_like(outs[0])

    def fn(total, i, o_ref, ins, outs):
        if low_rank is not None:
            total = total + _dot_nt(ins[3][...], ins[4][...])
        x = ins[0][...]
        r = lax.rsqrt(jnp.mean(x * x, axis=-1, keepdims=True) + RMS_EPS)
        nrm = x * r
        outs[0][...] += jnp.sum(total * nrm, axis=0, keepdims=True)
        dn = total * ins[2][...]
        o_ref[...] = ins[1][...] + r * (dn - nrm * jnp.mean(dn * nrm, axis=-1, keepdims=True))

    return _Epilogue(operands, in_specs, [jax.ShapeDtypeStruct((1, n), F32)], lambda tm: [_one_row(n)], fn, init)


def _loss_epilogue(target, gf):
    n = gf.shape[1]

    def in_specs(tm):
        nb = tm // ROW_TILE
        return [pl.BlockSpec((ROW_TILE, n), (lambda s: lambda i, j, kk: (jnp.maximum(i * nb + s - 1, 0), 0))(s))
                for s in range(nb)] + [_one_row(n)]

    def init(outs):
        outs[0][...] = jnp.zeros_like(outs[0])
        outs[1][...] = jnp.zeros_like(outs[1])

    def fn(total, i, o_ref, ins, outs):
        nb = len(ins) - 1
        g = ins[nb][...]
        for s in range(nb):
            rows = slice(s * ROW_TILE, (s + 1) * ROW_TILE)
            live = jnp.where(i * nb + s > 0, 1.0, 0.0)
            h = total[rows]
            r = lax.rsqrt(jnp.mean(h * h, axis=-1, keepdims=True) + RMS_EPS)
            nrm = h * r
            err = (nrm * g - ins[s][...]) * live
            outs[0][...] += 0.5 * jnp.sum(jnp.mean(err * err, axis=-1, keepdims=True), axis=0, keepdims=True)
            dout = err * (1.0 / n)
            outs[1][...] += jnp.sum(dout * nrm, axis=0, keepdims=True)
            dn = dout * g
            o_ref[rows, :] = r * (dn - nrm * jnp.mean(dn * nrm, axis=-1, keepdims=True))

    nb_ops = lambda tm: [target] * (tm // ROW_TILE) + [gf]
    return _Epilogue(nb_ops, in_specs, [jax.ShapeDtypeStruct((1, 1), F32), jax.ShapeDtypeStruct((1, n), F32)],
                     lambda tm: [pl.BlockSpec((1, 1), lambda i, j, kk: (0, 0)), _one_row(n)], fn, init)


def _row_mask(tile_index, rows):
    r = tile_index * rows + lax.broadcasted_iota(jnp.int32, (rows, 1), 0)
    return (r >= PAD).astype(F32)


def _prep(x, meta, g1, hosted):
    s, d = x.shape
    t = HEAD_ROWS + s
    tm = ROW_TILE
    nt = t // tm
    n_hin, n_hout = len(hosted.operands), len(hosted.out_shapes)

    def body(*refs):
        x_ref, meta_ref, g_ref = refs[:3]
        h_ins = refs[3:3 + n_hin]
        h_ref, u_ref = refs[3 + n_hin:5 + n_hin]
        h_outs = refs[5 + n_hin:5 + n_hin + n_hout]
        h_sems = refs[5 + n_hin + n_hout:]
        i = pl.program_id(0)

        @pl.when(i == 0)
        def _():
            hosted.start(h_ins, h_outs, h_sems)
            h_ref[0:PAD, :] = jnp.zeros((PAD, d), F32)
            h_ref[PAD:HEAD_ROWS, :] = meta_ref[...]

        @pl.when(i > 0)
        def _():
            h_ref[...] = x_ref[...]

        h = h_ref[...]
        r = lax.rsqrt(jnp.mean(h * h, axis=-1, keepdims=True) + RMS_EPS)
        u_ref[...] = (h * r * g_ref[...]).astype(BF16)

        @pl.when(i == nt - 1)
        def _():
            hosted.finish(h_ins, h_outs, h_sems)

    outs = pl.pallas_call(
        body, name="prep", grid=(nt,),
        in_specs=[pl.BlockSpec((tm, d), lambda i: (jnp.maximum(i - 1, 0), 0)),
                  pl.BlockSpec((N_META, d), lambda i: (0, 0)),
                  pl.BlockSpec((1, d), lambda i: (0, 0))] + [HBM_SPEC] * n_hin,
        out_specs=[pl.BlockSpec((tm, d), lambda i: (i, 0)), pl.BlockSpec((tm, d), lambda i: (i, 0))]
        + [HBM_SPEC] * n_hout,
        out_shape=[jax.ShapeDtypeStruct((t, d), F32), jax.ShapeDtypeStruct((t, d), BF16)] + list(hosted.out_shapes),
        scratch_shapes=list(hosted.sem_shapes),
        compiler_params=_cparams(("arbitrary",)),
    )(x, meta, g1, *hosted.operands)
    return outs[0], outs[1], list(outs[2:])


def _log_gate(alr, wau, b_alpha):
    t = alr.shape[0]
    dk = wau.shape[1]
    tm = ROW_TILE

    def body(alr_ref, w_ref, b_ref, o_ref):
        z = jnp.dot(alr_ref[...].astype(BF16), w_ref[...].astype(BF16), preferred_element_type=F32) + b_ref[...]
        ls = jnp.minimum(z, 0.0) - jnp.log(1.0 + jnp.exp(-jnp.abs(z)))
        o_ref[...] = ls * (1.0 / GATE_TAU) * _row_mask(pl.program_id(0), tm)

    return pl.pallas_call(
        body, name="log_gate", grid=(t // tm,),
        in_specs=[pl.BlockSpec((tm, RANK_PAD), lambda i: (i, 0)),
                  pl.BlockSpec((RANK_PAD, dk), lambda i: (0, 0)),
                  pl.BlockSpec((1, dk), lambda i: (0, 0))],
        out_specs=pl.BlockSpec((tm, dk), lambda i: (i, 0)),
        out_shape=jax.ShapeDtypeStruct((t, dk), F32),
        compiler_params=_cparams(("parallel",)),
    )(alr, wau, b_alpha)


def _tri(n, upper=False):
    r = lax.broadcasted_iota(jnp.int32, (n, n), 0)
    c = lax.broadcasted_iota(jnp.int32, (n, n), 1)
    return (r <= c) if upper else (r >= c)


_NT = (((1,), (1,)), ((), ()))
_TN = (((0,), (0,)), ((), ()))


def _dot(a, b):
    return jnp.dot(a, b, preferred_element_type=F32)


def _dot_nt(a, b):
    return lax.dot_general(a, b, _NT, preferred_element_type=F32)


def _dot_tn(a, b):
    return lax.dot_general(a, b, _TN, preferred_element_type=F32)


def _chunk_decays(la, tri_f32):
    b = jnp.dot(tri_f32, la, preferred_element_type=F32, precision=lax.Precision.HIGHEST)
    bl = b[CHUNK - 1:CHUNK, :]
    return jnp.exp(b), jnp.exp(bl - b), jnp.exp(-bl), jnp.exp(bl)


def _gla_fwd(p7, log_a, gla_g, d):
    t = p7.shape[0]
    dk_all = d // 2
    dkh, dvh = dk_all // HEADS, d // HEADS
    cb = ROW_TILE
    ncb = cb // CHUNK
    scale = dkh ** -0.5

    def body(qk_ref, v_ref, r_ref, la_ref, g_ref, o_ref, y_ref, s_ref, st_scr):
        @pl.when(pl.program_id(0) == 0)
        def _():
            st_scr[...] = jnp.zeros_like(st_scr)

        tri = _tri(CHUNK)
        tri_f = tri.astype(F32)
        for c in range(ncb):
            rows = slice(c * CHUNK, (c + 1) * CHUNK)
            eb, ekl, ebl_inv, gam = _chunk_decays(la_ref[rows, :], tri_f)
            for h in range(HEADS):
                ks = slice(h * dkh, (h + 1) * dkh)
                vs = slice(h * dvh, (h + 1) * dvh)
                q = qk_ref[rows, ks].astype(F32) * scale
                k = qk_ref[rows, dk_all + h * dkh:dk_all + (h + 1) * dkh].astype(F32)
                v = v_ref[rows, vs].astype(BF16)
                qb = q * eb[:, ks]
                kh = (k * ekl[:, ks]).astype(BF16)
                qc = (qb * ebl_inv[:, ks]).astype(BF16)
                a = jnp.where(tri, _dot_nt(qc, kh), 0.0)
                st = st_scr[h]
                st_b = st.astype(BF16)
                s_ref[c, h] = st_b
                o = _dot_nt(qb.astype(BF16), st_b) + _dot(a.astype(BF16), v)
                st_scr[h] = st * gam[:, ks] + _dot_tn(v, kh)
                o_ref[rows, vs] = o
                rr = lax.rsqrt(jnp.mean(o * o, axis=-1, keepdims=True) + RMS_EPS)
                r = r_ref[rows, vs].astype(F32)
                y_ref[rows, vs] = (o * rr * g_ref[:, vs] * (r * _sigmoid(r))).astype(BF16)

    return pl.pallas_call(
        body, name="gla_fwd", grid=(t // cb,),
        in_specs=[pl.BlockSpec((cb, d), lambda i: (i, 0)),
                  pl.BlockSpec((cb, d), lambda i: (i, 1)),
                  pl.BlockSpec((cb, d), lambda i: (i, 2)),
                  pl.BlockSpec((cb, dk_all), lambda i: (i, 0)),
                  pl.BlockSpec((1, d), lambda i: (0, 0))],
        out_specs=[pl.BlockSpec((cb, d), lambda i: (i, 0)),
                   pl.BlockSpec((cb, d), lambda i: (i, 0)),
                   pl.BlockSpec((ncb, HEADS, dvh, dkh), lambda i: (i, 0, 0, 0))],
        out_shape=[jax.ShapeDtypeStruct((t, d), F32), jax.ShapeDtypeStruct((t, d), BF16),
                   jax.ShapeDtypeStruct((t // CHUNK, HEADS, dvh, dkh), BF16)],
        scratch_shapes=[pltpu.VMEM((HEADS, dvh, dkh), F32)],
        compiler_params=_cparams(("arbitrary",)),
    )(p7, p7, p7, log_a, gla_g)


SUBLANES = 8


def _tap_phases(n_taps, first):
    phases = {}
    for j in range(n_taps):
        e = first + j
        phases.setdefault(e % SUBLANES, []).append((j, e - e % SUBLANES))
    return phases


CONV_ROWS = 64


def _shifted_windows(ext_ref, shf_ref, phases, rows, ls):
    for p, taps in phases.items():
        if p:
            span = max(off for _, off in taps) + rows
            shf_ref[p, 0:span, :] = ext_ref[p:p + span, ls]

    def window(p, start, n):
        return shf_ref[p, start:start + n, :] if p else ext_ref[start:start + n, ls]

    return window


def _conv_strip(ext_ref, shf_ref, w_ref, n_taps, first, rows, ls, emit, reverse=False):
    phases = _tap_phases(n_taps, first)
    window = _shifted_windows(ext_ref, shf_ref, phases, rows, ls)
    for r0 in range(0, rows, CONV_ROWS):
        acc = None
        for p, taps in phases.items():
            for j, off in taps:
                wj = w_ref[(n_taps - 1 - j) if reverse else j, ls]
                term = window(p, off + r0, CONV_ROWS) * wj
                acc = term if acc is None else acc + term
        emit(r0, acc)


def _corr_strip(dl_ref, ext_ref, shf_ref, acc_ref, n_taps, first, rows, ls):
    phases = _tap_phases(n_taps, first)
    window = _shifted_windows(ext_ref, shf_ref, phases, rows, ls)
    for r0 in range(0, rows, CONV_ROWS):
        dl = dl_ref[r0:r0 + CONV_ROWS, ls]
        for p, taps in phases.items():
            for j, off in taps:
                prod = dl * window(p, off + r0, CONV_ROWS)
                acc_ref[SUBLANES * j:SUBLANES * (j + 1), ls] += jnp.sum(
                    prod.reshape(CONV_ROWS // SUBLANES, SUBLANES, prod.shape[-1]), axis=0)


def _conf_fwd(p7, conv_w, conv_b, ln_g, ln_b, d):
    t = p7.shape[0]
    tm = ROW_TILE

    def body(c1_ref, c2_ref, w_ref, b_ref, g_ref, bb_ref, cc_ref, sc_ref, ext, shf):
        @pl.when(pl.program_id(0) == 0)
        def _():
            ext[0:HALO, :] = jnp.zeros((HALO, d), F32)

        ext[HALO:HALO + tm, :] = c1_ref[...].astype(F32) * _sigmoid(c2_ref[...].astype(F32))
        for s0 in range(0, d, LANES):
            ls = slice(s0, s0 + LANES)

            def emit(r0, acc, ls=ls):
                cc_ref[r0:r0 + CONV_ROWS, ls] = acc + b_ref[:, ls]

            _conv_strip(ext, shf, w_ref, CONF_K, HALO - (CONF_K - 1), tm, ls, emit)
        ext[0:HALO, :] = ext[tm:tm + HALO, :]
        g = g_ref[...]
        bb = bb_ref[...]
        rows_per_step = 2 * SUBLANES

        def rows_step(k, carry):
            rs = pl.ds(pl.multiple_of(k * rows_per_step, rows_per_step), rows_per_step)
            cc = cc_ref[rs, :]
            xc = cc - jnp.mean(cc, axis=-1, keepdims=True)
            rstd = lax.rsqrt(jnp.mean(xc * xc, axis=-1, keepdims=True) + LN_EPS)
            cn = xc * rstd * g + bb
            sc_ref[rs, :] = (cn * _sigmoid(cn)).astype(BF16)
            return carry

        lax.fori_loop(0, tm // rows_per_step, rows_step, 0, unroll=4)

    vec = pl.BlockSpec((1, d), lambda i: (0, 0))
    return pl.pallas_call(
        body, name="conf_fwd", grid=(t // tm,),
        in_specs=[pl.BlockSpec((tm, d), lambda i: (i, 3)), pl.BlockSpec((tm, d), lambda i: (i, 4)),
                  pl.BlockSpec((CONF_K, d), lambda i: (0, 0)), vec, vec, vec],
        out_specs=[pl.BlockSpec((tm, d), lambda i: (i, 0)), pl.BlockSpec((tm, d), lambda i: (i, 0))],
        out_shape=[jax.ShapeDtypeStruct((t, d), F32), jax.ShapeDtypeStruct((t, d), BF16)],
        scratch_shapes=[pltpu.VMEM((tm + HALO, d), F32), pltpu.VMEM((SUBLANES, tm + HALO, LANES), F32)],
        compiler_params=_cparams(("arbitrary",)),
    )(p7, p7, conv_w, conv_b, ln_g, ln_b)


def _merge(p7, br_gla, br_conf, d):
    t = p7.shape[0]
    tm = ROW_TILE

    def body(gg_ref, gc_ref, a_ref, b_ref, o_ref):
        o_ref[...] = (_sigmoid(gg_ref[...].astype(F32)) * a_ref[...].astype(F32)
                      + _sigmoid(gc_ref[...].astype(F32)) * b_ref[...].astype(F32)).astype(BF16)

    row = pl.BlockSpec((tm, d), lambda i: (i, 0))
    return pl.pallas_call(
        body, name="merge", grid=(t // tm,),
        in_specs=[pl.BlockSpec((tm, d), lambda i: (i, 5)), pl.BlockSpec((tm, d), lambda i: (i, 6)), row, row],
        out_specs=row, out_shape=jax.ShapeDtypeStruct((t, d), BF16),
        compiler_params=_cparams(("parallel",)),
    )(p7, p7, br_gla, br_conf)


def _ffn_mid(up, w, b, dff):
    t = up.shape[0]
    tm = ROW_TILE
    hal = 8

    def body(a_ref, bv_ref, w_ref, b_ref, y_ref, ext, shf):
        i = pl.program_id(0)

        @pl.when(i == 0)
        def _():
            ext[0:hal, :] = jnp.zeros((hal, dff), F32)

        ext[hal:hal + tm, :] = a_ref[...].astype(F32) * _row_mask(i, tm)
        for s0 in range(0, dff, LANES):
            ls = slice(s0, s0 + LANES)

            def emit(r0, acc, ls=ls):
                rb = slice(r0, r0 + CONV_ROWS)
                ac = acc + b_ref[:, ls]
                y_ref[rb, ls] = (ac * _sigmoid(ac) * bv_ref[rb, ls].astype(F32)).astype(BF16)

            _conv_strip(ext, shf, w_ref, FFN_K, hal - (FFN_K - 1), tm, ls, emit)
        ext[0:hal, :] = ext[tm:tm + hal, :]

    return pl.pallas_call(
        body, name="ffn_mid", grid=(t // tm,),
        in_specs=[pl.BlockSpec((tm, dff), lambda i: (i, 0)), pl.BlockSpec((tm, dff), lambda i: (i, 1)),
                  pl.BlockSpec((FFN_K, dff), lambda i: (0, 0)), pl.BlockSpec((1, dff), lambda i: (0, 0))],
        out_specs=pl.BlockSpec((tm, dff), lambda i: (i, 0)), out_shape=jax.ShapeDtypeStruct((t, dff), BF16),
        scratch_shapes=[pltpu.VMEM((tm + hal, dff), F32), pltpu.VMEM((SUBLANES, tm + hal, LANES), F32)],
        compiler_params=_cparams(("arbitrary",)),
    )(up, up, w, b)


def _dsilu(x, sig):
    return sig * (1.0 + x * (1.0 - sig))


def _ffn_mid_bwd(up, dy, w, b, dff):
    t = up.shape[0]
    tm = ROW_TILE
    hal = 8
    prev_rows = 2 * SUBLANES
    nt = t // tm

    def body(a_ref, ap_ref, bv_ref, dy_ref, w_ref, b_ref, dup_ref, dw_ref, db_ref, ext, dext, dw_acc, db_acc, shf):
        i = pl.program_id(0)
        tile = nt - 1 - i

        @pl.when(i == 0)
        def _():
            dext[tm:tm + hal, :] = jnp.zeros((hal, dff), F32)
            dw_acc[...] = jnp.zeros_like(dw_acc)
            db_acc[...] = jnp.zeros_like(db_acc)

        ext[0:hal, :] = ap_ref[prev_rows - hal:prev_rows, :].astype(F32) * jnp.where(tile > 0, 1.0, 0.0)
        ext[hal:hal + tm, :] = a_ref[...].astype(F32) * _row_mask(tile, tm)
        first = hal - (FFN_K - 1)
        for s0 in range(0, dff, LANES):
            ls = slice(s0, s0 + LANES)

            def emit_fwd(r0, acc, ls=ls, s0=s0):
                rb = slice(r0, r0 + CONV_ROWS)
                ac = acc + b_ref[:, ls]
                sig = _sigmoid(ac)
                dyv = dy_ref[rb, ls].astype(F32)
                dup_ref[rb, dff + s0:dff + s0 + LANES] = (dyv * ac * sig).astype(BF16)
                dac = dyv * bv_ref[rb, ls].astype(F32) * _dsilu(ac, sig)
                dext[rb, ls] = dac
                db_acc[:, ls] += jnp.sum(dac.reshape(CONV_ROWS // SUBLANES, SUBLANES, LANES), axis=0)

            _conv_strip(ext, shf, w_ref, FFN_K, first, tm, ls, emit_fwd)
            _corr_strip(dext, ext, shf, dw_acc, FFN_K, first, tm, ls)

            def emit_bwd(r0, da, ls=ls):
                rb = slice(r0, r0 + CONV_ROWS)
                mask = ((tile * tm + r0 + lax.broadcasted_iota(jnp.int32, (CONV_ROWS, 1), 0)) >= PAD).astype(F32)
                dup_ref[rb, ls] = (da * mask).astype(BF16)

            _conv_strip(dext, shf, w_ref, FFN_K, 0, tm, ls, emit_bwd, reverse=True)
        dext[tm:tm + hal, :] = dext[0:hal, :]

        @pl.when(i == nt - 1)
        def _():
            db_ref[...] = jnp.sum(db_acc[...], axis=0, keepdims=True)
            for j in range(FFN_K):
                dw_ref[j:j + 1, :] = jnp.sum(dw_acc[SUBLANES * j:SUBLANES * (j + 1), :], axis=0, keepdims=True)

    rev = lambda i: (nt - 1 - i, 0)
    return pl.pallas_call(
        body, name="ffn_mid_bwd", grid=(nt,),
        in_specs=[pl.BlockSpec((tm, dff), rev),
                  pl.BlockSpec((prev_rows, dff), lambda i: (jnp.maximum((nt - 1 - i) * (tm // prev_rows) - 1, 0), 0)),
                  pl.BlockSpec((tm, dff), lambda i: (nt - 1 - i, 1)),
                  pl.BlockSpec((tm, dff), rev),
                  pl.BlockSpec((FFN_K, dff), lambda i: (0, 0)), pl.BlockSpec((1, dff), lambda i: (0, 0))],
        out_specs=[pl.BlockSpec((tm, 2 * dff), rev),
                   pl.BlockSpec((FFN_K, dff), lambda i: (0, 0)), pl.BlockSpec((1, dff), lambda i: (0, 0))],
        out_shape=[jax.ShapeDtypeStruct((t, 2 * dff), BF16),
                   jax.ShapeDtypeStruct((FFN_K, dff), F32), jax.ShapeDtypeStruct((1, dff), F32)],
        scratch_shapes=[pltpu.VMEM((tm + hal, dff), F32), pltpu.VMEM((tm + hal, dff), F32),
                        pltpu.VMEM((SUBLANES * FFN_K, dff), F32), pltpu.VMEM((SUBLANES, dff), F32),
                        pltpu.VMEM((SUBLANES, tm + hal, LANES), F32)],
        compiler_params=_cparams(("arbitrary",)),
    )(up, up, up, dy, w, b)


DP_BLOCKS = 8
ANY_SPEC = pl.BlockSpec(memory_space=pl.ANY)


def _merge_bwd(p7, br_gla, br_conf, dmerged, d):
    t = p7.shape[0]
    tm = ROW_TILE

    def body(gg_ref, gc_ref, a_ref, b_ref, dm_ref, da_ref, db_ref, dp_ref):
        dm = dm_ref[...].astype(F32)
        sg = _sigmoid(gg_ref[...].astype(F32))
        sc = _sigmoid(gc_ref[...].astype(F32))
        da_ref[...] = (dm * sg).astype(BF16)
        db_ref[...] = (dm * sc).astype(BF16)
        dp_ref[:, 0:d] = (dm * a_ref[...].astype(F32) * sg * (1.0 - sg)).astype(BF16)
        dp_ref[:, d:2 * d] = (dm * b_ref[...].astype(F32) * sc * (1.0 - sc)).astype(BF16)

    row = pl.BlockSpec((tm, d), lambda i: (i, 0))
    outs = pl.pallas_call(
        body, name="merge_bwd", grid=(t // tm,),
        in_specs=[pl.BlockSpec((tm, d), lambda i: (i, 5)), pl.BlockSpec((tm, d), lambda i: (i, 6)), row, row, row],
        out_specs=[row, row, pl.BlockSpec((tm, 2 * d), lambda i: (i, 0))],
        out_shape=[jax.ShapeDtypeStruct((t, d), BF16)] * 2 + [jax.ShapeDtypeStruct((t, DP_BLOCKS * d), BF16)],
        compiler_params=_cparams(("parallel",)),
    )(p7, p7, br_gla, br_conf, dmerged)
    return outs


def _conf_bwd(p7, cc, dsc, conv_w, ln_g, ln_b, dp, d):
    t = p7.shape[0]
    tm = ROW_TILE
    nt = t // tm

    def body(c1_ref, c2_ref, c1p_ref, c2p_ref, cc_ref, dsc_ref, w_ref, g_ref, bb_ref, dp_in,
             dp_ref, dw_ref, db_ref, dg_ref, dbb_ref, ext, dext, dw_acc, shf):
        i = pl.program_id(0)
        tile = nt - 1 - i

        @pl.when(i == 0)
        def _():
            dext[tm:tm + HALO, :] = jnp.zeros((HALO, d), F32)
            dw_acc[...] = jnp.zeros_like(dw_acc)
            db_ref[...] = jnp.zeros_like(db_ref)
            dg_ref[...] = jnp.zeros_like(dg_ref)
            dbb_ref[...] = jnp.zeros_like(dbb_ref)

        ext[0:HALO, :] = (c1p_ref[...].astype(F32) * _sigmoid(c2p_ref[...].astype(F32))
                          * jnp.where(tile > 0, 1.0, 0.0))
        g = g_ref[...]
        bb = bb_ref[...]

        groups = 4
        grp = 2 * SUBLANES

        def fold(a):
            return a[:SUBLANES] + a[SUBLANES:]

        def rows_step(k, sums):
            sdg, sdbb, sdb = sums
            for u in range(groups):
                r0 = (k * groups + u) * grp
                rs = pl.ds(pl.multiple_of(r0, grp), grp)
                ext[pl.ds(pl.multiple_of(HALO + r0, grp), grp), :] = (
                    c1_ref[rs, :].astype(F32) * _sigmoid(c2_ref[rs, :].astype(F32)))
                cc_ = cc_ref[rs, :]
                xc = cc_ - jnp.mean(cc_, axis=-1, keepdims=True)
                rstd = lax.rsqrt(jnp.mean(xc * xc, axis=-1, keepdims=True) + LN_EPS)
                xh = xc * rstd
                cn = xh * g + bb
                dcn = dsc_ref[rs, :].astype(F32) * _dsilu(cn, _sigmoid(cn))
                dxh = dcn * g
                dcc = rstd * (dxh - jnp.mean(dxh, axis=-1, keepdims=True)
                              - xh * jnp.mean(dxh * xh, axis=-1, keepdims=True))
                dext[rs, :] = dcc
                sdg, sdbb, sdb = sdg + fold(dcn * xh), sdbb + fold(dcn), sdb + fold(dcc)
            return sdg, sdbb, sdb

        zero = jnp.zeros((SUBLANES, d), F32)
        sdg, sdbb, sdb = lax.fori_loop(0, tm // (groups * grp), rows_step, (zero, zero, zero))
        dg_ref[...] += jnp.sum(sdg, axis=0, keepdims=True)
        dbb_ref[...] += jnp.sum(sdbb, axis=0, keepdims=True)
        db_ref[...] += jnp.sum(sdb, axis=0, keepdims=True)
        for s0 in range(0, d, LANES):
            ls = slice(s0, s0 + LANES)
            _corr_strip(dext, ext, shf, dw_acc, CONF_K, HALO - (CONF_K - 1), tm, ls)

            def emit(r0, dc, ls=ls):
                rb = slice(r0, r0 + CONV_ROWS)
                s2l = _sigmoid(c2_ref[rb, ls].astype(F32))
                dp_ref[rb, ls] = (dc * s2l).astype(BF16)
                dp_ref[rb, d + ls.start:d + ls.stop] = (
                    dc * c1_ref[rb, ls].astype(F32) * s2l * (1.0 - s2l)).astype(BF16)

            _conv_strip(dext, shf, w_ref, CONF_K, 0, tm, ls, emit, reverse=True)
        dext[tm:tm + HALO, :] = dext[0:HALO, :]

        @pl.when(i == nt - 1)
        def _():
            for j in range(CONF_K):
                dw_ref[j:j + 1, :] = jnp.sum(dw_acc[SUBLANES * j:SUBLANES * (j + 1), :], axis=0, keepdims=True)

    rev = lambda i: (nt - 1 - i, 0)
    prev = lambda col: (lambda i: (jnp.maximum((nt - 1 - i) * (tm // HALO) - 1, 0), col))
    vec = pl.BlockSpec((1, d), lambda i: (0, 0))
    return pl.pallas_call(
        body, name="conf_bwd", grid=(nt,),
        in_specs=[pl.BlockSpec((tm, d), lambda i: (nt - 1 - i, 3)), pl.BlockSpec((tm, d), lambda i: (nt - 1 - i, 4)),
                  pl.BlockSpec((HALO, d), prev(3)), pl.BlockSpec((HALO, d), prev(4)),
                  pl.BlockSpec((tm, d), rev), pl.BlockSpec((tm, d), rev),
                  pl.BlockSpec((CONF_K, d), lambda i: (0, 0)), vec, vec, ANY_SPEC],
        out_specs=[pl.BlockSpec((tm, 2 * d), lambda i: (nt - 1 - i, 1)),
                   pl.BlockSpec((CONF_K, d), lambda i: (0, 0)), vec, vec, vec],
        out_shape=[jax.ShapeDtypeStruct(dp.shape, dp.dtype),
                   jax.ShapeDtypeStruct((CONF_K, d), F32)] + [jax.ShapeDtypeStruct((1, d), F32)] * 3,
        input_output_aliases={9: 0},
        scratch_shapes=[pltpu.VMEM((tm + HALO, d), F32), pltpu.VMEM((tm + HALO, d), F32),
                        pltpu.VMEM((SUBLANES * CONF_K, d), F32), pltpu.VMEM((SUBLANES, tm + HALO, LANES), F32)],
        compiler_params=_cparams(("arbitrary",)),
    )(p7, p7, p7, p7, cc, dsc, conv_w, ln_g, ln_b, dp)


def _gla_bwd(p7, log_a, alr, wau, b_alpha, gla_g, o, states, dy, dp, d):
    t = p7.shape[0]
    dk_all = d // 2
    dkh, dvh = dk_all // HEADS, d // HEADS
    cb = ROW_TILE
    ncb = cb // CHUNK
    nb = t // cb
    scale = dkh ** -0.5

    def body(qk_ref, v_ref, r_ref, la_ref, alr_ref, wau_ref, ba_ref, g_ref, o_ref, s_ref, dy_ref, dp_in,
             dp_ref, dz_ref, dg_ref, dba_ref, dst_scr, dla_scr):
        i = pl.program_id(0)
        blk = nb - 1 - i

        @pl.when(i == 0)
        def _():
            dst_scr[...] = jnp.zeros_like(dst_scr)
            dg_ref[...] = jnp.zeros_like(dg_ref)
            dba_ref[...] = jnp.zeros_like(dba_ref)

        dp_ref[:, 3 * d:4 * d] = jnp.zeros((cb, d), BF16)

        tri = _tri(CHUNK)
        tri_f = tri.astype(F32)
        triu_f = _tri(CHUNK, upper=True).astype(F32)
        for c in reversed(range(ncb)):
            rows = slice(c * CHUNK, (c + 1) * CHUNK)
            eb, ekl, ebl_inv, gam = _chunk_decays(la_ref[rows, :], tri_f)
            for h in range(HEADS):
                ks = slice(h * dkh, (h + 1) * dkh)
                kcols = slice(dk_all + h * dkh, dk_all + (h + 1) * dkh)
                vs = slice(h * dvh, (h + 1) * dvh)
                q = qk_ref[rows, ks].astype(F32) * scale
                k = qk_ref[rows, kcols].astype(F32)
                v = v_ref[rows, vs].astype(BF16)
                ebh, eklh, eih, gamh = eb[:, ks], ekl[:, ks], ebl_inv[:, ks], gam[:, ks]
                qb = q * ebh
                kh = k * eklh
                qc = qb * eih
                qb_b, kh_b, qc_b = qb.astype(BF16), kh.astype(BF16), qc.astype(BF16)
                ov = o_ref[rows, vs]
                r = r_ref[rows, vs].astype(F32)
                dyv = dy_ref[rows, vs].astype(F32)
                sig = _sigmoid(r)
                rr = lax.rsqrt(jnp.mean(ov * ov, axis=-1, keepdims=True) + RMS_EPS)
                n = ov * rr
                g = g_ref[:, vs]
                dp_ref[rows, 2 * d + h * dvh:2 * d + (h + 1) * dvh] = (dyv * n * g * _dsilu(r, sig)).astype(BF16)
                don = dyv * (r * sig)
                dg_ref[:, vs] += jnp.sum(don * n, axis=0, keepdims=True)
                dn = don * g
                do = (rr * (dn - n * jnp.mean(dn * n, axis=-1, keepdims=True))).astype(BF16)
                st_b = s_ref[c, h]
                a = jnp.where(tri, _dot_nt(qc_b, kh_b), 0.0).astype(BF16)
                da = jnp.where(tri, _dot_nt(do, v), 0.0).astype(BF16)
                dst = dst_scr[h]
                dst_b = dst.astype(BF16)
                dp_ref[rows, d + h * dvh:d + (h + 1) * dvh] = (_dot_tn(a, do) + _dot_nt(kh_b, dst_b)).astype(BF16)
                dqb = _dot(do, st_b)
                dqc = _dot(da, kh_b)
                dkh_ = _dot_tn(da, qc_b) + _dot(v, dst_b)
                dgam = jnp.sum(st_b.astype(F32) * dst, axis=0, keepdims=True)
                dst_scr[h] = dst * gamh + _dot_tn(do, qb_b)
                dp_ref[rows, ks] = ((dqb * ebh + dqc * (ebh * eih)) * scale).astype(BF16)
                dp_ref[rows, kcols] = (dkh_ * eklh).astype(BF16)
                qq = dqc * qc
                kk = dkh_ * kh
                db = dqb * qb + qq - kk
                dbl = jnp.sum(kk - qq, axis=0, keepdims=True) + dgam * gamh
                dla_scr[rows, ks] = jnp.dot(triu_f, db, preferred_element_type=F32,
                                            precision=lax.Precision.HIGHEST) + dbl
        z = jnp.dot(alr_ref[...].astype(BF16), wau_ref[...].astype(BF16), preferred_element_type=F32) + ba_ref[...]
        dz = dla_scr[...] * (1.0 / GATE_TAU) * _sigmoid(-z) * _row_mask(blk, cb)
        dba_ref[...] += jnp.sum(dz, axis=0, keepdims=True)
        dz_ref[...] = dz.astype(BF16)

    rev = lambda i: (nb - 1 - i, 0)
    row = pl.BlockSpec((cb, d), rev)
    return pl.pallas_call(
        body, name="gla_bwd", grid=(nb,),
        in_specs=[row, pl.BlockSpec((cb, d), lambda i: (nb - 1 - i, 1)), pl.BlockSpec((cb, d), lambda i: (nb - 1 - i, 2)),
                  pl.BlockSpec((cb, dk_all), rev), pl.BlockSpec((cb, RANK_PAD), rev),
                  pl.BlockSpec((RANK_PAD, dk_all), lambda i: (0, 0)), pl.BlockSpec((1, dk_all), lambda i: (0, 0)),
                  pl.BlockSpec((1, d), lambda i: (0, 0)), row,
                  pl.BlockSpec((ncb, HEADS, dvh, dkh), lambda i: (nb - 1 - i, 0, 0, 0)), row, ANY_SPEC],
        out_specs=[pl.BlockSpec((cb, 4 * d), lambda i: (nb - 1 - i, 1)), pl.BlockSpec((cb, dk_all), rev),
                   pl.BlockSpec((1, d), lambda i: (0, 0)), pl.BlockSpec((1, dk_all), lambda i: (0, 0))],
        out_shape=[jax.ShapeDtypeStruct(dp.shape, dp.dtype), jax.ShapeDtypeStruct((t, dk_all), BF16),
                   jax.ShapeDtypeStruct((1, d), F32), jax.ShapeDtypeStruct((1, dk_all), F32)],
        input_output_aliases={11: 0},
        scratch_shapes=[pltpu.VMEM((HEADS, dvh, dkh), F32), pltpu.VMEM((cb, dk_all), F32)],
        compiler_params=_cparams(("arbitrary",)),
    )(p7, p7, p7, log_a, alr, wau, b_alpha, gla_g, o, states, dy, dp)


REST = ("w_up", "w_down", "w_gla_o", "w_conf_o", "w_out")


def _chip_partials(grads, names, tag):
    place = jnp.stack([2 * lax.axis_index("x") + lax.axis_index("y"), lax.axis_index("c")])
    from_sibling = _swap_core_halves(grads, "swap_core_halves_" + tag)
    both = [_add_core_halves(place, g, s, "add_core_halves_" + n) for n, g, s in zip(names, grads, from_sibling)]
    return [b[0] for b in both], [b[1] for b in both]


def _local_step(x, target, w, shard):
    s, d = x.shape
    dk_all = d // 2
    wau = jnp.pad(w["w_alpha_up"], ((0, RANK_PAD - GLA_RANK), (0, 0)))

    h0, u1, gathered = _prep(x, w["meta_tokens"], w["norm_mix_g"], _gather_plan([shard["w_in"]]))
    w_in = _from_chip_major(gathered[0])
    lo, hi = 3 * d, 3 * d + GLA_RANK
    wq = jnp.concatenate([w_in[:, :lo], w_in[:, hi:]], axis=1)
    w_alr = jnp.pad(w_in[:, lo:hi], ((0, 0), (0, RANK_PAD - GLA_RANK)))
    shards = [shard[n] for n in REST]
    p7, gathered, _ = _matmul(u1, wq, dims="nn", name="proj", out_dtype=BF16, hosted=_gather_plan(shards))
    w = dict(w)
    for n, slabs in zip(REST, gathered):
        w[n] = slabs if n == "w_up" else slabs.reshape(-1, slabs.shape[-1])
    dff = w["w_down"].shape[0]
    alr = _matmul(u1, w_alr, dims="nn", name="proj_alr")
    log_a = _log_gate(alr, wau, w["b_alpha"])
    o, y_gla, states = _gla_fwd(p7, log_a, w["gla_norm_g"], d)
    br_gla = _matmul(y_gla, w["w_gla_o"], dims="nn", name="gla_out", out_dtype=BF16)
    cc, s_c = _conf_fwd(p7, w["conf_dw_w"], w["conf_dw_b"], w["conf_ln_g"], w["conf_ln_b"], d)
    br_conf = _matmul(s_c, w["w_conf_o"], dims="nn", name="conf_out", out_dtype=BF16)
    merged = _merge(p7, br_gla, br_conf, d)
    h1, _, (u2,) = _matmul(merged, w["w_out"], dims="nn", name="mix_out", add=h0,
                           epilogue=_rms_fwd_epilogue(h0.shape[0], w["norm_ffn_g"]))
    up = _matmul(u2, w["w_up"], dims="nn", name="ffn_up", out_dtype=BF16, chips="b")
    y = _ffn_mid(up, w["ffn_dw_w"], w["ffn_dw_b"], dff)
    dh2, _, (loss, d_gf) = _matmul(y, w["w_down"], dims="nn", name="ffn_down", add=h1,
                                   epilogue=_loss_epilogue(target, w["final_norm_g"]))

    g = {"final_norm_g": d_gf}
    dy = _matmul(dh2, w["w_down"], dims="nt", name="d_ffn_y", out_dtype=BF16)
    g["w_down"] = _matmul(y, dh2, dims="tn", name="dw_down", out_dtype=BF16)
    dup, g["ffn_dw_w"], g["ffn_dw_b"] = _ffn_mid_bwd(up, dy, w["ffn_dw_w"], w["ffn_dw_b"], dff)
    dh1, _, (g["norm_ffn_g"],) = _matmul(dup, w["w_up"], dims="nt", name="d_u2", chips="b",
                                         epilogue=_rms_bwd_epilogue(h1, w["norm_ffn_g"], dh2))
    g["w_up"] = _matmul(u2, dup, dims="tn", name="dw_up", out_dtype=BF16, chips="out")
    dmerged = _matmul(dh1, w["w_out"], dims="nt", name="d_merged", out_dtype=BF16)
    g["w_out"] = _matmul(merged, dh1, dims="tn", name="dw_out", out_dtype=BF16)
    d_br_gla, d_br_conf, dp = _merge_bwd(p7, br_gla, br_conf, dmerged, d)
    dsc = _matmul(d_br_conf, w["w_conf_o"], dims="nt", name="d_conf_s", out_dtype=BF16)
    g["w_conf_o"] = _matmul(s_c, d_br_conf, dims="tn", name="dw_conf_o", out_dtype=BF16)
    dp, g["conf_dw_w"], g["conf_dw_b"], g["conf_ln_g"], g["conf_ln_b"] = _conf_bwd(
        p7, cc, dsc, w["conf_dw_w"], w["conf_ln_g"], w["conf_ln_b"], dp, d)
    dyg = _matmul(d_br_gla, w["w_gla_o"], dims="nt", name="d_gla_y", out_dtype=BF16)
    g["w_gla_o"] = _matmul(y_gla, d_br_gla, dims="tn", name="dw_gla_o", out_dtype=BF16)
    dp, dz, g["gla_norm_g"], g["b_alpha"] = _gla_bwd(
        p7, log_a, alr, wau, w["b_alpha"], w["gla_norm_g"], o, states, dyg, dp, d)
    dalr = _matmul(dz, wau, dims="nt", name="d_alr", out_dtype=BF16)
    g["w_alpha_up"] = _matmul(alr, dz, dims="tn", name="dw_alpha_up")[:GLA_RANK]
    wq_b = jnp.concatenate([wq[:, 5 * d:], wq[:, 3 * d:5 * d], wq[:, :3 * d], jnp.zeros((d, d), wq.dtype)], axis=1)
    rest_grads = [g.pop(n) for n in REST]
    rest_grads = [a if a.ndim == 3 else a.reshape(N_CHIPS, -1, a.shape[-1]) for a in rest_grads]
    parts, owns = _chip_partials(rest_grads, REST, "rest")
    own_chip = dict(zip(REST, owns))
    dwq, arrived, _ = _matmul(u1, dp, dims="tn", name="dw_in", out_dtype=BF16, hosted=_exchange_plan(parts))
    from_chips = dict(zip(REST, arrived))
    dw_alr = _matmul(u1, dalr, dims="tn", name="dw_in_alr", out_dtype=BF16)
    dw_in = _to_chip_major(jnp.concatenate(
        [dwq[:, 4 * d:7 * d], dw_alr[:, :GLA_RANK], dwq[:, 2 * d:4 * d], dwq[:, :2 * d]], axis=1))
    parts, owns = _chip_partials([dw_in], ["w_in"], "w_in")
    own_chip["w_in"] = owns[0]
    dh0, arrived, (g["norm_mix_g"],) = _matmul(
        dp, wq_b, dims="nt", name="d_u1", hosted=_exchange_plan(parts),
        epilogue=_rms_bwd_epilogue(h0, w["norm_mix_g"], dh1, low_rank=(dalr, w_alr)))
    from_chips["w_in"] = arrived[0]
    g["meta_tokens"] = dh0[PAD:HEAD_ROWS]
    return loss, dh0[HEAD_ROWS:], g, own_chip, from_chips


HBM_SPEC = pl.BlockSpec(memory_space=pltpu.HBM)
FLIPS = ((1, 0), (0, 1), (1, 1))


def _place():
    x, y, c = lax.axis_index("x"), lax.axis_index("y"), lax.axis_index("c")
    return x, y, c


def _half_rows(ref, h, lead=()):
    rh = ref.shape[-2] // 2
    return ref.at[(*lead, pl.ds(pl.multiple_of(h * rh, 2 * SUBLANES), rh), slice(None))]


class _Hosted:
    def __init__(self, operands, out_shapes, sem_shapes, start, finish):
        self.operands, self.out_shapes, self.sem_shapes = operands, out_shapes, sem_shapes
        self.start, self.finish = start, finish


def _gather_plan(shards):
    n = len(shards)

    def copies(ins, outs, sems, kinds):
        send_sems, recv_sems = sems
        x, y, c = _place()
        chips = [(x ^ fx, y ^ fy) for fx, fy in FLIPS]
        me, sibling = (x, y, c), (x, y, 1 - c)

        def copy(k, sem, chip, h, to, src=None):
            slot = _half_rows(outs[k], h, lead=(2 * chip[0] + chip[1],))
            return pltpu.make_async_remote_copy(src_ref=slot if src is None else src, dst_ref=slot,
                                                send_sem=send_sems.at[sem], recv_sem=recv_sems.at[sem],
                                                device_id=to, device_id_type=MESH)

        def own(k):
            return pltpu.make_async_remote_copy(src_ref=ins[k], dst_ref=outs[k].at[2 * x + y],
                                                send_sem=send_sems.at[6 * n + k], recv_sem=recv_sems.at[6 * n + k],
                                                device_id=sibling, device_id_type=MESH)

        make = {
            "first": lambda k, j, chip: copy(k, 3 * k + j, (x, y), c, (*chip, c), src=_half_rows(ins[k], c)),
            "landed": lambda k, j, chip: copy(k, 3 * k + j, chip, c, me),
            "passed": lambda k, j, chip: copy(k, 3 * n + 3 * k + j, chip, c, sibling),
            "from_sibling": lambda k, j, chip: copy(k, 3 * n + 3 * k + j, chip, 1 - c, me),
        }
        return [[own(k) for k in range(n)] if kind == "own" else
                [make[kind](k, j, chip) for k in range(n) for j, chip in enumerate(chips)] for kind in kinds]

    def start(ins, outs, sems):
        first, own = copies(ins, outs, sems, ["first", "own"])
        for cp in first + own:
            cp.start()

    def finish(ins, outs, sems):
        first, landed, passed, from_sibling, own = copies(
            ins, outs, sems, ["first", "landed", "passed", "from_sibling", "own"])
        for arrived, fwd in zip(landed, passed):
            arrived.wait_recv()
            fwd.start()
        for cp in from_sibling:
            cp.wait_recv()
        for cp in own:
            cp.wait()
        for cp in first + passed:
            cp.wait_send()

    return _Hosted(list(shards), [jax.ShapeDtypeStruct((N_CHIPS, *s.shape), s.dtype) for s in shards],
                   [pltpu.SemaphoreType.DMA((7 * n,)), pltpu.SemaphoreType.DMA((7 * n,))], start, finish)


def _swap_core_halves(grads, name):
    n = len(grads)

    def body(*refs):
        ins, outs = refs[:n], refs[n:2 * n]
        send_sems, recv_sems = refs[2 * n:]
        x, y, c = _place()
        cps = [pltpu.make_async_remote_copy(
            src_ref=_half_rows(ins[k], 1 - c, lead=(slice(None),)), dst_ref=outs[k], send_sem=send_sems.at[k],
            recv_sem=recv_sems.at[k], device_id=(x, y, 1 - c), device_id_type=MESH) for k in range(n)]
        for cp in cps:
            cp.start()
        for cp in cps:
            cp.wait()

    return pl.pallas_call(
        body, name=name, in_specs=[HBM_SPEC] * n, out_specs=[HBM_SPEC] * n,
        out_shape=[jax.ShapeDtypeStruct((g.shape[0], g.shape[1] // 2, g.shape[2]), g.dtype) for g in grads],
        scratch_shapes=[pltpu.SemaphoreType.DMA((n,)), pltpu.SemaphoreType.DMA((n,))],
    )(*grads)


def _exchange_plan(parts):
    n = len(parts)

    def copies(ins, outs, sems):
        send_sems, recv_sems = sems
        x, y, c = _place()
        cps = []
        for k in range(n):
            for j, (fx, fy) in enumerate(FLIPS):
                tx, ty = x ^ fx, y ^ fy
                cps.append(pltpu.make_async_remote_copy(
                    src_ref=ins[k].at[2 * tx + ty], dst_ref=_half_rows(outs[k], c, lead=(j,)),
                    send_sem=send_sems.at[3 * k + j], recv_sem=recv_sems.at[3 * k + j],
                    device_id=(tx, ty, c), device_id_type=MESH))
        return cps

    def start(ins, outs, sems):
        for cp in copies(ins, outs, sems):
            cp.start()

    def finish(ins, outs, sems):
        for cp in copies(ins, outs, sems):
            cp.wait()

    return _Hosted(list(parts), [jax.ShapeDtypeStruct((3, 2 * p.shape[1], p.shape[2]), p.dtype) for p in parts],
                   [pltpu.SemaphoreType.DMA((3 * n,)), pltpu.SemaphoreType.DMA((3 * n,))], start, finish)


def _join_core_halves(bufs):
    n = len(bufs)

    def body(*refs):
        outs = refs[n:2 * n]
        send_sems, recv_sems = refs[2 * n:]
        x, y, c = _place()

        def rows(k, h):
            return _half_rows(outs[k], h, lead=(slice(None),) * (len(outs[k].shape) - 2))

        cps = [pltpu.make_async_remote_copy(
            src_ref=rows(k, c), dst_ref=rows(k, c), send_sem=send_sems.at[k],
            recv_sem=recv_sems.at[k], device_id=(x, y, 1 - c), device_id_type=MESH) for k in range(n)]
        for cp in cps:
            cp.start()
        for k in range(n):
            cps[k].wait_send()
            pltpu.make_async_remote_copy(
                src_ref=rows(k, c), dst_ref=rows(k, 1 - c), send_sem=send_sems.at[k],
                recv_sem=recv_sems.at[k], device_id=(x, y, 1 - c), device_id_type=MESH).wait_recv()

    return pl.pallas_call(
        body, name="join_core_halves", in_specs=[HBM_SPEC] * n, out_specs=[HBM_SPEC] * n,
        out_shape=[jax.ShapeDtypeStruct(f.shape, f.dtype) for f in bufs],
        input_output_aliases={k: k for k in range(n)},
        scratch_shapes=[pltpu.SemaphoreType.DMA((n,)), pltpu.SemaphoreType.DMA((n,))],
    )(*bufs)


def _gather_small(block, name):
    m, n = block.shape

    def body(x_ref, out_ref, send_sems, recv_sems, local_sem):
        x, y, c = _place()
        me, sibling = (x, y, c), (x, y, 1 - c)
        chips = [(x ^ fx, y ^ fy) for fx, fy in FLIPS]

        def rows(px, py, pc):
            return out_ref.at[pl.ds((4 * px + 2 * py + pc) * m, m), :]

        def copy(k, blk, to, src=None):
            return pltpu.make_async_remote_copy(
                src_ref=rows(*blk) if src is None else src, dst_ref=rows(*blk),
                send_sem=send_sems.at[k], recv_sem=recv_sems.at[k], device_id=to, device_id_type=MESH)

        mine = pltpu.make_async_copy(x_ref, rows(*me), local_sem)
        mine.start()
        first = [copy(0, me, sibling, src=x_ref)]
        first += [copy(1 + j, me, (*chip, c), src=x_ref) for j, chip in enumerate(chips)]
        for cp in first:
            cp.start()
        passed = [copy(4 + j, (*chip, c), sibling) for j, chip in enumerate(chips)]
        for j, chip in enumerate(chips):
            copy(1 + j, (*chip, c), me).wait_recv()
            passed[j].start()
        copy(0, sibling, me).wait_recv()
        for j, chip in enumerate(chips):
            copy(4 + j, (*chip, 1 - c), me).wait_recv()
        for cp in first + passed:
            cp.wait_send()
        mine.wait()

    out = pl.pallas_call(
        body, name=name,
        out_shape=jax.ShapeDtypeStruct((8 * m, n), block.dtype),
        in_specs=[pl.BlockSpec(memory_space=pltpu.VMEM)],
        out_specs=pl.BlockSpec(memory_space=pltpu.VMEM),
        scratch_shapes=[pltpu.SemaphoreType.DMA((7,)), pltpu.SemaphoreType.DMA((7,)), pltpu.SemaphoreType.DMA],
    )(block)
    return out.reshape(8, m, n)


def _add_core_halves(place, grad, from_sibling, name):
    nc, r, cols = grad.shape
    rh = r // 2

    def body(place_ref, g_ref, s_ref, o_ref, own_ref):
        total = (g_ref[...].astype(F32) + s_ref[...].astype(F32)).astype(BF16)
        o_ref[...] = total

        @pl.when(pl.program_id(0) == place_ref[0])
        def _():
            own_ref[...] = total[0]

    spec = pl.BlockSpec((1, rh, cols), lambda t, place_ref: (t, 0, 0))
    return pl.pallas_call(
        body, name=name,
        grid_spec=pltpu.PrefetchScalarGridSpec(
            num_scalar_prefetch=1, grid=(nc,),
            in_specs=[pl.BlockSpec((1, rh, cols), lambda t, place_ref: (t, place_ref[1], 0)), spec],
            out_specs=[spec, pl.BlockSpec((rh, cols), lambda t, place_ref: (place_ref[1], 0))]),
        out_shape=[jax.ShapeDtypeStruct((nc, rh, cols), BF16), jax.ShapeDtypeStruct((r, cols), BF16)],
        compiler_params=_cparams(("arbitrary",)),
    )(place, grad, from_sibling)


def _sum_adamw(own, others, w, m, v, name):
    rws, cols = w.shape
    tr = rws
    for cand in (256, 128, 64, 32, 16):
        if rws % cand == 0 and cand * cols * 4 <= 2 * 1024 * 1024:
            tr = cand
            break
    c1 = 1.0 - ADAM_B1 ** ADAM_STEP
    c2 = 1.0 - ADAM_B2 ** ADAM_STEP

    def body(a_ref, b_ref, w_ref, m_ref, v_ref, g_ref, d_ref, nm_ref, nv_ref):
        gv = a_ref[...].astype(F32)
        for j in range(3):
            gv = gv + b_ref[j].astype(F32)
        nm = ADAM_B1 * m_ref[...] + (1.0 - ADAM_B1) * gv
        nv = ADAM_B2 * v_ref[...] + (1.0 - ADAM_B2) * (gv * gv)
        m_hat = nm / c1
        v_hat = nv / c2
        g_ref[...] = gv
        d_ref[...] = -ADAM_LR * (m_hat / (jnp.sqrt(v_hat) + ADAM_EPS) + ADAM_WD * w_ref[...])
        nm_ref[...] = nm
        nv_ref[...] = nv

    spec = pl.BlockSpec((tr, cols), lambda i: (i, 0))
    return pl.pallas_call(
        body, name=name, grid=(rws // tr,),
        in_specs=[spec, pl.BlockSpec((3, tr, cols), lambda i: (0, i, 0))] + [spec] * 3, out_specs=[spec] * 4,
        out_shape=[jax.ShapeDtypeStruct((rws, cols), F32)] * 4,
        compiler_params=_cparams(("parallel",)),
    )(own, others, w, m, v)


def _sum_devices(blocks):
    n, m, _ = blocks.shape

    def body(b_ref, o_ref):
        acc = b_ref[0]
        for j in range(1, n):
            acc = acc + b_ref[j]
        o_ref[...] = acc

    return pl.pallas_call(
        body, name="sum_devices", out_shape=jax.ShapeDtypeStruct((m, LANES), F32),
        in_specs=[pl.BlockSpec(memory_space=pltpu.VMEM)], out_specs=pl.BlockSpec(memory_space=pltpu.VMEM),
    )(blocks)


def _adamw(w, g, m, v, name):
    rws, cols = w.shape
    tr = rws
    for cand in (256, 128, 64, 32, 16, 8):
        if rws % cand == 0 and cand * cols * 4 <= 2 * 1024 * 1024:
            tr = cand
            break
    c1 = 1.0 - ADAM_B1 ** ADAM_STEP
    c2 = 1.0 - ADAM_B2 ** ADAM_STEP

    def body(w_ref, g_ref, m_ref, v_ref, d_ref, nm_ref, nv_ref):
        gv = g_ref[...]
        nm = ADAM_B1 * m_ref[...] + (1.0 - ADAM_B1) * gv
        nv = ADAM_B2 * v_ref[...] + (1.0 - ADAM_B2) * (gv * gv)
        m_hat = nm / c1
        v_hat = nv / c2
        d_ref[...] = -ADAM_LR * (m_hat / (jnp.sqrt(v_hat) + ADAM_EPS) + ADAM_WD * w_ref[...])
        nm_ref[...] = nm
        nv_ref[...] = nv

    spec = pl.BlockSpec((tr, cols), lambda i: (i, 0))
    return pl.pallas_call(
        body, name=name, grid=(rws // tr,), in_specs=[spec] * 4, out_specs=[spec] * 3,
        out_shape=[jax.ShapeDtypeStruct((rws, cols), F32)] * 3,
        compiler_params=_cparams(("parallel",)),
    )(w, g, m, v)


WEIGHTS = (
    ("meta_tokens", (16, 1024), 1), ("norm_mix_g", (1024,), None), ("w_in", (1024, 7184), 1),
    ("w_alpha_up", (16, 512), 1), ("b_alpha", (512,), None), ("gla_norm_g", (1024,), None),
    ("w_gla_o", (1024, 1024), 0), ("conf_dw_w", (31, 1024), 1), ("conf_dw_b", (1024,), None),
    ("conf_ln_g", (1024,), None), ("conf_ln_b", (1024,), None), ("w_conf_o", (1024, 1024), 0),
    ("w_out", (1024, 1024), 0), ("norm_ffn_g", (1024,), None), ("w_up", (1024, 5632), 1),
    ("ffn_dw_w", (3, 2816), 1), ("ffn_dw_b", (2816,), None), ("w_down", (2816, 1024), 0),
    ("final_norm_g", (1024,), None),
)
BIG = ("w_in", "w_up", "w_down", "w_gla_o", "w_conf_o", "w_out")
SMALL_SHARDED = ("meta_tokens", "w_alpha_up", "conf_dw_w", "ffn_dw_w")
REPLICATED = tuple(n for n, _, ax in WEIGHTS if ax is None)
SHAPES = {n: s for n, s, _ in WEIGHTS}
AXIS = {n: ax for n, _, ax in WEIGHTS}
N_CHIPS = 4


def _shard_shape(name):
    s = list(SHAPES[name])
    s[AXIS[name]] //= N_CHIPS
    return tuple(s)


def _pack(parts, mult):
    flat = jnp.concatenate([p.reshape(-1) for p in parts])
    pad = (-flat.shape[0]) % mult
    return jnp.pad(flat, (0, pad))


def _unpack(flat, names, shape_of):
    out, off = {}, 0
    for n in names:
        shp = shape_of(n)
        size = math.prod(shp)
        out[n] = flat[off:off + size].reshape(shp)
        off += size
    return out


def _to_chip_major(full):
    r, cols = full.shape
    return full.reshape(r, N_CHIPS, cols // N_CHIPS).transpose(1, 0, 2)


def _from_chip_major(slabs):
    nc, r, cs = slabs.shape
    return slabs.transpose(1, 0, 2).reshape(r, nc * cs)


def kernel(x, meta_tokens, norm_mix_g, w_in, w_alpha_up, b_alpha, gla_norm_g, w_gla_o, conf_dw_w, conf_dw_b, conf_ln_g, conf_ln_b, w_conf_o, w_out, norm_ffn_g, w_up, ffn_dw_w, ffn_dw_b, w_down, final_norm_g, loss_target, m_meta_tokens, m_norm_mix_g, m_w_in, m_w_alpha_up, m_b_alpha, m_gla_norm_g, m_w_gla_o, m_conf_dw_w, m_conf_dw_b, m_conf_ln_g, m_conf_ln_b, m_w_conf_o, m_w_out, m_norm_ffn_g, m_w_up, m_ffn_dw_w, m_ffn_dw_b, m_w_down, m_final_norm_g, v_meta_tokens, v_norm_mix_g, v_w_in, v_w_alpha_up, v_b_alpha, v_gla_norm_g, v_w_gla_o, v_conf_dw_w, v_conf_dw_b, v_conf_ln_g, v_conf_ln_b, v_w_conf_o, v_w_out, v_norm_ffn_g, v_w_up, v_ffn_dw_w, v_ffn_dw_b, v_w_down, v_final_norm_g):
    names = [n for n, _, _ in WEIGHTS]
    w_args = (meta_tokens, norm_mix_g, w_in, w_alpha_up, b_alpha, gla_norm_g, w_gla_o, conf_dw_w, conf_dw_b, conf_ln_g,
              conf_ln_b, w_conf_o, w_out, norm_ffn_g, w_up, ffn_dw_w, ffn_dw_b, w_down, final_norm_g)
    m_args = (m_meta_tokens, m_norm_mix_g, m_w_in, m_w_alpha_up, m_b_alpha, m_gla_norm_g, m_w_gla_o, m_conf_dw_w,
              m_conf_dw_b, m_conf_ln_g, m_conf_ln_b, m_w_conf_o, m_w_out, m_norm_ffn_g, m_w_up, m_ffn_dw_w, m_ffn_dw_b,
              m_w_down, m_final_norm_g)
    v_args = (v_meta_tokens, v_norm_mix_g, v_w_in, v_w_alpha_up, v_b_alpha, v_gla_norm_g, v_w_gla_o, v_conf_dw_w,
              v_conf_dw_b, v_conf_ln_g, v_conf_ln_b, v_w_conf_o, v_w_out, v_norm_ffn_g, v_w_up, v_ffn_dw_w, v_ffn_dw_b,
              v_w_down, v_final_norm_g)
    in_shape = {n: a.shape for n, a in zip(names, w_args)}
    local = {n: a.reshape(_shard_shape(n) if AXIS[n] is not None else SHAPES[n]) for n, a in zip(names, w_args)}
    m_loc = {n: a.reshape(local[n].shape) for n, a in zip(names, m_args)}
    v_loc = {n: a.reshape(local[n].shape) for n, a in zip(names, v_args)}

    me = 2 * lax.axis_index("x") + lax.axis_index("y")

    small = _pack([local[n] for n in SMALL_SHARDED], 8 * LANES).reshape(-1, LANES)
    small_all = _gather_small(small, "gather_small_weights")[::2].reshape(N_CHIPS, -1)
    per_chip_small = [_unpack(small_all[t], SMALL_SHARDED, _shard_shape) for t in range(N_CHIPS)]
    full = {n: jnp.concatenate([per_chip_small[t][n] for t in range(N_CHIPS)], axis=1) for n in SMALL_SHARDED}
    for n in REPLICATED:
        full[n] = local[n].reshape(1, -1)

    loss_part, grad_x, grads, own_chip, from_chips = _local_step(
        x[0], loss_target[0], full, {n: local[n].astype(BF16) for n in BIG})

    smalls = REPLICATED + SMALL_SHARDED
    rep = _pack([grads[n] for n in smalls] + [loss_part], 8 * LANES).reshape(-1, LANES)
    rep_sum = _sum_devices(_gather_small(rep, "gather_small_grads")).reshape(-1)
    g_loc = _unpack(rep_sum, smalls, lambda n: SHAPES[n])
    loss = rep_sum[sum(math.prod(SHAPES[n]) for n in smalls)]
    for n in SMALL_SHARDED:
        width = _shard_shape(n)[1]
        g_loc[n] = lax.dynamic_slice_in_dim(g_loc[n], me * width, width, axis=1)

    joined = _join_core_halves([own_chip[n] for n in BIG] + [from_chips[n] for n in BIG])
    delta, new_m, new_v = {}, {}, {}
    for k, n in enumerate(BIG):
        g_loc[n], delta[n], new_m[n], new_v[n] = _sum_adamw(
            joined[k], joined[len(BIG) + k], local[n], m_loc[n], v_loc[n], "sum_adamw_" + n)
    rest = SMALL_SHARDED + REPLICATED
    pk = lambda dct: _pack([dct[n] for n in rest], 8 * LANES).reshape(-1, LANES)
    ds, ms, vs = _adamw(pk(local), pk(g_loc), pk(m_loc), pk(v_loc), "adamw_small")
    shape_loc = lambda n: local[n].shape
    for dct, flat in ((delta, ds), (new_m, ms), (new_v, vs)):
        dct.update(_unpack(flat.reshape(-1), rest, shape_loc))

    outs = [loss, grad_x[None]]
    for dct in (g_loc, delta, new_m, new_v):
        outs += [dct[n].reshape(in_shape[n]) for n in names]
    return tuple(outs)
```

```python
import functools
import math

import jax
import jax.numpy as jnp
from jax import lax
from jax.experimental import pallas as pl
from jax.experimental.pallas import tpu as pltpu

F32 = jnp.float32
BF16 = jnp.bfloat16

N_META = 16
PAD = 240
HEAD_ROWS = PAD + N_META
HEADS = 4
GLA_RANK = 16
RANK_PAD = 128
GATE_TAU = 16.0
CHUNK = 64
CONF_K = 31
FFN_K = 3
RMS_EPS = 1e-6
LN_EPS = 1e-5
ADAM_LR, ADAM_B1, ADAM_B2, ADAM_EPS, ADAM_WD, ADAM_STEP = 0.001, 0.9, 0.999, 1e-08, 0.01, 10

ROW_TILE = 256
HALO = 32
LANES = 128
V7X_VMEM_LIMIT = 56 * 1024 * 1024
MESH = pl.DeviceIdType.MESH


def _cparams(sem):
    return pltpu.CompilerParams(dimension_semantics=sem, vmem_limit_bytes=V7X_VMEM_LIMIT)


def _sigmoid(x):
    return 1.0 / (1.0 + jnp.exp(-x))


def _pick(n, prefs):
    for p in prefs:
        if n % p == 0:
            return p
    return n


MATMUL_TILES = {
    "proj": (2816, 1024, 1024), "gla_out": (1408, 1024, 1024), "conf_out": (1408, 1024, 1024),
    "mix_out": (1408, 1024, 1024), "ffn_up": (1408, 1408, 1024), "ffn_down": (768, 1024, 2816),
    "d_ffn_y": (1408, 1408, 1024), "d_u2": (768, 1024, 1408), "d_merged": (768, 1024, 1024),
    "d_conf_s": (1408, 1024, 1024), "d_gla_y": (1408, 1024, 1024), "d_u1": (768, 1024, 2048),
    "dw_in": (1024, 1024, 2816), "dw_up": (1024, 1408, 1408), "dw_down": (1408, 1024, 1408),
    "dw_out": (1024, 1024, 1408), "dw_conf_o": (1024, 1024, 1408), "dw_gla_o": (1024, 1024, 1408),
}


class _Epilogue:
    def __init__(self, operands, in_specs, out_shapes, out_specs, fn, init=None):
        self.operands, self.in_specs, self.out_shapes, self.out_specs = operands, in_specs, out_shapes, out_specs
        self.fn, self.init = fn, init


def _matmul(a, b, *, dims, name, tm=None, tn=None, tk=None, out_dtype=F32, add=None, chips=None, hosted=None,
            epilogue=None):
    if chips == "b":
        nc, r, cs = b.shape
        b_shape = (r, nc * cs)
    else:
        b_shape = b.shape
    if dims == "nn":
        (m, k), (_, n) = a.shape, b_shape
    elif dims == "nt":
        (m, k), (n, _) = a.shape, b_shape
    else:
        (k, m), (_, n) = a.shape, b_shape
    want = MATMUL_TILES.get(name, (None, None, None))
    tm, tn, tk = tm or want[0], tn or want[1], tk or want[2]
    tm = tm if tm and m % tm == 0 else _pick(m, (768, 1024, 1408, 512, 256, 128))
    tn = tn if tn and n % tn == 0 else _pick(n, (1024, 1408, 512, 256, 128))
    tk = tk if tk and k % tk == 0 else _pick(k, (1024, 768, 1408, 512, 256, 128))
    if chips == "b":
        tn, tk = (cs, tk) if dims == "nn" else (tn, cs)
    if chips == "out":
        tn, tm = n // N_CHIPS, m
    nk = k // tk
    assert m % tm == 0 and n % tn == 0 and k % tk == 0, (name, m, n, k, tm, tn, tk)
    a_spec = {"nn": pl.BlockSpec((tm, tk), lambda i, j, kk: (i, kk)),
              "nt": pl.BlockSpec((tm, tk), lambda i, j, kk: (i, kk)),
              "tn": pl.BlockSpec((tk, tm), lambda i, j, kk: (kk, i))}[dims]
    if chips == "b":
        b_spec = {"nn": pl.BlockSpec((None, tk, tn), lambda i, j, kk: (j, kk, 0)),
                  "nt": pl.BlockSpec((None, tn, tk), lambda i, j, kk: (kk, j, 0))}[dims]
    else:
        b_spec = {"nn": pl.BlockSpec((tk, tn), lambda i, j, kk: (kk, j)),
                  "nt": pl.BlockSpec((tn, tk), lambda i, j, kk: (j, kk)),
                  "tn": pl.BlockSpec((tk, tn), lambda i, j, kk: (kk, j))}[dims]
    contract = {"nn": (((1,), (0,)), ((), ())), "nt": (((1,), (1,)), ((), ())), "tn": (((0,), (0,)), ((), ()))}[dims]
    if chips == "out":
        o_spec = pl.BlockSpec((None, tm, tn), lambda i, j, kk: (j, 0, 0))
        out_struct = jax.ShapeDtypeStruct((N_CHIPS, m, tn), out_dtype)
    else:
        o_spec = pl.BlockSpec((tm, tn), lambda i, j, kk: (i, j))
        out_struct = jax.ShapeDtypeStruct((m, n), out_dtype)
    has_add = add is not None

    n_hin = len(hosted.operands) if hosted is not None else 0
    n_hout = len(hosted.out_shapes) if hosted is not None else 0
    e_operands = []
    if epilogue is not None:
        e_operands = epilogue.operands(tm) if callable(epilogue.operands) else epilogue.operands
    n_ein = len(e_operands)
    n_eout = len(epilogue.out_shapes) if epilogue is not None else 0

    def body(*refs):
        refs = list(refs)
        a_ref, b_ref = refs[:2]
        add_ref = refs[2] if has_add else None
        pos = 2 + has_add
        h_ins, pos = refs[pos:pos + n_hin], pos + n_hin
        e_ins, pos = refs[pos:pos + n_ein], pos + n_ein
        o_ref, pos = refs[pos], pos + 1
        h_outs, pos = refs[pos:pos + n_hout], pos + n_hout
        e_outs, pos = refs[pos:pos + n_eout], pos + n_eout
        acc_ref = refs[pos] if nk > 1 else None
        h_sems = refs[pos + (nk > 1):]
        i, j, kk = pl.program_id(0), pl.program_id(1), pl.program_id(2)
        first = (i == 0) & (j == 0) & (kk == 0)
        if hosted is not None:
            @pl.when(first)
            def _():
                hosted.start(h_ins, h_outs, h_sems)

        if epilogue is not None and epilogue.init is not None:
            @pl.when(first)
            def _():
                epilogue.init(e_outs)

        def finish(total):
            if epilogue is None:
                o_ref[...] = total.astype(out_dtype)
            else:
                epilogue.fn(total, i, o_ref, e_ins, e_outs)

        prod = lax.dot_general(a_ref[...].astype(BF16), b_ref[...].astype(BF16), contract,
                               preferred_element_type=F32)
        if nk == 1:
            finish(prod + add_ref[...].astype(F32) if has_add else prod)
        else:
            @pl.when(kk == 0)
            def _():
                acc_ref[...] = prod + add_ref[...].astype(F32) if has_add else prod

            @pl.when((kk > 0) & (kk < nk - 1))
            def _():
                acc_ref[...] += prod

            @pl.when(kk == nk - 1)
            def _():
                finish(acc_ref[...] + prod)

        if hosted is not None:
            @pl.when((i == m // tm - 1) & (j == n // tn - 1) & (kk == nk - 1))
            def _():
                hosted.finish(h_ins, h_outs, h_sems)

    in_specs = ([a_spec, b_spec] + ([o_spec] if has_add else []) + [HBM_SPEC] * n_hin
                + (list(epilogue.in_specs(tm)) if epilogue is not None else []))
    args = ((a, b) + ((add,) if has_add else ()) + (tuple(hosted.operands) if hosted is not None else ())
            + tuple(e_operands))
    serial = hosted is not None or epilogue is not None
    outs = pl.pallas_call(
        body, name=name, grid=(m // tm, n // tn, nk),
        in_specs=in_specs,
        out_specs=[o_spec] + [HBM_SPEC] * n_hout + (list(epilogue.out_specs(tm)) if epilogue is not None else []),
        out_shape=([out_struct] + (list(hosted.out_shapes) if hosted is not None else [])
                   + (list(epilogue.out_shapes) if epilogue is not None else [])),
        scratch_shapes=([pltpu.VMEM((tm, tn), F32)] if nk > 1 else [])
        + (list(hosted.sem_shapes) if hosted is not None else []),
        compiler_params=_cparams(("arbitrary",) * 3 if serial else ("parallel", "parallel", "arbitrary")),
    )(*args)
    if not serial:
        return outs[0]
    return outs[0], list(outs[1:1 + n_hout]), list(outs[1 + n_hout:])


def _tile_rows(n):
    return lambda tm: pl.BlockSpec((tm, n), lambda i, j, kk: (i, 0))


def _one_row(n):
    return pl.BlockSpec((1, n), lambda i, j, kk: (0, 0))


def _rms_fwd_epilogue(rows, g):
    n = g.shape[1]

    def fn(total, i, o_ref, ins, outs):
        o_ref[...] = total
        r = lax.rsqrt(jnp.mean(total * total, axis=-1, keepdims=True) + RMS_EPS)
        outs[0][...] = (total * r * ins[0][...]).astype(BF16)

    return _Epilogue([g], lambda tm: [_one_row(n)], [jax.ShapeDtypeStruct((rows, n), BF16)],
                     lambda tm: [_tile_rows(n)(tm)], fn)


def _rms_bwd_epilogue(h, g, dres, low_rank=None):
    n = g.shape[1]
    operands = [h, dres, g] + (list(low_rank) if low_rank is not None else [])

    def in_specs(tm):
        specs = [_tile_rows(n)(tm), _tile_rows(n)(tm), _one_row(n)]
        if low_rank is not None:
            specs += [_tile_rows(low_rank[0].shape[1])(tm),
                      pl.BlockSpec(low_rank[1].shape, lambda i, j, kk: (0, 0))]
        return specs

    def init(outs):
        outs[0][...] = jnp.zeros_like(outs[0])

    def fn(total, i, o_ref, ins, outs):
        if low_rank is not None:
            total = total + _dot_nt(ins[3][...], ins[4][...])
        x = ins[0][...]
        r = lax.rsqrt(jnp.mean(x * x, axis=-1, keepdims=True) + RMS_EPS)
        nrm = x * r
        outs[0][...] += jnp.sum(total * nrm, axis=0, keepdims=True)
        dn = total * ins[2][...]
        o_ref[...] = ins[1][...] + r * (dn - nrm * jnp.mean(dn * nrm, axis=-1, keepdims=True))

    return _Epilogue(operands, in_specs, [jax.ShapeDtypeStruct((1, n), F32)], lambda tm: [_one_row(n)], fn, init)


def _merge_bwd_epilogue(p7, br_gla, br_conf, d):
    t = p7.shape[0]

    def in_specs(tm):
        return [pl.BlockSpec((tm, d), lambda i, j, kk: (i, 5)), pl.BlockSpec((tm, d), lambda i, j, kk: (i, 6)),
                _tile_rows(d)(tm), _tile_rows(d)(tm)]

    def fn(total, i, o_ref, ins, outs):
        sg = _sigmoid(ins[0][...].astype(F32))
        sc = _sigmoid(ins[1][...].astype(F32))
        o_ref[...] = (total * sg).astype(BF16)
        outs[0][...] = (total * sc).astype(BF16)
        outs[1][:, 0:d] = (total * ins[2][...].astype(F32) * sg * (1.0 - sg)).astype(BF16)
        outs[1][:, d:2 * d] = (total * ins[3][...].astype(F32) * sc * (1.0 - sc)).astype(BF16)

    return _Epilogue([p7, p7, br_gla, br_conf], in_specs,
                     [jax.ShapeDtypeStruct((t, d), BF16), jax.ShapeDtypeStruct((t, DP_BLOCKS * d), BF16)],
                     lambda tm: [_tile_rows(d)(tm), pl.BlockSpec((tm, 2 * d), lambda i, j, kk: (i, 0))], fn)


def _loss_epilogue(target, gf):
    n = gf.shape[1]

    def in_specs(tm):
        nb = tm // ROW_TILE
        return [pl.BlockSpec((ROW_TILE, n), (lambda s: lambda i, j, kk: (jnp.maximum(i * nb + s - 1, 0), 0))(s))
                for s in range(nb)] + [_one_row(n)]

    def init(outs):
        outs[0][...] = jnp.zeros_like(outs[0])
        outs[1][...] = jnp.zeros_like(outs[1])

    def fn(total, i, o_ref, ins, outs):
        nb = len(ins) - 1
        g = ins[nb][...]
        for s in range(nb):
            rows = slice(s * ROW_TILE, (s + 1) * ROW_TILE)
            live = jnp.where(i * nb + s > 0, 1.0, 0.0)
            h = total[rows]
            r = lax.rsqrt(jnp.mean(h * h, axis=-1, keepdims=True) + RMS_EPS)
            nrm = h * r
            err = (nrm * g - ins[s][...]) * live
            outs[0][...] += 0.5 * jnp.sum(jnp.mean(err * err, axis=-1, keepdims=True), axis=0, keepdims=True)
            dout = err * (1.0 / n)
            outs[1][...] += jnp.sum(dout * nrm, axis=0, keepdims=True)
            dn = dout * g
            o_ref[rows, :] = r * (dn - nrm * jnp.mean(dn * nrm, axis=-1, keepdims=True))

    nb_ops = lambda tm: [target] * (tm // ROW_TILE) + [gf]
    return _Epilogue(nb_ops, in_specs, [jax.ShapeDtypeStruct((1, 1), F32), jax.ShapeDtypeStruct((1, n), F32)],
                     lambda tm: [pl.BlockSpec((1, 1), lambda i, j, kk: (0, 0)), _one_row(n)], fn, init)


def _row_mask(tile_index, rows):
    r = tile_index * rows + lax.broadcasted_iota(jnp.int32, (rows, 1), 0)
    return (r >= PAD).astype(F32)


def _prep(x, meta, g1, hosted):
    s, d = x.shape
    t = HEAD_ROWS + s
    tm = ROW_TILE
    nt = t // tm
    n_hin, n_hout = len(hosted.operands), len(hosted.out_shapes)

    def body(*refs):
        x_ref, meta_ref, g_ref = refs[:3]
        h_ins = refs[3:3 + n_hin]
        h_ref, u_ref = refs[3 + n_hin:5 + n_hin]
        h_outs = refs[5 + n_hin:5 + n_hin + n_hout]
        h_sems = refs[5 + n_hin + n_hout:]
        i = pl.program_id(0)

        @pl.when(i == 0)
        def _():
            hosted.start(h_ins, h_outs, h_sems)
            h_ref[0:PAD, :] = jnp.zeros((PAD, d), F32)
            h_ref[PAD:HEAD_ROWS, :] = meta_ref[...]

        @pl.when(i > 0)
        def _():
            h_ref[...] = x_ref[...]

        h = h_ref[...]
        r = lax.rsqrt(jnp.mean(h * h, axis=-1, keepdims=True) + RMS_EPS)
        u_ref[...] = (h * r * g_ref[...]).astype(BF16)

        @pl.when(i == nt - 1)
        def _():
            hosted.finish(h_ins, h_outs, h_sems)

    outs = pl.pallas_call(
        body, name="prep", grid=(nt,),
        in_specs=[pl.BlockSpec((tm, d), lambda i: (jnp.maximum(i - 1, 0), 0)),
                  pl.BlockSpec((N_META, d), lambda i: (0, 0)),
                  pl.BlockSpec((1, d), lambda i: (0, 0))] + [HBM_SPEC] * n_hin,
        out_specs=[pl.BlockSpec((tm, d), lambda i: (i, 0)), pl.BlockSpec((tm, d), lambda i: (i, 0))]
        + [HBM_SPEC] * n_hout,
        out_shape=[jax.ShapeDtypeStruct((t, d), F32), jax.ShapeDtypeStruct((t, d), BF16)] + list(hosted.out_shapes),
        scratch_shapes=list(hosted.sem_shapes),
        compiler_params=_cparams(("arbitrary",)),
    )(x, meta, g1, *hosted.operands)
    return outs[0], outs[1], list(outs[2:])


def _tri(n, upper=False):
    r = lax.broadcasted_iota(jnp.int32, (n, n), 0)
    c = lax.broadcasted_iota(jnp.int32, (n, n), 1)
    return (r <= c) if upper else (r >= c)


_NT = (((1,), (1,)), ((), ()))
_TN = (((0,), (0,)), ((), ()))


def _dot(a, b):
    return jnp.dot(a, b, preferred_element_type=F32)


def _dot_nt(a, b):
    return lax.dot_general(a, b, _NT, preferred_element_type=F32)


def _dot_tn(a, b):
    return lax.dot_general(a, b, _TN, preferred_element_type=F32)


def _chunk_decays(la, tri_f32):
    b = jnp.dot(tri_f32, la, preferred_element_type=F32, precision=lax.Precision.HIGHEST)
    bl = b[CHUNK - 1:CHUNK, :]
    return jnp.exp(b), jnp.exp(bl - b), jnp.exp(-bl), jnp.exp(bl)


def _gla_fwd(p7, alr, wau, b_alpha, gla_g, d):
    t = p7.shape[0]
    dk_all = d // 2
    dkh, dvh = dk_all // HEADS, d // HEADS
    cb = ROW_TILE
    ncb = cb // CHUNK
    scale = dkh ** -0.5

    def body(qk_ref, v_ref, r_ref, alr_ref, wau_ref, ba_ref, g_ref, o_ref, y_ref, s_ref, la_ref, st_scr):
        @pl.when(pl.program_id(0) == 0)
        def _():
            st_scr[...] = jnp.zeros_like(st_scr)

        z = jnp.dot(alr_ref[...].astype(BF16), wau_ref[...].astype(BF16), preferred_element_type=F32) + ba_ref[...]
        ls = jnp.minimum(z, 0.0) - jnp.log(1.0 + jnp.exp(-jnp.abs(z)))
        la_ref[...] = ls * (1.0 / GATE_TAU) * _row_mask(pl.program_id(0), cb)
        tri = _tri(CHUNK)
        tri_f = tri.astype(F32)
        for c in range(ncb):
            rows = slice(c * CHUNK, (c + 1) * CHUNK)
            eb, ekl, ebl_inv, gam = _chunk_decays(la_ref[rows, :], tri_f)
            for h in range(HEADS):
                ks = slice(h * dkh, (h + 1) * dkh)
                vs = slice(h * dvh, (h + 1) * dvh)
                q = qk_ref[rows, ks].astype(F32) * scale
                k = qk_ref[rows, dk_all + h * dkh:dk_all + (h + 1) * dkh].astype(F32)
                v = v_ref[rows, vs].astype(BF16)
                qb = q * eb[:, ks]
                kh = (k * ekl[:, ks]).astype(BF16)
                qc = (qb * ebl_inv[:, ks]).astype(BF16)
                a = jnp.where(tri, _dot_nt(qc, kh), 0.0)
                st = st_scr[h]
                st_b = st.astype(BF16)
                s_ref[c, h] = st_b
                o = _dot_nt(qb.astype(BF16), st_b) + _dot(a.astype(BF16), v)
                st_scr[h] = st * gam[:, ks] + _dot_tn(v, kh)
                o_ref[rows, vs] = o
                rr = lax.rsqrt(jnp.mean(o * o, axis=-1, keepdims=True) + RMS_EPS)
                r = r_ref[rows, vs].astype(F32)
                y_ref[rows, vs] = (o * rr * g_ref[:, vs] * (r * _sigmoid(r))).astype(BF16)

    return pl.pallas_call(
        body, name="gla_fwd", grid=(t // cb,),
        in_specs=[pl.BlockSpec((cb, d), lambda i: (i, 0)),
                  pl.BlockSpec((cb, d), lambda i: (i, 1)),
                  pl.BlockSpec((cb, d), lambda i: (i, 2)),
                  pl.BlockSpec((cb, RANK_PAD), lambda i: (i, 0)),
                  pl.BlockSpec((RANK_PAD, dk_all), lambda i: (0, 0)),
                  pl.BlockSpec((1, dk_all), lambda i: (0, 0)),
                  pl.BlockSpec((1, d), lambda i: (0, 0))],
        out_specs=[pl.BlockSpec((cb, d), lambda i: (i, 0)),
                   pl.BlockSpec((cb, d), lambda i: (i, 0)),
                   pl.BlockSpec((ncb, HEADS, dvh, dkh), lambda i: (i, 0, 0, 0)),
                   pl.BlockSpec((cb, dk_all), lambda i: (i, 0))],
        out_shape=[jax.ShapeDtypeStruct((t, d), F32), jax.ShapeDtypeStruct((t, d), BF16),
                   jax.ShapeDtypeStruct((t // CHUNK, HEADS, dvh, dkh), BF16),
                   jax.ShapeDtypeStruct((t, dk_all), F32)],
        scratch_shapes=[pltpu.VMEM((HEADS, dvh, dkh), F32)],
        compiler_params=_cparams(("arbitrary",)),
    )(p7, p7, p7, alr, wau, b_alpha, gla_g)


SUBLANES = 8


def _tap_phases(n_taps, first):
    phases = {}
    for j in range(n_taps):
        e = first + j
        phases.setdefault(e % SUBLANES, []).append((j, e - e % SUBLANES))
    return phases


CONV_ROWS = 64


def _shifted_windows(ext_ref, shf_ref, phases, rows, ls):
    for p, taps in phases.items():
        if p:
            span = max(off for _, off in taps) + rows
            shf_ref[p, 0:span, :] = ext_ref[p:p + span, ls]

    def window(p, start, n):
        return shf_ref[p, start:start + n, :] if p else ext_ref[start:start + n, ls]

    return window


def _conv_strip(ext_ref, shf_ref, w_ref, n_taps, first, rows, ls, emit, reverse=False):
    phases = _tap_phases(n_taps, first)
    window = _shifted_windows(ext_ref, shf_ref, phases, rows, ls)
    for r0 in range(0, rows, CONV_ROWS):
        acc = None
        for p, taps in phases.items():
            for j, off in taps:
                wj = w_ref[(n_taps - 1 - j) if reverse else j, ls]
                term = window(p, off + r0, CONV_ROWS) * wj
                acc = term if acc is None else acc + term
        emit(r0, acc)


def _corr_strip(dl_ref, ext_ref, shf_ref, acc_ref, n_taps, first, rows, ls):
    phases = _tap_phases(n_taps, first)
    window = _shifted_windows(ext_ref, shf_ref, phases, rows, ls)
    for r0 in range(0, rows, CONV_ROWS):
        dl = dl_ref[r0:r0 + CONV_ROWS, ls]
        for p, taps in phases.items():
            for j, off in taps:
                prod = dl * window(p, off + r0, CONV_ROWS)
                acc_ref[SUBLANES * j:SUBLANES * (j + 1), ls] += jnp.sum(
                    prod.reshape(CONV_ROWS // SUBLANES, SUBLANES, prod.shape[-1]), axis=0)


def _conf_fwd(p7, conv_w, conv_b, ln_g, ln_b, d):
    t = p7.shape[0]
    tm = ROW_TILE

    def body(c1_ref, c2_ref, w_ref, b_ref, g_ref, bb_ref, cc_ref, sc_ref, ext, shf):
        @pl.when(pl.program_id(0) == 0)
        def _():
            ext[0:HALO, :] = jnp.zeros((HALO, d), F32)

        ext[HALO:HALO + tm, :] = c1_ref[...].astype(F32) * _sigmoid(c2_ref[...].astype(F32))
        for s0 in range(0, d, LANES):
            ls = slice(s0, s0 + LANES)

            def emit(r0, acc, ls=ls):
                cc_ref[r0:r0 + CONV_ROWS, ls] = acc + b_ref[:, ls]

            _conv_strip(ext, shf, w_ref, CONF_K, HALO - (CONF_K - 1), tm, ls, emit)
        ext[0:HALO, :] = ext[tm:tm + HALO, :]
        g = g_ref[...]
        bb = bb_ref[...]
        rows_per_step = 2 * SUBLANES

        def rows_step(k, carry):
            rs = pl.ds(pl.multiple_of(k * rows_per_step, rows_per_step), rows_per_step)
            cc = cc_ref[rs, :]
            xc = cc - jnp.mean(cc, axis=-1, keepdims=True)
            rstd = lax.rsqrt(jnp.mean(xc * xc, axis=-1, keepdims=True) + LN_EPS)
            cn = xc * rstd * g + bb
            sc_ref[rs, :] = (cn * _sigmoid(cn)).astype(BF16)
            return carry

        lax.fori_loop(0, tm // rows_per_step, rows_step, 0, unroll=4)

    vec = pl.BlockSpec((1, d), lambda i: (0, 0))
    return pl.pallas_call(
        body, name="conf_fwd", grid=(t // tm,),
        in_specs=[pl.BlockSpec((tm, d), lambda i: (i, 3)), pl.BlockSpec((tm, d), lambda i: (i, 4)),
                  pl.BlockSpec((CONF_K, d), lambda i: (0, 0)), vec, vec, vec],
        out_specs=[pl.BlockSpec((tm, d), lambda i: (i, 0)), pl.BlockSpec((tm, d), lambda i: (i, 0))],
        out_shape=[jax.ShapeDtypeStruct((t, d), F32), jax.ShapeDtypeStruct((t, d), BF16)],
        scratch_shapes=[pltpu.VMEM((tm + HALO, d), F32), pltpu.VMEM((SUBLANES, tm + HALO, LANES), F32)],
        compiler_params=_cparams(("arbitrary",)),
    )(p7, p7, conv_w, conv_b, ln_g, ln_b)


def _merge(p7, br_gla, br_conf, d):
    t = p7.shape[0]
    tm = ROW_TILE

    def body(gg_ref, gc_ref, a_ref, b_ref, o_ref):
        o_ref[...] = (_sigmoid(gg_ref[...].astype(F32)) * a_ref[...].astype(F32)
                      + _sigmoid(gc_ref[...].astype(F32)) * b_ref[...].astype(F32)).astype(BF16)

    row = pl.BlockSpec((tm, d), lambda i: (i, 0))
    return pl.pallas_call(
        body, name="merge", grid=(t // tm,),
        in_specs=[pl.BlockSpec((tm, d), lambda i: (i, 5)), pl.BlockSpec((tm, d), lambda i: (i, 6)), row, row],
        out_specs=row, out_shape=jax.ShapeDtypeStruct((t, d), BF16),
        compiler_params=_cparams(("parallel",)),
    )(p7, p7, br_gla, br_conf)


def _ffn_mid(up, w, b, dff):
    t = up.shape[0]
    tm = ROW_TILE
    hal = 8

    def body(a_ref, bv_ref, w_ref, b_ref, y_ref, ext, shf):
        i = pl.program_id(0)

        @pl.when(i == 0)
        def _():
            ext[0:hal, :] = jnp.zeros((hal, dff), F32)

        ext[hal:hal + tm, :] = a_ref[...].astype(F32) * _row_mask(i, tm)
        for s0 in range(0, dff, LANES):
            ls = slice(s0, s0 + LANES)

            def emit(r0, acc, ls=ls):
                rb = slice(r0, r0 + CONV_ROWS)
                ac = acc + b_ref[:, ls]
                y_ref[rb, ls] = (ac * _sigmoid(ac) * bv_ref[rb, ls].astype(F32)).astype(BF16)

            _conv_strip(ext, shf, w_ref, FFN_K, hal - (FFN_K - 1), tm, ls, emit)
        ext[0:hal, :] = ext[tm:tm + hal, :]

    return pl.pallas_call(
        body, name="ffn_mid", grid=(t // tm,),
        in_specs=[pl.BlockSpec((tm, dff), lambda i: (i, 0)), pl.BlockSpec((tm, dff), lambda i: (i, 1)),
                  pl.BlockSpec((FFN_K, dff), lambda i: (0, 0)), pl.BlockSpec((1, dff), lambda i: (0, 0))],
        out_specs=pl.BlockSpec((tm, dff), lambda i: (i, 0)), out_shape=jax.ShapeDtypeStruct((t, dff), BF16),
        scratch_shapes=[pltpu.VMEM((tm + hal, dff), F32), pltpu.VMEM((SUBLANES, tm + hal, LANES), F32)],
        compiler_params=_cparams(("arbitrary",)),
    )(up, up, w, b)


def _dsilu(x, sig):
    return sig * (1.0 + x * (1.0 - sig))


def _ffn_mid_bwd(up, dy, w, b, dff):
    t = up.shape[0]
    tm = ROW_TILE
    hal = 8
    prev_rows = 2 * SUBLANES
    nt = t // tm

    def body(a_ref, ap_ref, bv_ref, dy_ref, w_ref, b_ref, dup_ref, dw_ref, db_ref, ext, dext, dw_acc, db_acc, shf):
        i = pl.program_id(0)
        tile = nt - 1 - i

        @pl.when(i == 0)
        def _():
            dext[tm:tm + hal, :] = jnp.zeros((hal, dff), F32)
            dw_acc[...] = jnp.zeros_like(dw_acc)
            db_acc[...] = jnp.zeros_like(db_acc)

        ext[0:hal, :] = ap_ref[prev_rows - hal:prev_rows, :].astype(F32) * jnp.where(tile > 0, 1.0, 0.0)
        ext[hal:hal + tm, :] = a_ref[...].astype(F32) * _row_mask(tile, tm)
        first = hal - (FFN_K - 1)
        for s0 in range(0, dff, LANES):
            ls = slice(s0, s0 + LANES)

            def emit_fwd(r0, acc, ls=ls, s0=s0):
                rb = slice(r0, r0 + CONV_ROWS)
                ac = acc + b_ref[:, ls]
                sig = _sigmoid(ac)
                dyv = dy_ref[rb, ls].astype(F32)
                dup_ref[rb, dff + s0:dff + s0 + LANES] = (dyv * ac * sig).astype(BF16)
                dac = dyv * bv_ref[rb, ls].astype(F32) * _dsilu(ac, sig)
                dext[rb, ls] = dac
                db_acc[:, ls] += jnp.sum(dac.reshape(CONV_ROWS // SUBLANES, SUBLANES, LANES), axis=0)

            _conv_strip(ext, shf, w_ref, FFN_K, first, tm, ls, emit_fwd)
            _corr_strip(dext, ext, shf, dw_acc, FFN_K, first, tm, ls)

            def emit_bwd(r0, da, ls=ls):
                rb = slice(r0, r0 + CONV_ROWS)
                mask = ((tile * tm + r0 + lax.broadcasted_iota(jnp.int32, (CONV_ROWS, 1), 0)) >= PAD).astype(F32)
                dup_ref[rb, ls] = (da * mask).astype(BF16)

            _conv_strip(dext, shf, w_ref, FFN_K, 0, tm, ls, emit_bwd, reverse=True)
        dext[tm:tm + hal, :] = dext[0:hal, :]

        @pl.when(i == nt - 1)
        def _():
            db_ref[...] = jnp.sum(db_acc[...], axis=0, keepdims=True)
            for j in range(FFN_K):
                dw_ref[j:j + 1, :] = jnp.sum(dw_acc[SUBLANES * j:SUBLANES * (j + 1), :], axis=0, keepdims=True)

    rev = lambda i: (nt - 1 - i, 0)
    return pl.pallas_call(
        body, name="ffn_mid_bwd", grid=(nt,),
        in_specs=[pl.BlockSpec((tm, dff), rev),
                  pl.BlockSpec((prev_rows, dff), lambda i: (jnp.maximum((nt - 1 - i) * (tm // prev_rows) - 1, 0), 0)),
                  pl.BlockSpec((tm, dff), lambda i: (nt - 1 - i, 1)),
                  pl.BlockSpec((tm, dff), rev),
                  pl.BlockSpec((FFN_K, dff), lambda i: (0, 0)), pl.BlockSpec((1, dff), lambda i: (0, 0))],
        out_specs=[pl.BlockSpec((tm, 2 * dff), rev),
                   pl.BlockSpec((FFN_K, dff), lambda i: (0, 0)), pl.BlockSpec((1, dff), lambda i: (0, 0))],
        out_shape=[jax.ShapeDtypeStruct((t, 2 * dff), BF16),
                   jax.ShapeDtypeStruct((FFN_K, dff), F32), jax.ShapeDtypeStruct((1, dff), F32)],
        scratch_shapes=[pltpu.VMEM((tm + hal, dff), F32), pltpu.VMEM((tm + hal, dff), F32),
                        pltpu.VMEM((SUBLANES * FFN_K, dff), F32), pltpu.VMEM((SUBLANES, dff), F32),
                        pltpu.VMEM((SUBLANES, tm + hal, LANES), F32)],
        compiler_params=_cparams(("arbitrary",)),
    )(up, up, up, dy, w, b)


DP_BLOCKS = 8
ANY_SPEC = pl.BlockSpec(memory_space=pl.ANY)


def _conf_bwd(p7, cc, dsc, conv_w, ln_g, ln_b, dp, d):
    t = p7.shape[0]
    tm = ROW_TILE
    nt = t // tm

    def body(c1_ref, c2_ref, c1p_ref, c2p_ref, cc_ref, dsc_ref, w_ref, g_ref, bb_ref, dp_in,
             dp_ref, dw_ref, db_ref, dg_ref, dbb_ref, ext, dext, dw_acc, shf):
        i = pl.program_id(0)
        tile = nt - 1 - i

        @pl.when(i == 0)
        def _():
            dext[tm:tm + HALO, :] = jnp.zeros((HALO, d), F32)
            dw_acc[...] = jnp.zeros_like(dw_acc)
            db_ref[...] = jnp.zeros_like(db_ref)
            dg_ref[...] = jnp.zeros_like(dg_ref)
            dbb_ref[...] = jnp.zeros_like(dbb_ref)

        ext[0:HALO, :] = (c1p_ref[...].astype(F32) * _sigmoid(c2p_ref[...].astype(F32))
                          * jnp.where(tile > 0, 1.0, 0.0))
        g = g_ref[...]
        bb = bb_ref[...]

        groups = 4
        grp = 2 * SUBLANES

        def fold(a):
            return a[:SUBLANES] + a[SUBLANES:]

        def rows_step(k, sums):
            sdg, sdbb, sdb = sums
            for u in range(groups):
                r0 = (k * groups + u) * grp
                rs = pl.ds(pl.multiple_of(r0, grp), grp)
                ext[pl.ds(pl.multiple_of(HALO + r0, grp), grp), :] = (
                    c1_ref[rs, :].astype(F32) * _sigmoid(c2_ref[rs, :].astype(F32)))
                cc_ = cc_ref[rs, :]
                xc = cc_ - jnp.mean(cc_, axis=-1, keepdims=True)
                rstd = lax.rsqrt(jnp.mean(xc * xc, axis=-1, keepdims=True) + LN_EPS)
                xh = xc * rstd
                cn = xh * g + bb
                dcn = dsc_ref[rs, :].astype(F32) * _dsilu(cn, _sigmoid(cn))
                dxh = dcn * g
                dcc = rstd * (dxh - jnp.mean(dxh, axis=-1, keepdims=True)
                              - xh * jnp.mean(dxh * xh, axis=-1, keepdims=True))
                dext[rs, :] = dcc
                sdg, sdbb, sdb = sdg + fold(dcn * xh), sdbb + fold(dcn), sdb + fold(dcc)
            return sdg, sdbb, sdb

        zero = jnp.zeros((SUBLANES, d), F32)
        sdg, sdbb, sdb = lax.fori_loop(0, tm // (groups * grp), rows_step, (zero, zero, zero))
        dg_ref[...] += jnp.sum(sdg, axis=0, keepdims=True)
        dbb_ref[...] += jnp.sum(sdbb, axis=0, keepdims=True)
        db_ref[...] += jnp.sum(sdb, axis=0, keepdims=True)
        for s0 in range(0, d, LANES):
            ls = slice(s0, s0 + LANES)
            _corr_strip(dext, ext, shf, dw_acc, CONF_K, HALO - (CONF_K - 1), tm, ls)

            def emit(r0, dc, ls=ls):
                rb = slice(r0, r0 + CONV_ROWS)
                s2l = _sigmoid(c2_ref[rb, ls].astype(F32))
                dp_ref[rb, ls] = (dc * s2l).astype(BF16)
                dp_ref[rb, d + ls.start:d + ls.stop] = (
                    dc * c1_ref[rb, ls].astype(F32) * s2l * (1.0 - s2l)).astype(BF16)

            _conv_strip(dext, shf, w_ref, CONF_K, 0, tm, ls, emit, reverse=True)
        dext[tm:tm + HALO, :] = dext[0:HALO, :]

        @pl.when(i == nt - 1)
        def _():
            for j in range(CONF_K):
                dw_ref[j:j + 1, :] = jnp.sum(dw_acc[SUBLANES * j:SUBLANES * (j + 1), :], axis=0, keepdims=True)

    rev = lambda i: (nt - 1 - i, 0)
    prev = lambda col: (lambda i: (jnp.maximum((nt - 1 - i) * (tm // HALO) - 1, 0), col))
    vec = pl.BlockSpec((1, d), lambda i: (0, 0))
    return pl.pallas_call(
        body, name="conf_bwd", grid=(nt,),
        in_specs=[pl.BlockSpec((tm, d), lambda i: (nt - 1 - i, 3)), pl.BlockSpec((tm, d), lambda i: (nt - 1 - i, 4)),
                  pl.BlockSpec((HALO, d), prev(3)), pl.BlockSpec((HALO, d), prev(4)),
                  pl.BlockSpec((tm, d), rev), pl.BlockSpec((tm, d), rev),
                  pl.BlockSpec((CONF_K, d), lambda i: (0, 0)), vec, vec, ANY_SPEC],
        out_specs=[pl.BlockSpec((tm, 2 * d), lambda i: (nt - 1 - i, 1)),
                   pl.BlockSpec((CONF_K, d), lambda i: (0, 0)), vec, vec, vec],
        out_shape=[jax.ShapeDtypeStruct(dp.shape, dp.dtype),
                   jax.ShapeDtypeStruct((CONF_K, d), F32)] + [jax.ShapeDtypeStruct((1, d), F32)] * 3,
        input_output_aliases={9: 0},
        scratch_shapes=[pltpu.VMEM((tm + HALO, d), F32), pltpu.VMEM((tm + HALO, d), F32),
                        pltpu.VMEM((SUBLANES * CONF_K, d), F32), pltpu.VMEM((SUBLANES, tm + HALO, LANES), F32)],
        compiler_params=_cparams(("arbitrary",)),
    )(p7, p7, p7, p7, cc, dsc, conv_w, ln_g, ln_b, dp)


def _gla_bwd(p7, log_a, alr, wau, b_alpha, gla_g, o, states, dy, dp, d):
    t = p7.shape[0]
    dk_all = d // 2
    dkh, dvh = dk_all // HEADS, d // HEADS
    cb = ROW_TILE
    ncb = cb // CHUNK
    nb = t // cb
    scale = dkh ** -0.5

    def body(qk_ref, v_ref, r_ref, la_ref, alr_ref, wau_ref, ba_ref, g_ref, o_ref, s_ref, dy_ref, dp_in,
             dp_ref, dz_ref, dg_ref, dba_ref, dst_scr, dla_scr):
        i = pl.program_id(0)
        blk = nb - 1 - i

        @pl.when(i == 0)
        def _():
            dst_scr[...] = jnp.zeros_like(dst_scr)
            dg_ref[...] = jnp.zeros_like(dg_ref)
            dba_ref[...] = jnp.zeros_like(dba_ref)

        dp_ref[:, 3 * d:4 * d] = jnp.zeros((cb, d), BF16)

        tri = _tri(CHUNK)
        tri_f = tri.astype(F32)
        triu_f = _tri(CHUNK, upper=True).astype(F32)
        for c in reversed(range(ncb)):
            rows = slice(c * CHUNK, (c + 1) * CHUNK)
            eb, ekl, ebl_inv, gam = _chunk_decays(la_ref[rows, :], tri_f)
            for h in range(HEADS):
                ks = slice(h * dkh, (h + 1) * dkh)
                kcols = slice(dk_all + h * dkh, dk_all + (h + 1) * dkh)
                vs = slice(h * dvh, (h + 1) * dvh)
                q = qk_ref[rows, ks].astype(F32) * scale
                k = qk_ref[rows, kcols].astype(F32)
                v = v_ref[rows, vs].astype(BF16)
                ebh, eklh, eih, gamh = eb[:, ks], ekl[:, ks], ebl_inv[:, ks], gam[:, ks]
                qb = q * ebh
                kh = k * eklh
                qc = qb * eih
                qb_b, kh_b, qc_b = qb.astype(BF16), kh.astype(BF16), qc.astype(BF16)
                ov = o_ref[rows, vs]
                r = r_ref[rows, vs].astype(F32)
                dyv = dy_ref[rows, vs].astype(F32)
                sig = _sigmoid(r)
                rr = lax.rsqrt(jnp.mean(ov * ov, axis=-1, keepdims=True) + RMS_EPS)
                n = ov * rr
                g = g_ref[:, vs]
                dp_ref[rows, 2 * d + h * dvh:2 * d + (h + 1) * dvh] = (dyv * n * g * _dsilu(r, sig)).astype(BF16)
                don = dyv * (r * sig)
                dg_ref[:, vs] += jnp.sum(don * n, axis=0, keepdims=True)
                dn = don * g
                do = (rr * (dn - n * jnp.mean(dn * n, axis=-1, keepdims=True))).astype(BF16)
                st_b = s_ref[c, h]
                a = jnp.where(tri, _dot_nt(qc_b, kh_b), 0.0).astype(BF16)
                da = jnp.where(tri, _dot_nt(do, v), 0.0).astype(BF16)
                dst = dst_scr[h]
                dst_b = dst.astype(BF16)
                dp_ref[rows, d + h * dvh:d + (h + 1) * dvh] = (_dot_tn(a, do) + _dot_nt(kh_b, dst_b)).astype(BF16)
                dqb = _dot(do, st_b)
                dqc = _dot(da, kh_b)
                dkh_ = _dot_tn(da, qc_b) + _dot(v, dst_b)
                dgam = jnp.sum(st_b.astype(F32) * dst, axis=0, keepdims=True)
                dst_scr[h] = dst * gamh + _dot_tn(do, qb_b)
                dp_ref[rows, ks] = ((dqb * ebh + dqc * (ebh * eih)) * scale).astype(BF16)
                dp_ref[rows, kcols] = (dkh_ * eklh).astype(BF16)
                qq = dqc * qc
                kk = dkh_ * kh
                db = dqb * qb + qq - kk
                dbl = jnp.sum(kk - qq, axis=0, keepdims=True) + dgam * gamh
                dla_scr[rows, ks] = jnp.dot(triu_f, db, preferred_element_type=F32,
                                            precision=lax.Precision.HIGHEST) + dbl
        z = jnp.dot(alr_ref[...].astype(BF16), wau_ref[...].astype(BF16), preferred_element_type=F32) + ba_ref[...]
        dz = dla_scr[...] * (1.0 / GATE_TAU) * _sigmoid(-z) * _row_mask(blk, cb)
        dba_ref[...] += jnp.sum(dz, axis=0, keepdims=True)
        dz_ref[...] = dz.astype(BF16)

    rev = lambda i: (nb - 1 - i, 0)
    row = pl.BlockSpec((cb, d), rev)
    return pl.pallas_call(
        body, name="gla_bwd", grid=(nb,),
        in_specs=[row, pl.BlockSpec((cb, d), lambda i: (nb - 1 - i, 1)), pl.BlockSpec((cb, d), lambda i: (nb - 1 - i, 2)),
                  pl.BlockSpec((cb, dk_all), rev), pl.BlockSpec((cb, RANK_PAD), rev),
                  pl.BlockSpec((RANK_PAD, dk_all), lambda i: (0, 0)), pl.BlockSpec((1, dk_all), lambda i: (0, 0)),
                  pl.BlockSpec((1, d), lambda i: (0, 0)), row,
                  pl.BlockSpec((ncb, HEADS, dvh, dkh), lambda i: (nb - 1 - i, 0, 0, 0)), row, ANY_SPEC],
        out_specs=[pl.BlockSpec((cb, 4 * d), lambda i: (nb - 1 - i, 1)), pl.BlockSpec((cb, dk_all), rev),
                   pl.BlockSpec((1, d), lambda i: (0, 0)), pl.BlockSpec((1, dk_all), lambda i: (0, 0))],
        out_shape=[jax.ShapeDtypeStruct(dp.shape, dp.dtype), jax.ShapeDtypeStruct((t, dk_all), BF16),
                   jax.ShapeDtypeStruct((1, d), F32), jax.ShapeDtypeStruct((1, dk_all), F32)],
        input_output_aliases={11: 0},
        scratch_shapes=[pltpu.VMEM((HEADS, dvh, dkh), F32), pltpu.VMEM((cb, dk_all), F32)],
        compiler_params=_cparams(("arbitrary",)),
    )(p7, p7, p7, log_a, alr, wau, b_alpha, gla_g, o, states, dy, dp)


REST = ("w_up", "w_down", "w_gla_o", "w_conf_o", "w_out")


def _chip_partials(grads, names, tag):
    place = jnp.stack([2 * lax.axis_index("x") + lax.axis_index("y"), lax.axis_index("c")])
    from_sibling = _swap_core_halves(grads, "swap_core_halves_" + tag)
    both = [_add_core_halves(place, g, s, "add_core_halves_" + n) for n, g, s in zip(names, grads, from_sibling)]
    return [b[0] for b in both], [b[1] for b in both]


def _local_step(x, target, w, shard):
    s, d = x.shape
    dk_all = d // 2
    wau = jnp.pad(w["w_alpha_up"], ((0, RANK_PAD - GLA_RANK), (0, 0)))

    h0, u1, gathered = _prep(x, w["meta_tokens"], w["norm_mix_g"], _gather_plan([shard["w_in"]]))
    w_in = _from_chip_major(gathered[0])
    lo, hi = 3 * d, 3 * d + GLA_RANK
    wq = jnp.concatenate([w_in[:, :lo], w_in[:, hi:]], axis=1)
    w_alr = jnp.pad(w_in[:, lo:hi], ((0, 0), (0, RANK_PAD - GLA_RANK)))
    shards = [shard[n] for n in REST]
    p7, gathered, _ = _matmul(u1, wq, dims="nn", name="proj", out_dtype=BF16, hosted=_gather_plan(shards))
    w = dict(w)
    for n, slabs in zip(REST, gathered):
        w[n] = slabs if n == "w_up" else slabs.reshape(-1, slabs.shape[-1])
    dff = w["w_down"].shape[0]
    alr = _matmul(u1, w_alr, dims="nn", name="proj_alr")
    o, y_gla, states, log_a = _gla_fwd(p7, alr, wau, w["b_alpha"], w["gla_norm_g"], d)
    br_gla = _matmul(y_gla, w["w_gla_o"], dims="nn", name="gla_out", out_dtype=BF16)
    cc, s_c = _conf_fwd(p7, w["conf_dw_w"], w["conf_dw_b"], w["conf_ln_g"], w["conf_ln_b"], d)
    br_conf = _matmul(s_c, w["w_conf_o"], dims="nn", name="conf_out", out_dtype=BF16)
    merged = _merge(p7, br_gla, br_conf, d)
    h1, _, (u2,) = _matmul(merged, w["w_out"], dims="nn", name="mix_out", add=h0,
                           epilogue=_rms_fwd_epilogue(h0.shape[0], w["norm_ffn_g"]))
    up = _matmul(u2, w["w_up"], dims="nn", name="ffn_up", out_dtype=BF16, chips="b")
    y = _ffn_mid(up, w["ffn_dw_w"], w["ffn_dw_b"], dff)
    dh2, _, (loss, d_gf) = _matmul(y, w["w_down"], dims="nn", name="ffn_down", add=h1,
                                   epilogue=_loss_epilogue(target, w["final_norm_g"]))

    g = {"final_norm_g": d_gf}
    dy = _matmul(dh2, w["w_down"], dims="nt", name="d_ffn_y", out_dtype=BF16)
    g["w_down"] = _matmul(y, dh2, dims="tn", name="dw_down", out_dtype=BF16)
    dup, g["ffn_dw_w"], g["ffn_dw_b"] = _ffn_mid_bwd(up, dy, w["ffn_dw_w"], w["ffn_dw_b"], dff)
    dh1, _, (g["norm_ffn_g"],) = _matmul(dup, w["w_up"], dims="nt", name="d_u2", chips="b",
                                         epilogue=_rms_bwd_epilogue(h1, w["norm_ffn_g"], dh2))
    g["w_up"] = _matmul(u2, dup, dims="tn", name="dw_up", out_dtype=BF16, chips="out")
    d_br_gla, _, (d_br_conf, dp) = _matmul(dh1, w["w_out"], dims="nt", name="d_merged", out_dtype=BF16,
                                           epilogue=_merge_bwd_epilogue(p7, br_gla, br_conf, d))
    g["w_out"] = _matmul(merged, dh1, dims="tn", name="dw_out", out_dtype=BF16)
    dsc = _matmul(d_br_conf, w["w_conf_o"], dims="nt", name="d_conf_s", out_dtype=BF16)
    g["w_conf_o"] = _matmul(s_c, d_br_conf, dims="tn", name="dw_conf_o", out_dtype=BF16)
    dp, g["conf_dw_w"], g["conf_dw_b"], g["conf_ln_g"], g["conf_ln_b"] = _conf_bwd(
        p7, cc, dsc, w["conf_dw_w"], w["conf_ln_g"], w["conf_ln_b"], dp, d)
    dyg = _matmul(d_br_gla, w["w_gla_o"], dims="nt", name="d_gla_y", out_dtype=BF16)
    g["w_gla_o"] = _matmul(y_gla, d_br_gla, dims="tn", name="dw_gla_o", out_dtype=BF16)
    dp, dz, g["gla_norm_g"], g["b_alpha"] = _gla_bwd(
        p7, log_a, alr, wau, w["b_alpha"], w["gla_norm_g"], o, states, dyg, dp, d)
    dalr = _matmul(dz, wau, dims="nt", name="d_alr", out_dtype=BF16)
    g["w_alpha_up"] = _matmul(alr, dz, dims="tn", name="dw_alpha_up")[:GLA_RANK]
    wq_b = jnp.concatenate([wq[:, 5 * d:], wq[:, 3 * d:5 * d], wq[:, :3 * d], jnp.zeros((d, d), wq.dtype)], axis=1)
    rest_grads = [g.pop(n) for n in REST]
    rest_grads = [a if a.ndim == 3 else a.reshape(N_CHIPS, -1, a.shape[-1]) for a in rest_grads]
    parts, owns = _chip_partials(rest_grads, REST, "rest")
    own_chip = dict(zip(REST, owns))
    dwq, arrived, _ = _matmul(u1, dp, dims="tn", name="dw_in", out_dtype=BF16, hosted=_exchange_plan(parts))
    from_chips = dict(zip(REST, arrived))
    dw_alr = _matmul(u1, dalr, dims="tn", name="dw_in_alr", out_dtype=BF16)
    dw_in = _to_chip_major(jnp.concatenate(
        [dwq[:, 4 * d:7 * d], dw_alr[:, :GLA_RANK], dwq[:, 2 * d:4 * d], dwq[:, :2 * d]], axis=1))
    parts, owns = _chip_partials([dw_in], ["w_in"], "w_in")
    own_chip["w_in"] = owns[0]
    dh0, arrived, (g["norm_mix_g"],) = _matmul(
        dp, wq_b, dims="nt", name="d_u1", hosted=_exchange_plan(parts),
        epilogue=_rms_bwd_epilogue(h0, w["norm_mix_g"], dh1, low_rank=(dalr, w_alr)))
    from_chips["w_in"] = arrived[0]
    g["meta_tokens"] = dh0[PAD:HEAD_ROWS]
    return loss, dh0[HEAD_ROWS:], g, own_chip, from_chips


HBM_SPEC = pl.BlockSpec(memory_space=pltpu.HBM)
FLIPS = ((1, 0), (0, 1), (1, 1))


def _place():
    x, y, c = lax.axis_index("x"), lax.axis_index("y"), lax.axis_index("c")
    return x, y, c


def _half_rows(ref, h, lead=()):
    rh = ref.shape[-2] // 2
    return ref.at[(*lead, pl.ds(pl.multiple_of(h * rh, 2 * SUBLANES), rh), slice(None))]


class _Hosted:
    def __init__(self, operands, out_shapes, sem_shapes, start, finish):
        self.operands, self.out_shapes, self.sem_shapes = operands, out_shapes, sem_shapes
        self.start, self.finish = start, finish


def _gather_plan(shards):
    n = len(shards)

    def copies(ins, outs, sems, kinds):
        send_sems, recv_sems = sems
        x, y, c = _place()
        chips = [(x ^ fx, y ^ fy) for fx, fy in FLIPS]
        me, sibling = (x, y, c), (x, y, 1 - c)

        def copy(k, sem, chip, h, to, src=None):
            slot = _half_rows(outs[k], h, lead=(2 * chip[0] + chip[1],))
            return pltpu.make_async_remote_copy(src_ref=slot if src is None else src, dst_ref=slot,
                                                send_sem=send_sems.at[sem], recv_sem=recv_sems.at[sem],
                                                device_id=to, device_id_type=MESH)

        def own(k):
            return pltpu.make_async_remote_copy(src_ref=ins[k], dst_ref=outs[k].at[2 * x + y],
                                                send_sem=send_sems.at[6 * n + k], recv_sem=recv_sems.at[6 * n + k],
                                                device_id=sibling, device_id_type=MESH)

        make = {
            "first": lambda k, j, chip: copy(k, 3 * k + j, (x, y), c, (*chip, c), src=_half_rows(ins[k], c)),
            "landed": lambda k, j, chip: copy(k, 3 * k + j, chip, c, me),
            "passed": lambda k, j, chip: copy(k, 3 * n + 3 * k + j, chip, c, sibling),
            "from_sibling": lambda k, j, chip: copy(k, 3 * n + 3 * k + j, chip, 1 - c, me),
        }
        return [[own(k) for k in range(n)] if kind == "own" else
                [make[kind](k, j, chip) for k in range(n) for j, chip in enumerate(chips)] for kind in kinds]

    def start(ins, outs, sems):
        first, own = copies(ins, outs, sems, ["first", "own"])
        for cp in first + own:
            cp.start()

    def finish(ins, outs, sems):
        first, landed, passed, from_sibling, own = copies(
            ins, outs, sems, ["first", "landed", "passed", "from_sibling", "own"])
        for arrived, fwd in zip(landed, passed):
            arrived.wait_recv()
            fwd.start()
        for cp in from_sibling:
            cp.wait_recv()
        for cp in own:
            cp.wait()
        for cp in first + passed:
            cp.wait_send()

    return _Hosted(list(shards), [jax.ShapeDtypeStruct((N_CHIPS, *s.shape), s.dtype) for s in shards],
                   [pltpu.SemaphoreType.DMA((7 * n,)), pltpu.SemaphoreType.DMA((7 * n,))], start, finish)


def _swap_core_halves(grads, name):
    n = len(grads)

    def body(*refs):
        ins, outs = refs[:n], refs[n:2 * n]
        send_sems, recv_sems = refs[2 * n:]
        x, y, c = _place()
        cps = [pltpu.make_async_remote_copy(
            src_ref=_half_rows(ins[k], 1 - c, lead=(slice(None),)), dst_ref=outs[k], send_sem=send_sems.at[k],
            recv_sem=recv_sems.at[k], device_id=(x, y, 1 - c), device_id_type=MESH) for k in range(n)]
        for cp in cps:
            cp.start()
        for cp in cps:
            cp.wait()

    return pl.pallas_call(
        body, name=name, in_specs=[HBM_SPEC] * n, out_specs=[HBM_SPEC] * n,
        out_shape=[jax.ShapeDtypeStruct((g.shape[0], g.shape[1] // 2, g.shape[2]), g.dtype) for g in grads],
        scratch_shapes=[pltpu.SemaphoreType.DMA((n,)), pltpu.SemaphoreType.DMA((n,))],
    )(*grads)


def _exchange_plan(parts):
    n = len(parts)

    def copies(ins, outs, sems):
        send_sems, recv_sems = sems
        x, y, c = _place()
        cps = []
        for k in range(n):
            for j, (fx, fy) in enumerate(FLIPS):
                tx, ty = x ^ fx, y ^ fy
                cps.append(pltpu.make_async_remote_copy(
                    src_ref=ins[k].at[2 * tx + ty], dst_ref=_half_rows(outs[k], c, lead=(j,)),
                    send_sem=send_sems.at[3 * k + j], recv_sem=recv_sems.at[3 * k + j],
                    device_id=(tx, ty, c), device_id_type=MESH))
        return cps

    def start(ins, outs, sems):
        for cp in copies(ins, outs, sems):
            cp.start()

    def finish(ins, outs, sems):
        for cp in copies(ins, outs, sems):
            cp.wait()

    return _Hosted(list(parts), [jax.ShapeDtypeStruct((3, 2 * p.shape[1], p.shape[2]), p.dtype) for p in parts],
                   [pltpu.SemaphoreType.DMA((3 * n,)), pltpu.SemaphoreType.DMA((3 * n,))], start, finish)


def _join_core_halves(bufs):
    n = len(bufs)

    def body(*refs):
        outs = refs[n:2 * n]
        send_sems, recv_sems = refs[2 * n:]
        x, y, c = _place()

        def rows(k, h):
            return _half_rows(outs[k], h, lead=(slice(None),) * (len(outs[k].shape) - 2))

        cps = [pltpu.make_async_remote_copy(
            src_ref=rows(k, c), dst_ref=rows(k, c), send_sem=send_sems.at[k],
            recv_sem=recv_sems.at[k], device_id=(x, y, 1 - c), device_id_type=MESH) for k in range(n)]
        for cp in cps:
            cp.start()
        for k in range(n):
            cps[k].wait_send()
            pltpu.make_async_remote_copy(
                src_ref=rows(k, c), dst_ref=rows(k, 1 - c), send_sem=send_sems.at[k],
                recv_sem=recv_sems.at[k], device_id=(x, y, 1 - c), device_id_type=MESH).wait_recv()

    return pl.pallas_call(
        body, name="join_core_halves", in_specs=[HBM_SPEC] * n, out_specs=[HBM_SPEC] * n,
        out_shape=[jax.ShapeDtypeStruct(f.shape, f.dtype) for f in bufs],
        input_output_aliases={k: k for k in range(n)},
        scratch_shapes=[pltpu.SemaphoreType.DMA((n,)), pltpu.SemaphoreType.DMA((n,))],
    )(*bufs)


def _gather_small(block, name):
    m, n = block.shape

    def body(x_ref, out_ref, send_sems, recv_sems, local_sem):
        x, y, c = _place()
        me, sibling = (x, y, c), (x, y, 1 - c)
        chips = [(x ^ fx, y ^ fy) for fx, fy in FLIPS]

        def rows(px, py, pc):
            return out_ref.at[pl.ds((4 * px + 2 * py + pc) * m, m), :]

        def copy(k, blk, to, src=None):
            return pltpu.make_async_remote_copy(
                src_ref=rows(*blk) if src is None else src, dst_ref=rows(*blk),
                send_sem=send_sems.at[k], recv_sem=recv_sems.at[k], device_id=to, device_id_type=MESH)

        mine = pltpu.make_async_copy(x_ref, rows(*me), local_sem)
        mine.start()
        first = [copy(0, me, sibling, src=x_ref)]
        first += [copy(1 + j, me, (*chip, c), src=x_ref) for j, chip in enumerate(chips)]
        for cp in first:
            cp.start()
        passed = [copy(4 + j, (*chip, c), sibling) for j, chip in enumerate(chips)]
        for j, chip in enumerate(chips):
            copy(1 + j, (*chip, c), me).wait_recv()
            passed[j].start()
        copy(0, sibling, me).wait_recv()
        for j, chip in enumerate(chips):
            copy(4 + j, (*chip, 1 - c), me).wait_recv()
        for cp in first + passed:
            cp.wait_send()
        mine.wait()

    out = pl.pallas_call(
        body, name=name,
        out_shape=jax.ShapeDtypeStruct((8 * m, n), block.dtype),
        in_specs=[pl.BlockSpec(memory_space=pltpu.VMEM)],
        out_specs=pl.BlockSpec(memory_space=pltpu.VMEM),
        scratch_shapes=[pltpu.SemaphoreType.DMA((7,)), pltpu.SemaphoreType.DMA((7,)), pltpu.SemaphoreType.DMA],
    )(block)
    return out.reshape(8, m, n)


def _add_core_halves(place, grad, from_sibling, name):
    nc, r, cols = grad.shape
    rh = r // 2

    def body(place_ref, g_ref, s_ref, o_ref, own_ref):
        total = (g_ref[...].astype(F32) + s_ref[...].astype(F32)).astype(BF16)
        o_ref[...] = total

        @pl.when(pl.program_id(0) == place_ref[0])
        def _():
            own_ref[...] = total[0]

    spec = pl.BlockSpec((1, rh, cols), lambda t, place_ref: (t, 0, 0))
    return pl.pallas_call(
        body, name=name,
        grid_spec=pltpu.PrefetchScalarGridSpec(
            num_scalar_prefetch=1, grid=(nc,),
            in_specs=[pl.BlockSpec((1, rh, cols), lambda t, place_ref: (t, place_ref[1], 0)), spec],
            out_specs=[spec, pl.BlockSpec((rh, cols), lambda t, place_ref: (place_ref[1], 0))]),
        out_shape=[jax.ShapeDtypeStruct((nc, rh, cols), BF16), jax.ShapeDtypeStruct((r, cols), BF16)],
        compiler_params=_cparams(("arbitrary",)),
    )(place, grad, from_sibling)


def _sum_adamw(own, others, w, m, v, name):
    rws, cols = w.shape
    tr = rws
    for cand in (256, 128, 64, 32, 16):
        if rws % cand == 0 and cand * cols * 4 <= 2 * 1024 * 1024:
            tr = cand
            break
    c1 = 1.0 - ADAM_B1 ** ADAM_STEP
    c2 = 1.0 - ADAM_B2 ** ADAM_STEP

    def body(a_ref, b_ref, w_ref, m_ref, v_ref, g_ref, d_ref, nm_ref, nv_ref):
        gv = a_ref[...].astype(F32)
        for j in range(3):
            gv = gv + b_ref[j].astype(F32)
        nm = ADAM_B1 * m_ref[...] + (1.0 - ADAM_B1) * gv
        nv = ADAM_B2 * v_ref[...] + (1.0 - ADAM_B2) * (gv * gv)
        m_hat = nm / c1
        v_hat = nv / c2
        g_ref[...] = gv
        d_ref[...] = -ADAM_LR * (m_hat / (jnp.sqrt(v_hat) + ADAM_EPS) + ADAM_WD * w_ref[...])
        nm_ref[...] = nm
        nv_ref[...] = nv

    spec = pl.BlockSpec((tr, cols), lambda i: (i, 0))
    return pl.pallas_call(
        body, name=name, grid=(rws // tr,),
        in_specs=[spec, pl.BlockSpec((3, tr, cols), lambda i: (0, i, 0))] + [spec] * 3, out_specs=[spec] * 4,
        out_shape=[jax.ShapeDtypeStruct((rws, cols), F32)] * 4,
        compiler_params=_cparams(("parallel",)),
    )(own, others, w, m, v)


def _sum_devices(blocks):
    n, m, _ = blocks.shape

    def body(b_ref, o_ref):
        acc = b_ref[0]
        for j in range(1, n):
            acc = acc + b_ref[j]
        o_ref[...] = acc

    return pl.pallas_call(
        body, name="sum_devices", out_shape=jax.ShapeDtypeStruct((m, LANES), F32),
        in_specs=[pl.BlockSpec(memory_space=pltpu.VMEM)], out_specs=pl.BlockSpec(memory_space=pltpu.VMEM),
    )(blocks)


def _adamw(w, g, m, v, name):
    rws, cols = w.shape
    tr = rws
    for cand in (256, 128, 64, 32, 16, 8):
        if rws % cand == 0 and cand * cols * 4 <= 2 * 1024 * 1024:
            tr = cand
            break
    c1 = 1.0 - ADAM_B1 ** ADAM_STEP
    c2 = 1.0 - ADAM_B2 ** ADAM_STEP

    def body(w_ref, g_ref, m_ref, v_ref, d_ref, nm_ref, nv_ref):
        gv = g_ref[...]
        nm = ADAM_B1 * m_ref[...] + (1.0 - ADAM_B1) * gv
        nv = ADAM_B2 * v_ref[...] + (1.0 - ADAM_B2) * (gv * gv)
        m_hat = nm / c1
        v_hat = nv / c2
        d_ref[...] = -ADAM_LR * (m_hat / (jnp.sqrt(v_hat) + ADAM_EPS) + ADAM_WD * w_ref[...])
        nm_ref[...] = nm
        nv_ref[...] = nv

    spec = pl.BlockSpec((tr, cols), lambda i: (i, 0))
    return pl.pallas_call(
        body, name=name, grid=(rws // tr,), in_specs=[spec] * 4, out_specs=[spec] * 3,
        out_shape=[jax.ShapeDtypeStruct((rws, cols), F32)] * 3,
        compiler_params=_cparams(("parallel",)),
    )(w, g, m, v)


WEIGHTS = (
    ("meta_tokens", (16, 1024), 1), ("norm_mix_g", (1024,), None), ("w_in", (1024, 7184), 1),
    ("w_alpha_up", (16, 512), 1), ("b_alpha", (512,), None), ("gla_norm_g", (1024,), None),
    ("w_gla_o", (1024, 1024), 0), ("conf_dw_w", (31, 1024), 1), ("conf_dw_b", (1024,), None),
    ("conf_ln_g", (1024,), None), ("conf_ln_b", (1024,), None), ("w_conf_o", (1024, 1024), 0),
    ("w_out", (1024, 1024), 0), ("norm_ffn_g", (1024,), None), ("w_up", (1024, 5632), 1),
    ("ffn_dw_w", (3, 2816), 1), ("ffn_dw_b", (2816,), None), ("w_down", (2816, 1024), 0),
    ("final_norm_g", (1024,), None),
)
BIG = ("w_in", "w_up", "w_down", "w_gla_o", "w_conf_o", "w_out")
SMALL_SHARDED = ("meta_tokens", "w_alpha_up", "conf_dw_w", "ffn_dw_w")
REPLICATED = tuple(n for n, _, ax in WEIGHTS if ax is None)
SHAPES = {n: s for n, s, _ in WEIGHTS}
AXIS = {n: ax for n, _, ax in WEIGHTS}
N_CHIPS = 4


def _shard_shape(name):
    s = list(SHAPES[name])
    s[AXIS[name]] //= N_CHIPS
    return tuple(s)


def _pack(parts, mult):
    flat = jnp.concatenate([p.reshape(-1) for p in parts])
    pad = (-flat.shape[0]) % mult
    return jnp.pad(flat, (0, pad))


def _unpack(flat, names, shape_of):
    out, off = {}, 0
    for n in names:
        shp = shape_of(n)
        size = math.prod(shp)
        out[n] = flat[off:off + size].reshape(shp)
        off += size
    return out


def _to_chip_major(full):
    r, cols = full.shape
    return full.reshape(r, N_CHIPS, cols // N_CHIPS).transpose(1, 0, 2)


def _from_chip_major(slabs):
    nc, r, cs = slabs.shape
    return slabs.transpose(1, 0, 2).reshape(r, nc * cs)


def kernel(x, meta_tokens, norm_mix_g, w_in, w_alpha_up, b_alpha, gla_norm_g, w_gla_o, conf_dw_w, conf_dw_b, conf_ln_g, conf_ln_b, w_conf_o, w_out, norm_ffn_g, w_up, ffn_dw_w, ffn_dw_b, w_down, final_norm_g, loss_target, m_meta_tokens, m_norm_mix_g, m_w_in, m_w_alpha_up, m_b_alpha, m_gla_norm_g, m_w_gla_o, m_conf_dw_w, m_conf_dw_b, m_conf_ln_g, m_conf_ln_b, m_w_conf_o, m_w_out, m_norm_ffn_g, m_w_up, m_ffn_dw_w, m_ffn_dw_b, m_w_down, m_final_norm_g, v_meta_tokens, v_norm_mix_g, v_w_in, v_w_alpha_up, v_b_alpha, v_gla_norm_g, v_w_gla_o, v_conf_dw_w, v_conf_dw_b, v_conf_ln_g, v_conf_ln_b, v_w_conf_o, v_w_out, v_norm_ffn_g, v_w_up, v_ffn_dw_w, v_ffn_dw_b, v_w_down, v_final_norm_g):
    names = [n for n, _, _ in WEIGHTS]
    w_args = (meta_tokens, norm_mix_g, w_in, w_alpha_up, b_alpha, gla_norm_g, w_gla_o, conf_dw_w, conf_dw_b, conf_ln_g,
              conf_ln_b, w_conf_o, w_out, norm_ffn_g, w_up, ffn_dw_w, ffn_dw_b, w_down, final_norm_g)
    m_args = (m_meta_tokens, m_norm_mix_g, m_w_in, m_w_alpha_up, m_b_alpha, m_gla_norm_g, m_w_gla_o, m_conf_dw_w,
              m_conf_dw_b, m_conf_ln_g, m_conf_ln_b, m_w_conf_o, m_w_out, m_norm_ffn_g, m_w_up, m_ffn_dw_w, m_ffn_dw_b,
              m_w_down, m_final_norm_g)
    v_args = (v_meta_tokens, v_norm_mix_g, v_w_in, v_w_alpha_up, v_b_alpha, v_gla_norm_g, v_w_gla_o, v_conf_dw_w,
              v_conf_dw_b, v_conf_ln_g, v_conf_ln_b, v_w_conf_o, v_w_out, v_norm_ffn_g, v_w_up, v_ffn_dw_w, v_ffn_dw_b,
              v_w_down, v_final_norm_g)
    in_shape = {n: a.shape for n, a in zip(names, w_args)}
    local = {n: a.reshape(_shard_shape(n) if AXIS[n] is not None else SHAPES[n]) for n, a in zip(names, w_args)}
    m_loc = {n: a.reshape(local[n].shape) for n, a in zip(names, m_args)}
    v_loc = {n: a.reshape(local[n].shape) for n, a in zip(names, v_args)}

    me = 2 * lax.axis_index("x") + lax.axis_index("y")

    small = _pack([local[n] for n in SMALL_SHARDED], 8 * LANES).reshape(-1, LANES)
    small_all = _gather_small(small, "gather_small_weights")[::2].reshape(N_CHIPS, -1)
    per_chip_small = [_unpack(small_all[t], SMALL_SHARDED, _shard_shape) for t in range(N_CHIPS)]
    full = {n: jnp.concatenate([per_chip_small[t][n] for t in range(N_CHIPS)], axis=1) for n in SMALL_SHARDED}
    for n in REPLICATED:
        full[n] = local[n].reshape(1, -1)

    loss_part, grad_x, grads, own_chip, from_chips = _local_step(
        x[0], loss_target[0], full, {n: local[n].astype(BF16) for n in BIG})

    smalls = REPLICATED + SMALL_SHARDED
    rep = _pack([grads[n] for n in smalls] + [loss_part], 8 * LANES).reshape(-1, LANES)
    rep_sum = _sum_devices(_gather_small(rep, "gather_small_grads")).reshape(-1)
    g_loc = _unpack(rep_sum, smalls, lambda n: SHAPES[n])
    loss = rep_sum[sum(math.prod(SHAPES[n]) for n in smalls)]
    for n in SMALL_SHARDED:
        width = _shard_shape(n)[1]
        g_loc[n] = lax.dynamic_slice_in_dim(g_loc[n], me * width, width, axis=1)

    joined = _join_core_halves([own_chip[n] for n in BIG] + [from_chips[n] for n in BIG])
    delta, new_m, new_v = {}, {}, {}
    for k, n in enumerate(BIG):
        g_loc[n], delta[n], new_m[n], new_v[n] = _sum_adamw(
            joined[k], joined[len(BIG) + k], local[n], m_loc[n], v_loc[n], "sum_adamw_" + n)
    rest = SMALL_SHARDED + REPLICATED
    pk = lambda dct: _pack([dct[n] for n in rest], 8 * LANES).reshape(-1, LANES)
    ds, ms, vs = _adamw(pk(local), pk(g_loc), pk(m_loc), pk(v_loc), "adamw_small")
    shape_loc = lambda n: local[n].shape
    for dct, flat in ((delta, ds), (new_m, ms), (new_v, vs)):
        dct.update(_unpack(flat.reshape(-1), rest, shape_loc))

    outs = [loss, grad_x[None]]
    for dct in (g_loc, delta, new_m, new_v):
        outs += [dct[n].reshape(in_shape[n]) for n in names]
    return tuple(outs)
```

```python
import functools
import math

import jax
import jax.numpy as jnp
from jax import lax
from jax.experimental import pallas as pl
from jax.experimental.pallas import tpu as pltpu

F32 = jnp.float32
BF16 = jnp.bfloat16

N_META = 16
PAD = 240
HEAD_ROWS = PAD + N_META
HEADS = 4
GLA_RANK = 16
RANK_PAD = 128
GATE_TAU = 16.0
CHUNK = 64
CONF_K = 31
FFN_K = 3
RMS_EPS = 1e-6
LN_EPS = 1e-5
ADAM_LR, ADAM_B1, ADAM_B2, ADAM_EPS, ADAM_WD, ADAM_STEP = 0.001, 0.9, 0.999, 1e-08, 0.01, 10

ROW_TILE = 256
HALO = 32
LANES = 128
V7X_VMEM_LIMIT = 56 * 1024 * 1024
MESH = pl.DeviceIdType.MESH


def _cparams(sem):
    return pltpu.CompilerParams(dimension_semantics=sem, vmem_limit_bytes=V7X_VMEM_LIMIT)


def _sigmoid(x):
    return 1.0 / (1.0 + jnp.exp(-x))


def _pick(n, prefs):
    for p in prefs:
        if n % p == 0:
            return p
    return n


MATMUL_TILES = {
    "proj": (2816, 1024, 1024), "gla_out": (1408, 1024, 1024), "conf_out": (1408, 1024, 1024),
    "mix_out": (768, 1024, 1024), "ffn_up": (1408, 1408, 1024), "ffn_down": (768, 1024, 2816),
    "d_ffn_y": (1408, 1408, 1024), "d_u2": (768, 1024, 1408), "d_merged": (768, 1024, 1024),
    "d_conf_s": (1408, 1024, 1024), "d_gla_y": (1408, 1024, 1024), "d_u1": (768, 1024, 2048),
    "dw_in": (1024, 1024, 2816), "dw_up": (1024, 1408, 1408), "dw_down": (1408, 1024, 1408),
    "dw_out": (1024, 1024, 1408), "dw_conf_o": (1024, 1024, 1408), "dw_gla_o": (1024, 1024, 1408),
}


class _Epilogue:
    def __init__(self, operands, in_specs, out_shapes, out_specs, fn, init=None, lhs=None):
        self.operands, self.in_specs, self.out_shapes, self.out_specs = operands, in_specs, out_shapes, out_specs
        self.fn, self.init, self.lhs = fn, init, lhs


def _matmul(a, b, *, dims, name, tm=None, tn=None, tk=None, out_dtype=F32, add=None, chips=None, hosted=None,
            epilogue=None):
    if chips == "b":
        nc, r, cs = b.shape
        b_shape = (r, nc * cs)
    else:
        b_shape = b.shape
    if dims == "nn":
        (m, k), (_, n) = a.shape, b_shape
    elif dims == "nt":
        (m, k), (n, _) = a.shape, b_shape
    else:
        (k, m), (_, n) = a.shape, b_shape
    want = MATMUL_TILES.get(name, (None, None, None))
    tm, tn, tk = tm or want[0], tn or want[1], tk or want[2]
    tm = tm if tm and m % tm == 0 else _pick(m, (768, 1024, 1408, 512, 256, 128))
    tn = tn if tn and n % tn == 0 else _pick(n, (1024, 1408, 512, 256, 128))
    tk = tk if tk and k % tk == 0 else _pick(k, (1024, 768, 1408, 512, 256, 128))
    if chips == "b":
        tn, tk = (cs, tk) if dims == "nn" else (tn, cs)
    if chips == "out":
        tn, tm = n // N_CHIPS, m
    nk = k // tk
    assert m % tm == 0 and n % tn == 0 and k % tk == 0, (name, m, n, k, tm, tn, tk)
    assert epilogue is None or epilogue.lhs is None or (nk == 1 and n == tn), name
    a_spec = {"nn": pl.BlockSpec((tm, tk), lambda i, j, kk: (i, kk)),
              "nt": pl.BlockSpec((tm, tk), lambda i, j, kk: (i, kk)),
              "tn": pl.BlockSpec((tk, tm), lambda i, j, kk: (kk, i))}[dims]
    if chips == "b":
        b_spec = {"nn": pl.BlockSpec((None, tk, tn), lambda i, j, kk: (j, kk, 0)),
                  "nt": pl.BlockSpec((None, tn, tk), lambda i, j, kk: (kk, j, 0))}[dims]
    else:
        b_spec = {"nn": pl.BlockSpec((tk, tn), lambda i, j, kk: (kk, j)),
                  "nt": pl.BlockSpec((tn, tk), lambda i, j, kk: (j, kk)),
                  "tn": pl.BlockSpec((tk, tn), lambda i, j, kk: (kk, j))}[dims]
    contract = {"nn": (((1,), (0,)), ((), ())), "nt": (((1,), (1,)), ((), ())), "tn": (((0,), (0,)), ((), ()))}[dims]
    if chips == "out":
        o_spec = pl.BlockSpec((None, tm, tn), lambda i, j, kk: (j, 0, 0))
        out_struct = jax.ShapeDtypeStruct((N_CHIPS, m, tn), out_dtype)
    else:
        o_spec = pl.BlockSpec((tm, tn), lambda i, j, kk: (i, j))
        out_struct = jax.ShapeDtypeStruct((m, n), out_dtype)
    has_add = add is not None

    n_hin = len(hosted.operands) if hosted is not None else 0
    n_hout = len(hosted.out_shapes) if hosted is not None else 0
    e_operands = []
    if epilogue is not None:
        e_operands = epilogue.operands(tm) if callable(epilogue.operands) else epilogue.operands
    n_ein = len(e_operands)
    n_eout = len(epilogue.out_shapes) if epilogue is not None else 0

    def body(*refs):
        refs = list(refs)
        a_ref, b_ref = refs[:2]
        add_ref = refs[2] if has_add else None
        pos = 2 + has_add
        h_ins, pos = refs[pos:pos + n_hin], pos + n_hin
        e_ins, pos = refs[pos:pos + n_ein], pos + n_ein
        o_ref, pos = refs[pos], pos + 1
        h_outs, pos = refs[pos:pos + n_hout], pos + n_hout
        e_outs, pos = refs[pos:pos + n_eout], pos + n_eout
        acc_ref = refs[pos] if nk > 1 else None
        h_sems = refs[pos + (nk > 1):]
        i, j, kk = pl.program_id(0), pl.program_id(1), pl.program_id(2)
        first = (i == 0) & (j == 0) & (kk == 0)
        if hosted is not None:
            @pl.when(first)
            def _():
                hosted.start(h_ins, h_outs, h_sems)

        if epilogue is not None and epilogue.init is not None:
            @pl.when(first)
            def _():
                epilogue.init(e_outs)

        def finish(total):
            if epilogue is None:
                o_ref[...] = total.astype(out_dtype)
            else:
                epilogue.fn(total, i, o_ref, e_ins, e_outs)

        if epilogue is not None and epilogue.lhs is not None:
            lhs = epilogue.lhs(a_ref, e_ins, e_outs)
        else:
            lhs = a_ref[...].astype(BF16)
        prod = lax.dot_general(lhs, b_ref[...].astype(BF16), contract, preferred_element_type=F32)
        if nk == 1:
            finish(prod + add_ref[...].astype(F32) if has_add else prod)
        else:
            @pl.when(kk == 0)
            def _():
                acc_ref[...] = prod + add_ref[...].astype(F32) if has_add else prod

            @pl.when((kk > 0) & (kk < nk - 1))
            def _():
                acc_ref[...] += prod

            @pl.when(kk == nk - 1)
            def _():
                finish(acc_ref[...] + prod)

        if hosted is not None:
            @pl.when((i == m // tm - 1) & (j == n // tn - 1) & (kk == nk - 1))
            def _():
                hosted.finish(h_ins, h_outs, h_sems)

    in_specs = ([a_spec, b_spec] + ([o_spec] if has_add else []) + [HBM_SPEC] * n_hin
                + (list(epilogue.in_specs(tm)) if epilogue is not None else []))
    args = ((a, b) + ((add,) if has_add else ()) + (tuple(hosted.operands) if hosted is not None else ())
            + tuple(e_operands))
    serial = hosted is not None or epilogue is not None
    outs = pl.pallas_call(
        body, name=name, grid=(m // tm, n // tn, nk),
        in_specs=in_specs,
        out_specs=[o_spec] + [HBM_SPEC] * n_hout + (list(epilogue.out_specs(tm)) if epilogue is not None else []),
        out_shape=([out_struct] + (list(hosted.out_shapes) if hosted is not None else [])
                   + (list(epilogue.out_shapes) if epilogue is not None else [])),
        scratch_shapes=([pltpu.VMEM((tm, tn), F32)] if nk > 1 else [])
        + (list(hosted.sem_shapes) if hosted is not None else []),
        compiler_params=_cparams(("arbitrary",) * 3 if serial else ("parallel", "parallel", "arbitrary")),
    )(*args)
    if not serial:
        return outs[0]
    return outs[0], list(outs[1:1 + n_hout]), list(outs[1 + n_hout:])


def _tile_rows(n):
    return lambda tm: pl.BlockSpec((tm, n), lambda i, j, kk: (i, 0))


def _one_row(n):
    return pl.BlockSpec((1, n), lambda i, j, kk: (0, 0))


def _mix_epilogue(g, p7, br_conf, d):
    rows = p7.shape[0]
    n = g.shape[1]

    def in_specs(tm):
        return [_one_row(n), pl.BlockSpec((tm, d), lambda i, j, kk: (i, 5)),
                pl.BlockSpec((tm, d), lambda i, j, kk: (i, 6)), _tile_rows(d)(tm)]

    def lhs(a_ref, ins, outs):
        merged = (_sigmoid(ins[1][...].astype(F32)) * a_ref[...].astype(F32)
                  + _sigmoid(ins[2][...].astype(F32)) * ins[3][...].astype(F32)).astype(BF16)
        outs[1][...] = merged
        return merged

    def fn(total, i, o_ref, ins, outs):
        o_ref[...] = total
        r = lax.rsqrt(jnp.mean(total * total, axis=-1, keepdims=True) + RMS_EPS)
        outs[0][...] = (total * r * ins[0][...]).astype(BF16)

    return _Epilogue([g, p7, p7, br_conf], in_specs,
                     [jax.ShapeDtypeStruct((rows, n), BF16), jax.ShapeDtypeStruct((rows, d), BF16)],
                     lambda tm: [_tile_rows(n)(tm), _tile_rows(d)(tm)], fn, lhs=lhs)


def _rms_bwd_epilogue(h, g, dres, low_rank=None):
    n = g.shape[1]
    operands = [h, dres, g] + (list(low_rank) if low_rank is not None else [])

    def in_specs(tm):
        specs = [_tile_rows(n)(tm), _tile_rows(n)(tm), _one_row(n)]
        if low_rank is not None:
            specs += [_tile_rows(low_rank[0].shape[1])(tm),
                      pl.BlockSpec(low_rank[1].shape, lambda i, j, kk: (0, 0))]
        return specs

    def init(outs):
        outs[0][...] = jnp.zeros_like(outs[0])

    def fn(total, i, o_ref, ins, outs):
        if low_rank is not None:
            total = total + _dot_nt(ins[3][...], ins[4][...])
        x = ins[0][...]
        r = lax.rsqrt(jnp.mean(x * x, axis=-1, keepdims=True) + RMS_EPS)
        nrm = x * r
        outs[0][...] += jnp.sum(total * nrm, axis=0, keepdims=True)
        dn = total * ins[2][...]
        o_ref[...] = ins[1][...] + r * (dn - nrm * jnp.mean(dn * nrm, axis=-1, keepdims=True))

    return _Epilogue(operands, in_specs, [jax.ShapeDtypeStruct((1, n), F32)], lambda tm: [_one_row(n)], fn, init)


def _merge_bwd_epilogue(p7, br_gla, br_conf, d):
    t = p7.shape[0]

    def in_specs(tm):
        return [pl.BlockSpec((tm, d), lambda i, j, kk: (i, 5)), pl.BlockSpec((tm, d), lambda i, j, kk: (i, 6)),
                _tile_rows(d)(tm), _tile_rows(d)(tm)]

    def fn(total, i, o_ref, ins, outs):
        sg = _sigmoid(ins[0][...].astype(F32))
        sc = _sigmoid(ins[1][...].astype(F32))
        o_ref[...] = (total * sg).astype(BF16)
        outs[0][...] = (total * sc).astype(BF16)
        outs[1][:, 0:d] = (total * ins[2][...].astype(F32) * sg * (1.0 - sg)).astype(BF16)
        outs[1][:, d:2 * d] = (total * ins[3][...].astype(F32) * sc * (1.0 - sc)).astype(BF16)

    return _Epilogue([p7, p7, br_gla, br_conf], in_specs,
                     [jax.ShapeDtypeStruct((t, d), BF16), jax.ShapeDtypeStruct((t, DP_BLOCKS * d), BF16)],
                     lambda tm: [_tile_rows(d)(tm), pl.BlockSpec((tm, 2 * d), lambda i, j, kk: (i, 0))], fn)


def _loss_epilogue(target, gf):
    n = gf.shape[1]

    def in_specs(tm):
        nb = tm // ROW_TILE
        return [pl.BlockSpec((ROW_TILE, n), (lambda s: lambda i, j, kk: (jnp.maximum(i * nb + s - 1, 0), 0))(s))
                for s in range(nb)] + [_one_row(n)]

    def init(outs):
        outs[0][...] = jnp.zeros_like(outs[0])
        outs[1][...] = jnp.zeros_like(outs[1])

    def fn(total, i, o_ref, ins, outs):
        nb = len(ins) - 1
        g = ins[nb][...]
        for s in range(nb):
            rows = slice(s * ROW_TILE, (s + 1) * ROW_TILE)
            live = jnp.where(i * nb + s > 0, 1.0, 0.0)
            h = total[rows]
            r = lax.rsqrt(jnp.mean(h * h, axis=-1, keepdims=True) + RMS_EPS)
            nrm = h * r
            err = (nrm * g - ins[s][...]) * live
            outs[0][...] += 0.5 * jnp.sum(jnp.mean(err * err, axis=-1, keepdims=True), axis=0, keepdims=True)
            dout = err * (1.0 / n)
            outs[1][...] += jnp.sum(dout * nrm, axis=0, keepdims=True)
            dn = dout * g
            o_ref[rows, :] = r * (dn - nrm * jnp.mean(dn * nrm, axis=-1, keepdims=True))

    nb_ops = lambda tm: [target] * (tm // ROW_TILE) + [gf]
    return _Epilogue(nb_ops, in_specs, [jax.ShapeDtypeStruct((1, 1), F32), jax.ShapeDtypeStruct((1, n), F32)],
                     lambda tm: [pl.BlockSpec((1, 1), lambda i, j, kk: (0, 0)), _one_row(n)], fn, init)


def _row_mask(tile_index, rows):
    r = tile_index * rows + lax.broadcasted_iota(jnp.int32, (rows, 1), 0)
    return (r >= PAD).astype(F32)


def _prep(x, meta, g1, hosted):
    s, d = x.shape
    t = HEAD_ROWS + s
    tm = ROW_TILE
    nt = t // tm
    n_hin, n_hout = len(hosted.operands), len(hosted.out_shapes)

    def body(*refs):
        x_ref, meta_ref, g_ref = refs[:3]
        h_ins = refs[3:3 + n_hin]
        h_ref, u_ref = refs[3 + n_hin:5 + n_hin]
        h_outs = refs[5 + n_hin:5 + n_hin + n_hout]
        h_sems = refs[5 + n_hin + n_hout:]
        i = pl.program_id(0)

        @pl.when(i == 0)
        def _():
            hosted.start(h_ins, h_outs, h_sems)
            h_ref[0:PAD, :] = jnp.zeros((PAD, d), F32)
            h_ref[PAD:HEAD_ROWS, :] = meta_ref[...]

        @pl.when(i > 0)
        def _():
            h_ref[...] = x_ref[...]

        h = h_ref[...]
        r = lax.rsqrt(jnp.mean(h * h, axis=-1, keepdims=True) + RMS_EPS)
        u_ref[...] = (h * r * g_ref[...]).astype(BF16)

        @pl.when(i == nt - 1)
        def _():
            hosted.finish(h_ins, h_outs, h_sems)

    outs = pl.pallas_call(
        body, name="prep", grid=(nt,),
        in_specs=[pl.BlockSpec((tm, d), lambda i: (jnp.maximum(i - 1, 0), 0)),
                  pl.BlockSpec((N_META, d), lambda i: (0, 0)),
                  pl.BlockSpec((1, d), lambda i: (0, 0))] + [HBM_SPEC] * n_hin,
        out_specs=[pl.BlockSpec((tm, d), lambda i: (i, 0)), pl.BlockSpec((tm, d), lambda i: (i, 0))]
        + [HBM_SPEC] * n_hout,
        out_shape=[jax.ShapeDtypeStruct((t, d), F32), jax.ShapeDtypeStruct((t, d), BF16)] + list(hosted.out_shapes),
        scratch_shapes=list(hosted.sem_shapes),
        compiler_params=_cparams(("arbitrary",)),
    )(x, meta, g1, *hosted.operands)
    return outs[0], outs[1], list(outs[2:])


def _tri(n, upper=False):
    r = lax.broadcasted_iota(jnp.int32, (n, n), 0)
    c = lax.broadcasted_iota(jnp.int32, (n, n), 1)
    return (r <= c) if upper else (r >= c)


_NT = (((1,), (1,)), ((), ()))
_TN = (((0,), (0,)), ((), ()))


def _dot(a, b):
    return jnp.dot(a, b, preferred_element_type=F32)


def _dot_nt(a, b):
    return lax.dot_general(a, b, _NT, preferred_element_type=F32)


def _dot_tn(a, b):
    return lax.dot_general(a, b, _TN, preferred_element_type=F32)


def _chunk_decays(la, tri_f32):
    b = jnp.dot(tri_f32, la, preferred_element_type=F32, precision=lax.Precision.HIGHEST)
    bl = b[CHUNK - 1:CHUNK, :]
    return jnp.exp(b), jnp.exp(bl - b), jnp.exp(-bl), jnp.exp(bl)


def _gla_fwd(p7, alr, wau, b_alpha, gla_g, d):
    t = p7.shape[0]
    dk_all = d // 2
    dkh, dvh = dk_all // HEADS, d // HEADS
    cb = ROW_TILE
    ncb = cb // CHUNK
    scale = dkh ** -0.5

    def body(qk_ref, v_ref, r_ref, alr_ref, wau_ref, ba_ref, g_ref, o_ref, y_ref, s_ref, la_ref, st_scr):
        @pl.when(pl.program_id(0) == 0)
        def _():
            st_scr[...] = jnp.zeros_like(st_scr)

        z = jnp.dot(alr_ref[...].astype(BF16), wau_ref[...].astype(BF16), preferred_element_type=F32) + ba_ref[...]
        ls = jnp.minimum(z, 0.0) - jnp.log(1.0 + jnp.exp(-jnp.abs(z)))
        la_ref[...] = ls * (1.0 / GATE_TAU) * _row_mask(pl.program_id(0), cb)
        tri = _tri(CHUNK)
        tri_f = tri.astype(F32)
        for c in range(ncb):
            rows = slice(c * CHUNK, (c + 1) * CHUNK)
            eb, ekl, ebl_inv, gam = _chunk_decays(la_ref[rows, :], tri_f)
            for h in range(HEADS):
                ks = slice(h * dkh, (h + 1) * dkh)
                vs = slice(h * dvh, (h + 1) * dvh)
                q = qk_ref[rows, ks].astype(F32) * scale
                k = qk_ref[rows, dk_all + h * dkh:dk_all + (h + 1) * dkh].astype(F32)
                v = v_ref[rows, vs].astype(BF16)
                qb = q * eb[:, ks]
                kh = (k * ekl[:, ks]).astype(BF16)
                qc = (qb * ebl_inv[:, ks]).astype(BF16)
                a = jnp.where(tri, _dot_nt(qc, kh), 0.0)
                st = st_scr[h]
                st_b = st.astype(BF16)
                s_ref[c, h] = st_b
                o = _dot_nt(qb.astype(BF16), st_b) + _dot(a.astype(BF16), v)
                st_scr[h] = st * gam[:, ks] + _dot_tn(v, kh)
                o_ref[rows, vs] = o
                rr = lax.rsqrt(jnp.mean(o * o, axis=-1, keepdims=True) + RMS_EPS)
                r = r_ref[rows, vs].astype(F32)
                y_ref[rows, vs] = (o * rr * g_ref[:, vs] * (r * _sigmoid(r))).astype(BF16)

    return pl.pallas_call(
        body, name="gla_fwd", grid=(t // cb,),
        in_specs=[pl.BlockSpec((cb, d), lambda i: (i, 0)),
                  pl.BlockSpec((cb, d), lambda i: (i, 1)),
                  pl.BlockSpec((cb, d), lambda i: (i, 2)),
                  pl.BlockSpec((cb, RANK_PAD), lambda i: (i, 0)),
                  pl.BlockSpec((RANK_PAD, dk_all), lambda i: (0, 0)),
                  pl.BlockSpec((1, dk_all), lambda i: (0, 0)),
                  pl.BlockSpec((1, d), lambda i: (0, 0))],
        out_specs=[pl.BlockSpec((cb, d), lambda i: (i, 0)),
                   pl.BlockSpec((cb, d), lambda i: (i, 0)),
                   pl.BlockSpec((ncb, HEADS, dvh, dkh), lambda i: (i, 0, 0, 0)),
                   pl.BlockSpec((cb, dk_all), lambda i: (i, 0))],
        out_shape=[jax.ShapeDtypeStruct((t, d), F32), jax.ShapeDtypeStruct((t, d), BF16),
                   jax.ShapeDtypeStruct((t // CHUNK, HEADS, dvh, dkh), BF16),
                   jax.ShapeDtypeStruct((t, dk_all), F32)],
        scratch_shapes=[pltpu.VMEM((HEADS, dvh, dkh), F32)],
        compiler_params=_cparams(("arbitrary",)),
    )(p7, p7, p7, alr, wau, b_alpha, gla_g)


SUBLANES = 8


def _tap_phases(n_taps, first):
    phases = {}
    for j in range(n_taps):
        e = first + j
        phases.setdefault(e % SUBLANES, []).append((j, e - e % SUBLANES))
    return phases


CONV_ROWS = 64


def _shifted_windows(ext_ref, shf_ref, phases, rows, ls):
    for p, taps in phases.items():
        if p:
            span = max(off for _, off in taps) + rows
            shf_ref[p, 0:span, :] = ext_ref[p:p + span, ls]

    def window(p, start, n):
        return shf_ref[p, start:start + n, :] if p else ext_ref[start:start + n, ls]

    return window


def _conv_strip(ext_ref, shf_ref, w_ref, n_taps, first, rows, ls, emit, reverse=False):
    phases = _tap_phases(n_taps, first)
    window = _shifted_windows(ext_ref, shf_ref, phases, rows, ls)
    for r0 in range(0, rows, CONV_ROWS):
        acc = None
        for p, taps in phases.items():
            for j, off in taps:
                wj = w_ref[(n_taps - 1 - j) if reverse else j, ls]
                term = window(p, off + r0, CONV_ROWS) * wj
                acc = term if acc is None else acc + term
        emit(r0, acc)


def _corr_strip(dl_ref, ext_ref, shf_ref, acc_ref, n_taps, first, rows, ls):
    phases = _tap_phases(n_taps, first)
    window = _shifted_windows(ext_ref, shf_ref, phases, rows, ls)
    for r0 in range(0, rows, CONV_ROWS):
        dl = dl_ref[r0:r0 + CONV_ROWS, ls]
        for p, taps in phases.items():
            for j, off in taps:
                prod = dl * window(p, off + r0, CONV_ROWS)
                acc_ref[SUBLANES * j:SUBLANES * (j + 1), ls] += jnp.sum(
                    prod.reshape(CONV_ROWS // SUBLANES, SUBLANES, prod.shape[-1]), axis=0)


def _conf_fwd(p7, conv_w, conv_b, ln_g, ln_b, d):
    t = p7.shape[0]
    tm = ROW_TILE

    def body(c1_ref, c2_ref, w_ref, b_ref, g_ref, bb_ref, cc_ref, sc_ref, ext, shf):
        @pl.when(pl.program_id(0) == 0)
        def _():
            ext[0:HALO, :] = jnp.zeros((HALO, d), F32)

        ext[HALO:HALO + tm, :] = c1_ref[...].astype(F32) * _sigmoid(c2_ref[...].astype(F32))
        for s0 in range(0, d, LANES):
            ls = slice(s0, s0 + LANES)

            def emit(r0, acc, ls=ls):
                cc_ref[r0:r0 + CONV_ROWS, ls] = acc + b_ref[:, ls]

            _conv_strip(ext, shf, w_ref, CONF_K, HALO - (CONF_K - 1), tm, ls, emit)
        ext[0:HALO, :] = ext[tm:tm + HALO, :]
        g = g_ref[...]
        bb = bb_ref[...]
        rows_per_step = 2 * SUBLANES

        def rows_step(k, carry):
            rs = pl.ds(pl.multiple_of(k * rows_per_step, rows_per_step), rows_per_step)
            cc = cc_ref[rs, :]
            xc = cc - jnp.mean(cc, axis=-1, keepdims=True)
            rstd = lax.rsqrt(jnp.mean(xc * xc, axis=-1, keepdims=True) + LN_EPS)
            cn = xc * rstd * g + bb
            sc_ref[rs, :] = (cn * _sigmoid(cn)).astype(BF16)
            return carry

        lax.fori_loop(0, tm // rows_per_step, rows_step, 0, unroll=4)

    vec = pl.BlockSpec((1, d), lambda i: (0, 0))
    return pl.pallas_call(
        body, name="conf_fwd", grid=(t // tm,),
        in_specs=[pl.BlockSpec((tm, d), lambda i: (i, 3)), pl.BlockSpec((tm, d), lambda i: (i, 4)),
                  pl.BlockSpec((CONF_K, d), lambda i: (0, 0)), vec, vec, vec],
        out_specs=[pl.BlockSpec((tm, d), lambda i: (i, 0)), pl.BlockSpec((tm, d), lambda i: (i, 0))],
        out_shape=[jax.ShapeDtypeStruct((t, d), F32), jax.ShapeDtypeStruct((t, d), BF16)],
        scratch_shapes=[pltpu.VMEM((tm + HALO, d), F32), pltpu.VMEM((SUBLANES, tm + HALO, LANES), F32)],
        compiler_params=_cparams(("arbitrary",)),
    )(p7, p7, conv_w, conv_b, ln_g, ln_b)


def _ffn_mid(up, w, b, dff):
    t = up.shape[0]
    tm = ROW_TILE
    hal = 8

    def body(a_ref, bv_ref, w_ref, b_ref, y_ref, ext, shf):
        i = pl.program_id(0)

        @pl.when(i == 0)
        def _():
            ext[0:hal, :] = jnp.zeros((hal, dff), F32)

        ext[hal:hal + tm, :] = a_ref[...].astype(F32) * _row_mask(i, tm)
        for s0 in range(0, dff, LANES):
            ls = slice(s0, s0 + LANES)

            def emit(r0, acc, ls=ls):
                rb = slice(r0, r0 + CONV_ROWS)
                ac = acc + b_ref[:, ls]
                y_ref[rb, ls] = (ac * _sigmoid(ac) * bv_ref[rb, ls].astype(F32)).astype(BF16)

            _conv_strip(ext, shf, w_ref, FFN_K, hal - (FFN_K - 1), tm, ls, emit)
        ext[0:hal, :] = ext[tm:tm + hal, :]

    return pl.pallas_call(
        body, name="ffn_mid", grid=(t // tm,),
        in_specs=[pl.BlockSpec((tm, dff), lambda i: (i, 0)), pl.BlockSpec((tm, dff), lambda i: (i, 1)),
                  pl.BlockSpec((FFN_K, dff), lambda i: (0, 0)), pl.BlockSpec((1, dff), lambda i: (0, 0))],
        out_specs=pl.BlockSpec((tm, dff), lambda i: (i, 0)), out_shape=jax.ShapeDtypeStruct((t, dff), BF16),
        scratch_shapes=[pltpu.VMEM((tm + hal, dff), F32), pltpu.VMEM((SUBLANES, tm + hal, LANES), F32)],
        compiler_params=_cparams(("arbitrary",)),
    )(up, up, w, b)


def _dsilu(x, sig):
    return sig * (1.0 + x * (1.0 - sig))


def _ffn_mid_bwd(up, dy, w, b, dff):
    t = up.shape[0]
    tm = ROW_TILE
    hal = 8
    prev_rows = 2 * SUBLANES
    nt = t // tm

    def body(a_ref, ap_ref, bv_ref, dy_ref, w_ref, b_ref, dup_ref, dw_ref, db_ref, ext, dext, dw_acc, db_acc, shf):
        i = pl.program_id(0)
        tile = nt - 1 - i

        @pl.when(i == 0)
        def _():
            dext[tm:tm + hal, :] = jnp.zeros((hal, dff), F32)
            dw_acc[...] = jnp.zeros_like(dw_acc)
            db_acc[...] = jnp.zeros_like(db_acc)

        ext[0:hal, :] = ap_ref[prev_rows - hal:prev_rows, :].astype(F32) * jnp.where(tile > 0, 1.0, 0.0)
        ext[hal:hal + tm, :] = a_ref[...].astype(F32) * _row_mask(tile, tm)
        first = hal - (FFN_K - 1)
        for s0 in range(0, dff, LANES):
            ls = slice(s0, s0 + LANES)

            def emit_fwd(r0, acc, ls=ls, s0=s0):
                rb = slice(r0, r0 + CONV_ROWS)
                ac = acc + b_ref[:, ls]
                sig = _sigmoid(ac)
                dyv = dy_ref[rb, ls].astype(F32)
                dup_ref[rb, dff + s0:dff + s0 + LANES] = (dyv * ac * sig).astype(BF16)
                dac = dyv * bv_ref[rb, ls].astype(F32) * _dsilu(ac, sig)
                dext[rb, ls] = dac
                db_acc[:, ls] += jnp.sum(dac.reshape(CONV_ROWS // SUBLANES, SUBLANES, LANES), axis=0)

            _conv_strip(ext, shf, w_ref, FFN_K, first, tm, ls, emit_fwd)
            _corr_strip(dext, ext, shf, dw_acc, FFN_K, first, tm, ls)

            def emit_bwd(r0, da, ls=ls):
                rb = slice(r0, r0 + CONV_ROWS)
                mask = ((tile * tm + r0 + lax.broadcasted_iota(jnp.int32, (CONV_ROWS, 1), 0)) >= PAD).astype(F32)
                dup_ref[rb, ls] = (da * mask).astype(BF16)

            _conv_strip(dext, shf, w_ref, FFN_K, 0, tm, ls, emit_bwd, reverse=True)
        dext[tm:tm + hal, :] = dext[0:hal, :]

        @pl.when(i == nt - 1)
        def _():
            db_ref[...] = jnp.sum(db_acc[...], axis=0, keepdims=True)
            for j in range(FFN_K):
                dw_ref[j:j + 1, :] = jnp.sum(dw_acc[SUBLANES * j:SUBLANES * (j + 1), :], axis=0, keepdims=True)

    rev = lambda i: (nt - 1 - i, 0)
    return pl.pallas_call(
        body, name="ffn_mid_bwd", grid=(nt,),
        in_specs=[pl.BlockSpec((tm, dff), rev),
                  pl.BlockSpec((prev_rows, dff), lambda i: (jnp.maximum((nt - 1 - i) * (tm // prev_rows) - 1, 0), 0)),
                  pl.BlockSpec((tm, dff), lambda i: (nt - 1 - i, 1)),
                  pl.BlockSpec((tm, dff), rev),
                  pl.BlockSpec((FFN_K, dff), lambda i: (0, 0)), pl.BlockSpec((1, dff), lambda i: (0, 0))],
        out_specs=[pl.BlockSpec((tm, 2 * dff), rev),
                   pl.BlockSpec((FFN_K, dff), lambda i: (0, 0)), pl.BlockSpec((1, dff), lambda i: (0, 0))],
        out_shape=[jax.ShapeDtypeStruct((t, 2 * dff), BF16),
                   jax.ShapeDtypeStruct((FFN_K, dff), F32), jax.ShapeDtypeStruct((1, dff), F32)],
        scratch_shapes=[pltpu.VMEM((tm + hal, dff), F32), pltpu.VMEM((tm + hal, dff), F32),
                        pltpu.VMEM((SUBLANES * FFN_K, dff), F32), pltpu.VMEM((SUBLANES, dff), F32),
                        pltpu.VMEM((SUBLANES, tm + hal, LANES), F32)],
        compiler_params=_cparams(("arbitrary",)),
    )(up, up, up, dy, w, b)


DP_BLOCKS = 8
ANY_SPEC = pl.BlockSpec(memory_space=pl.ANY)


def _conf_bwd(p7, cc, dsc, conv_w, ln_g, ln_b, dp, d):
    t = p7.shape[0]
    tm = ROW_TILE
    nt = t // tm

    def body(c1_ref, c2_ref, c1p_ref, c2p_ref, cc_ref, dsc_ref, w_ref, g_ref, bb_ref, dp_in,
             dp_ref, dw_ref, db_ref, dg_ref, dbb_ref, ext, dext, dw_acc, shf):
        i = pl.program_id(0)
        tile = nt - 1 - i

        @pl.when(i == 0)
        def _():
            dext[tm:tm + HALO, :] = jnp.zeros((HALO, d), F32)
            dw_acc[...] = jnp.zeros_like(dw_acc)
            db_ref[...] = jnp.zeros_like(db_ref)
            dg_ref[...] = jnp.zeros_like(dg_ref)
            dbb_ref[...] = jnp.zeros_like(dbb_ref)

        ext[0:HALO, :] = (c1p_ref[...].astype(F32) * _sigmoid(c2p_ref[...].astype(F32))
                          * jnp.where(tile > 0, 1.0, 0.0))
        g = g_ref[...]
        bb = bb_ref[...]

        groups = 4
        grp = 2 * SUBLANES

        def fold(a):
            return a[:SUBLANES] + a[SUBLANES:]

        def rows_step(k, sums):
            sdg, sdbb, sdb = sums
            for u in range(groups):
                r0 = (k * groups + u) * grp
                rs = pl.ds(pl.multiple_of(r0, grp), grp)
                ext[pl.ds(pl.multiple_of(HALO + r0, grp), grp), :] = (
                    c1_ref[rs, :].astype(F32) * _sigmoid(c2_ref[rs, :].astype(F32)))
                cc_ = cc_ref[rs, :]
                xc = cc_ - jnp.mean(cc_, axis=-1, keepdims=True)
                rstd = lax.rsqrt(jnp.mean(xc * xc, axis=-1, keepdims=True) + LN_EPS)
                xh = xc * rstd
                cn = xh * g + bb
                dcn = dsc_ref[rs, :].astype(F32) * _dsilu(cn, _sigmoid(cn))
                dxh = dcn * g
                dcc = rstd * (dxh - jnp.mean(dxh, axis=-1, keepdims=True)
                              - xh * jnp.mean(dxh * xh, axis=-1, keepdims=True))
                dext[rs, :] = dcc
                sdg, sdbb, sdb = sdg + fold(dcn * xh), sdbb + fold(dcn), sdb + fold(dcc)
            return sdg, sdbb, sdb

        zero = jnp.zeros((SUBLANES, d), F32)
        sdg, sdbb, sdb = lax.fori_loop(0, tm // (groups * grp), rows_step, (zero, zero, zero))
        dg_ref[...] += jnp.sum(sdg, axis=0, keepdims=True)
        dbb_ref[...] += jnp.sum(sdbb, axis=0, keepdims=True)
        db_ref[...] += jnp.sum(sdb, axis=0, keepdims=True)
        for s0 in range(0, d, LANES):
            ls = slice(s0, s0 + LANES)
            _corr_strip(dext, ext, shf, dw_acc, CONF_K, HALO - (CONF_K - 1), tm, ls)

            def emit(r0, dc, ls=ls):
                rb = slice(r0, r0 + CONV_ROWS)
                s2l = _sigmoid(c2_ref[rb, ls].astype(F32))
                dp_ref[rb, ls] = (dc * s2l).astype(BF16)
                dp_ref[rb, d + ls.start:d + ls.stop] = (
                    dc * c1_ref[rb, ls].astype(F32) * s2l * (1.0 - s2l)).astype(BF16)

            _conv_strip(dext, shf, w_ref, CONF_K, 0, tm, ls, emit, reverse=True)
        dext[tm:tm + HALO, :] = dext[0:HALO, :]

        @pl.when(i == nt - 1)
        def _():
            for j in range(CONF_K):
                dw_ref[j:j + 1, :] = jnp.sum(dw_acc[SUBLANES * j:SUBLANES * (j + 1), :], axis=0, keepdims=True)

    rev = lambda i: (nt - 1 - i, 0)
    prev = lambda col: (lambda i: (jnp.maximum((nt - 1 - i) * (tm // HALO) - 1, 0), col))
    vec = pl.BlockSpec((1, d), lambda i: (0, 0))
    return pl.pallas_call(
        body, name="conf_bwd", grid=(nt,),
        in_specs=[pl.BlockSpec((tm, d), lambda i: (nt - 1 - i, 3)), pl.BlockSpec((tm, d), lambda i: (nt - 1 - i, 4)),
                  pl.BlockSpec((HALO, d), prev(3)), pl.BlockSpec((HALO, d), prev(4)),
                  pl.BlockSpec((tm, d), rev), pl.BlockSpec((tm, d), rev),
                  pl.BlockSpec((CONF_K, d), lambda i: (0, 0)), vec, vec, ANY_SPEC],
        out_specs=[pl.BlockSpec((tm, 2 * d), lambda i: (nt - 1 - i, 1)),
                   pl.BlockSpec((CONF_K, d), lambda i: (0, 0)), vec, vec, vec],
        out_shape=[jax.ShapeDtypeStruct(dp.shape, dp.dtype),
                   jax.ShapeDtypeStruct((CONF_K, d), F32)] + [jax.ShapeDtypeStruct((1, d), F32)] * 3,
        input_output_aliases={9: 0},
        scratch_shapes=[pltpu.VMEM((tm + HALO, d), F32), pltpu.VMEM((tm + HALO, d), F32),
                        pltpu.VMEM((SUBLANES * CONF_K, d), F32), pltpu.VMEM((SUBLANES, tm + HALO, LANES), F32)],
        compiler_params=_cparams(("arbitrary",)),
    )(p7, p7, p7, p7, cc, dsc, conv_w, ln_g, ln_b, dp)


def _gla_bwd(p7, log_a, alr, wau, b_alpha, gla_g, o, states, dy, dp, d):
    t = p7.shape[0]
    dk_all = d // 2
    dkh, dvh = dk_all // HEADS, d // HEADS
    cb = ROW_TILE
    ncb = cb // CHUNK
    nb = t // cb
    scale = dkh ** -0.5

    def body(qk_ref, v_ref, r_ref, la_ref, alr_ref, wau_ref, ba_ref, g_ref, o_ref, s_ref, dy_ref, dp_in,
             dp_ref, dz_ref, dg_ref, dba_ref, dst_scr, dla_scr):
        i = pl.program_id(0)
        blk = nb - 1 - i

        @pl.when(i == 0)
        def _():
            dst_scr[...] = jnp.zeros_like(dst_scr)
            dg_ref[...] = jnp.zeros_like(dg_ref)
            dba_ref[...] = jnp.zeros_like(dba_ref)

        dp_ref[:, 3 * d:4 * d] = jnp.zeros((cb, d), BF16)

        tri = _tri(CHUNK)
        tri_f = tri.astype(F32)
        triu_f = _tri(CHUNK, upper=True).astype(F32)
        for c in reversed(range(ncb)):
            rows = slice(c * CHUNK, (c + 1) * CHUNK)
            eb, ekl, ebl_inv, gam = _chunk_decays(la_ref[rows, :], tri_f)
            for h in range(HEADS):
                ks = slice(h * dkh, (h + 1) * dkh)
                kcols = slice(dk_all + h * dkh, dk_all + (h + 1) * dkh)
                vs = slice(h * dvh, (h + 1) * dvh)
                q = qk_ref[rows, ks].astype(F32) * scale
                k = qk_ref[rows, kcols].astype(F32)
                v = v_ref[rows, vs].astype(BF16)
                ebh, eklh, eih, gamh = eb[:, ks], ekl[:, ks], ebl_inv[:, ks], gam[:, ks]
                qb = q * ebh
                kh = k * eklh
                qc = qb * eih
                qb_b, kh_b, qc_b = qb.astype(BF16), kh.astype(BF16), qc.astype(BF16)
                ov = o_ref[rows, vs]
                r = r_ref[rows, vs].astype(F32)
                dyv = dy_ref[rows, vs].astype(F32)
                sig = _sigmoid(r)
                rr = lax.rsqrt(jnp.mean(ov * ov, axis=-1, keepdims=True) + RMS_EPS)
                n = ov * rr
                g = g_ref[:, vs]
                dp_ref[rows, 2 * d + h * dvh:2 * d + (h + 1) * dvh] = (dyv * n * g * _dsilu(r, sig)).astype(BF16)
                don = dyv * (r * sig)
                dg_ref[:, vs] += jnp.sum(don * n, axis=0, keepdims=True)
                dn = don * g
                do = (rr * (dn - n * jnp.mean(dn * n, axis=-1, keepdims=True))).astype(BF16)
                st_b = s_ref[c, h]
                a = jnp.where(tri, _dot_nt(qc_b, kh_b), 0.0).astype(BF16)
                da = jnp.where(tri, _dot_nt(do, v), 0.0).astype(BF16)
                dst = dst_scr[h]
                dst_b = dst.astype(BF16)
                dp_ref[rows, d + h * dvh:d + (h + 1) * dvh] = (_dot_tn(a, do) + _dot_nt(kh_b, dst_b)).astype(BF16)
                dqb = _dot(do, st_b)
                dqc = _dot(da, kh_b)
                dkh_ = _dot_tn(da, qc_b) + _dot(v, dst_b)
                dgam = jnp.sum(st_b.astype(F32) * dst, axis=0, keepdims=True)
                dst_scr[h] = dst * gamh + _dot_tn(do, qb_b)
                dp_ref[rows, ks] = ((dqb * ebh + dqc * (ebh * eih)) * scale).astype(BF16)
                dp_ref[rows, kcols] = (dkh_ * eklh).astype(BF16)
                qq = dqc * qc
                kk = dkh_ * kh
                db = dqb * qb + qq - kk
                dbl = jnp.sum(kk - qq, axis=0, keepdims=True) + dgam * gamh
                dla_scr[rows, ks] = jnp.dot(triu_f, db, preferred_element_type=F32,
                                            precision=lax.Precision.HIGHEST) + dbl
        z = jnp.dot(alr_ref[...].astype(BF16), wau_ref[...].astype(BF16), preferred_element_type=F32) + ba_ref[...]
        dz = dla_scr[...] * (1.0 / GATE_TAU) * _sigmoid(-z) * _row_mask(blk, cb)
        dba_ref[...] += jnp.sum(dz, axis=0, keepdims=True)
        dz_ref[...] = dz.astype(BF16)

    rev = lambda i: (nb - 1 - i, 0)
    row = pl.BlockSpec((cb, d), rev)
    return pl.pallas_call(
        body, name="gla_bwd", grid=(nb,),
        in_specs=[row, pl.BlockSpec((cb, d), lambda i: (nb - 1 - i, 1)), pl.BlockSpec((cb, d), lambda i: (nb - 1 - i, 2)),
                  pl.BlockSpec((cb, dk_all), rev), pl.BlockSpec((cb, RANK_PAD), rev),
                  pl.BlockSpec((RANK_PAD, dk_all), lambda i: (0, 0)), pl.BlockSpec((1, dk_all), lambda i: (0, 0)),
                  pl.BlockSpec((1, d), lambda i: (0, 0)), row,
                  pl.BlockSpec((ncb, HEADS, dvh, dkh), lambda i: (nb - 1 - i, 0, 0, 0)), row, ANY_SPEC],
        out_specs=[pl.BlockSpec((cb, 4 * d), lambda i: (nb - 1 - i, 1)), pl.BlockSpec((cb, dk_all), rev),
                   pl.BlockSpec((1, d), lambda i: (0, 0)), pl.BlockSpec((1, dk_all), lambda i: (0, 0))],
        out_shape=[jax.ShapeDtypeStruct(dp.shape, dp.dtype), jax.ShapeDtypeStruct((t, dk_all), BF16),
                   jax.ShapeDtypeStruct((1, d), F32), jax.ShapeDtypeStruct((1, dk_all), F32)],
        input_output_aliases={11: 0},
        scratch_shapes=[pltpu.VMEM((HEADS, dvh, dkh), F32), pltpu.VMEM((cb, dk_all), F32)],
        compiler_params=_cparams(("arbitrary",)),
    )(p7, p7, p7, log_a, alr, wau, b_alpha, gla_g, o, states, dy, dp)


REST = ("w_up", "w_down", "w_gla_o", "w_conf_o", "w_out")


def _chip_partials(grads, names, tag):
    place = jnp.stack([2 * lax.axis_index("x") + lax.axis_index("y"), lax.axis_index("c")])
    from_sibling = _swap_core_halves(grads, "swap_core_halves_" + tag)
    both = [_add_core_halves(place, g, s, "add_core_halves_" + n) for n, g, s in zip(names, grads, from_sibling)]
    return [b[0] for b in both], [b[1] for b in both]


def _local_step(x, target, w, shard):
    s, d = x.shape
    dk_all = d // 2
    wau = jnp.pad(w["w_alpha_up"], ((0, RANK_PAD - GLA_RANK), (0, 0)))

    h0, u1, gathered = _prep(x, w["meta_tokens"], w["norm_mix_g"], _gather_plan([shard["w_in"]]))
    w_in = _from_chip_major(gathered[0])
    lo, hi = 3 * d, 3 * d + GLA_RANK
    wq = jnp.concatenate([w_in[:, :lo], w_in[:, hi:]], axis=1)
    w_alr = jnp.pad(w_in[:, lo:hi], ((0, 0), (0, RANK_PAD - GLA_RANK)))
    shards = [shard[n] for n in REST]
    p7, gathered, _ = _matmul(u1, wq, dims="nn", name="proj", out_dtype=BF16, hosted=_gather_plan(shards))
    w = dict(w)
    for n, slabs in zip(REST, gathered):
        w[n] = slabs if n == "w_up" else slabs.reshape(-1, slabs.shape[-1])
    dff = w["w_down"].shape[0]
    alr = _matmul(u1, w_alr, dims="nn", name="proj_alr")
    o, y_gla, states, log_a = _gla_fwd(p7, alr, wau, w["b_alpha"], w["gla_norm_g"], d)
    br_gla = _matmul(y_gla, w["w_gla_o"], dims="nn", name="gla_out", out_dtype=BF16)
    cc, s_c = _conf_fwd(p7, w["conf_dw_w"], w["conf_dw_b"], w["conf_ln_g"], w["conf_ln_b"], d)
    br_conf = _matmul(s_c, w["w_conf_o"], dims="nn", name="conf_out", out_dtype=BF16)
    h1, _, (u2, merged) = _matmul(br_gla, w["w_out"], dims="nn", name="mix_out", add=h0,
                                  epilogue=_mix_epilogue(w["norm_ffn_g"], p7, br_conf, d))
    up = _matmul(u2, w["w_up"], dims="nn", name="ffn_up", out_dtype=BF16, chips="b")
    y = _ffn_mid(up, w["ffn_dw_w"], w["ffn_dw_b"], dff)
    dh2, _, (loss, d_gf) = _matmul(y, w["w_down"], dims="nn", name="ffn_down", add=h1,
                                   epilogue=_loss_epilogue(target, w["final_norm_g"]))

    g = {"final_norm_g": d_gf}
    dy = _matmul(dh2, w["w_down"], dims="nt", name="d_ffn_y", out_dtype=BF16)
    g["w_down"] = _matmul(y, dh2, dims="tn", name="dw_down", out_dtype=BF16)
    dup, g["ffn_dw_w"], g["ffn_dw_b"] = _ffn_mid_bwd(up, dy, w["ffn_dw_w"], w["ffn_dw_b"], dff)
    dh1, _, (g["norm_ffn_g"],) = _matmul(dup, w["w_up"], dims="nt", name="d_u2", chips="b",
                                         epilogue=_rms_bwd_epilogue(h1, w["norm_ffn_g"], dh2))
    g["w_up"] = _matmul(u2, dup, dims="tn", name="dw_up", out_dtype=BF16, chips="out")
    d_br_gla, _, (d_br_conf, dp) = _matmul(dh1, w["w_out"], dims="nt", name="d_merged", out_dtype=BF16,
                                           epilogue=_merge_bwd_epilogue(p7, br_gla, br_conf, d))
    g["w_out"] = _matmul(merged, dh1, dims="tn", name="dw_out", out_dtype=BF16)
    dsc = _matmul(d_br_conf, w["w_conf_o"], dims="nt", name="d_conf_s", out_dtype=BF16)
    g["w_conf_o"] = _matmul(s_c, d_br_conf, dims="tn", name="dw_conf_o", out_dtype=BF16)
    dp, g["conf_dw_w"], g["conf_dw_b"], g["conf_ln_g"], g["conf_ln_b"] = _conf_bwd(
        p7, cc, dsc, w["conf_dw_w"], w["conf_ln_g"], w["conf_ln_b"], dp, d)
    dyg = _matmul(d_br_gla, w["w_gla_o"], dims="nt", name="d_gla_y", out_dtype=BF16)
    g["w_gla_o"] = _matmul(y_gla, d_br_gla, dims="tn", name="dw_gla_o", out_dtype=BF16)
    dp, dz, g["gla_norm_g"], g["b_alpha"] = _gla_bwd(
        p7, log_a, alr, wau, w["b_alpha"], w["gla_norm_g"], o, states, dyg, dp, d)
    dalr = _matmul(dz, wau, dims="nt", name="d_alr", out_dtype=BF16)
    g["w_alpha_up"] = _matmul(alr, dz, dims="tn", name="dw_alpha_up")[:GLA_RANK]
    wq_b = jnp.concatenate([wq[:, 5 * d:], wq[:, 3 * d:5 * d], wq[:, :3 * d], jnp.zeros((d, d), wq.dtype)], axis=1)
    rest_grads = [g.pop(n) for n in REST]
    rest_grads = [a if a.ndim == 3 else a.reshape(N_CHIPS, -1, a.shape[-1]) for a in rest_grads]
    parts, owns = _chip_partials(rest_grads, REST, "rest")
    own_chip = dict(zip(REST, owns))
    dwq, arrived, _ = _matmul(u1, dp, dims="tn", name="dw_in", out_dtype=BF16, hosted=_exchange_plan(parts))
    from_chips = dict(zip(REST, arrived))
    dw_alr = _matmul(u1, dalr, dims="tn", name="dw_in_alr", out_dtype=BF16)
    dw_in = _to_chip_major(jnp.concatenate(
        [dwq[:, 4 * d:7 * d], dw_alr[:, :GLA_RANK], dwq[:, 2 * d:4 * d], dwq[:, :2 * d]], axis=1))
    parts, owns = _chip_partials([dw_in], ["w_in"], "w_in")
    own_chip["w_in"] = owns[0]
    dh0, arrived, (g["norm_mix_g"],) = _matmul(
        dp, wq_b, dims="nt", name="d_u1", hosted=_exchange_plan(parts),
        epilogue=_rms_bwd_epilogue(h0, w["norm_mix_g"], dh1, low_rank=(dalr, w_alr)))
    from_chips["w_in"] = arrived[0]
    g["meta_tokens"] = dh0[PAD:HEAD_ROWS]
    return loss, dh0[HEAD_ROWS:], g, own_chip, from_chips


HBM_SPEC = pl.BlockSpec(memory_space=pltpu.HBM)
FLIPS = ((1, 0), (0, 1), (1, 1))


def _place():
    x, y, c = lax.axis_index("x"), lax.axis_index("y"), lax.axis_index("c")
    return x, y, c


def _half_rows(ref, h, lead=()):
    rh = ref.shape[-2] // 2
    return ref.at[(*lead, pl.ds(pl.multiple_of(h * rh, 2 * SUBLANES), rh), slice(None))]


class _Hosted:
    def __init__(self, operands, out_shapes, sem_shapes, start, finish):
        self.operands, self.out_shapes, self.sem_shapes = operands, out_shapes, sem_shapes
        self.start, self.finish = start, finish


def _gather_plan(shards):
    n = len(shards)

    def copies(ins, outs, sems, kinds):
        send_sems, recv_sems = sems
        x, y, c = _place()
        chips = [(x ^ fx, y ^ fy) for fx, fy in FLIPS]
        me, sibling = (x, y, c), (x, y, 1 - c)

        def copy(k, sem, chip, h, to, src=None):
            slot = _half_rows(outs[k], h, lead=(2 * chip[0] + chip[1],))
            return pltpu.make_async_remote_copy(src_ref=slot if src is None else src, dst_ref=slot,
                                                send_sem=send_sems.at[sem], recv_sem=recv_sems.at[sem],
                                                device_id=to, device_id_type=MESH)

        def own(k):
            return pltpu.make_async_remote_copy(src_ref=ins[k], dst_ref=outs[k].at[2 * x + y],
                                                send_sem=send_sems.at[6 * n + k], recv_sem=recv_sems.at[6 * n + k],
                                                device_id=sibling, device_id_type=MESH)

        make = {
            "first": lambda k, j, chip: copy(k, 3 * k + j, (x, y), c, (*chip, c), src=_half_rows(ins[k], c)),
            "landed": lambda k, j, chip: copy(k, 3 * k + j, chip, c, me),
            "passed": lambda k, j, chip: copy(k, 3 * n + 3 * k + j, chip, c, sibling),
            "from_sibling": lambda k, j, chip: copy(k, 3 * n + 3 * k + j, chip, 1 - c, me),
        }
        return [[own(k) for k in range(n)] if kind == "own" else
                [make[kind](k, j, chip) for k in range(n) for j, chip in enumerate(chips)] for kind in kinds]

    def start(ins, outs, sems):
        first, own = copies(ins, outs, sems, ["first", "own"])
        for cp in first + own:
            cp.start()

    def finish(ins, outs, sems):
        first, landed, passed, from_sibling, own = copies(
            ins, outs, sems, ["first", "landed", "passed", "from_sibling", "own"])
        for arrived, fwd in zip(landed, passed):
            arrived.wait_recv()
            fwd.start()
        for cp in from_sibling:
            cp.wait_recv()
        for cp in own:
            cp.wait()
        for cp in first + passed:
            cp.wait_send()

    return _Hosted(list(shards), [jax.ShapeDtypeStruct((N_CHIPS, *s.shape), s.dtype) for s in shards],
                   [pltpu.SemaphoreType.DMA((7 * n,)), pltpu.SemaphoreType.DMA((7 * n,))], start, finish)


def _swap_core_halves(grads, name):
    n = len(grads)

    def body(*refs):
        ins, outs = refs[:n], refs[n:2 * n]
        send_sems, recv_sems = refs[2 * n:]
        x, y, c = _place()
        cps = [pltpu.make_async_remote_copy(
            src_ref=_half_rows(ins[k], 1 - c, lead=(slice(None),)), dst_ref=outs[k], send_sem=send_sems.at[k],
            recv_sem=recv_sems.at[k], device_id=(x, y, 1 - c), device_id_type=MESH) for k in range(n)]
        for cp in cps:
            cp.start()
        for cp in cps:
            cp.wait()

    return pl.pallas_call(
        body, name=name, in_specs=[HBM_SPEC] * n, out_specs=[HBM_SPEC] * n,
        out_shape=[jax.ShapeDtypeStruct((g.shape[0], g.shape[1] // 2, g.shape[2]), g.dtype) for g in grads],
        scratch_shapes=[pltpu.SemaphoreType.DMA((n,)), pltpu.SemaphoreType.DMA((n,))],
    )(*grads)


def _exchange_plan(parts):
    n = len(parts)

    def copies(ins, outs, sems):
        send_sems, recv_sems = sems
        x, y, c = _place()
        cps = []
        for k in range(n):
            for j, (fx, fy) in enumerate(FLIPS):
                tx, ty = x ^ fx, y ^ fy
                cps.append(pltpu.make_async_remote_copy(
                    src_ref=ins[k].at[2 * tx + ty], dst_ref=_half_rows(outs[k], c, lead=(j,)),
                    send_sem=send_sems.at[3 * k + j], recv_sem=recv_sems.at[3 * k + j],
                    device_id=(tx, ty, c), device_id_type=MESH))
        return cps

    def start(ins, outs, sems):
        for cp in copies(ins, outs, sems):
            cp.start()

    def finish(ins, outs, sems):
        for cp in copies(ins, outs, sems):
            cp.wait()

    return _Hosted(list(parts), [jax.ShapeDtypeStruct((3, 2 * p.shape[1], p.shape[2]), p.dtype) for p in parts],
                   [pltpu.SemaphoreType.DMA((3 * n,)), pltpu.SemaphoreType.DMA((3 * n,))], start, finish)


def _join_core_halves(bufs):
    n = len(bufs)

    def body(*refs):
        outs = refs[n:2 * n]
        send_sems, recv_sems = refs[2 * n:]
        x, y, c = _place()

        def rows(k, h):
            return _half_rows(outs[k], h, lead=(slice(None),) * (len(outs[k].shape) - 2))

        cps = [pltpu.make_async_remote_copy(
            src_ref=rows(k, c), dst_ref=rows(k, c), send_sem=send_sems.at[k],
            recv_sem=recv_sems.at[k], device_id=(x, y, 1 - c), device_id_type=MESH) for k in range(n)]
        for cp in cps:
            cp.start()
        for k in range(n):
            cps[k].wait_send()
            pltpu.make_async_remote_copy(
                src_ref=rows(k, c), dst_ref=rows(k, 1 - c), send_sem=send_sems.at[k],
                recv_sem=recv_sems.at[k], device_id=(x, y, 1 - c), device_id_type=MESH).wait_recv()

    return pl.pallas_call(
        body, name="join_core_halves", in_specs=[HBM_SPEC] * n, out_specs=[HBM_SPEC] * n,
        out_shape=[jax.ShapeDtypeStruct(f.shape, f.dtype) for f in bufs],
        input_output_aliases={k: k for k in range(n)},
        scratch_shapes=[pltpu.SemaphoreType.DMA((n,)), pltpu.SemaphoreType.DMA((n,))],
    )(*bufs)


def _gather_small(block, name):
    m, n = block.shape

    def body(x_ref, out_ref, send_sems, recv_sems, local_sem):
        x, y, c = _place()
        me, sibling = (x, y, c), (x, y, 1 - c)
        chips = [(x ^ fx, y ^ fy) for fx, fy in FLIPS]

        def rows(px, py, pc):
            return out_ref.at[pl.ds((4 * px + 2 * py + pc) * m, m), :]

        def copy(k, blk, to, src=None):
            return pltpu.make_async_remote_copy(
                src_ref=rows(*blk) if src is None else src, dst_ref=rows(*blk),
                send_sem=send_sems.at[k], recv_sem=recv_sems.at[k], device_id=to, device_id_type=MESH)

        mine = pltpu.make_async_copy(x_ref, rows(*me), local_sem)
        mine.start()
        first = [copy(0, me, sibling, src=x_ref)]
        first += [copy(1 + j, me, (*chip, c), src=x_ref) for j, chip in enumerate(chips)]
        for cp in first:
            cp.start()
        passed = [copy(4 + j, (*chip, c), sibling) for j, chip in enumerate(chips)]
        for j, chip in enumerate(chips):
            copy(1 + j, (*chip, c), me).wait_recv()
            passed[j].start()
        copy(0, sibling, me).wait_recv()
        for j, chip in enumerate(chips):
            copy(4 + j, (*chip, 1 - c), me).wait_recv()
        for cp in first + passed:
            cp.wait_send()
        mine.wait()

    out = pl.pallas_call(
        body, name=name,
        out_shape=jax.ShapeDtypeStruct((8 * m, n), block.dtype),
        in_specs=[pl.BlockSpec(memory_space=pltpu.VMEM)],
        out_specs=pl.BlockSpec(memory_space=pltpu.VMEM),
        scratch_shapes=[pltpu.SemaphoreType.DMA((7,)), pltpu.SemaphoreType.DMA((7,)), pltpu.SemaphoreType.DMA],
    )(block)
    return out.reshape(8, m, n)


def _add_core_halves(place, grad, from_sibling, name):
    nc, r, cols = grad.shape
    rh = r // 2

    def body(place_ref, g_ref, s_ref, o_ref, own_ref):
        total = (g_ref[...].astype(F32) + s_ref[...].astype(F32)).astype(BF16)
        o_ref[...] = total

        @pl.when(pl.program_id(0) == place_ref[0])
        def _():
            own_ref[...] = total[0]

    spec = pl.BlockSpec((1, rh, cols), lambda t, place_ref: (t, 0, 0))
    return pl.pallas_call(
        body, name=name,
        grid_spec=pltpu.PrefetchScalarGridSpec(
            num_scalar_prefetch=1, grid=(nc,),
            in_specs=[pl.BlockSpec((1, rh, cols), lambda t, place_ref: (t, place_ref[1], 0)), spec],
            out_specs=[spec, pl.BlockSpec((rh, cols), lambda t, place_ref: (place_ref[1], 0))]),
        out_shape=[jax.ShapeDtypeStruct((nc, rh, cols), BF16), jax.ShapeDtypeStruct((r, cols), BF16)],
        compiler_params=_cparams(("arbitrary",)),
    )(place, grad, from_sibling)


def _sum_adamw(own, others, w, m, v, name):
    rws, cols = w.shape
    tr = rws
    for cand in (256, 128, 64, 32, 16):
        if rws % cand == 0 and cand * cols * 4 <= 2 * 1024 * 1024:
            tr = cand
            break
    c1 = 1.0 - ADAM_B1 ** ADAM_STEP
    c2 = 1.0 - ADAM_B2 ** ADAM_STEP

    def body(a_ref, b_ref, w_ref, m_ref, v_ref, g_ref, d_ref, nm_ref, nv_ref):
        gv = a_ref[...].astype(F32)
        for j in range(3):
            gv = gv + b_ref[j].astype(F32)
        nm = ADAM_B1 * m_ref[...] + (1.0 - ADAM_B1) * gv
        nv = ADAM_B2 * v_ref[...] + (1.0 - ADAM_B2) * (gv * gv)
        m_hat = nm / c1
        v_hat = nv / c2
        g_ref[...] = gv
        d_ref[...] = -ADAM_LR * (m_hat / (jnp.sqrt(v_hat) + ADAM_EPS) + ADAM_WD * w_ref[...])
        nm_ref[...] = nm
        nv_ref[...] = nv

    spec = pl.BlockSpec((tr, cols), lambda i: (i, 0))
    return pl.pallas_call(
        body, name=name, grid=(rws // tr,),
        in_specs=[spec, pl.BlockSpec((3, tr, cols), lambda i: (0, i, 0))] + [spec] * 3, out_specs=[spec] * 4,
        out_shape=[jax.ShapeDtypeStruct((rws, cols), F32)] * 4,
        compiler_params=_cparams(("parallel",)),
    )(own, others, w, m, v)


def _sum_devices(blocks):
    n, m, _ = blocks.shape

    def body(b_ref, o_ref):
        acc = b_ref[0]
        for j in range(1, n):
            acc = acc + b_ref[j]
        o_ref[...] = acc

    return pl.pallas_call(
        body, name="sum_devices", out_shape=jax.ShapeDtypeStruct((m, LANES), F32),
        in_specs=[pl.BlockSpec(memory_space=pltpu.VMEM)], out_specs=pl.BlockSpec(memory_space=pltpu.VMEM),
    )(blocks)


def _adamw(w, g, m, v, name):
    rws, cols = w.shape
    tr = rws
    for cand in (256, 128, 64, 32, 16, 8):
        if rws % cand == 0 and cand * cols * 4 <= 2 * 1024 * 1024:
            tr = cand
            break
    c1 = 1.0 - ADAM_B1 ** ADAM_STEP
    c2 = 1.0 - ADAM_B2 ** ADAM_STEP

    def body(w_ref, g_ref, m_ref, v_ref, d_ref, nm_ref, nv_ref):
        gv = g_ref[...]
        nm = ADAM_B1 * m_ref[...] + (1.0 - ADAM_B1) * gv
        nv = ADAM_B2 * v_ref[...] + (1.0 - ADAM_B2) * (gv * gv)
        m_hat = nm / c1
        v_hat = nv / c2
        d_ref[...] = -ADAM_LR * (m_hat / (jnp.sqrt(v_hat) + ADAM_EPS) + ADAM_WD * w_ref[...])
        nm_ref[...] = nm
        nv_ref[...] = nv

    spec = pl.BlockSpec((tr, cols), lambda i: (i, 0))
    return pl.pallas_call(
        body, name=name, grid=(rws // tr,), in_specs=[spec] * 4, out_specs=[spec] * 3,
        out_shape=[jax.ShapeDtypeStruct((rws, cols), F32)] * 3,
        compiler_params=_cparams(("parallel",)),
    )(w, g, m, v)


WEIGHTS = (
    ("meta_tokens", (16, 1024), 1), ("norm_mix_g", (1024,), None), ("w_in", (1024, 7184), 1),
    ("w_alpha_up", (16, 512), 1), ("b_alpha", (512,), None), ("gla_norm_g", (1024,), None),
    ("w_gla_o", (1024, 1024), 0), ("conf_dw_w", (31, 1024), 1), ("conf_dw_b", (1024,), None),
    ("conf_ln_g", (1024,), None), ("conf_ln_b", (1024,), None), ("w_conf_o", (1024, 1024), 0),
    ("w_out", (1024, 1024), 0), ("norm_ffn_g", (1024,), None), ("w_up", (1024, 5632), 1),
    ("ffn_dw_w", (3, 2816), 1), ("ffn_dw_b", (2816,), None), ("w_down", (2816, 1024), 0),
    ("final_norm_g", (1024,), None),
)
BIG = ("w_in", "w_up", "w_down", "w_gla_o", "w_conf_o", "w_out")
SMALL_SHARDED = ("meta_tokens", "w_alpha_up", "conf_dw_w", "ffn_dw_w")
REPLICATED = tuple(n for n, _, ax in WEIGHTS if ax is None)
SHAPES = {n: s for n, s, _ in WEIGHTS}
AXIS = {n: ax for n, _, ax in WEIGHTS}
N_CHIPS = 4


def _shard_shape(name):
    s = list(SHAPES[name])
    s[AXIS[name]] //= N_CHIPS
    return tuple(s)


def _pack(parts, mult):
    flat = jnp.concatenate([p.reshape(-1) for p in parts])
    pad = (-flat.shape[0]) % mult
    return jnp.pad(flat, (0, pad))


def _unpack(flat, names, shape_of):
    out, off = {}, 0
    for n in names:
        shp = shape_of(n)
        size = math.prod(shp)
        out[n] = flat[off:off + size].reshape(shp)
        off += size
    return out


def _to_chip_major(full):
    r, cols = full.shape
    return full.reshape(r, N_CHIPS, cols // N_CHIPS).transpose(1, 0, 2)


def _from_chip_major(slabs):
    nc, r, cs = slabs.shape
    return slabs.transpose(1, 0, 2).reshape(r, nc * cs)


def kernel(x, meta_tokens, norm_mix_g, w_in, w_alpha_up, b_alpha, gla_norm_g, w_gla_o, conf_dw_w, conf_dw_b, conf_ln_g, conf_ln_b, w_conf_o, w_out, norm_ffn_g, w_up, ffn_dw_w, ffn_dw_b, w_down, final_norm_g, loss_target, m_meta_tokens, m_norm_mix_g, m_w_in, m_w_alpha_up, m_b_alpha, m_gla_norm_g, m_w_gla_o, m_conf_dw_w, m_conf_dw_b, m_conf_ln_g, m_conf_ln_b, m_w_conf_o, m_w_out, m_norm_ffn_g, m_w_up, m_ffn_dw_w, m_ffn_dw_b, m_w_down, m_final_norm_g, v_meta_tokens, v_norm_mix_g, v_w_in, v_w_alpha_up, v_b_alpha, v_gla_norm_g, v_w_gla_o, v_conf_dw_w, v_conf_dw_b, v_conf_ln_g, v_conf_ln_b, v_w_conf_o, v_w_out, v_norm_ffn_g, v_w_up, v_ffn_dw_w, v_ffn_dw_b, v_w_down, v_final_norm_g):
    names = [n for n, _, _ in WEIGHTS]
    w_args = (meta_tokens, norm_mix_g, w_in, w_alpha_up, b_alpha, gla_norm_g, w_gla_o, conf_dw_w, conf_dw_b, conf_ln_g,
              conf_ln_b, w_conf_o, w_out, norm_ffn_g, w_up, ffn_dw_w, ffn_dw_b, w_down, final_norm_g)
    m_args = (m_meta_tokens, m_norm_mix_g, m_w_in, m_w_alpha_up, m_b_alpha, m_gla_norm_g, m_w_gla_o, m_conf_dw_w,
              m_conf_dw_b, m_conf_ln_g, m_conf_ln_b, m_w_conf_o, m_w_out, m_norm_ffn_g, m_w_up, m_ffn_dw_w, m_ffn_dw_b,
              m_w_down, m_final_norm_g)
    v_args = (v_meta_tokens, v_norm_mix_g, v_w_in, v_w_alpha_up, v_b_alpha, v_gla_norm_g, v_w_gla_o, v_conf_dw_w,
              v_conf_dw_b, v_conf_ln_g, v_conf_ln_b, v_w_conf_o, v_w_out, v_norm_ffn_g, v_w_up, v_ffn_dw_w, v_ffn_dw_b,
              v_w_down, v_final_norm_g)
    in_shape = {n: a.shape for n, a in zip(names, w_args)}
    local = {n: a.reshape(_shard_shape(n) if AXIS[n] is not None else SHAPES[n]) for n, a in zip(names, w_args)}
    m_loc = {n: a.reshape(local[n].shape) for n, a in zip(names, m_args)}
    v_loc = {n: a.reshape(local[n].shape) for n, a in zip(names, v_args)}

    me = 2 * lax.axis_index("x") + lax.axis_index("y")

    small = _pack([local[n] for n in SMALL_SHARDED], 8 * LANES).reshape(-1, LANES)
    small_all = _gather_small(small, "gather_small_weights")[::2].reshape(N_CHIPS, -1)
    per_chip_small = [_unpack(small_all[t], SMALL_SHARDED, _shard_shape) for t in range(N_CHIPS)]
    full = {n: jnp.concatenate([per_chip_small[t][n] for t in range(N_CHIPS)], axis=1) for n in SMALL_SHARDED}
    for n in REPLICATED:
        full[n] = local[n].reshape(1, -1)

    loss_part, grad_x, grads, own_chip, from_chips = _local_step(
        x[0], loss_target[0], full, {n: local[n].astype(BF16) for n in BIG})

    smalls = REPLICATED + SMALL_SHARDED
    rep = _pack([grads[n] for n in smalls] + [loss_part], 8 * LANES).reshape(-1, LANES)
    rep_sum = _sum_devices(_gather_small(rep, "gather_small_grads")).reshape(-1)
    g_loc = _unpack(rep_sum, smalls, lambda n: SHAPES[n])
    loss = rep_sum[sum(math.prod(SHAPES[n]) for n in smalls)]
    for n in SMALL_SHARDED:
        width = _shard_shape(n)[1]
        g_loc[n] = lax.dynamic_slice_in_dim(g_loc[n], me * width, width, axis=1)

    joined = _join_core_halves([own_chip[n] for n in BIG] + [from_chips[n] for n in BIG])
    delta, new_m, new_v = {}, {}, {}
    for k, n in enumerate(BIG):
        g_loc[n], delta[n], new_m[n], new_v[n] = _sum_adamw(
            joined[k], joined[len(BIG) + k], local[n], m_loc[n], v_loc[n], "sum_adamw_" + n)
    rest = SMALL_SHARDED + REPLICATED
    pk = lambda dct: _pack([dct[n] for n in rest], 8 * LANES).reshape(-1, LANES)
    ds, ms, vs = _adamw(pk(local), pk(g_loc), pk(m_loc), pk(v_loc), "adamw_small")
    shape_loc = lambda n: local[n].shape
    for dct, flat in ((delta, ds), (new_m, ms), (new_v, vs)):
        dct.update(_unpack(flat.reshape(-1), rest, shape_loc))

    outs = [loss, grad_x[None]]
    for dct in (g_loc, delta, new_m, new_v):
        outs += [dct[n].reshape(in_shape[n]) for n in names]
    return tuple(outs)
```

```python
import functools
import math

import jax
import jax.numpy as jnp
from jax import lax
from jax.experimental import pallas as pl
from jax.experimental.pallas import tpu as pltpu

F32 = jnp.float32
BF16 = jnp.bfloat16

N_META = 16
PAD = 240
HEAD_ROWS = PAD + N_META
HEADS = 4
GLA_RANK = 16
RANK_PAD = 128
GATE_TAU = 16.0
CHUNK = 64
CONF_K = 31
FFN_K = 3
RMS_EPS = 1e-6
LN_EPS = 1e-5
ADAM_LR, ADAM_B1, ADAM_B2, ADAM_EPS, ADAM_WD, ADAM_STEP = 0.001, 0.9, 0.999, 1e-08, 0.01, 10

ROW_TILE = 256
HALO = 32
LANES = 128
V7X_VMEM_LIMIT = 56 * 1024 * 1024
MESH = pl.DeviceIdType.MESH


def _cparams(sem):
    return pltpu.CompilerParams(dimension_semantics=sem, vmem_limit_bytes=V7X_VMEM_LIMIT)


def _sigmoid(x):
    return 1.0 / (1.0 + jnp.exp(-x))


def _pick(n, prefs):
    for p in prefs:
        if n % p == 0:
            return p
    return n


MATMUL_TILES = {
    "proj": (2816, 1024, 1024), "gla_out": (1408, 1024, 1024), "conf_out": (1408, 1024, 1024),
    "mix_out": (768, 1024, 1024), "ffn_up": (1408, 1408, 1024), "ffn_down": (768, 1024, 2816),
    "d_ffn_y": (1408, 1408, 1024), "d_u2": (768, 1024, 1408), "d_merged": (768, 1024, 1024),
    "d_conf_s": (1408, 1024, 1024), "d_gla_y": (1408, 1024, 1024), "d_u1": (768, 1024, 2048),
    "dw_in": (1024, 1024, 2816), "dw_up": (1024, 1408, 1408), "dw_down": (1408, 1024, 1408),
    "dw_out": (1024, 1024, 1408), "dw_conf_o": (1024, 1024, 1408), "dw_gla_o": (1024, 1024, 1408),
}


class _Epilogue:
    def __init__(self, operands, in_specs, out_shapes, out_specs, fn, init=None, lhs=None):
        self.operands, self.in_specs, self.out_shapes, self.out_specs = operands, in_specs, out_shapes, out_specs
        self.fn, self.init, self.lhs = fn, init, lhs


def _matmul(a, b, *, dims, name, tm=None, tn=None, tk=None, out_dtype=F32, add=None, chips=None, hosted=None,
            epilogue=None):
    if chips == "b":
        nc, r, cs = b.shape
        b_shape = (r, nc * cs)
    else:
        b_shape = b.shape
    if dims == "nn":
        (m, k), (_, n) = a.shape, b_shape
    elif dims == "nt":
        (m, k), (n, _) = a.shape, b_shape
    else:
        (k, m), (_, n) = a.shape, b_shape
    want = MATMUL_TILES.get(name, (None, None, None))
    tm, tn, tk = tm or want[0], tn or want[1], tk or want[2]
    tm = tm if tm and m % tm == 0 else _pick(m, (768, 1024, 1408, 512, 256, 128))
    tn = tn if tn and n % tn == 0 else _pick(n, (1024, 1408, 512, 256, 128))
    tk = tk if tk and k % tk == 0 else _pick(k, (1024, 768, 1408, 512, 256, 128))
    if chips == "b":
        tn, tk = (cs, tk) if dims == "nn" else (tn, cs)
    if chips == "out":
        tn, tm = n // N_CHIPS, m
    nk = k // tk
    assert m % tm == 0 and n % tn == 0 and k % tk == 0, (name, m, n, k, tm, tn, tk)
    assert epilogue is None or epilogue.lhs is None or (nk == 1 and n == tn), name
    a_spec = {"nn": pl.BlockSpec((tm, tk), lambda i, j, kk: (i, kk)),
              "nt": pl.BlockSpec((tm, tk), lambda i, j, kk: (i, kk)),
              "tn": pl.BlockSpec((tk, tm), lambda i, j, kk: (kk, i))}[dims]
    if chips == "b":
        b_spec = {"nn": pl.BlockSpec((None, tk, tn), lambda i, j, kk: (j, kk, 0)),
                  "nt": pl.BlockSpec((None, tn, tk), lambda i, j, kk: (kk, j, 0))}[dims]
    else:
        b_spec = {"nn": pl.BlockSpec((tk, tn), lambda i, j, kk: (kk, j)),
                  "nt": pl.BlockSpec((tn, tk), lambda i, j, kk: (j, kk)),
                  "tn": pl.BlockSpec((tk, tn), lambda i, j, kk: (kk, j))}[dims]
    contract = {"nn": (((1,), (0,)), ((), ())), "nt": (((1,), (1,)), ((), ())), "tn": (((0,), (0,)), ((), ()))}[dims]
    if chips == "out":
        o_spec = pl.BlockSpec((None, tm, tn), lambda i, j, kk: (j, 0, 0))
        out_struct = jax.ShapeDtypeStruct((N_CHIPS, m, tn), out_dtype)
    else:
        o_spec = pl.BlockSpec((tm, tn), lambda i, j, kk: (i, j))
        out_struct = jax.ShapeDtypeStruct((m, n), out_dtype)
    has_add = add is not None

    n_hin = len(hosted.operands) if hosted is not None else 0
    n_hout = len(hosted.out_shapes) if hosted is not None else 0
    e_operands = []
    if epilogue is not None:
        e_operands = epilogue.operands(tm) if callable(epilogue.operands) else epilogue.operands
    n_ein = len(e_operands)
    n_eout = len(epilogue.out_shapes) if epilogue is not None else 0

    def body(*refs):
        refs = list(refs)
        a_ref, b_ref = refs[:2]
        add_ref = refs[2] if has_add else None
        pos = 2 + has_add
        h_ins, pos = refs[pos:pos + n_hin], pos + n_hin
        e_ins, pos = refs[pos:pos + n_ein], pos + n_ein
        o_ref, pos = refs[pos], pos + 1
        h_outs, pos = refs[pos:pos + n_hout], pos + n_hout
        e_outs, pos = refs[pos:pos + n_eout], pos + n_eout
        acc_ref = refs[pos] if nk > 1 else None
        h_sems = refs[pos + (nk > 1):]
        i, j, kk = pl.program_id(0), pl.program_id(1), pl.program_id(2)
        first = (i == 0) & (j == 0) & (kk == 0)
        if hosted is not None:
            @pl.when(first)
            def _():
                hosted.start(h_ins, h_outs, h_sems)

        if epilogue is not None and epilogue.init is not None:
            @pl.when(first)
            def _():
                epilogue.init(e_outs)

        def finish(total):
            if epilogue is None:
                o_ref[...] = total.astype(out_dtype)
            else:
                epilogue.fn(total, i, o_ref, e_ins, e_outs)

        if epilogue is not None and epilogue.lhs is not None:
            lhs = epilogue.lhs(a_ref, e_ins, e_outs)
        else:
            lhs = a_ref[...].astype(BF16)
        prod = lax.dot_general(lhs, b_ref[...].astype(BF16), contract, preferred_element_type=F32)
        if nk == 1:
            finish(prod + add_ref[...].astype(F32) if has_add else prod)
        else:
            @pl.when(kk == 0)
            def _():
                acc_ref[...] = prod + add_ref[...].astype(F32) if has_add else prod

            @pl.when((kk > 0) & (kk < nk - 1))
            def _():
                acc_ref[...] += prod

            @pl.when(kk == nk - 1)
            def _():
                finish(acc_ref[...] + prod)

        if hosted is not None:
            @pl.when((i == m // tm - 1) & (j == n // tn - 1) & (kk == nk - 1))
            def _():
                hosted.finish(h_ins, h_outs, h_sems)

    in_specs = ([a_spec, b_spec] + ([o_spec] if has_add else []) + [HBM_SPEC] * n_hin
                + (list(epilogue.in_specs(tm)) if epilogue is not None else []))
    args = ((a, b) + ((add,) if has_add else ()) + (tuple(hosted.operands) if hosted is not None else ())
            + tuple(e_operands))
    serial = hosted is not None or epilogue is not None
    outs = pl.pallas_call(
        body, name=name, grid=(m // tm, n // tn, nk),
        in_specs=in_specs,
        out_specs=[o_spec] + [HBM_SPEC] * n_hout + (list(epilogue.out_specs(tm)) if epilogue is not None else []),
        out_shape=([out_struct] + (list(hosted.out_shapes) if hosted is not None else [])
                   + (list(epilogue.out_shapes) if epilogue is not None else [])),
        scratch_shapes=([pltpu.VMEM((tm, tn), F32)] if nk > 1 else [])
        + (list(hosted.sem_shapes) if hosted is not None else []),
        compiler_params=_cparams(("arbitrary",) * 3 if serial else ("parallel", "parallel", "arbitrary")),
    )(*args)
    if not serial:
        return outs[0]
    return outs[0], list(outs[1:1 + n_hout]), list(outs[1 + n_hout:])


def _tile_rows(n):
    return lambda tm: pl.BlockSpec((tm, n), lambda i, j, kk: (i, 0))


def _one_row(n):
    return pl.BlockSpec((1, n), lambda i, j, kk: (0, 0))


def _mix_epilogue(g, p7, br_conf, d):
    rows = p7.shape[0]
    n = g.shape[1]

    def in_specs(tm):
        return [_one_row(n), pl.BlockSpec((tm, d), lambda i, j, kk: (i, 5)),
                pl.BlockSpec((tm, d), lambda i, j, kk: (i, 6)), _tile_rows(d)(tm)]

    def lhs(a_ref, ins, outs):
        merged = (_sigmoid(ins[1][...].astype(F32)) * a_ref[...].astype(F32)
                  + _sigmoid(ins[2][...].astype(F32)) * ins[3][...].astype(F32)).astype(BF16)
        outs[1][...] = merged
        return merged

    def fn(total, i, o_ref, ins, outs):
        o_ref[...] = total
        r = lax.rsqrt(jnp.mean(total * total, axis=-1, keepdims=True) + RMS_EPS)
        outs[0][...] = (total * r * ins[0][...]).astype(BF16)

    return _Epilogue([g, p7, p7, br_conf], in_specs,
                     [jax.ShapeDtypeStruct((rows, n), BF16), jax.ShapeDtypeStruct((rows, d), BF16)],
                     lambda tm: [_tile_rows(n)(tm), _tile_rows(d)(tm)], fn, lhs=lhs)


def _rms_bwd_epilogue(h, g, dres, low_rank=None):
    n = g.shape[1]
    bf16_copy = low_rank is None
    operands = [h, dres, g] + (list(low_rank) if low_rank is not None else [])

    def in_specs(tm):
        specs = [_tile_rows(n)(tm), _tile_rows(n)(tm), _one_row(n)]
        if low_rank is not None:
            specs += [_tile_rows(low_rank[0].shape[1])(tm),
                      pl.BlockSpec(low_rank[1].shape, lambda i, j, kk: (0, 0))]
        return specs

    def init(outs):
        outs[0][...] = jnp.zeros_like(outs[0])

    def fn(total, i, o_ref, ins, outs):
        if low_rank is not None:
            total = total + _dot_nt(ins[3][...], ins[4][...])
        x = ins[0][...]
        r = lax.rsqrt(jnp.mean(x * x, axis=-1, keepdims=True) + RMS_EPS)
        nrm = x * r
        outs[0][...] += jnp.sum(total * nrm, axis=0, keepdims=True)
        dn = total * ins[2][...]
        dh = ins[1][...] + r * (dn - nrm * jnp.mean(dn * nrm, axis=-1, keepdims=True))
        o_ref[...] = dh
        if bf16_copy:
            outs[1][...] = dh.astype(BF16)

    return _Epilogue(operands, in_specs,
                     [jax.ShapeDtypeStruct((1, n), F32)] + [jax.ShapeDtypeStruct((h.shape[0], n), BF16)] * bf16_copy,
                     lambda tm: [_one_row(n)] + [_tile_rows(n)(tm)] * bf16_copy, fn, init)


def _merge_bwd_epilogue(p7, br_gla, br_conf, d):
    t = p7.shape[0]

    def in_specs(tm):
        return [pl.BlockSpec((tm, d), lambda i, j, kk: (i, 5)), pl.BlockSpec((tm, d), lambda i, j, kk: (i, 6)),
                _tile_rows(d)(tm), _tile_rows(d)(tm)]

    def fn(total, i, o_ref, ins, outs):
        sg = _sigmoid(ins[0][...].astype(F32))
        sc = _sigmoid(ins[1][...].astype(F32))
        o_ref[...] = (total * sg).astype(BF16)
        outs[0][...] = (total * sc).astype(BF16)
        outs[1][:, 0:d] = (total * ins[2][...].astype(F32) * sg * (1.0 - sg)).astype(BF16)
        outs[1][:, d:2 * d] = (total * ins[3][...].astype(F32) * sc * (1.0 - sc)).astype(BF16)

    return _Epilogue([p7, p7, br_gla, br_conf], in_specs,
                     [jax.ShapeDtypeStruct((t, d), BF16), jax.ShapeDtypeStruct((t, DP_BLOCKS * d), BF16)],
                     lambda tm: [_tile_rows(d)(tm), pl.BlockSpec((tm, 2 * d), lambda i, j, kk: (i, 0))], fn)


def _loss_epilogue(target, gf):
    n = gf.shape[1]

    def in_specs(tm):
        nb = tm // ROW_TILE
        return [pl.BlockSpec((ROW_TILE, n), (lambda s: lambda i, j, kk: (jnp.maximum(i * nb + s - 1, 0), 0))(s))
                for s in range(nb)] + [_one_row(n)]

    def init(outs):
        outs[0][...] = jnp.zeros_like(outs[0])
        outs[1][...] = jnp.zeros_like(outs[1])

    def fn(total, i, o_ref, ins, outs):
        nb = len(ins) - 1
        g = ins[nb][...]
        for s in range(nb):
            rows = slice(s * ROW_TILE, (s + 1) * ROW_TILE)
            live = jnp.where(i * nb + s > 0, 1.0, 0.0)
            h = total[rows]
            r = lax.rsqrt(jnp.mean(h * h, axis=-1, keepdims=True) + RMS_EPS)
            nrm = h * r
            err = (nrm * g - ins[s][...]) * live
            outs[0][...] += 0.5 * jnp.sum(jnp.mean(err * err, axis=-1, keepdims=True), axis=0, keepdims=True)
            dout = err * (1.0 / n)
            outs[1][...] += jnp.sum(dout * nrm, axis=0, keepdims=True)
            dn = dout * g
            dh = r * (dn - nrm * jnp.mean(dn * nrm, axis=-1, keepdims=True))
            o_ref[rows, :] = dh
            outs[2][rows, :] = dh.astype(BF16)

    nb_ops = lambda tm: [target] * (tm // ROW_TILE) + [gf]
    rows_all = target.shape[0] + HEAD_ROWS
    return _Epilogue(nb_ops, in_specs,
                     [jax.ShapeDtypeStruct((1, 1), F32), jax.ShapeDtypeStruct((1, n), F32),
                      jax.ShapeDtypeStruct((rows_all, n), BF16)],
                     lambda tm: [pl.BlockSpec((1, 1), lambda i, j, kk: (0, 0)), _one_row(n), _tile_rows(n)(tm)], fn, init)


def _row_mask(tile_index, rows):
    r = tile_index * rows + lax.broadcasted_iota(jnp.int32, (rows, 1), 0)
    return (r >= PAD).astype(F32)


def _prep(x, meta, g1, hosted):
    s, d = x.shape
    t = HEAD_ROWS + s
    tm = ROW_TILE
    nt = t // tm
    n_hin, n_hout = len(hosted.operands), len(hosted.out_shapes)

    def body(*refs):
        x_ref, meta_ref, g_ref = refs[:3]
        h_ins = refs[3:3 + n_hin]
        h_ref, u_ref = refs[3 + n_hin:5 + n_hin]
        h_outs = refs[5 + n_hin:5 + n_hin + n_hout]
        h_sems = refs[5 + n_hin + n_hout:]
        i = pl.program_id(0)

        @pl.when(i == 0)
        def _():
            hosted.start(h_ins, h_outs, h_sems)
            h_ref[0:PAD, :] = jnp.zeros((PAD, d), F32)
            h_ref[PAD:HEAD_ROWS, :] = meta_ref[...]

        @pl.when(i > 0)
        def _():
            h_ref[...] = x_ref[...]

        h = h_ref[...]
        r = lax.rsqrt(jnp.mean(h * h, axis=-1, keepdims=True) + RMS_EPS)
        u_ref[...] = (h * r * g_ref[...]).astype(BF16)

        @pl.when(i == nt - 1)
        def _():
            hosted.finish(h_ins, h_outs, h_sems)

    outs = pl.pallas_call(
        body, name="prep", grid=(nt,),
        in_specs=[pl.BlockSpec((tm, d), lambda i: (jnp.maximum(i - 1, 0), 0)),
                  pl.BlockSpec((N_META, d), lambda i: (0, 0)),
                  pl.BlockSpec((1, d), lambda i: (0, 0))] + [HBM_SPEC] * n_hin,
        out_specs=[pl.BlockSpec((tm, d), lambda i: (i, 0)), pl.BlockSpec((tm, d), lambda i: (i, 0))]
        + [HBM_SPEC] * n_hout,
        out_shape=[jax.ShapeDtypeStruct((t, d), F32), jax.ShapeDtypeStruct((t, d), BF16)] + list(hosted.out_shapes),
        scratch_shapes=list(hosted.sem_shapes),
        compiler_params=_cparams(("arbitrary",)),
    )(x, meta, g1, *hosted.operands)
    return outs[0], outs[1], list(outs[2:])


def _tri(n, upper=False):
    r = lax.broadcasted_iota(jnp.int32, (n, n), 0)
    c = lax.broadcasted_iota(jnp.int32, (n, n), 1)
    return (r <= c) if upper else (r >= c)


_NT = (((1,), (1,)), ((), ()))
_TN = (((0,), (0,)), ((), ()))


def _dot(a, b):
    return jnp.dot(a, b, preferred_element_type=F32)


def _dot_nt(a, b):
    return lax.dot_general(a, b, _NT, preferred_element_type=F32)


def _dot_tn(a, b):
    return lax.dot_general(a, b, _TN, preferred_element_type=F32)


def _chunk_decays(la, tri_f32):
    b = jnp.dot(tri_f32, la, preferred_element_type=F32, precision=lax.Precision.HIGHEST)
    bl = b[CHUNK - 1:CHUNK, :]
    return jnp.exp(b), jnp.exp(bl - b), jnp.exp(-bl), jnp.exp(bl)


def _gla_fwd(p7, alr, wau, b_alpha, gla_g, d):
    t = p7.shape[0]
    dk_all = d // 2
    dkh, dvh = dk_all // HEADS, d // HEADS
    cb = ROW_TILE
    ncb = cb // CHUNK
    scale = dkh ** -0.5

    def body(qk_ref, v_ref, r_ref, alr_ref, wau_ref, ba_ref, g_ref, o_ref, y_ref, s_ref, la_ref, st_scr):
        @pl.when(pl.program_id(0) == 0)
        def _():
            st_scr[...] = jnp.zeros_like(st_scr)

        z = jnp.dot(alr_ref[...].astype(BF16), wau_ref[...].astype(BF16), preferred_element_type=F32) + ba_ref[...]
        ls = jnp.minimum(z, 0.0) - jnp.log(1.0 + jnp.exp(-jnp.abs(z)))
        la_ref[...] = ls * (1.0 / GATE_TAU) * _row_mask(pl.program_id(0), cb)
        tri = _tri(CHUNK)
        tri_f = tri.astype(F32)
        for c in range(ncb):
            rows = slice(c * CHUNK, (c + 1) * CHUNK)
            eb, ekl, ebl_inv, gam = _chunk_decays(la_ref[rows, :], tri_f)
            for h in range(HEADS):
                ks = slice(h * dkh, (h + 1) * dkh)
                vs = slice(h * dvh, (h + 1) * dvh)
                q = qk_ref[rows, ks].astype(F32) * scale
                k = qk_ref[rows, dk_all + h * dkh:dk_all + (h + 1) * dkh].astype(F32)
                v = v_ref[rows, vs].astype(BF16)
                qb = q * eb[:, ks]
                kh = (k * ekl[:, ks]).astype(BF16)
                qc = (qb * ebl_inv[:, ks]).astype(BF16)
                a = jnp.where(tri, _dot_nt(qc, kh), 0.0)
                st = st_scr[h]
                st_b = st.astype(BF16)
                s_ref[c, h] = st_b
                o = _dot_nt(qb.astype(BF16), st_b) + _dot(a.astype(BF16), v)
                st_scr[h] = st * gam[:, ks] + _dot_tn(v, kh)
                o_ref[rows, vs] = o
                rr = lax.rsqrt(jnp.mean(o * o, axis=-1, keepdims=True) + RMS_EPS)
                r = r_ref[rows, vs].astype(F32)
                y_ref[rows, vs] = (o * rr * g_ref[:, vs] * (r * _sigmoid(r))).astype(BF16)

    return pl.pallas_call(
        body, name="gla_fwd", grid=(t // cb,),
        in_specs=[pl.BlockSpec((cb, d), lambda i: (i, 0)),
                  pl.BlockSpec((cb, d), lambda i: (i, 1)),
                  pl.BlockSpec((cb, d), lambda i: (i, 2)),
                  pl.BlockSpec((cb, RANK_PAD), lambda i: (i, 0)),
                  pl.BlockSpec((RANK_PAD, dk_all), lambda i: (0, 0)),
                  pl.BlockSpec((1, dk_all), lambda i: (0, 0)),
                  pl.BlockSpec((1, d), lambda i: (0, 0))],
        out_specs=[pl.BlockSpec((cb, d), lambda i: (i, 0)),
                   pl.BlockSpec((cb, d), lambda i: (i, 0)),
                   pl.BlockSpec((ncb, HEADS, dvh, dkh), lambda i: (i, 0, 0, 0)),
                   pl.BlockSpec((cb, dk_all), lambda i: (i, 0))],
        out_shape=[jax.ShapeDtypeStruct((t, d), F32), jax.ShapeDtypeStruct((t, d), BF16),
                   jax.ShapeDtypeStruct((t // CHUNK, HEADS, dvh, dkh), BF16),
                   jax.ShapeDtypeStruct((t, dk_all), F32)],
        scratch_shapes=[pltpu.VMEM((HEADS, dvh, dkh), F32)],
        compiler_params=_cparams(("arbitrary",)),
    )(p7, p7, p7, alr, wau, b_alpha, gla_g)


SUBLANES = 8


def _tap_phases(n_taps, first):
    phases = {}
    for j in range(n_taps):
        e = first + j
        phases.setdefault(e % SUBLANES, []).append((j, e - e % SUBLANES))
    return phases


CONV_ROWS = 64


def _shifted_windows(ext_ref, shf_ref, phases, rows, ls):
    for p, taps in phases.items():
        if p:
            span = max(off for _, off in taps) + rows
            shf_ref[p, 0:span, :] = ext_ref[p:p + span, ls]

    def window(p, start, n):
        return shf_ref[p, start:start + n, :] if p else ext_ref[start:start + n, ls]

    return window


def _conv_strip(ext_ref, shf_ref, w_ref, n_taps, first, rows, ls, emit, reverse=False):
    phases = _tap_phases(n_taps, first)
    window = _shifted_windows(ext_ref, shf_ref, phases, rows, ls)
    for r0 in range(0, rows, CONV_ROWS):
        acc = None
        for p, taps in phases.items():
            for j, off in taps:
                wj = w_ref[(n_taps - 1 - j) if reverse else j, ls]
                term = window(p, off + r0, CONV_ROWS) * wj
                acc = term if acc is None else acc + term
        emit(r0, acc)


def _corr_strip(dl_ref, ext_ref, shf_ref, acc_ref, n_taps, first, rows, ls):
    phases = _tap_phases(n_taps, first)
    window = _shifted_windows(ext_ref, shf_ref, phases, rows, ls)
    for r0 in range(0, rows, CONV_ROWS):
        dl = dl_ref[r0:r0 + CONV_ROWS, ls]
        for p, taps in phases.items():
            for j, off in taps:
                prod = dl * window(p, off + r0, CONV_ROWS)
                acc_ref[SUBLANES * j:SUBLANES * (j + 1), ls] += jnp.sum(
                    prod.reshape(CONV_ROWS // SUBLANES, SUBLANES, prod.shape[-1]), axis=0)


def _conf_fwd(p7, conv_w, conv_b, ln_g, ln_b, d):
    t = p7.shape[0]
    tm = ROW_TILE

    def body(c1_ref, c2_ref, w_ref, b_ref, g_ref, bb_ref, cc_ref, sc_ref, ext, shf):
        @pl.when(pl.program_id(0) == 0)
        def _():
            ext[0:HALO, :] = jnp.zeros((HALO, d), F32)

        ext[HALO:HALO + tm, :] = c1_ref[...].astype(F32) * _sigmoid(c2_ref[...].astype(F32))
        for s0 in range(0, d, LANES):
            ls = slice(s0, s0 + LANES)

            def emit(r0, acc, ls=ls):
                cc_ref[r0:r0 + CONV_ROWS, ls] = acc + b_ref[:, ls]

            _conv_strip(ext, shf, w_ref, CONF_K, HALO - (CONF_K - 1), tm, ls, emit)
        ext[0:HALO, :] = ext[tm:tm + HALO, :]
        g = g_ref[...]
        bb = bb_ref[...]
        rows_per_step = 2 * SUBLANES

        def rows_step(k, carry):
            rs = pl.ds(pl.multiple_of(k * rows_per_step, rows_per_step), rows_per_step)
            cc = cc_ref[rs, :]
            xc = cc - jnp.mean(cc, axis=-1, keepdims=True)
            rstd = lax.rsqrt(jnp.mean(xc * xc, axis=-1, keepdims=True) + LN_EPS)
            cn = xc * rstd * g + bb
            sc_ref[rs, :] = (cn * _sigmoid(cn)).astype(BF16)
            return carry

        lax.fori_loop(0, tm // rows_per_step, rows_step, 0, unroll=4)

    vec = pl.BlockSpec((1, d), lambda i: (0, 0))
    return pl.pallas_call(
        body, name="conf_fwd", grid=(t // tm,),
        in_specs=[pl.BlockSpec((tm, d), lambda i: (i, 3)), pl.BlockSpec((tm, d), lambda i: (i, 4)),
                  pl.BlockSpec((CONF_K, d), lambda i: (0, 0)), vec, vec, vec],
        out_specs=[pl.BlockSpec((tm, d), lambda i: (i, 0)), pl.BlockSpec((tm, d), lambda i: (i, 0))],
        out_shape=[jax.ShapeDtypeStruct((t, d), F32), jax.ShapeDtypeStruct((t, d), BF16)],
        scratch_shapes=[pltpu.VMEM((tm + HALO, d), F32), pltpu.VMEM((SUBLANES, tm + HALO, LANES), F32)],
        compiler_params=_cparams(("arbitrary",)),
    )(p7, p7, conv_w, conv_b, ln_g, ln_b)


def _ffn_mid(up, w, b, dff):
    t = up.shape[0]
    tm = ROW_TILE
    hal = 8

    def body(a_ref, bv_ref, w_ref, b_ref, y_ref, ext, shf):
        i = pl.program_id(0)

        @pl.when(i == 0)
        def _():
            ext[0:hal, :] = jnp.zeros((hal, dff), F32)

        ext[hal:hal + tm, :] = a_ref[...].astype(F32) * _row_mask(i, tm)
        for s0 in range(0, dff, LANES):
            ls = slice(s0, s0 + LANES)

            def emit(r0, acc, ls=ls):
                rb = slice(r0, r0 + CONV_ROWS)
                ac = acc + b_ref[:, ls]
                y_ref[rb, ls] = (ac * _sigmoid(ac) * bv_ref[rb, ls].astype(F32)).astype(BF16)

            _conv_strip(ext, shf, w_ref, FFN_K, hal - (FFN_K - 1), tm, ls, emit)
        ext[0:hal, :] = ext[tm:tm + hal, :]

    return pl.pallas_call(
        body, name="ffn_mid", grid=(t // tm,),
        in_specs=[pl.BlockSpec((tm, dff), lambda i: (i, 0)), pl.BlockSpec((tm, dff), lambda i: (i, 1)),
                  pl.BlockSpec((FFN_K, dff), lambda i: (0, 0)), pl.BlockSpec((1, dff), lambda i: (0, 0))],
        out_specs=pl.BlockSpec((tm, dff), lambda i: (i, 0)), out_shape=jax.ShapeDtypeStruct((t, dff), BF16),
        scratch_shapes=[pltpu.VMEM((tm + hal, dff), F32), pltpu.VMEM((SUBLANES, tm + hal, LANES), F32)],
        compiler_params=_cparams(("arbitrary",)),
    )(up, up, w, b)


def _dsilu(x, sig):
    return sig * (1.0 + x * (1.0 - sig))


def _ffn_mid_bwd(up, dy, w, b, dff):
    t = up.shape[0]
    tm = ROW_TILE
    hal = 8
    prev_rows = 2 * SUBLANES
    nt = t // tm

    def body(a_ref, ap_ref, bv_ref, dy_ref, w_ref, b_ref, dup_ref, dw_ref, db_ref, ext, dext, dw_acc, db_acc, shf):
        i = pl.program_id(0)
        tile = nt - 1 - i

        @pl.when(i == 0)
        def _():
            dext[tm:tm + hal, :] = jnp.zeros((hal, dff), F32)
            dw_acc[...] = jnp.zeros_like(dw_acc)
            db_acc[...] = jnp.zeros_like(db_acc)

        ext[0:hal, :] = ap_ref[prev_rows - hal:prev_rows, :].astype(F32) * jnp.where(tile > 0, 1.0, 0.0)
        ext[hal:hal + tm, :] = a_ref[...].astype(F32) * _row_mask(tile, tm)
        first = hal - (FFN_K - 1)
        for s0 in range(0, dff, LANES):
            ls = slice(s0, s0 + LANES)

            def emit_fwd(r0, acc, ls=ls, s0=s0):
                rb = slice(r0, r0 + CONV_ROWS)
                ac = acc + b_ref[:, ls]
                sig = _sigmoid(ac)
                dyv = dy_ref[rb, ls].astype(F32)
                dup_ref[rb, dff + s0:dff + s0 + LANES] = (dyv * ac * sig).astype(BF16)
                dac = dyv * bv_ref[rb, ls].astype(F32) * _dsilu(ac, sig)
                dext[rb, ls] = dac
                db_acc[:, ls] += jnp.sum(dac.reshape(CONV_ROWS // SUBLANES, SUBLANES, LANES), axis=0)

            _conv_strip(ext, shf, w_ref, FFN_K, first, tm, ls, emit_fwd)
            _corr_strip(dext, ext, shf, dw_acc, FFN_K, first, tm, ls)

            def emit_bwd(r0, da, ls=ls):
                rb = slice(r0, r0 + CONV_ROWS)
                mask = ((tile * tm + r0 + lax.broadcasted_iota(jnp.int32, (CONV_ROWS, 1), 0)) >= PAD).astype(F32)
                dup_ref[rb, ls] = (da * mask).astype(BF16)

            _conv_strip(dext, shf, w_ref, FFN_K, 0, tm, ls, emit_bwd, reverse=True)
        dext[tm:tm + hal, :] = dext[0:hal, :]

        @pl.when(i == nt - 1)
        def _():
            db_ref[...] = jnp.sum(db_acc[...], axis=0, keepdims=True)
            for j in range(FFN_K):
                dw_ref[j:j + 1, :] = jnp.sum(dw_acc[SUBLANES * j:SUBLANES * (j + 1), :], axis=0, keepdims=True)

    rev = lambda i: (nt - 1 - i, 0)
    return pl.pallas_call(
        body, name="ffn_mid_bwd", grid=(nt,),
        in_specs=[pl.BlockSpec((tm, dff), rev),
                  pl.BlockSpec((prev_rows, dff), lambda i: (jnp.maximum((nt - 1 - i) * (tm // prev_rows) - 1, 0), 0)),
                  pl.BlockSpec((tm, dff), lambda i: (nt - 1 - i, 1)),
                  pl.BlockSpec((tm, dff), rev),
                  pl.BlockSpec((FFN_K, dff), lambda i: (0, 0)), pl.BlockSpec((1, dff), lambda i: (0, 0))],
        out_specs=[pl.BlockSpec((tm, 2 * dff), rev),
                   pl.BlockSpec((FFN_K, dff), lambda i: (0, 0)), pl.BlockSpec((1, dff), lambda i: (0, 0))],
        out_shape=[jax.ShapeDtypeStruct((t, 2 * dff), BF16),
                   jax.ShapeDtypeStruct((FFN_K, dff), F32), jax.ShapeDtypeStruct((1, dff), F32)],
        scratch_shapes=[pltpu.VMEM((tm + hal, dff), F32), pltpu.VMEM((tm + hal, dff), F32),
                        pltpu.VMEM((SUBLANES * FFN_K, dff), F32), pltpu.VMEM((SUBLANES, dff), F32),
                        pltpu.VMEM((SUBLANES, tm + hal, LANES), F32)],
        compiler_params=_cparams(("arbitrary",)),
    )(up, up, up, dy, w, b)


DP_BLOCKS = 8
ANY_SPEC = pl.BlockSpec(memory_space=pl.ANY)


def _conf_bwd(p7, cc, dsc, conv_w, ln_g, ln_b, dp, d):
    t = p7.shape[0]
    tm = ROW_TILE
    nt = t // tm

    def body(c1_ref, c2_ref, c1p_ref, c2p_ref, cc_ref, dsc_ref, w_ref, g_ref, bb_ref, dp_in,
             dp_ref, dw_ref, db_ref, dg_ref, dbb_ref, ext, dext, dw_acc, shf):
        i = pl.program_id(0)
        tile = nt - 1 - i

        @pl.when(i == 0)
        def _():
            dext[tm:tm + HALO, :] = jnp.zeros((HALO, d), F32)
            dw_acc[...] = jnp.zeros_like(dw_acc)
            db_ref[...] = jnp.zeros_like(db_ref)
            dg_ref[...] = jnp.zeros_like(dg_ref)
            dbb_ref[...] = jnp.zeros_like(dbb_ref)

        ext[0:HALO, :] = (c1p_ref[...].astype(F32) * _sigmoid(c2p_ref[...].astype(F32))
                          * jnp.where(tile > 0, 1.0, 0.0))
        g = g_ref[...]
        bb = bb_ref[...]

        groups = 4
        grp = 2 * SUBLANES

        def fold(a):
            return a[:SUBLANES] + a[SUBLANES:]

        def rows_step(k, sums):
            sdg, sdbb, sdb = sums
            for u in range(groups):
                r0 = (k * groups + u) * grp
                rs = pl.ds(pl.multiple_of(r0, grp), grp)
                ext[pl.ds(pl.multiple_of(HALO + r0, grp), grp), :] = (
                    c1_ref[rs, :].astype(F32) * _sigmoid(c2_ref[rs, :].astype(F32)))
                cc_ = cc_ref[rs, :]
                xc = cc_ - jnp.mean(cc_, axis=-1, keepdims=True)
                rstd = lax.rsqrt(jnp.mean(xc * xc, axis=-1, keepdims=True) + LN_EPS)
                xh = xc * rstd
                cn = xh * g + bb
                dcn = dsc_ref[rs, :].astype(F32) * _dsilu(cn, _sigmoid(cn))
                dxh = dcn * g
                dcc = rstd * (dxh - jnp.mean(dxh, axis=-1, keepdims=True)
                              - xh * jnp.mean(dxh * xh, axis=-1, keepdims=True))
                dext[rs, :] = dcc
                sdg, sdbb, sdb = sdg + fold(dcn * xh), sdbb + fold(dcn), sdb + fold(dcc)
            return sdg, sdbb, sdb

        zero = jnp.zeros((SUBLANES, d), F32)
        sdg, sdbb, sdb = lax.fori_loop(0, tm // (groups * grp), rows_step, (zero, zero, zero))
        dg_ref[...] += jnp.sum(sdg, axis=0, keepdims=True)
        dbb_ref[...] += jnp.sum(sdbb, axis=0, keepdims=True)
        db_ref[...] += jnp.sum(sdb, axis=0, keepdims=True)
        for s0 in range(0, d, LANES):
            ls = slice(s0, s0 + LANES)
            _corr_strip(dext, ext, shf, dw_acc, CONF_K, HALO - (CONF_K - 1), tm, ls)

            def emit(r0, dc, ls=ls):
                rb = slice(r0, r0 + CONV_ROWS)
                s2l = _sigmoid(c2_ref[rb, ls].astype(F32))
                dp_ref[rb, ls] = (dc * s2l).astype(BF16)
                dp_ref[rb, d + ls.start:d + ls.stop] = (
                    dc * c1_ref[rb, ls].astype(F32) * s2l * (1.0 - s2l)).astype(BF16)

            _conv_strip(dext, shf, w_ref, CONF_K, 0, tm, ls, emit, reverse=True)
        dext[tm:tm + HALO, :] = dext[0:HALO, :]

        @pl.when(i == nt - 1)
        def _():
            for j in range(CONF_K):
                dw_ref[j:j + 1, :] = jnp.sum(dw_acc[SUBLANES * j:SUBLANES * (j + 1), :], axis=0, keepdims=True)

    rev = lambda i: (nt - 1 - i, 0)
    prev = lambda col: (lambda i: (jnp.maximum((nt - 1 - i) * (tm // HALO) - 1, 0), col))
    vec = pl.BlockSpec((1, d), lambda i: (0, 0))
    return pl.pallas_call(
        body, name="conf_bwd", grid=(nt,),
        in_specs=[pl.BlockSpec((tm, d), lambda i: (nt - 1 - i, 3)), pl.BlockSpec((tm, d), lambda i: (nt - 1 - i, 4)),
                  pl.BlockSpec((HALO, d), prev(3)), pl.BlockSpec((HALO, d), prev(4)),
                  pl.BlockSpec((tm, d), rev), pl.BlockSpec((tm, d), rev),
                  pl.BlockSpec((CONF_K, d), lambda i: (0, 0)), vec, vec, ANY_SPEC],
        out_specs=[pl.BlockSpec((tm, 2 * d), lambda i: (nt - 1 - i, 1)),
                   pl.BlockSpec((CONF_K, d), lambda i: (0, 0)), vec, vec, vec],
        out_shape=[jax.ShapeDtypeStruct(dp.shape, dp.dtype),
                   jax.ShapeDtypeStruct((CONF_K, d), F32)] + [jax.ShapeDtypeStruct((1, d), F32)] * 3,
        input_output_aliases={9: 0},
        scratch_shapes=[pltpu.VMEM((tm + HALO, d), F32), pltpu.VMEM((tm + HALO, d), F32),
                        pltpu.VMEM((SUBLANES * CONF_K, d), F32), pltpu.VMEM((SUBLANES, tm + HALO, LANES), F32)],
        compiler_params=_cparams(("arbitrary",)),
    )(p7, p7, p7, p7, cc, dsc, conv_w, ln_g, ln_b, dp)


def _gla_bwd(p7, log_a, alr, wau, b_alpha, gla_g, o, states, dy, dp, d):
    t = p7.shape[0]
    dk_all = d // 2
    dkh, dvh = dk_all // HEADS, d // HEADS
    cb = ROW_TILE
    ncb = cb // CHUNK
    nb = t // cb
    scale = dkh ** -0.5

    def body(qk_ref, v_ref, r_ref, la_ref, alr_ref, wau_ref, ba_ref, g_ref, o_ref, s_ref, dy_ref, dp_in,
             dp_ref, dz_ref, dg_ref, dba_ref, dst_scr, dla_scr):
        i = pl.program_id(0)
        blk = nb - 1 - i

        @pl.when(i == 0)
        def _():
            dst_scr[...] = jnp.zeros_like(dst_scr)
            dg_ref[...] = jnp.zeros_like(dg_ref)
            dba_ref[...] = jnp.zeros_like(dba_ref)

        dp_ref[:, 3 * d:4 * d] = jnp.zeros((cb, d), BF16)

        tri = _tri(CHUNK)
        tri_f = tri.astype(F32)
        triu_f = _tri(CHUNK, upper=True).astype(F32)
        for c in reversed(range(ncb)):
            rows = slice(c * CHUNK, (c + 1) * CHUNK)
            eb, ekl, ebl_inv, gam = _chunk_decays(la_ref[rows, :], tri_f)
            for h in range(HEADS):
                ks = slice(h * dkh, (h + 1) * dkh)
                kcols = slice(dk_all + h * dkh, dk_all + (h + 1) * dkh)
                vs = slice(h * dvh, (h + 1) * dvh)
                q = qk_ref[rows, ks].astype(F32) * scale
                k = qk_ref[rows, kcols].astype(F32)
                v = v_ref[rows, vs].astype(BF16)
                ebh, eklh, eih, gamh = eb[:, ks], ekl[:, ks], ebl_inv[:, ks], gam[:, ks]
                qb = q * ebh
                kh = k * eklh
                qc = qb * eih
                qb_b, kh_b, qc_b = qb.astype(BF16), kh.astype(BF16), qc.astype(BF16)
                ov = o_ref[rows, vs]
                r = r_ref[rows, vs].astype(F32)
                dyv = dy_ref[rows, vs].astype(F32)
                sig = _sigmoid(r)
                rr = lax.rsqrt(jnp.mean(ov * ov, axis=-1, keepdims=True) + RMS_EPS)
                n = ov * rr
                g = g_ref[:, vs]
                dp_ref[rows, 2 * d + h * dvh:2 * d + (h + 1) * dvh] = (dyv * n * g * _dsilu(r, sig)).astype(BF16)
                don = dyv * (r * sig)
                dg_ref[:, vs] += jnp.sum(don * n, axis=0, keepdims=True)
                dn = don * g
                do = (rr * (dn - n * jnp.mean(dn * n, axis=-1, keepdims=True))).astype(BF16)
                st_b = s_ref[c, h]
                a = jnp.where(tri, _dot_nt(qc_b, kh_b), 0.0).astype(BF16)
                da = jnp.where(tri, _dot_nt(do, v), 0.0).astype(BF16)
                dst = dst_scr[h]
                dst_b = dst.astype(BF16)
                dp_ref[rows, d + h * dvh:d + (h + 1) * dvh] = (_dot_tn(a, do) + _dot_nt(kh_b, dst_b)).astype(BF16)
                dqb = _dot(do, st_b)
                dqc = _dot(da, kh_b)
                dkh_ = _dot_tn(da, qc_b) + _dot(v, dst_b)
                dgam = jnp.sum(st_b.astype(F32) * dst, axis=0, keepdims=True)
                dst_scr[h] = dst * gamh + _dot_tn(do, qb_b)
                dp_ref[rows, ks] = ((dqb * ebh + dqc * (ebh * eih)) * scale).astype(BF16)
                dp_ref[rows, kcols] = (dkh_ * eklh).astype(BF16)
                qq = dqc * qc
                kk = dkh_ * kh
                db = dqb * qb + qq - kk
                dbl = jnp.sum(kk - qq, axis=0, keepdims=True) + dgam * gamh
                dla_scr[rows, ks] = jnp.dot(triu_f, db, preferred_element_type=F32,
                                            precision=lax.Precision.HIGHEST) + dbl
        z = jnp.dot(alr_ref[...].astype(BF16), wau_ref[...].astype(BF16), preferred_element_type=F32) + ba_ref[...]
        dz = dla_scr[...] * (1.0 / GATE_TAU) * _sigmoid(-z) * _row_mask(blk, cb)
        dba_ref[...] += jnp.sum(dz, axis=0, keepdims=True)
        dz_ref[...] = dz.astype(BF16)

    rev = lambda i: (nb - 1 - i, 0)
    row = pl.BlockSpec((cb, d), rev)
    return pl.pallas_call(
        body, name="gla_bwd", grid=(nb,),
        in_specs=[row, pl.BlockSpec((cb, d), lambda i: (nb - 1 - i, 1)), pl.BlockSpec((cb, d), lambda i: (nb - 1 - i, 2)),
                  pl.BlockSpec((cb, dk_all), rev), pl.BlockSpec((cb, RANK_PAD), rev),
                  pl.BlockSpec((RANK_PAD, dk_all), lambda i: (0, 0)), pl.BlockSpec((1, dk_all), lambda i: (0, 0)),
                  pl.BlockSpec((1, d), lambda i: (0, 0)), row,
                  pl.BlockSpec((ncb, HEADS, dvh, dkh), lambda i: (nb - 1 - i, 0, 0, 0)), row, ANY_SPEC],
        out_specs=[pl.BlockSpec((cb, 4 * d), lambda i: (nb - 1 - i, 1)), pl.BlockSpec((cb, dk_all), rev),
                   pl.BlockSpec((1, d), lambda i: (0, 0)), pl.BlockSpec((1, dk_all), lambda i: (0, 0))],
        out_shape=[jax.ShapeDtypeStruct(dp.shape, dp.dtype), jax.ShapeDtypeStruct((t, dk_all), BF16),
                   jax.ShapeDtypeStruct((1, d), F32), jax.ShapeDtypeStruct((1, dk_all), F32)],
        input_output_aliases={11: 0},
        scratch_shapes=[pltpu.VMEM((HEADS, dvh, dkh), F32), pltpu.VMEM((cb, dk_all), F32)],
        compiler_params=_cparams(("arbitrary",)),
    )(p7, p7, p7, log_a, alr, wau, b_alpha, gla_g, o, states, dy, dp)


REST = ("w_up", "w_down", "w_gla_o", "w_conf_o", "w_out")


def _chip_partials(grads, names, tag):
    place = jnp.stack([2 * lax.axis_index("x") + lax.axis_index("y"), lax.axis_index("c")])
    from_sibling = _swap_core_halves(grads, "swap_core_halves_" + tag)
    both = [_add_core_halves(place, g, s, "add_core_halves_" + n) for n, g, s in zip(names, grads, from_sibling)]
    return [b[0] for b in both], [b[1] for b in both]


def _local_step(x, target, w, shard):
    s, d = x.shape
    dk_all = d // 2
    wau = jnp.pad(w["w_alpha_up"], ((0, RANK_PAD - GLA_RANK), (0, 0)))

    h0, u1, gathered = _prep(x, w["meta_tokens"], w["norm_mix_g"], _gather_plan([shard["w_in"]]))
    w_in = _from_chip_major(gathered[0])
    lo, hi = 3 * d, 3 * d + GLA_RANK
    wq = jnp.concatenate([w_in[:, :lo], w_in[:, hi:]], axis=1)
    w_alr = jnp.pad(w_in[:, lo:hi], ((0, 0), (0, RANK_PAD - GLA_RANK)))
    shards = [shard[n] for n in REST]
    p7, gathered, _ = _matmul(u1, wq, dims="nn", name="proj", out_dtype=BF16, hosted=_gather_plan(shards))
    w = dict(w)
    for n, slabs in zip(REST, gathered):
        w[n] = slabs if n == "w_up" else slabs.reshape(-1, slabs.shape[-1])
    dff = w["w_down"].shape[0]
    alr = _matmul(u1, w_alr, dims="nn", name="proj_alr")
    o, y_gla, states, log_a = _gla_fwd(p7, alr, wau, w["b_alpha"], w["gla_norm_g"], d)
    br_gla = _matmul(y_gla, w["w_gla_o"], dims="nn", name="gla_out", out_dtype=BF16)
    cc, s_c = _conf_fwd(p7, w["conf_dw_w"], w["conf_dw_b"], w["conf_ln_g"], w["conf_ln_b"], d)
    br_conf = _matmul(s_c, w["w_conf_o"], dims="nn", name="conf_out", out_dtype=BF16)
    h1, _, (u2, merged) = _matmul(br_gla, w["w_out"], dims="nn", name="mix_out", add=h0,
                                  epilogue=_mix_epilogue(w["norm_ffn_g"], p7, br_conf, d))
    up = _matmul(u2, w["w_up"], dims="nn", name="ffn_up", out_dtype=BF16, chips="b")
    y = _ffn_mid(up, w["ffn_dw_w"], w["ffn_dw_b"], dff)
    dh2, _, (loss, d_gf, dh2_b) = _matmul(y, w["w_down"], dims="nn", name="ffn_down", add=h1,
                                          epilogue=_loss_epilogue(target, w["final_norm_g"]))

    g = {"final_norm_g": d_gf}
    dy = _matmul(dh2_b, w["w_down"], dims="nt", name="d_ffn_y", out_dtype=BF16)
    g["w_down"] = _matmul(y, dh2_b, dims="tn", name="dw_down", out_dtype=BF16)
    dup, g["ffn_dw_w"], g["ffn_dw_b"] = _ffn_mid_bwd(up, dy, w["ffn_dw_w"], w["ffn_dw_b"], dff)
    dh1, _, (g["norm_ffn_g"], dh1_b) = _matmul(dup, w["w_up"], dims="nt", name="d_u2", chips="b",
                                               epilogue=_rms_bwd_epilogue(h1, w["norm_ffn_g"], dh2))
    g["w_up"] = _matmul(u2, dup, dims="tn", name="dw_up", out_dtype=BF16, chips="out")
    d_br_gla, _, (d_br_conf, dp) = _matmul(dh1_b, w["w_out"], dims="nt", name="d_merged", out_dtype=BF16,
                                           epilogue=_merge_bwd_epilogue(p7, br_gla, br_conf, d))
    g["w_out"] = _matmul(merged, dh1_b, dims="tn", name="dw_out", out_dtype=BF16)
    dsc = _matmul(d_br_conf, w["w_conf_o"], dims="nt", name="d_conf_s", out_dtype=BF16)
    g["w_conf_o"] = _matmul(s_c, d_br_conf, dims="tn", name="dw_conf_o", out_dtype=BF16)
    dp, g["conf_dw_w"], g["conf_dw_b"], g["conf_ln_g"], g["conf_ln_b"] = _conf_bwd(
        p7, cc, dsc, w["conf_dw_w"], w["conf_ln_g"], w["conf_ln_b"], dp, d)
    dyg = _matmul(d_br_gla, w["w_gla_o"], dims="nt", name="d_gla_y", out_dtype=BF16)
    g["w_gla_o"] = _matmul(y_gla, d_br_gla, dims="tn", name="dw_gla_o", out_dtype=BF16)
    dp, dz, g["gla_norm_g"], g["b_alpha"] = _gla_bwd(
        p7, log_a, alr, wau, w["b_alpha"], w["gla_norm_g"], o, states, dyg, dp, d)
    dalr = _matmul(dz, wau, dims="nt", name="d_alr", out_dtype=BF16)
    g["w_alpha_up"] = _matmul(alr, dz, dims="tn", name="dw_alpha_up")[:GLA_RANK]
    wq_b = jnp.concatenate([wq[:, 5 * d:], wq[:, 3 * d:5 * d], wq[:, :3 * d], jnp.zeros((d, d), wq.dtype)], axis=1)
    rest_grads = [g.pop(n) for n in REST]
    rest_grads = [a if a.ndim == 3 else a.reshape(N_CHIPS, -1, a.shape[-1]) for a in rest_grads]
    parts, owns = _chip_partials(rest_grads, REST, "rest")
    own_chip = dict(zip(REST, owns))
    dwq, arrived, _ = _matmul(u1, dp, dims="tn", name="dw_in", out_dtype=BF16, hosted=_exchange_plan(parts))
    from_chips = dict(zip(REST, arrived))
    dw_alr = _matmul(u1, dalr, dims="tn", name="dw_in_alr", out_dtype=BF16)
    dw_in = _to_chip_major(jnp.concatenate(
        [dwq[:, 4 * d:7 * d], dw_alr[:, :GLA_RANK], dwq[:, 2 * d:4 * d], dwq[:, :2 * d]], axis=1))
    parts, owns = _chip_partials([dw_in], ["w_in"], "w_in")
    own_chip["w_in"] = owns[0]
    dh0, arrived, (g["norm_mix_g"],) = _matmul(
        dp, wq_b, dims="nt", name="d_u1", hosted=_exchange_plan(parts),
        epilogue=_rms_bwd_epilogue(h0, w["norm_mix_g"], dh1, low_rank=(dalr, w_alr)))
    from_chips["w_in"] = arrived[0]
    g["meta_tokens"] = dh0[PAD:HEAD_ROWS]
    return loss, dh0[HEAD_ROWS:], g, own_chip, from_chips


HBM_SPEC = pl.BlockSpec(memory_space=pltpu.HBM)
FLIPS = ((1, 0), (0, 1), (1, 1))


def _place():
    x, y, c = lax.axis_index("x"), lax.axis_index("y"), lax.axis_index("c")
    return x, y, c


def _half_rows(ref, h, lead=()):
    rh = ref.shape[-2] // 2
    return ref.at[(*lead, pl.ds(pl.multiple_of(h * rh, 2 * SUBLANES), rh), slice(None))]


class _Hosted:
    def __init__(self, operands, out_shapes, sem_shapes, start, finish):
        self.operands, self.out_shapes, self.sem_shapes = operands, out_shapes, sem_shapes
        self.start, self.finish = start, finish


def _gather_plan(shards):
    n = len(shards)

    def copies(ins, outs, sems, kinds):
        send_sems, recv_sems = sems
        x, y, c = _place()
        chips = [(x ^ fx, y ^ fy) for fx, fy in FLIPS]
        me, sibling = (x, y, c), (x, y, 1 - c)

        def copy(k, sem, chip, h, to, src=None):
            slot = _half_rows(outs[k], h, lead=(2 * chip[0] + chip[1],))
            return pltpu.make_async_remote_copy(src_ref=slot if src is None else src, dst_ref=slot,
                                                send_sem=send_sems.at[sem], recv_sem=recv_sems.at[sem],
                                                device_id=to, device_id_type=MESH)

        def own(k):
            return pltpu.make_async_remote_copy(src_ref=ins[k], dst_ref=outs[k].at[2 * x + y],
                                                send_sem=send_sems.at[6 * n + k], recv_sem=recv_sems.at[6 * n + k],
                                                device_id=sibling, device_id_type=MESH)

        make = {
            "first": lambda k, j, chip: copy(k, 3 * k + j, (x, y), c, (*chip, c), src=_half_rows(ins[k], c)),
            "landed": lambda k, j, chip: copy(k, 3 * k + j, chip, c, me),
            "passed": lambda k, j, chip: copy(k, 3 * n + 3 * k + j, chip, c, sibling),
            "from_sibling": lambda k, j, chip: copy(k, 3 * n + 3 * k + j, chip, 1 - c, me),
        }
        return [[own(k) for k in range(n)] if kind == "own" else
                [make[kind](k, j, chip) for k in range(n) for j, chip in enumerate(chips)] for kind in kinds]

    def start(ins, outs, sems):
        first, own = copies(ins, outs, sems, ["first", "own"])
        for cp in first + own:
            cp.start()

    def finish(ins, outs, sems):
        first, landed, passed, from_sibling, own = copies(
            ins, outs, sems, ["first", "landed", "passed", "from_sibling", "own"])
        for arrived, fwd in zip(landed, passed):
            arrived.wait_recv()
            fwd.start()
        for cp in from_sibling:
            cp.wait_recv()
        for cp in own:
            cp.wait()
        for cp in first + passed:
            cp.wait_send()

    return _Hosted(list(shards), [jax.ShapeDtypeStruct((N_CHIPS, *s.shape), s.dtype) for s in shards],
                   [pltpu.SemaphoreType.DMA((7 * n,)), pltpu.SemaphoreType.DMA((7 * n,))], start, finish)


def _swap_core_halves(grads, name):
    n = len(grads)

    def body(*refs):
        ins, outs = refs[:n], refs[n:2 * n]
        send_sems, recv_sems = refs[2 * n:]
        x, y, c = _place()
        cps = [pltpu.make_async_remote_copy(
            src_ref=_half_rows(ins[k], 1 - c, lead=(slice(None),)), dst_ref=outs[k], send_sem=send_sems.at[k],
            recv_sem=recv_sems.at[k], device_id=(x, y, 1 - c), device_id_type=MESH) for k in range(n)]
        for cp in cps:
            cp.start()
        for cp in cps:
            cp.wait()

    return pl.pallas_call(
        body, name=name, in_specs=[HBM_SPEC] * n, out_specs=[HBM_SPEC] * n,
        out_shape=[jax.ShapeDtypeStruct((g.shape[0], g.shape[1] // 2, g.shape[2]), g.dtype) for g in grads],
        scratch_shapes=[pltpu.SemaphoreType.DMA((n,)), pltpu.SemaphoreType.DMA((n,))],
    )(*grads)


def _exchange_plan(parts):
    n = len(parts)

    def copies(ins, outs, sems):
        send_sems, recv_sems = sems
        x, y, c = _place()
        cps = []
        for k in range(n):
            for j, (fx, fy) in enumerate(FLIPS):
                tx, ty = x ^ fx, y ^ fy
                cps.append(pltpu.make_async_remote_copy(
                    src_ref=ins[k].at[2 * tx + ty], dst_ref=_half_rows(outs[k], c, lead=(j,)),
                    send_sem=send_sems.at[3 * k + j], recv_sem=recv_sems.at[3 * k + j],
                    device_id=(tx, ty, c), device_id_type=MESH))
        return cps

    def start(ins, outs, sems):
        for cp in copies(ins, outs, sems):
            cp.start()

    def finish(ins, outs, sems):
        for cp in copies(ins, outs, sems):
            cp.wait()

    return _Hosted(list(parts), [jax.ShapeDtypeStruct((3, 2 * p.shape[1], p.shape[2]), p.dtype) for p in parts],
                   [pltpu.SemaphoreType.DMA((3 * n,)), pltpu.SemaphoreType.DMA((3 * n,))], start, finish)


def _join_core_halves(bufs):
    n = len(bufs)

    def body(*refs):
        outs = refs[n:2 * n]
        send_sems, recv_sems = refs[2 * n:]
        x, y, c = _place()

        def rows(k, h):
            return _half_rows(outs[k], h, lead=(slice(None),) * (len(outs[k].shape) - 2))

        cps = [pltpu.make_async_remote_copy(
            src_ref=rows(k, c), dst_ref=rows(k, c), send_sem=send_sems.at[k],
            recv_sem=recv_sems.at[k], device_id=(x, y, 1 - c), device_id_type=MESH) for k in range(n)]
        for cp in cps:
            cp.start()
        for k in range(n):
            cps[k].wait_send()
            pltpu.make_async_remote_copy(
                src_ref=rows(k, c), dst_ref=rows(k, 1 - c), send_sem=send_sems.at[k],
                recv_sem=recv_sems.at[k], device_id=(x, y, 1 - c), device_id_type=MESH).wait_recv()

    return pl.pallas_call(
        body, name="join_core_halves", in_specs=[HBM_SPEC] * n, out_specs=[HBM_SPEC] * n,
        out_shape=[jax.ShapeDtypeStruct(f.shape, f.dtype) for f in bufs],
        input_output_aliases={k: k for k in range(n)},
        scratch_shapes=[pltpu.SemaphoreType.DMA((n,)), pltpu.SemaphoreType.DMA((n,))],
    )(*bufs)


def _gather_small(block, name):
    m, n = block.shape

    def body(x_ref, out_ref, send_sems, recv_sems, local_sem):
        x, y, c = _place()
        me, sibling = (x, y, c), (x, y, 1 - c)
        chips = [(x ^ fx, y ^ fy) for fx, fy in FLIPS]

        def rows(px, py, pc):
            return out_ref.at[pl.ds((4 * px + 2 * py + pc) * m, m), :]

        def copy(k, blk, to, src=None):
            return pltpu.make_async_remote_copy(
                src_ref=rows(*blk) if src is None else src, dst_ref=rows(*blk),
                send_sem=send_sems.at[k], recv_sem=recv_sems.at[k], device_id=to, device_id_type=MESH)

        mine = pltpu.make_async_copy(x_ref, rows(*me), local_sem)
        mine.start()
        first = [copy(0, me, sibling, src=x_ref)]
        first += [copy(1 + j, me, (*chip, c), src=x_ref) for j, chip in enumerate(chips)]
        for cp in first:
            cp.start()
        passed = [copy(4 + j, (*chip, c), sibling) for j, chip in enumerate(chips)]
        for j, chip in enumerate(chips):
            copy(1 + j, (*chip, c), me).wait_recv()
            passed[j].start()
        copy(0, sibling, me).wait_recv()
        for j, chip in enumerate(chips):
            copy(4 + j, (*chip, 1 - c), me).wait_recv()
        for cp in first + passed:
            cp.wait_send()
        mine.wait()

    out = pl.pallas_call(
        body, name=name,
        out_shape=jax.ShapeDtypeStruct((8 * m, n), block.dtype),
        in_specs=[pl.BlockSpec(memory_space=pltpu.VMEM)],
        out_specs=pl.BlockSpec(memory_space=pltpu.VMEM),
        scratch_shapes=[pltpu.SemaphoreType.DMA((7,)), pltpu.SemaphoreType.DMA((7,)), pltpu.SemaphoreType.DMA],
    )(block)
    return out.reshape(8, m, n)


def _add_core_halves(place, grad, from_sibling, name):
    nc, r, cols = grad.shape
    rh = r // 2

    def body(place_ref, g_ref, s_ref, o_ref, own_ref):
        total = (g_ref[...].astype(F32) + s_ref[...].astype(F32)).astype(BF16)
        o_ref[...] = total

        @pl.when(pl.program_id(0) == place_ref[0])
        def _():
            own_ref[...] = total[0]

    spec = pl.BlockSpec((1, rh, cols), lambda t, place_ref: (t, 0, 0))
    return pl.pallas_call(
        body, name=name,
        grid_spec=pltpu.PrefetchScalarGridSpec(
            num_scalar_prefetch=1, grid=(nc,),
            in_specs=[pl.BlockSpec((1, rh, cols), lambda t, place_ref: (t, place_ref[1], 0)), spec],
            out_specs=[spec, pl.BlockSpec((rh, cols), lambda t, place_ref: (place_ref[1], 0))]),
        out_shape=[jax.ShapeDtypeStruct((nc, rh, cols), BF16), jax.ShapeDtypeStruct((r, cols), BF16)],
        compiler_params=_cparams(("arbitrary",)),
    )(place, grad, from_sibling)


def _sum_adamw(own, others, w, m, v, name):
    rws, cols = w.shape
    tr = rws
    for cand in (256, 128, 64, 32, 16):
        if rws % cand == 0 and cand * cols * 4 <= 2 * 1024 * 1024:
            tr = cand
            break
    c1 = 1.0 - ADAM_B1 ** ADAM_STEP
    c2 = 1.0 - ADAM_B2 ** ADAM_STEP

    def body(a_ref, b_ref, w_ref, m_ref, v_ref, g_ref, d_ref, nm_ref, nv_ref):
        gv = a_ref[...].astype(F32)
        for j in range(3):
            gv = gv + b_ref[j].astype(F32)
        nm = ADAM_B1 * m_ref[...] + (1.0 - ADAM_B1) * gv
        nv = ADAM_B2 * v_ref[...] + (1.0 - ADAM_B2) * (gv * gv)
        m_hat = nm / c1
        v_hat = nv / c2
        g_ref[...] = gv
        d_ref[...] = -ADAM_LR * (m_hat / (jnp.sqrt(v_hat) + ADAM_EPS) + ADAM_WD * w_ref[...])
        nm_ref[...] = nm
        nv_ref[...] = nv

    spec = pl.BlockSpec((tr, cols), lambda i: (i, 0))
    return pl.pallas_call(
        body, name=name, grid=(rws // tr,),
        in_specs=[spec, pl.BlockSpec((3, tr, cols), lambda i: (0, i, 0))] + [spec] * 3, out_specs=[spec] * 4,
        out_shape=[jax.ShapeDtypeStruct((rws, cols), F32)] * 4,
        compiler_params=_cparams(("parallel",)),
    )(own, others, w, m, v)


def _sum_devices(blocks):
    n, m, _ = blocks.shape

    def body(b_ref, o_ref):
        acc = b_ref[0]
        for j in range(1, n):
            acc = acc + b_ref[j]
        o_ref[...] = acc

    return pl.pallas_call(
        body, name="sum_devices", out_shape=jax.ShapeDtypeStruct((m, LANES), F32),
        in_specs=[pl.BlockSpec(memory_space=pltpu.VMEM)], out_specs=pl.BlockSpec(memory_space=pltpu.VMEM),
    )(blocks)


def _adamw(w, g, m, v, name):
    rws, cols = w.shape
    tr = rws
    for cand in (256, 128, 64, 32, 16, 8):
        if rws % cand == 0 and cand * cols * 4 <= 2 * 1024 * 1024:
            tr = cand
            break
    c1 = 1.0 - ADAM_B1 ** ADAM_STEP
    c2 = 1.0 - ADAM_B2 ** ADAM_STEP

    def body(w_ref, g_ref, m_ref, v_ref, d_ref, nm_ref, nv_ref):
        gv = g_ref[...]
        nm = ADAM_B1 * m_ref[...] + (1.0 - ADAM_B1) * gv
        nv = ADAM_B2 * v_ref[...] + (1.0 - ADAM_B2) * (gv * gv)
        m_hat = nm / c1
        v_hat = nv / c2
        d_ref[...] = -ADAM_LR * (m_hat / (jnp.sqrt(v_hat) + ADAM_EPS) + ADAM_WD * w_ref[...])
        nm_ref[...] = nm
        nv_ref[...] = nv

    spec = pl.BlockSpec((tr, cols), lambda i: (i, 0))
    return pl.pallas_call(
        body, name=name, grid=(rws // tr,), in_specs=[spec] * 4, out_specs=[spec] * 3,
        out_shape=[jax.ShapeDtypeStruct((rws, cols), F32)] * 3,
        compiler_params=_cparams(("parallel",)),
    )(w, g, m, v)


WEIGHTS = (
    ("meta_tokens", (16, 1024), 1), ("norm_mix_g", (1024,), None), ("w_in", (1024, 7184), 1),
    ("w_alpha_up", (16, 512), 1), ("b_alpha", (512,), None), ("gla_norm_g", (1024,), None),
    ("w_gla_o", (1024, 1024), 0), ("conf_dw_w", (31, 1024), 1), ("conf_dw_b", (1024,), None),
    ("conf_ln_g", (1024,), None), ("conf_ln_b", (1024,), None), ("w_conf_o", (1024, 1024), 0),
    ("w_out", (1024, 1024), 0), ("norm_ffn_g", (1024,), None), ("w_up", (1024, 5632), 1),
    ("ffn_dw_w", (3, 2816), 1), ("ffn_dw_b", (2816,), None), ("w_down", (2816, 1024), 0),
    ("final_norm_g", (1024,), None),
)
BIG = ("w_in", "w_up", "w_down", "w_gla_o", "w_conf_o", "w_out")
SMALL_SHARDED = ("meta_tokens", "w_alpha_up", "conf_dw_w", "ffn_dw_w")
REPLICATED = tuple(n for n, _, ax in WEIGHTS if ax is None)
SHAPES = {n: s for n, s, _ in WEIGHTS}
AXIS = {n: ax for n, _, ax in WEIGHTS}
N_CHIPS = 4


def _shard_shape(name):
    s = list(SHAPES[name])
    s[AXIS[name]] //= N_CHIPS
    return tuple(s)


def _pack(parts, mult):
    flat = jnp.concatenate([p.reshape(-1) for p in parts])
    pad = (-flat.shape[0]) % mult
    return jnp.pad(flat, (0, pad))


def _unpack(flat, names, shape_of):
    out, off = {}, 0
    for n in names:
        shp = shape_of(n)
        size = math.prod(shp)
        out[n] = flat[off:off + size].reshape(shp)
        off += size
    return out


def _to_chip_major(full):
    r, cols = full.shape
    return full.reshape(r, N_CHIPS, cols // N_CHIPS).transpose(1, 0, 2)


def _from_chip_major(slabs):
    nc, r, cs = slabs.shape
    return slabs.transpose(1, 0, 2).reshape(r, nc * cs)


def kernel(x, meta_tokens, norm_mix_g, w_in, w_alpha_up, b_alpha, gla_norm_g, w_gla_o, conf_dw_w, conf_dw_b, conf_ln_g, conf_ln_b, w_conf_o, w_out, norm_ffn_g, w_up, ffn_dw_w, ffn_dw_b, w_down, final_norm_g, loss_target, m_meta_tokens, m_norm_mix_g, m_w_in, m_w_alpha_up, m_b_alpha, m_gla_norm_g, m_w_gla_o, m_conf_dw_w, m_conf_dw_b, m_conf_ln_g, m_conf_ln_b, m_w_conf_o, m_w_out, m_norm_ffn_g, m_w_up, m_ffn_dw_w, m_ffn_dw_b, m_w_down, m_final_norm_g, v_meta_tokens, v_norm_mix_g, v_w_in, v_w_alpha_up, v_b_alpha, v_gla_norm_g, v_w_gla_o, v_conf_dw_w, v_conf_dw_b, v_conf_ln_g, v_conf_ln_b, v_w_conf_o, v_w_out, v_norm_ffn_g, v_w_up, v_ffn_dw_w, v_ffn_dw_b, v_w_down, v_final_norm_g):
    names = [n for n, _, _ in WEIGHTS]
    w_args = (meta_tokens, norm_mix_g, w_in, w_alpha_up, b_alpha, gla_norm_g, w_gla_o, conf_dw_w, conf_dw_b, conf_ln_g,
              conf_ln_b, w_conf_o, w_out, norm_ffn_g, w_up, ffn_dw_w, ffn_dw_b, w_down, final_norm_g)
    m_args = (m_meta_tokens, m_norm_mix_g, m_w_in, m_w_alpha_up, m_b_alpha, m_gla_norm_g, m_w_gla_o, m_conf_dw_w,
              m_conf_dw_b, m_conf_ln_g, m_conf_ln_b, m_w_conf_o, m_w_out, m_norm_ffn_g, m_w_up, m_ffn_dw_w, m_ffn_dw_b,
              m_w_down, m_final_norm_g)
    v_args = (v_meta_tokens, v_norm_mix_g, v_w_in, v_w_alpha_up, v_b_alpha, v_gla_norm_g, v_w_gla_o, v_conf_dw_w,
              v_conf_dw_b, v_conf_ln_g, v_conf_ln_b, v_w_conf_o, v_w_out, v_norm_ffn_g, v_w_up, v_ffn_dw_w, v_ffn_dw_b,
              v_w_down, v_final_norm_g)
    in_shape = {n: a.shape for n, a in zip(names, w_args)}
    local = {n: a.reshape(_shard_shape(n) if AXIS[n] is not None else SHAPES[n]) for n, a in zip(names, w_args)}
    m_loc = {n: a.reshape(local[n].shape) for n, a in zip(names, m_args)}
    v_loc = {n: a.reshape(local[n].shape) for n, a in zip(names, v_args)}

    me = 2 * lax.axis_index("x") + lax.axis_index("y")

    small = _pack([local[n] for n in SMALL_SHARDED], 8 * LANES).reshape(-1, LANES)
    small_all = _gather_small(small, "gather_small_weights")[::2].reshape(N_CHIPS, -1)
    per_chip_small = [_unpack(small_all[t], SMALL_SHARDED, _shard_shape) for t in range(N_CHIPS)]
    full = {n: jnp.concatenate([per_chip_small[t][n] for t in range(N_CHIPS)], axis=1) for n in SMALL_SHARDED}
    for n in REPLICATED:
        full[n] = local[n].reshape(1, -1)

    loss_part, grad_x, grads, own_chip, from_chips = _local_step(
        x[0], loss_target[0], full, {n: local[n].astype(BF16) for n in BIG})

    smalls = REPLICATED + SMALL_SHARDED
    rep = _pack([grads[n] for n in smalls] + [loss_part], 8 * LANES).reshape(-1, LANES)
    rep_sum = _sum_devices(_gather_small(rep, "gather_small_grads")).reshape(-1)
    g_loc = _unpack(rep_sum, smalls, lambda n: SHAPES[n])
    loss = rep_sum[sum(math.prod(SHAPES[n]) for n in smalls)]
    for n in SMALL_SHARDED:
        width = _shard_shape(n)[1]
        g_loc[n] = lax.dynamic_slice_in_dim(g_loc[n], me * width, width, axis=1)

    joined = _join_core_halves([own_chip[n] for n in BIG] + [from_chips[n] for n in BIG])
    delta, new_m, new_v = {}, {}, {}
    for k, n in enumerate(BIG):
        g_loc[n], delta[n], new_m[n], new_v[n] = _sum_adamw(
            joined[k], joined[len(BIG) + k], local[n], m_loc[n], v_loc[n], "sum_adamw_" + n)
    rest = SMALL_SHARDED + REPLICATED
    pk = lambda dct: _pack([dct[n] for n in rest], 8 * LANES).reshape(-1, LANES)
    ds, ms, vs = _adamw(pk(local), pk(g_loc), pk(m_loc), pk(v_loc), "adamw_small")
    shape_loc = lambda n: local[n].shape
    for dct, flat in ((delta, ds), (new_m, ms), (new_v, vs)):
        dct.update(_unpack(flat.reshape(-1), rest, shape_loc))

    outs = [loss, grad_x[None]]
    for dct in (g_loc, delta, new_m, new_v):
        outs += [dct[n].reshape(in_shape[n]) for n in names]
    return tuple(outs)
```
